```python
import math
import jax, jax.numpy as jnp
from jax import lax
import numpy as np

D_MODEL = 2048
BATCH = 1
SEQ = 8192
DEPTH = 1
DEC_BATCH = 128
DEC_SEQ = 8
PAST_LEN = 2048
PAGE_SIZE = 128

D_MIX = D_MODEL
ATT_HEAD_DIM = 64
D_ATT = 3 * D_MIX // 8
N_ATT_HEADS = D_ATT // ATT_HEAD_DIM
DILATION_PAIRS = ((128, 1), (512, 4), (2048, 16))
N_DIL = len(DILATION_PAIRS)
H_PER_DIL = N_ATT_HEADS // N_DIL
D_SSM = D_MIX - D_ATT
SSM_HEAD_DIM = 64
N_SSM_HEADS = D_SSM // SSM_HEAD_DIM
N_SSM_GROUPS = 4
HEADS_PER_SSM_GROUP = N_SSM_HEADS // N_SSM_GROUPS
D_STATE = 128
CONV_W = 4
CONV_DIM = D_SSM + 2 * N_SSM_GROUPS * D_STATE
SSD_CHUNK = 128
D_IN_PROJ = D_SSM + CONV_DIM + N_SSM_HEADS + 3 * D_ATT
D_FF = ((8 * D_MODEL // 3 + 255) // 256) * 256
RMS_EPS = 1e-6
ALIBI_MAX_EXP = 8.0

kernel_name = "hymba_ssd_dilated_alibi_macaron_step"


def rms_norm(x, g):
    xf = x.astype(jnp.float32)
    y = xf * lax.rsqrt(jnp.mean(xf * xf, axis=-1, keepdims=True) + RMS_EPS)
    return (y * g.astype(jnp.float32)).astype(x.dtype)


def swiglu(h, w_gate, w_up, w_down):
    return (jax.nn.silu(h @ w_gate) * (h @ w_up)) @ w_down


def alibi_slopes():
    i = jnp.arange(1, N_ATT_HEADS + 1, dtype=jnp.float32)
    return jnp.exp2(-ALIBI_MAX_EXP * i / N_ATT_HEADS)


def causal_dwconv(x_full, w, b):
    t_len = x_full.shape[1] - (CONV_W - 1)
    acc = b
    for j in range(CONV_W):
        acc = acc + x_full[:, j:j + t_len] * w[j]
    return acc


def ssd_chunked(x, dt, a, bm, cm, h0):
    f32 = jnp.float32
    b, L = x.shape[:2]
    cl = min(SSD_CHUNK, L)
    pad = (-L) % cl

    def padt(t):
        return jnp.pad(t.astype(f32), ((0, 0), (0, pad)) + ((0, 0),) * (t.ndim - 2))

    x, dt, bm, cm = padt(x), padt(dt), padt(bm), padt(cm)
    nc = (L + pad) // cl
    G, E, P = x.shape[2:]
    N = bm.shape[-1]
    xdt = (x * dt[..., None]).reshape(b, nc, cl, G, E, P)
    acum = jnp.cumsum((dt * a).reshape(b, nc, cl, G, E), axis=2)
    bc = bm.reshape(b, nc, cl, G, N)
    cc = cm.reshape(b, nc, cl, G, N)
    causal = jnp.tril(jnp.ones((cl, cl), dtype=bool))
    seg = acum[:, :, :, None] - acum[:, :, None]
    decay = jnp.exp(jnp.where(causal[:, :, None, None], seg, -jnp.inf))
    cb = jnp.einsum("bclgn,bcsgn->bclsg", cc, bc)
    y_diag = jnp.einsum("bclsge,bcsgep->bclgep", cb[..., None] * decay, xdt)
    to_end = jnp.exp(acum[:, :, -1:] - acum)
    chunk_states = jnp.einsum("bclgn,bclgep->bcgepn", bc, xdt * to_end[..., None])
    chunk_decay = jnp.exp(acum[:, :, -1])

    def carry(h, inp):
        dec, st = inp
        return h * dec[..., None, None] + st, h

    h_fin, h_in = lax.scan(carry, h0, (jnp.moveaxis(chunk_decay, 1, 0), jnp.moveaxis(chunk_states, 1, 0)))
    h_in = jnp.moveaxis(h_in, 0, 1)
    y_off = jnp.einsum("bclgn,bcgepn->bclgep", cc, h_in) * jnp.exp(acum)[..., None]
    y = (y_diag + y_off).reshape(b, nc * cl, G, E, P)[:, :L]
    return y, h_fin


def dilated_attn_prompt(q, k, v, window, dil, slopes):
    f32 = jnp.float32
    b, S, H, Dh = q.shape
    nb = window // dil
    L = S // dil
    nblk = -(-L // nb)
    Lp = nblk * nb

    def to_blocks(t):
        t = t.reshape(b, L, dil, H, Dh).transpose(0, 2, 1, 3, 4).reshape(b * dil, L, H, Dh)
        t = jnp.pad(t, ((0, 0), (0, Lp - L), (0, 0), (0, 0)))
        return t.reshape(b * dil, nblk, nb, H, Dh)

    def with_prev(t):
        prev = jnp.pad(t[:, :-1], ((0, 0), (1, 0), (0, 0), (0, 0), (0, 0)))
        return jnp.concatenate([prev, t], axis=2)

    qb = to_blocks(q)
    kc = with_prev(to_blocks(k))
    vc = with_prev(to_blocks(v))
    s = jnp.einsum("nbqhd,nbkhd->nbhqk", qb, kc, preferred_element_type=f32) * (Dh ** -0.5)
    qi = jnp.arange(nb)[:, None]
    kj = jnp.arange(2 * nb)[None, :]
    dist = nb + qi - kj
    first = (jnp.arange(nblk) == 0)[:, None, None]
    valid = ((dist >= 0) & (dist <= nb))[None] & ~(first & (kj < nb)[None])
    bias = -slopes[:, None, None] * (dist * dil).astype(f32)[None]
    s = jnp.where(valid[None, :, None], s + bias[None, None], -jnp.inf)
    m = jnp.max(s, axis=-1, keepdims=True)
    p = jnp.exp(s - m)
    l = jnp.sum(p, axis=-1, keepdims=True)
    o = jnp.einsum("nbhqk,nbkhd->nbqhd", p / l, vc.astype(f32))
    lse = (m + jnp.log(l))[..., 0].transpose(0, 1, 3, 2)

    def from_blocks(t):
        tail = t.shape[3:]
        t = t.reshape((b, dil, Lp) + tail)[:, :, :L]
        t = jnp.moveaxis(t, 1, 2)
        return t.reshape((b, S) + tail)

    return from_blocks(o).astype(q.dtype), from_blocks(lse)


def dilated_attn_sample(q, k, v, k_buf, v_buf, window, dil, slopes):
    f32 = jnp.float32
    b, T, H, Dh = q.shape
    n_buf = k_buf.shape[1]
    kf = jnp.concatenate([k_buf.astype(k.dtype), k], axis=1)
    vf = jnp.concatenate([v_buf.astype(v.dtype), v], axis=1)
    steps = jnp.arange(window // dil + 1)
    idx = n_buf + jnp.arange(T)[:, None] - steps[None, :] * dil
    valid = idx >= 0
    idx = jnp.maximum(idx, 0)
    kg = kf[:, idx]
    vg = vf[:, idx]
    s = jnp.einsum("bthd,btkhd->bthk", q, kg, preferred_element_type=f32) * (Dh ** -0.5)
    bias = -slopes[:, None] * (steps * dil).astype(f32)[None]
    s = jnp.where(valid[None, :, None, :], s + bias[None, None], -jnp.inf)
    m = jnp.max(s, axis=-1, keepdims=True)
    p = jnp.exp(s - m)
    l = jnp.sum(p, axis=-1, keepdims=True)
    o = jnp.einsum("bthk,btkhd->bthd", p / l, vg.astype(f32))
    lse = (m + jnp.log(l))[..., 0]
    return o.astype(q.dtype), lse


def token_mixer(hn, past, w_in, conv_w, conv_b, dt_bias, a_log, d_skip, ssm_norm, w_out):
    f32 = jnp.float32
    b, T, _ = hn.shape
    G, E, P, N = N_SSM_GROUPS, HEADS_PER_SSM_GROUP, SSM_HEAD_DIM, D_STATE
    proj = hn @ w_in
    cuts = [D_SSM, D_SSM + CONV_DIM, D_SSM + CONV_DIM + N_SSM_HEADS,
            D_SSM + CONV_DIM + N_SSM_HEADS + D_ATT, D_SSM + CONV_DIM + N_SSM_HEADS + 2 * D_ATT]
    z, xbc, dt_raw, q, k, v = jnp.split(proj, cuts, axis=-1)
    if past is None:
        conv_buf = jnp.zeros((b, CONV_W - 1, CONV_DIM), hn.dtype)
        h0 = jnp.zeros((b, G, E, P, N), f32)
    else:
        k_bufs, v_bufs, ssm_state, conv_buf = past
        h0 = ssm_state.astype(f32).reshape(b, G, E, P, N)

    xbc_full = jnp.concatenate([conv_buf.astype(xbc.dtype), xbc], axis=1)
    new_conv = xbc_full[:, xbc_full.shape[1] - (CONV_W - 1):]
    xbc = jax.nn.silu(causal_dwconv(xbc_full, conv_w, conv_b))
    xs, bm, cm = jnp.split(xbc, [D_SSM, D_SSM + G * N], axis=-1)
    xs = xs.reshape(b, T, G, E, P)
    bm = bm.reshape(b, T, G, N)
    cm = cm.reshape(b, T, G, N)
    dt = jax.nn.softplus(dt_raw.astype(f32) + dt_bias.astype(f32)).reshape(b, T, G, E)
    a = -jnp.exp(a_log.astype(f32)).reshape(G, E)
    y, h_fin = ssd_chunked(xs, dt, a, bm, cm, h0)
    y = y + d_skip.astype(f32).reshape(G, E, 1) * xs.astype(f32)
    y = y.reshape(b, T, D_SSM) * jax.nn.silu(z.astype(f32))
    yg = y.reshape(b, T, G, D_SSM // G)
    yg = yg * lax.rsqrt(jnp.mean(yg * yg, axis=-1, keepdims=True) + RMS_EPS)
    ssm_out = (yg.reshape(b, T, D_SSM) * ssm_norm.astype(f32)).astype(hn.dtype)

    q = q.reshape(b, T, N_DIL, H_PER_DIL, ATT_HEAD_DIM)
    k = k.reshape(b, T, N_DIL, H_PER_DIL, ATT_HEAD_DIM)
    v = v.reshape(b, T, N_DIL, H_PER_DIL, ATT_HEAD_DIM)
    slopes = alibi_slopes().reshape(N_DIL, H_PER_DIL)
    outs, lses, new_kv = [], [], []
    for g, (win, dil) in enumerate(DILATION_PAIRS):
        if past is None:
            o, lse = dilated_attn_prompt(q[:, :, g], k[:, :, g], v[:, :, g], win, dil, slopes[g])
            keep = min(win, T)
            new_kv += [k[:, T - keep:, g], v[:, T - keep:, g]]
        else:
            o, lse = dilated_attn_sample(q[:, :, g], k[:, :, g], v[:, :, g], k_bufs[g], v_bufs[g], win, dil, slopes[g])
            new_kv += [k[:, :, g], v[:, :, g]]
        outs.append(o)
        lses.append(lse)
    alpha = jnp.moveaxis(jax.nn.softmax(jnp.stack(lses, axis=0), axis=0), 0, 2)
    att = jnp.stack(outs, axis=2) * alpha[..., None].astype(hn.dtype)
    att_out = att.reshape(b, T, D_ATT)

    out = jnp.concatenate([ssm_out, att_out], axis=-1) @ w_out
    new_ssm = h_fin.reshape(b, N_SSM_HEADS, P, N).astype(hn.dtype)
    return out, (*new_kv, new_ssm, new_conv)


def decoder_layer(x, past, lp):
    (f1n, f1g, f1u, f1d, mn, w_in, conv_w, conv_b, dt_bias, a_log, d_skip, ssm_norm, w_out,
     f2n, f2g, f2u, f2d) = lp
    x = x + 0.5 * swiglu(rms_norm(x, f1n), f1g, f1u, f1d)
    mix, state = token_mixer(rms_norm(x, mn), past, w_in, conv_w, conv_b, dt_bias, a_log, d_skip, ssm_norm, w_out)
    x = x + mix
    x = x + 0.5 * swiglu(rms_norm(x, f2n), f2g, f2u, f2d)
    return x, state


def setup_inputs(seed: int = 0) -> dict:
    key = jax.random.key(seed)
    ks = jax.random.split(key, 32)
    f32 = jnp.float32

    def nrm(i, shape, scale):
        return scale * jax.random.normal(ks[i], shape, f32)

    nb0 = min(DILATION_PAIRS[0][0], PAST_LEN)
    nb1 = min(DILATION_PAIRS[1][0], PAST_LEN)
    nb2 = min(DILATION_PAIRS[2][0], PAST_LEN)
    kvs = (DEPTH, DEC_BATCH)
    hd = (H_PER_DIL, ATT_HEAD_DIM)
    dt0 = jnp.exp(jax.random.uniform(ks[20], (DEPTH, N_SSM_HEADS), f32, math.log(1e-3), math.log(1e-1)))
    return {
        "x_prompt": nrm(0, (BATCH, SEQ, D_MODEL), 1.0),
        "x_sample": nrm(1, (DEC_BATCH, DEC_SEQ, D_MODEL), 1.0),
        "cache_k0": nrm(2, kvs + (nb0,) + hd, 1.0),
        "cache_v0": nrm(3, kvs + (nb0,) + hd, 1.0),
        "cache_k1": nrm(4, kvs + (nb1,) + hd, 1.0),
        "cache_v1": nrm(5, kvs + (nb1,) + hd, 1.0),
        "cache_k2": nrm(6, kvs + (nb2,) + hd, 1.0),
        "cache_v2": nrm(7, kvs + (nb2,) + hd, 1.0),
        "state_ssm": nrm(8, (DEPTH, DEC_BATCH, N_SSM_HEADS, SSM_HEAD_DIM, D_STATE), 0.5),
        "state_conv": nrm(9, (DEPTH, DEC_BATCH, CONV_W - 1, CONV_DIM), 1.0),
        "ffn1_norm": 1.0 + nrm(10, (DEPTH, D_MODEL), 0.02),
        "ffn1_w_gate": nrm(11, (DEPTH, D_MODEL, D_FF), D_MODEL ** -0.5),
        "ffn1_w_up": nrm(12, (DEPTH, D_MODEL, D_FF), D_MODEL ** -0.5),
        "ffn1_w_down": nrm(13, (DEPTH, D_FF, D_MODEL), D_FF ** -0.5),
        "mix_norm": 1.0 + nrm(14, (DEPTH, D_MODEL), 0.02),
        "w_in": nrm(15, (DEPTH, D_MODEL, D_IN_PROJ), D_MODEL ** -0.5),
        "conv_w": nrm(16, (DEPTH, CONV_W, CONV_DIM), CONV_W ** -0.5),
        "conv_b": nrm(17, (DEPTH, CONV_DIM), 0.01),
        "dt_bias": dt0 + jnp.log(-jnp.expm1(-dt0)),
        "a_log": jnp.log(jax.random.uniform(ks[21], (DEPTH, N_SSM_HEADS), f32, 1.0, 16.0)),
        "d_skip": 1.0 + nrm(22, (DEPTH, N_SSM_HEADS), 0.1),
        "ssm_norm": 1.0 + nrm(23, (DEPTH, D_SSM), 0.02),
        "w_out": nrm(24, (DEPTH, D_MIX, D_MODEL), D_MIX ** -0.5),
        "ffn2_norm": 1.0 + nrm(25, (DEPTH, D_MODEL), 0.02),
        "ffn2_w_gate": nrm(26, (DEPTH, D_MODEL, D_FF), D_MODEL ** -0.5),
        "ffn2_w_up": nrm(27, (DEPTH, D_MODEL, D_FF), D_MODEL ** -0.5),
        "ffn2_w_down": nrm(28, (DEPTH, D_FF, D_MODEL), D_FF ** -0.5),
        "final_norm": 1.0 + nrm(29, (D_MODEL,), 0.02),
    }


def reference(x_prompt, x_sample, cache_k0, cache_v0, cache_k1, cache_v1, cache_k2, cache_v2,
              state_ssm, state_conv, ffn1_norm, ffn1_w_gate, ffn1_w_up, ffn1_w_down, mix_norm,
              w_in, conv_w, conv_b, dt_bias, a_log, d_skip, ssm_norm, w_out, ffn2_norm,
              ffn2_w_gate, ffn2_w_up, ffn2_w_down, final_norm):
    xp, xs = x_prompt, x_sample
    st_p, st_s = [], []
    for layer in range(DEPTH):
        lp = (ffn1_norm[layer], ffn1_w_gate[layer], ffn1_w_up[layer], ffn1_w_down[layer],
              mix_norm[layer], w_in[layer], conv_w[layer], conv_b[layer], dt_bias[layer],
              a_log[layer], d_skip[layer], ssm_norm[layer], w_out[layer],
              ffn2_norm[layer], ffn2_w_gate[layer], ffn2_w_up[layer], ffn2_w_down[layer])
        past = ((cache_k0[layer], cache_k1[layer], cache_k2[layer]),
                (cache_v0[layer], cache_v1[layer], cache_v2[layer]),
                state_ssm[layer], state_conv[layer])
        xp, sp = decoder_layer(xp, None, lp)
        xs, ss = decoder_layer(xs, past, lp)
        st_p.append(sp)
        st_s.append(ss)
    y_prompt = rms_norm(xp, final_norm)
    y_sample = rms_norm(xs, final_norm)
    p_k0, p_v0, p_k1, p_v1, p_k2, p_v2, p_ssm, p_conv = [jnp.stack(t, axis=0) for t in zip(*st_p)]
    s_k0, s_v0, s_k1, s_v1, s_k2, s_v2, s_ssm, s_conv = [jnp.stack(t, axis=0) for t in zip(*st_s)]
    return (y_prompt, y_sample, p_k0, p_v0, p_k1, p_v1, p_k2, p_v2, p_ssm, p_conv,
            s_k0, s_v0, s_k1, s_v1, s_k2, s_v2, s_ssm, s_conv)
```

```python
import functools

import numpy as np
import jax
import jax.numpy as jnp
from jax import lax
from jax.experimental import pallas as pl
from jax.experimental.pallas import tpu as pltpu

F32 = jnp.float32
BF16 = jnp.bfloat16

D_MODEL = 2048
D_FF = 5632
D_ATT = 768
ATT_HEAD_DIM = 64
H_PER_DIL = 4
D_GROUP = H_PER_DIL * ATT_HEAD_DIM
DILATION_PAIRS = ((128, 1), (512, 4), (2048, 16))
N_DIL = len(DILATION_PAIRS)
ATT_BLOCK = 128
D_SSM = 1280
SSM_HEAD_DIM = 64
N_SSM_HEADS = 20
N_SSM_GROUPS = 4
HEADS_PER_SSM_GROUP = 5
D_SSM_GROUP = D_SSM // N_SSM_GROUPS
D_STATE = 128
CONV_W = 4
CONV_DIM = 2304
SSD_CHUNK = 128
RMS_EPS = 1e-6
ALIBI_MAX_EXP = 8.0
N_ATT_HEADS = 12

LANES = 128
SUBLANES = 8
DT_PAD = LANES
IN_Z0, IN_XBC0, IN_QKV0, IN_DT0 = 0, D_SSM, D_SSM + CONV_DIM, D_SSM + CONV_DIM + 3 * D_ATT
IN_COLS = IN_DT0 + DT_PAD

VMEM_LIMIT = 56 * 1024 * 1024


def _params(n_grid_dims, vmem=VMEM_LIMIT):
    return pltpu.CompilerParams(dimension_semantics=("arbitrary",) * n_grid_dims,
                                vmem_limit_bytes=vmem)


def _sigmoid(x):
    return 1.0 / (1.0 + jnp.exp(-x))


def _silu(x):
    return x * _sigmoid(x)


def _softplus(x):
    return jnp.maximum(x, 0.0) + jnp.log(1.0 + jnp.exp(-jnp.abs(x)))


def _rms_scale(x):
    return x * lax.rsqrt(jnp.mean(x * x, axis=-1, keepdims=True) + RMS_EPS)


def _dot(a, b):
    return jnp.dot(a, b, preferred_element_type=F32)


def _dot_nt(a, b):
    return lax.dot_general(a, b, (((1,), (1,)), ((), ())), preferred_element_type=F32)


def _dot_tn(a, b):
    return lax.dot_general(a, b, (((0,), (0,)), ((), ())), preferred_element_type=F32)


def _cumsum_rows(x):
    n = x.shape[0]
    rows = lax.broadcasted_iota(jnp.int32, x.shape, 0)
    shift = 1
    while shift < n:
        x = x + jnp.where(rows >= shift, pltpu.roll(x, shift, 0), 0.0)
        shift *= 2
    return x


def _ffn_body(*refs, n_f, final):
    if final:
        x_ref, g_ref, wg_ref, wu_ref, wd_ref, fg_ref, o_ref, h_ref = refs
    else:
        x_ref, g_ref, wg_ref, wu_ref, wd_ref, o_ref, h_ref = refs
    f = pl.program_id(1)

    @pl.when(f == 0)
    def _():
        h_ref[...] = (_rms_scale(x_ref[...]) * g_ref[...]).astype(BF16)
        o_ref[...] = jnp.zeros_like(o_ref)

    h = h_ref[...]
    gate = _dot(h, wg_ref[...])
    up = _dot(h, wu_ref[...])
    act = (_silu(gate) * up).astype(BF16)
    o_ref[...] += _dot(act, wd_ref[...])

    @pl.when(f == n_f - 1)
    def _():
        y = x_ref[...] + 0.5 * o_ref[...]
        if final:
            y = _rms_scale(y) * fg_ref[...]
        o_ref[...] = y


def _ffn(x, gain, wg, wu, wd, final_gain=None, *, tm=512, tf=512):
    t = x.shape[0]
    n_f = D_FF // tf
    final = final_gain is not None
    row = lambda i, f: (i, 0)
    const = lambda i, f: (0, 0)
    in_specs = [pl.BlockSpec((tm, D_MODEL), row), pl.BlockSpec((1, D_MODEL), const),
                pl.BlockSpec((D_MODEL, tf), lambda i, f: (0, f)),
                pl.BlockSpec((D_MODEL, tf), lambda i, f: (0, f)),
                pl.BlockSpec((tf, D_MODEL), lambda i, f: (f, 0))]
    args = [x, gain, wg, wu, wd]
    if final:
        in_specs.append(pl.BlockSpec((1, D_MODEL), const))
        args.append(final_gain)
    return pl.pallas_call(
        functools.partial(_ffn_body, n_f=n_f, final=final),
        grid=(t // tm, n_f),
        in_specs=in_specs,
        out_specs=pl.BlockSpec((tm, D_MODEL), row),
        out_shape=jax.ShapeDtypeStruct((t, D_MODEL), F32),
        scratch_shapes=[pltpu.VMEM((tm, D_MODEL), BF16)],
        compiler_params=_params(2),
        name="ffn_final" if final else "ffn",
    )(*args)


def _inproj_body(x_ref, g_ref, w_ref, z_ref, xbc_ref, qkv_ref, dt_ref):
    h = (_rms_scale(x_ref[...]) * g_ref[...]).astype(BF16)
    z_ref[...] = _dot(h, w_ref[:, IN_Z0:IN_XBC0])
    xbc_ref[...] = _dot(h, w_ref[:, IN_XBC0:IN_QKV0])
    qkv_ref[...] = _dot(h, w_ref[:, IN_QKV0:IN_DT0])
    dt_ref[...] = _dot(h, w_ref[:, IN_DT0:IN_COLS])


def _inproj(x, gain, w_all, *, tm=256):
    t = x.shape[0]
    row = lambda i: (i, 0)
    const = lambda i: (0, 0)
    widths = (D_SSM, CONV_DIM, 3 * D_ATT, DT_PAD)
    return pl.pallas_call(
        _inproj_body,
        grid=(t // tm,),
        in_specs=[pl.BlockSpec((tm, D_MODEL), row), pl.BlockSpec((1, D_MODEL), const),
                  pl.BlockSpec((D_MODEL, IN_COLS), const, pipeline_mode=pl.Buffered(1))],
        out_specs=[pl.BlockSpec((tm, w), row) for w in widths],
        out_shape=[jax.ShapeDtypeStruct((t, w), F32) for w in widths],
        compiler_params=_params(1),
        name="inproj",
    )(x, gain, w_all)


def _pair_blockdiag(x, low_half):
    zero = jnp.zeros_like(x)
    return jnp.concatenate([jnp.where(low_half, x, zero), jnp.where(low_half, zero, x)],
                           axis=0).astype(BF16)


def _ssd_prompt_body(xbc_ref, z_ref, dt_ref, cw_ref, cb_ref, dtb_ref, alog_ref, dsk_ref, nrm_ref,
                     y_ref, hfin_ref, xf_scr, ht_scr):
    c = pl.program_id(0)
    n_c = pl.num_programs(0)
    cl = SSD_CHUNK

    @pl.when(c == 0)
    def _():
        xf_scr[0:SUBLANES, :] = jnp.zeros((SUBLANES, CONV_DIM), F32)
        ht_scr[...] = jnp.zeros_like(ht_scr)

    x = xbc_ref[...]
    xf_scr[SUBLANES:SUBLANES + cl, :] = x
    conv = cb_ref[...]
    for j in range(CONV_W - 1):
        conv = conv + xf_scr[SUBLANES - 3 + j:SUBLANES - 3 + j + cl, :] * cw_ref[j:j + 1, :]
    conv = conv + x * cw_ref[CONV_W - 1:CONV_W, :]
    xf_scr[0:SUBLANES, :] = x[cl - SUBLANES:cl, :]
    xa = _silu(conv)
    xs = xa[:, :D_SSM]
    bm = xa[:, D_SSM:D_SSM + N_SSM_GROUPS * D_STATE]
    cm = xa[:, D_SSM + N_SSM_GROUPS * D_STATE:]

    dt = _softplus(dt_ref[...] + dtb_ref[...])
    a = -jnp.exp(alog_ref[...])
    acum = _cumsum_rows(dt * a)
    w_end = dt * jnp.exp(acum[cl - 1:cl, :] - acum)
    acum_t = acum.T
    dt_t = dt.T
    w_end_t = w_end.T

    rows = lax.broadcasted_iota(jnp.int32, (cl, cl), 0)
    cols = lax.broadcasted_iota(jnp.int32, (cl, cl), 1)
    causal = rows >= cols
    low_half = cols < SSM_HEAD_DIM

    b_g = [bm[:, g * D_STATE:(g + 1) * D_STATE] for g in range(N_SSM_GROUPS)]
    c_g = [cm[:, g * D_STATE:(g + 1) * D_STATE].astype(BF16) for g in range(N_SSM_GROUPS)]
    b_t = [b.T for b in b_g]
    cb = [_dot_nt(c_g[g], b_g[g].astype(BF16)) for g in range(N_SSM_GROUPS)]

    sumsq = [jnp.zeros((cl, 1), F32) for _ in range(N_SSM_GROUPS)]
    for j in range(N_SSM_HEADS // 2):
        lo, hi = j * LANES, (j + 1) * LANES
        heads = (2 * j, 2 * j + 1)
        groups = tuple(h // HEADS_PER_SSM_GROUP for h in heads)
        xs_p = xs[:, lo:hi]
        x_bd = _pair_blockdiag(xs_p, low_half)
        ht_p = ht_scr[:, lo:hi]
        h_bd = _pair_blockdiag(ht_p, low_half)
        g_mats, w_mats, e_cols = [], [], []
        for h, g in zip(heads, groups):
            row_b = jnp.broadcast_to(acum_t[h:h + 1, :], (cl, cl))
            col_b = row_b.T
            decay = jnp.where(causal, jnp.exp(col_b - row_b), 0.0)
            g_mats.append((cb[g] * decay * dt_t[h:h + 1, :]).astype(BF16))
            w_mats.append((b_t[g] * w_end_t[h:h + 1, :]).astype(BF16))
            e_cols.append(jnp.exp(col_b))
        y_diag = _dot(jnp.concatenate(g_mats, axis=1), x_bd)
        y_off = _dot(jnp.concatenate([c_g[groups[0]], c_g[groups[1]]], axis=1), h_bd)
        y_off = y_off * jnp.where(low_half, e_cols[0], e_cols[1])
        state = _dot(jnp.concatenate(w_mats, axis=1), x_bd)
        chunk_decay = jnp.where(low_half[0:1, :], e_cols[0][cl - 1:cl, :], e_cols[1][cl - 1:cl, :])
        ht_scr[:, lo:hi] = ht_p * chunk_decay + state
        y_p = y_diag + y_off + dsk_ref[:, lo:hi] * xs_p
        y_p = y_p * _silu(z_ref[:, lo:hi])
        y_ref[:, lo:hi] = y_p
        sq = y_p * y_p
        sumsq[groups[0]] = sumsq[groups[0]] + jnp.sum(jnp.where(low_half, sq, 0.0), axis=1, keepdims=True)
        sumsq[groups[1]] = sumsq[groups[1]] + jnp.sum(jnp.where(low_half, 0.0, sq), axis=1, keepdims=True)

    scale = [lax.rsqrt(s / D_SSM_GROUP + RMS_EPS) for s in sumsq]
    for j in range(N_SSM_HEADS // 2):
        lo, hi = j * LANES, (j + 1) * LANES
        g0, g1 = (2 * j) // HEADS_PER_SSM_GROUP, (2 * j + 1) // HEADS_PER_SSM_GROUP
        y_ref[:, lo:hi] = y_ref[:, lo:hi] * jnp.where(low_half, scale[g0], scale[g1]) * nrm_ref[:, lo:hi]

    @pl.when(c == n_c - 1)
    def _():
        for j in range(D_SSM // LANES):
            hfin_ref[j * LANES:(j + 1) * LANES, :] = ht_scr[:, j * LANES:(j + 1) * LANES].T


def _ssd_prompt(xbc, z, dt, cw, cb, dtb, alog, dsk, nrm):
    t = xbc.shape[0]
    cl = SSD_CHUNK
    row = lambda c: (c, 0)
    const = lambda c: (0, 0)
    vec = lambda n: pl.BlockSpec((1, n), const)
    return pl.pallas_call(
        _ssd_prompt_body,
        grid=(t // cl,),
        in_specs=[pl.BlockSpec((cl, CONV_DIM), row), pl.BlockSpec((cl, D_SSM), row),
                  pl.BlockSpec((cl, DT_PAD), row), pl.BlockSpec((CONV_W, CONV_DIM), const),
                  vec(CONV_DIM), vec(DT_PAD), vec(DT_PAD), vec(D_SSM), vec(D_SSM)],
        out_specs=[pl.BlockSpec((cl, D_SSM), row), pl.BlockSpec((D_SSM, D_STATE), const)],
        out_shape=[jax.ShapeDtypeStruct((t, D_SSM), F32),
                   jax.ShapeDtypeStruct((D_SSM, D_STATE), F32)],
        scratch_shapes=[pltpu.VMEM((SUBLANES + cl, CONV_DIM), F32),
                        pltpu.VMEM((D_STATE, D_SSM), F32)],
        compiler_params=_params(1),
        name="ssd_prompt",
    )(xbc, z, dt, cw, cb, dtb, alog, dsk, nrm)


def _expand_heads(x, e_ref):
    hi = x.astype(BF16)
    r1 = x - hi.astype(F32)
    mid = r1.astype(BF16)
    lo = (r1 - mid.astype(F32)).astype(BF16)
    e = e_ref[...]
    return _dot(hi, e) + _dot(mid, e) + _dot(lo, e)


def _ssd_sample_body(xbc_ref, sc_ref, z_ref, dt_ref, h0_ref, cw_ref, cb_ref, dtb_ref, alog_ref,
                     dsk_ref, nrm_ref, e_ref, y_ref, hout_ref, xf_scr, xw_scr, b_scr, *, n_tok):
    step = pl.program_id(0)

    @pl.when(step == 0)
    def _():
        xw_scr[...] = jnp.zeros_like(xw_scr)
        b_scr[...] = jnp.zeros_like(b_scr)

    x = xbc_ref[...]
    xf_scr[SUBLANES - 3:SUBLANES, :] = sc_ref[0]
    xf_scr[SUBLANES:SUBLANES + n_tok, :] = x
    conv = cb_ref[...]
    for j in range(CONV_W - 1):
        conv = conv + xf_scr[SUBLANES - 3 + j:SUBLANES - 3 + j + n_tok, :] * cw_ref[j:j + 1, :]
    conv = conv + x * cw_ref[CONV_W - 1:CONV_W, :]
    xa = _silu(conv)
    xs = xa[:, :D_SSM]
    bm = xa[:, D_SSM:D_SSM + N_SSM_GROUPS * D_STATE]
    cm = xa[:, D_SSM + N_SSM_GROUPS * D_STATE:]

    dt = _softplus(dt_ref[...] + dtb_ref[...])
    a = -jnp.exp(alog_ref[...])
    acum = _cumsum_rows(dt * a)
    last = acum[n_tok - 1:n_tok, :]
    e_acum = jnp.exp(acum)
    w_end = dt * jnp.exp(last - acum)

    rows = lax.broadcasted_iota(jnp.int32, (n_tok, LANES), 0)
    lane = lax.broadcasted_iota(jnp.int32, (n_tok, LANES), 1)
    lane_group = lane // HEADS_PER_SSM_GROUP
    pieces = []
    for s in range(n_tok):
        prod = cm * bm[s:s + 1, :]
        cb_s = jnp.zeros((n_tok, LANES), F32)
        for g in range(N_SSM_GROUPS):
            r = jnp.sum(prod[:, g * D_STATE:(g + 1) * D_STATE], axis=1, keepdims=True)
            cb_s = jnp.where(lane_group == g, r, cb_s)
        g_s = cb_s * jnp.exp(acum - acum[s:s + 1, :]) * dt[s:s + 1, :]
        pieces.append(jnp.where(rows >= s, g_s, 0.0))
    expanded = _expand_heads(jnp.concatenate(pieces + [e_acum, w_end], axis=0), e_ref)
    y_diag = jnp.zeros((n_tok, D_SSM), F32)
    for s in range(n_tok):
        y_diag = y_diag + expanded[s * n_tok:(s + 1) * n_tok, :] * xs[s:s + 1, :]
    e_acum_x = expanded[n_tok * n_tok:n_tok * (n_tok + 1), :]
    w_end_x = expanded[n_tok * (n_tok + 1):n_tok * (n_tok + 2), :]

    h0 = h0_ref[0]
    c_stack = jnp.concatenate([cm[:, g * D_STATE:(g + 1) * D_STATE] for g in range(N_SSM_GROUPS)],
                              axis=0).astype(BF16)
    y_all = _dot_nt(c_stack, h0.astype(BF16))
    col_group = lax.broadcasted_iota(jnp.int32, (n_tok, D_SSM), 1) // D_SSM_GROUP
    y_off = jnp.zeros((n_tok, D_SSM), F32)
    for g in range(N_SSM_GROUPS):
        y_off = jnp.where(col_group == g, y_all[g * n_tok:(g + 1) * n_tok, :], y_off)
    y = y_diag + y_off * e_acum_x + dsk_ref[...] * xs
    y = y * _silu(z_ref[...])
    sq = y * y
    scale = jnp.zeros((n_tok, D_SSM), F32)
    for g in range(N_SSM_GROUPS):
        ms = jnp.sum(jnp.where(col_group == g, sq, 0.0), axis=1, keepdims=True) / D_SSM_GROUP
        scale = jnp.where(col_group == g, lax.rsqrt(ms + RMS_EPS), scale)
    y_ref[...] = y * scale * nrm_ref[...]

    xw_scr[0:n_tok, :] = xs * w_end_x
    b_scr[0:n_tok, :] = bm
    upd = _dot_tn(xw_scr[...].astype(BF16), b_scr[...].astype(BF16))
    lane1 = lax.broadcasted_iota(jnp.int32, (1, LANES), 1)
    for h in range(N_SSM_HEADS):
        g = h // HEADS_PER_SSM_GROUP
        r0, r1 = h * SSM_HEAD_DIM, (h + 1) * SSM_HEAD_DIM
        decay = jnp.exp(jnp.sum(jnp.where(lane1 == h, last, 0.0), axis=1, keepdims=True))
        hout_ref[0, r0:r1, :] = h0[r0:r1, :] * decay + upd[r0:r1, g * D_STATE:(g + 1) * D_STATE]


def _ssd_sample(xbc, conv_state, z, dt, h0, cw, cb, dtb, alog, dsk, nrm, expand, *, n_tok):
    t = xbc.shape[0]
    n_b = t // n_tok
    row = lambda b: (b, 0)
    const = lambda b: (0, 0)
    vec = lambda n: pl.BlockSpec((1, n), const)
    return pl.pallas_call(
        functools.partial(_ssd_sample_body, n_tok=n_tok),
        grid=(n_b,),
        in_specs=[pl.BlockSpec((n_tok, CONV_DIM), row),
                  pl.BlockSpec((1, CONV_W - 1, CONV_DIM), lambda b: (b, 0, 0)),
                  pl.BlockSpec((n_tok, D_SSM), row), pl.BlockSpec((n_tok, DT_PAD), row),
                  pl.BlockSpec((1, D_SSM, D_STATE), lambda b: (b, 0, 0)),
                  pl.BlockSpec((CONV_W, CONV_DIM), const),
                  vec(CONV_DIM), vec(DT_PAD), vec(DT_PAD), vec(D_SSM), vec(D_SSM),
                  pl.BlockSpec((LANES, D_SSM), const)],
        out_specs=[pl.BlockSpec((n_tok, D_SSM), row),
                   pl.BlockSpec((1, D_SSM, D_STATE), lambda b: (b, 0, 0))],
        out_shape=[jax.ShapeDtypeStruct((t, D_SSM), F32),
                   jax.ShapeDtypeStruct((n_b, D_SSM, D_STATE), F32)],
        scratch_shapes=[pltpu.VMEM((2 * SUBLANES, CONV_DIM), F32),
                        pltpu.VMEM((LANES, D_SSM), F32),
                        pltpu.VMEM((LANES, N_SSM_GROUPS * D_STATE), F32)],
        compiler_params=_params(1),
        name="ssd_sample",
    )(xbc, conv_state, z, dt, h0, cw, cb, dtb, alog, dsk, nrm, expand)


def _attn_prompt_body(slope_ref, q_ref, kp_ref, kc_ref, vp_ref, vc_ref, o_ref, l_ref, *, dil, group):
    i = pl.program_id(1)
    nb = ATT_BLOCK
    q = q_ref[...]
    k = jnp.concatenate([kp_ref[...], kc_ref[...]], axis=0).astype(BF16)
    v = jnp.concatenate([vp_ref[...], vc_ref[...]], axis=0).astype(BF16)
    qi = lax.broadcasted_iota(jnp.int32, (nb, 2 * nb), 0)
    kj = lax.broadcasted_iota(jnp.int32, (nb, 2 * nb), 1)
    dist = nb + qi - kj
    first_key = jnp.where(i > 0, 0, nb)
    valid = (dist >= 0) & (dist <= nb) & (kj >= first_key)
    dist_f = (dist * dil).astype(F32)
    lane_head = lax.broadcasted_iota(jnp.int32, (nb, D_GROUP), 1) // ATT_HEAD_DIM
    out = jnp.zeros((nb, D_GROUP), F32)
    lse = jnp.zeros((nb, D_GROUP), F32)
    for h in range(H_PER_DIL):
        mine = lane_head == h
        qh = jnp.where(mine, q, 0.0).astype(BF16)
        s = _dot_nt(qh, k) * (ATT_HEAD_DIM ** -0.5)
        s = jnp.where(valid, s - slope_ref[group * H_PER_DIL + h] * dist_f, -jnp.inf)
        m = jnp.max(s, axis=1, keepdims=True)
        p = jnp.exp(s - m)
        l = jnp.sum(p, axis=1, keepdims=True)
        pv = _dot(p.astype(BF16), v)
        out = jnp.where(mine, pv / l, out)
        lse = jnp.where(mine, m + jnp.log(l), lse)
    o_ref[...] = out
    l_ref[...] = lse


def _attn_prompt(slopes, qkv, *, group, dil, seq):
    t = qkv.shape[0]
    n_col_blocks = 3 * D_ATT // D_GROUP
    view = qkv.reshape(t // dil, dil * 3 * D_ATT)
    n_blk = seq // (dil * ATT_BLOCK)
    q_blk = lambda r, i: (i, r * n_col_blocks + group)
    k_prev = lambda r, i: (jnp.maximum(i - 1, 0), r * n_col_blocks + N_DIL + group)
    k_cur = lambda r, i: (i, r * n_col_blocks + N_DIL + group)
    v_prev = lambda r, i: (jnp.maximum(i - 1, 0), r * n_col_blocks + 2 * N_DIL + group)
    v_cur = lambda r, i: (i, r * n_col_blocks + 2 * N_DIL + group)
    blk = (ATT_BLOCK, D_GROUP)
    out_blk = pl.BlockSpec(blk, lambda r, i: (i, r))
    o, lse = pl.pallas_call(
        functools.partial(_attn_prompt_body, dil=dil, group=group),
        grid=(dil, n_blk),
        in_specs=[pl.BlockSpec(memory_space=pltpu.SMEM),
                  pl.BlockSpec(blk, q_blk), pl.BlockSpec(blk, k_prev), pl.BlockSpec(blk, k_cur),
                  pl.BlockSpec(blk, v_prev), pl.BlockSpec(blk, v_cur)],
        out_specs=[out_blk, out_blk],
        out_shape=[jax.ShapeDtypeStruct((seq // dil, dil * D_GROUP), F32)] * 2,
        compiler_params=_params(2),
        name=f"attn_prompt_g{group}",
    )(slopes, view, view, view, view, view)
    return o.reshape(seq, D_GROUP), lse.reshape(seq, D_GROUP)


def _attn_sample_body(slope_ref, qkv_ref, k0_ref, v0_ref, k1_ref, v1_ref, k2_ref, v2_ref, att_ref,
                      *, n_tok):
    n_rows = H_PER_DIL * n_tok
    qkv = qkv_ref[...]
    kv_refs = ((k0_ref, v0_ref), (k1_ref, v1_ref), (k2_ref, v2_ref))
    row2 = lax.broadcasted_iota(jnp.int32, (n_rows, D_GROUP), 0)
    lane2 = lax.broadcasted_iota(jnp.int32, (n_rows, D_GROUP), 1)
    head_row2 = row2 // n_tok
    tok_row2 = row2 % n_tok
    own_head = (lane2 // ATT_HEAD_DIM) == head_row2
    key_j = lax.broadcasted_iota(jnp.int32, (n_rows, ATT_BLOCK), 1)
    row_s = lax.broadcasted_iota(jnp.int32, (n_rows, ATT_BLOCK), 0)
    tok_s = row_s % n_tok
    head_s = row_s // n_tok
    row1 = lax.broadcasted_iota(jnp.int32, (n_rows, 1), 0)
    tok1 = row1 % n_tok
    head1 = row1 // n_tok
    lane_head8 = lax.broadcasted_iota(jnp.int32, (n_tok, D_GROUP), 1) // ATT_HEAD_DIM
    scale = ATT_HEAD_DIM ** -0.5

    outs, lses = [], []
    for g, (win, dil) in enumerate(DILATION_PAIRS):
        n_res = min(dil, n_tok)
        k_ref, v_ref = kv_refs[g]
        q = qkv[:, g * D_GROUP:(g + 1) * D_GROUP]
        k_new = qkv[:, D_ATT + g * D_GROUP:D_ATT + (g + 1) * D_GROUP]
        v_new = qkv[:, 2 * D_ATT + g * D_GROUP:2 * D_ATT + (g + 1) * D_GROUP]
        q_rows = jnp.where(own_head, jnp.concatenate([q] * H_PER_DIL, axis=0), 0.0)
        q_blk = jnp.concatenate(
            [jnp.where((tok_row2 % dil) == rho, q_rows, 0.0) for rho in range(n_res)], axis=1)
        slope_s = jnp.zeros((n_rows, ATT_BLOCK), F32)
        slope_1 = jnp.zeros((n_rows, 1), F32)
        for h in range(H_PER_DIL):
            slope_s = jnp.where(head_s == h, slope_ref[g * H_PER_DIL + h], slope_s)
            slope_1 = jnp.where(head1 == h, slope_ref[g * H_PER_DIL + h], slope_1)

        s_buf = _dot_nt(q_blk.astype(BF16), k_ref[0].astype(BF16)) * scale
        steps = ATT_BLOCK + tok_s // dil - key_j
        s_buf = jnp.where(steps <= ATT_BLOCK, s_buf - slope_s * (steps * dil).astype(F32), -jnp.inf)
        m = jnp.max(s_buf, axis=1, keepdims=True)
        s_new = []
        for tp in range(n_tok):
            sn = jnp.sum(q_rows * k_new[tp:tp + 1, :], axis=1, keepdims=True) * scale
            ok = (tok1 >= tp) & (((tok1 - tp) % dil) == 0)
            sn = jnp.where(ok, sn - slope_1 * (tok1 - tp).astype(F32), -jnp.inf)
            s_new.append(sn)
            m = jnp.maximum(m, sn)
        p_buf = jnp.exp(s_buf - m)
        l = jnp.sum(p_buf, axis=1, keepdims=True)
        o_all = _dot(p_buf.astype(BF16), v_ref[0].astype(BF16))
        o_rows = jnp.zeros((n_rows, D_GROUP), F32)
        for rho in range(n_res):
            o_rows = jnp.where((tok_row2 % dil) == rho, o_all[:, rho * D_GROUP:(rho + 1) * D_GROUP], o_rows)
        for tp in range(n_tok):
            p_new = jnp.exp(s_new[tp] - m)
            l = l + p_new
            o_rows = o_rows + p_new * v_new[tp:tp + 1, :]
        o_rows = o_rows / l
        lse_rows = m + jnp.log(l)
        o = jnp.zeros((n_tok, D_GROUP), F32)
        lse = jnp.zeros((n_tok, D_GROUP), F32)
        for h in range(H_PER_DIL):
            o = jnp.where(lane_head8 == h, o_rows[h * n_tok:(h + 1) * n_tok, :], o)
            lse = jnp.where(lane_head8 == h, lse_rows[h * n_tok:(h + 1) * n_tok, :], lse)
        outs.append(o)
        lses.append(lse)

    m = jnp.maximum(jnp.maximum(lses[0], lses[1]), lses[2])
    e = [jnp.exp(x - m) for x in lses]
    den = e[0] + e[1] + e[2]
    for g in range(N_DIL):
        att_ref[:, g * D_GROUP:(g + 1) * D_GROUP] = outs[g] * (e[g] / den)


def _attn_sample(slopes, qkv, caches, *, n_tok):
    t = qkv.shape[0]
    n_b = t // n_tok
    in_specs = [pl.BlockSpec(memory_space=pltpu.SMEM),
                pl.BlockSpec((n_tok, 3 * D_ATT), lambda b: (b, 0))]
    args = [slopes, qkv]
    for g, (win, dil) in enumerate(DILATION_PAIRS):
        n_res = min(dil, n_tok)
        for cache in caches[g]:
            assert cache.shape[1] == win, "window buffers must hold a full window"
            args.append(cache.reshape(n_b, ATT_BLOCK, dil * D_GROUP))
            in_specs.append(pl.BlockSpec((1, ATT_BLOCK, n_res * D_GROUP), lambda b: (b, 0, 0)))
    return pl.pallas_call(
        functools.partial(_attn_sample_body, n_tok=n_tok),
        grid=(n_b,),
        in_specs=in_specs,
        out_specs=pl.BlockSpec((n_tok, D_ATT), lambda b: (b, 0)),
        out_shape=jax.ShapeDtypeStruct((t, D_ATT), F32),
        compiler_params=_params(1),
        name="attn_sample",
    )(*args)


def _outproj_body(*refs, combine):
    if combine:
        x_ref, ssm_ref, o0, o1, o2, l0, l1, l2, w_ref, out_ref = refs
        lses = [l0[...], l1[...], l2[...]]
        m = jnp.maximum(jnp.maximum(lses[0], lses[1]), lses[2])
        e = [jnp.exp(x - m) for x in lses]
        den = e[0] + e[1] + e[2]
        att = [o[...] * (w / den) for o, w in zip((o0, o1, o2), e)]
    else:
        x_ref, ssm_ref, att_ref, w_ref, out_ref = refs
        att = [att_ref[:, g * D_GROUP:(g + 1) * D_GROUP] for g in range(N_DIL)]
    acc = _dot(ssm_ref[...].astype(BF16), w_ref[0:D_SSM, :])
    for g in range(N_DIL):
        r0 = D_SSM + g * D_GROUP
        acc = acc + _dot(att[g].astype(BF16), w_ref[r0:r0 + D_GROUP, :])
    out_ref[...] = x_ref[...] + acc


def _outproj(x, ssm, att, w_out, *, tm=256):
    t = x.shape[0]
    row = lambda i: (i, 0)
    combine = isinstance(att, (tuple, list))
    att_args = list(att[0]) + list(att[1]) if combine else [att]
    att_specs = [pl.BlockSpec((tm, a.shape[1]), row) for a in att_args]
    return pl.pallas_call(
        functools.partial(_outproj_body, combine=combine),
        grid=(t // tm,),
        in_specs=[pl.BlockSpec((tm, D_MODEL), row), pl.BlockSpec((tm, D_SSM), row)] + att_specs
                 + [pl.BlockSpec((D_MODEL, D_MODEL), lambda i: (0, 0), pipeline_mode=pl.Buffered(1))],
        out_specs=pl.BlockSpec((tm, D_MODEL), row),
        out_shape=jax.ShapeDtypeStruct((t, D_MODEL), F32),
        compiler_params=_params(1),
        name="outproj_prompt" if combine else "outproj_sample",
    )(x, ssm, *att_args, w_out)


def kernel(x_prompt, x_sample, cache_k0, cache_v0, cache_k1, cache_v1, cache_k2, cache_v2, state_ssm, state_conv, ffn1_norm, ffn1_w_gate, ffn1_w_up, ffn1_w_down, mix_norm, w_in, conv_w, conv_b, dt_bias, a_log, d_skip, ssm_norm, w_out, ffn2_norm, ffn2_w_gate, ffn2_w_up, ffn2_w_down, final_norm):
    assert w_in.shape[0] == 1, "single layer"
    batch, seq, _ = x_prompt.shape
    dec_batch, dec_seq, _ = x_sample.shape
    assert batch == 1 and dec_seq == SUBLANES

    row = lambda v: v.reshape(1, -1).astype(F32)
    pad_heads = lambda v: jnp.pad(v.reshape(1, -1).astype(F32), ((0, 0), (0, DT_PAD - N_SSM_HEADS)))
    w = w_in[0]
    c_dt = D_SSM + CONV_DIM
    c_q = c_dt + N_SSM_HEADS
    w_all = jnp.concatenate(
        [w[:, :c_dt], w[:, c_q:], jnp.pad(w[:, c_dt:c_q], ((0, 0), (0, DT_PAD - N_SSM_HEADS)))],
        axis=1).astype(BF16)
    ffn1 = (row(ffn1_norm[0]), ffn1_w_gate[0].astype(BF16), ffn1_w_up[0].astype(BF16),
            ffn1_w_down[0].astype(BF16))
    ffn2 = (row(ffn2_norm[0]), ffn2_w_gate[0].astype(BF16), ffn2_w_up[0].astype(BF16),
            ffn2_w_down[0].astype(BF16))
    w_o = w_out[0].astype(BF16)
    ssd_params = (conv_w[0].astype(F32), row(conv_b[0]), pad_heads(dt_bias[0]), pad_heads(a_log[0]),
                  row(jnp.repeat(d_skip[0], SSM_HEAD_DIM)), row(ssm_norm[0]))
    slopes = jnp.exp2(-ALIBI_MAX_EXP * jnp.arange(1, N_ATT_HEADS + 1, dtype=F32) / N_ATT_HEADS)
    expand = jnp.asarray(np.arange(D_SSM)[None, :] // SSM_HEAD_DIM == np.arange(LANES)[:, None], BF16)

    def front(x):
        x1 = _ffn(x, *ffn1)
        z, xbc, qkv, dt = _inproj(x1, row(mix_norm[0]), w_all)
        return x1, z, xbc, qkv, dt

    def back(x2):
        return _ffn(x2, *ffn2, row(final_norm))

    x1, z, xbc_p, qkv_p, dt = front(x_prompt.reshape(seq, D_MODEL))
    ssm_p, hfin_p = _ssd_prompt(xbc_p, z, dt, *ssd_params)
    outs, lses = [], []
    for g, (win, dil) in enumerate(DILATION_PAIRS):
        o, lse = _attn_prompt(slopes, qkv_p, group=g, dil=dil, seq=seq)
        outs.append(o)
        lses.append(lse)
    y_prompt = back(_outproj(x1, ssm_p, (outs, lses), w_o)).reshape(1, seq, D_MODEL)

    n_s = dec_batch * dec_seq
    x1, z, xbc_s, qkv_s, dt = front(x_sample.reshape(n_s, D_MODEL))
    ssm_s, h_s = _ssd_sample(xbc_s, state_conv[0], z, dt, state_ssm[0].reshape(dec_batch, D_SSM, D_STATE),
                             *ssd_params, expand, n_tok=dec_seq)
    caches = ((cache_k0[0], cache_v0[0]), (cache_k1[0], cache_v1[0]), (cache_k2[0], cache_v2[0]))
    att_s = _attn_sample(slopes, qkv_s, caches, n_tok=dec_seq)
    y_sample = back(_outproj(x1, ssm_s, att_s, w_o)).reshape(dec_batch, dec_seq, D_MODEL)

    head_shape = (H_PER_DIL, ATT_HEAD_DIM)
    p_kv, s_kv = [], []
    for g, (win, dil) in enumerate(DILATION_PAIRS):
        keep = min(win, seq)
        for base in (D_ATT, 2 * D_ATT):
            c0 = base + g * D_GROUP
            p_kv.append(qkv_p[seq - keep:, c0:c0 + D_GROUP].reshape((1, 1, keep) + head_shape))
            s_kv.append(qkv_s[:, c0:c0 + D_GROUP].reshape((1, dec_batch, dec_seq) + head_shape))
    p_ssm = hfin_p.reshape(1, 1, N_SSM_HEADS, SSM_HEAD_DIM, D_STATE)
    p_conv = xbc_p[seq - (CONV_W - 1):].reshape(1, 1, CONV_W - 1, CONV_DIM)
    s_ssm = h_s.reshape(1, dec_batch, N_SSM_HEADS, SSM_HEAD_DIM, D_STATE)
    s_conv = xbc_s.reshape(dec_batch, dec_seq, CONV_DIM)[:, dec_seq - (CONV_W - 1):].reshape(
        1, dec_batch, CONV_W - 1, CONV_DIM)
    return (y_prompt, y_sample, *p_kv, p_ssm, p_conv, *s_kv, s_ssm, s_conv)
```

```python
import functools

import numpy as np
import jax
import jax.numpy as jnp
from jax import lax
from jax.experimental import pallas as pl
from jax.experimental.pallas import tpu as pltpu

F32 = jnp.float32
BF16 = jnp.bfloat16

D_MODEL = 2048
D_FF = 5632
D_ATT = 768
ATT_HEAD_DIM = 64
H_PER_DIL = 4
D_GROUP = H_PER_DIL * ATT_HEAD_DIM
DILATION_PAIRS = ((128, 1), (512, 4), (2048, 16))
N_DIL = len(DILATION_PAIRS)
ATT_BLOCK = 128
D_SSM = 1280
SSM_HEAD_DIM = 64
N_SSM_HEADS = 20
N_SSM_GROUPS = 4
HEADS_PER_SSM_GROUP = 5
D_SSM_GROUP = D_SSM // N_SSM_GROUPS
D_STATE = 128
CONV_W = 4
CONV_DIM = 2304
SSD_CHUNK = 128
RMS_EPS = 1e-6
ALIBI_MAX_EXP = 8.0
N_ATT_HEADS = 12

LANES = 128
SUBLANES = 8
DT_PAD = LANES
IN_Z0, IN_XBC0, IN_QKV0, IN_DT0 = 0, D_SSM, D_SSM + CONV_DIM, D_SSM + CONV_DIM + 3 * D_ATT
IN_COLS = IN_DT0 + DT_PAD

VMEM_LIMIT = 56 * 1024 * 1024


def _params(n_grid_dims, vmem=VMEM_LIMIT):
    return pltpu.CompilerParams(dimension_semantics=("arbitrary",) * n_grid_dims,
                                vmem_limit_bytes=vmem)


def _sigmoid(x):
    return 1.0 / (1.0 + jnp.exp(-x))


def _silu(x):
    return x * _sigmoid(x)


def _softplus(x):
    return jnp.maximum(x, 0.0) + jnp.log(1.0 + jnp.exp(-jnp.abs(x)))


def _rms_scale(x):
    return x * lax.rsqrt(jnp.mean(x * x, axis=-1, keepdims=True) + RMS_EPS)


def _dot(a, b):
    return jnp.dot(a, b, preferred_element_type=F32)


def _dot_nt(a, b):
    return lax.dot_general(a, b, (((1,), (1,)), ((), ())), preferred_element_type=F32)


def _dot_tn(a, b):
    return lax.dot_general(a, b, (((0,), (0,)), ((), ())), preferred_element_type=F32)


def _cumsum_rows(x):
    n = x.shape[0]
    rows = lax.broadcasted_iota(jnp.int32, x.shape, 0)
    shift = 1
    while shift < n:
        x = x + jnp.where(rows >= shift, pltpu.roll(x, shift, 0), 0.0)
        shift *= 2
    return x


def _ffn_body(*refs, n_f, final):
    if final:
        x_ref, g_ref, wg_ref, wu_ref, wd_ref, fg_ref, o_ref, h_ref = refs
    else:
        x_ref, g_ref, wg_ref, wu_ref, wd_ref, o_ref, h_ref = refs
    f = pl.program_id(1)

    @pl.when(f == 0)
    def _():
        h_ref[...] = (_rms_scale(x_ref[...]) * g_ref[...]).astype(BF16)
        o_ref[...] = jnp.zeros_like(o_ref)

    h = h_ref[...]
    gate = _dot(h, wg_ref[...])
    up = _dot(h, wu_ref[...])
    act = (_silu(gate) * up).astype(BF16)
    o_ref[...] += _dot(act, wd_ref[...])

    @pl.when(f == n_f - 1)
    def _():
        y = x_ref[...] + 0.5 * o_ref[...]
        if final:
            y = _rms_scale(y) * fg_ref[...]
        o_ref[...] = y


def _ffn(x, gain, wg, wu, wd, final_gain=None, *, tm=512, tf=512):
    t = x.shape[0]
    n_f = D_FF // tf
    final = final_gain is not None
    row = lambda i, f: (i, 0)
    const = lambda i, f: (0, 0)
    in_specs = [pl.BlockSpec((tm, D_MODEL), row), pl.BlockSpec((1, D_MODEL), const),
                pl.BlockSpec((D_MODEL, tf), lambda i, f: (0, f)),
                pl.BlockSpec((D_MODEL, tf), lambda i, f: (0, f)),
                pl.BlockSpec((tf, D_MODEL), lambda i, f: (f, 0))]
    args = [x, gain, wg, wu, wd]
    if final:
        in_specs.append(pl.BlockSpec((1, D_MODEL), const))
        args.append(final_gain)
    return pl.pallas_call(
        functools.partial(_ffn_body, n_f=n_f, final=final),
        grid=(t // tm, n_f),
        in_specs=in_specs,
        out_specs=pl.BlockSpec((tm, D_MODEL), row),
        out_shape=jax.ShapeDtypeStruct((t, D_MODEL), F32),
        scratch_shapes=[pltpu.VMEM((tm, D_MODEL), BF16)],
        compiler_params=_params(2),
        name="ffn_final" if final else "ffn",
    )(*args)


def _inproj_body(x_ref, g_ref, w_ref, z_ref, xbc_ref, qkv_ref, dt_ref):
    h = (_rms_scale(x_ref[...]) * g_ref[...]).astype(BF16)
    z_ref[...] = _dot(h, w_ref[:, IN_Z0:IN_XBC0])
    xbc_ref[...] = _dot(h, w_ref[:, IN_XBC0:IN_QKV0])
    qkv_ref[...] = _dot(h, w_ref[:, IN_QKV0:IN_DT0])
    dt_ref[...] = _dot(h, w_ref[:, IN_DT0:IN_COLS])


def _inproj(x, gain, w_all, *, tm=256):
    t = x.shape[0]
    row = lambda i: (i, 0)
    const = lambda i: (0, 0)
    widths = (D_SSM, CONV_DIM, 3 * D_ATT, DT_PAD)
    return pl.pallas_call(
        _inproj_body,
        grid=(t // tm,),
        in_specs=[pl.BlockSpec((tm, D_MODEL), row), pl.BlockSpec((1, D_MODEL), const),
                  pl.BlockSpec((D_MODEL, IN_COLS), const, pipeline_mode=pl.Buffered(1))],
        out_specs=[pl.BlockSpec((tm, w), row) for w in widths],
        out_shape=[jax.ShapeDtypeStruct((t, w), F32) for w in widths],
        compiler_params=_params(1),
        name="inproj",
    )(x, gain, w_all)


def _pair_blockdiag(x, low_half):
    zero = jnp.zeros_like(x)
    return jnp.concatenate([jnp.where(low_half, x, zero), jnp.where(low_half, zero, x)],
                           axis=0).astype(BF16)


def _ssd_prompt_body(xbc_ref, z_ref, dt_ref, cw_ref, cb_ref, dtb_ref, alog_ref, dsk_ref, nrm_ref,
                     y_ref, hfin_ref, xf_scr, ht_scr):
    c = pl.program_id(0)
    n_c = pl.num_programs(0)
    cl = SSD_CHUNK

    @pl.when(c == 0)
    def _():
        xf_scr[0:SUBLANES, :] = jnp.zeros((SUBLANES, CONV_DIM), F32)
        ht_scr[...] = jnp.zeros_like(ht_scr)

    x = xbc_ref[...]
    xf_scr[SUBLANES:SUBLANES + cl, :] = x
    conv = cb_ref[...]
    for j in range(CONV_W - 1):
        conv = conv + xf_scr[SUBLANES - 3 + j:SUBLANES - 3 + j + cl, :] * cw_ref[j:j + 1, :]
    conv = conv + x * cw_ref[CONV_W - 1:CONV_W, :]
    xf_scr[0:SUBLANES, :] = x[cl - SUBLANES:cl, :]
    xa = _silu(conv)
    xs = xa[:, :D_SSM]
    bm = xa[:, D_SSM:D_SSM + N_SSM_GROUPS * D_STATE]
    cm = xa[:, D_SSM + N_SSM_GROUPS * D_STATE:]

    dt = _softplus(dt_ref[...] + dtb_ref[...])
    a = -jnp.exp(alog_ref[...])
    acum = _cumsum_rows(dt * a)
    w_end = dt * jnp.exp(acum[cl - 1:cl, :] - acum)
    acum_t = acum.T
    dt_t = dt.T
    w_end_t = w_end.T

    rows = lax.broadcasted_iota(jnp.int32, (cl, cl), 0)
    cols = lax.broadcasted_iota(jnp.int32, (cl, cl), 1)
    causal = rows >= cols
    low_half = cols < SSM_HEAD_DIM

    b_g = [bm[:, g * D_STATE:(g + 1) * D_STATE] for g in range(N_SSM_GROUPS)]
    c_g = [cm[:, g * D_STATE:(g + 1) * D_STATE].astype(BF16) for g in range(N_SSM_GROUPS)]
    b_t = [b.T for b in b_g]
    cb = [_dot_nt(c_g[g], b_g[g].astype(BF16)) for g in range(N_SSM_GROUPS)]

    sumsq = [jnp.zeros((cl, 1), F32) for _ in range(N_SSM_GROUPS)]
    for j in range(N_SSM_HEADS // 2):
        lo, hi = j * LANES, (j + 1) * LANES
        heads = (2 * j, 2 * j + 1)
        groups = tuple(h // HEADS_PER_SSM_GROUP for h in heads)
        xs_p = xs[:, lo:hi]
        x_bd = _pair_blockdiag(xs_p, low_half)
        ht_p = ht_scr[:, lo:hi]
        h_bd = _pair_blockdiag(ht_p, low_half)
        g_mats, w_mats, e_cols = [], [], []
        for h, g in zip(heads, groups):
            row_b = jnp.broadcast_to(acum_t[h:h + 1, :], (cl, cl))
            col_b = row_b.T
            decay = jnp.where(causal, jnp.exp(col_b - row_b), 0.0)
            g_mats.append((cb[g] * decay * dt_t[h:h + 1, :]).astype(BF16))
            w_mats.append((b_t[g] * w_end_t[h:h + 1, :]).astype(BF16))
            e_cols.append(jnp.exp(col_b))
        y_diag = _dot(jnp.concatenate(g_mats, axis=1), x_bd)
        y_off = _dot(jnp.concatenate([c_g[groups[0]], c_g[groups[1]]], axis=1), h_bd)
        y_off = y_off * jnp.where(low_half, e_cols[0], e_cols[1])
        state = _dot(jnp.concatenate(w_mats, axis=1), x_bd)
        chunk_decay = jnp.where(low_half[0:1, :], e_cols[0][cl - 1:cl, :], e_cols[1][cl - 1:cl, :])
        ht_scr[:, lo:hi] = ht_p * chunk_decay + state
        y_p = y_diag + y_off + dsk_ref[:, lo:hi] * xs_p
        y_p = y_p * _silu(z_ref[:, lo:hi])
        y_ref[:, lo:hi] = y_p
        sq = y_p * y_p
        sumsq[groups[0]] = sumsq[groups[0]] + jnp.sum(jnp.where(low_half, sq, 0.0), axis=1, keepdims=True)
        sumsq[groups[1]] = sumsq[groups[1]] + jnp.sum(jnp.where(low_half, 0.0, sq), axis=1, keepdims=True)

    scale = [lax.rsqrt(s / D_SSM_GROUP + RMS_EPS) for s in sumsq]
    for j in range(N_SSM_HEADS // 2):
        lo, hi = j * LANES, (j + 1) * LANES
        g0, g1 = (2 * j) // HEADS_PER_SSM_GROUP, (2 * j + 1) // HEADS_PER_SSM_GROUP
        y_ref[:, lo:hi] = y_ref[:, lo:hi] * jnp.where(low_half, scale[g0], scale[g1]) * nrm_ref[:, lo:hi]

    @pl.when(c == n_c - 1)
    def _():
        for j in range(D_SSM // LANES):
            hfin_ref[j * LANES:(j + 1) * LANES, :] = ht_scr[:, j * LANES:(j + 1) * LANES].T


def _ssd_prompt(xbc, z, dt, cw, cb, dtb, alog, dsk, nrm):
    t = xbc.shape[0]
    cl = SSD_CHUNK
    row = lambda c: (c, 0)
    const = lambda c: (0, 0)
    vec = lambda n: pl.BlockSpec((1, n), const)
    return pl.pallas_call(
        _ssd_prompt_body,
        grid=(t // cl,),
        in_specs=[pl.BlockSpec((cl, CONV_DIM), row), pl.BlockSpec((cl, D_SSM), row),
                  pl.BlockSpec((cl, DT_PAD), row), pl.BlockSpec((CONV_W, CONV_DIM), const),
                  vec(CONV_DIM), vec(DT_PAD), vec(DT_PAD), vec(D_SSM), vec(D_SSM)],
        out_specs=[pl.BlockSpec((cl, D_SSM), row), pl.BlockSpec((D_SSM, D_STATE), const)],
        out_shape=[jax.ShapeDtypeStruct((t, D_SSM), F32),
                   jax.ShapeDtypeStruct((D_SSM, D_STATE), F32)],
        scratch_shapes=[pltpu.VMEM((SUBLANES + cl, CONV_DIM), F32),
                        pltpu.VMEM((D_STATE, D_SSM), F32)],
        compiler_params=_params(1),
        name="ssd_prompt",
    )(xbc, z, dt, cw, cb, dtb, alog, dsk, nrm)


def _expand_heads(x, e_ref):
    hi = x.astype(BF16)
    r1 = x - hi.astype(F32)
    mid = r1.astype(BF16)
    lo = (r1 - mid.astype(F32)).astype(BF16)
    e = e_ref[...]
    return _dot(hi, e) + _dot(mid, e) + _dot(lo, e)


def _ssd_sample_body(xbc_ref, sc_ref, z_ref, dt_ref, h0_ref, cw_ref, cb_ref, dtb_ref, alog_ref,
                     dsk_ref, nrm_ref, e_ref, y_ref, hout_ref, xf_scr, xw_scr, b_scr, *, n_tok):
    step = pl.program_id(0)

    @pl.when(step == 0)
    def _():
        xw_scr[...] = jnp.zeros_like(xw_scr)
        b_scr[...] = jnp.zeros_like(b_scr)

    x = xbc_ref[...]
    xf_scr[SUBLANES - 3:SUBLANES, :] = sc_ref[0]
    xf_scr[SUBLANES:SUBLANES + n_tok, :] = x
    conv = cb_ref[...]
    for j in range(CONV_W - 1):
        conv = conv + xf_scr[SUBLANES - 3 + j:SUBLANES - 3 + j + n_tok, :] * cw_ref[j:j + 1, :]
    conv = conv + x * cw_ref[CONV_W - 1:CONV_W, :]
    xa = _silu(conv)
    xs = xa[:, :D_SSM]
    bm = xa[:, D_SSM:D_SSM + N_SSM_GROUPS * D_STATE]
    cm = xa[:, D_SSM + N_SSM_GROUPS * D_STATE:]

    dt = _softplus(dt_ref[...] + dtb_ref[...])
    a = -jnp.exp(alog_ref[...])
    acum = _cumsum_rows(dt * a)
    last = acum[n_tok - 1:n_tok, :]
    e_acum = jnp.exp(acum)
    w_end = dt * jnp.exp(last - acum)

    rows = lax.broadcasted_iota(jnp.int32, (n_tok, LANES), 0)
    lane = lax.broadcasted_iota(jnp.int32, (n_tok, LANES), 1)
    lane_group = lane // HEADS_PER_SSM_GROUP
    pieces = []
    for s in range(n_tok):
        prod = cm * bm[s:s + 1, :]
        cb_s = jnp.zeros((n_tok, LANES), F32)
        for g in range(N_SSM_GROUPS):
            r = jnp.sum(prod[:, g * D_STATE:(g + 1) * D_STATE], axis=1, keepdims=True)
            cb_s = jnp.where(lane_group == g, r, cb_s)
        g_s = cb_s * jnp.exp(acum - acum[s:s + 1, :]) * dt[s:s + 1, :]
        pieces.append(jnp.where(rows >= s, g_s, 0.0))
    expanded = _expand_heads(jnp.concatenate(pieces + [e_acum, w_end], axis=0), e_ref)
    y_diag = jnp.zeros((n_tok, D_SSM), F32)
    for s in range(n_tok):
        y_diag = y_diag + expanded[s * n_tok:(s + 1) * n_tok, :] * xs[s:s + 1, :]
    e_acum_x = expanded[n_tok * n_tok:n_tok * (n_tok + 1), :]
    w_end_x = expanded[n_tok * (n_tok + 1):n_tok * (n_tok + 2), :]

    h0 = h0_ref[0]
    c_stack = jnp.concatenate([cm[:, g * D_STATE:(g + 1) * D_STATE] for g in range(N_SSM_GROUPS)],
                              axis=0).astype(BF16)
    y_all = _dot_nt(c_stack, h0.astype(BF16))
    col_group = lax.broadcasted_iota(jnp.int32, (n_tok, D_SSM), 1) // D_SSM_GROUP
    y_off = jnp.zeros((n_tok, D_SSM), F32)
    for g in range(N_SSM_GROUPS):
        y_off = jnp.where(col_group == g, y_all[g * n_tok:(g + 1) * n_tok, :], y_off)
    y = y_diag + y_off * e_acum_x + dsk_ref[...] * xs
    y = y * _silu(z_ref[...])
    sq = y * y
    scale = jnp.zeros((n_tok, D_SSM), F32)
    for g in range(N_SSM_GROUPS):
        ms = jnp.sum(jnp.where(col_group == g, sq, 0.0), axis=1, keepdims=True) / D_SSM_GROUP
        scale = jnp.where(col_group == g, lax.rsqrt(ms + RMS_EPS), scale)
    y_ref[...] = y * scale * nrm_ref[...]

    xw_scr[0:n_tok, :] = xs * w_end_x
    b_scr[0:n_tok, :] = bm
    upd = _dot_tn(xw_scr[...].astype(BF16), b_scr[...].astype(BF16))
    lane1 = lax.broadcasted_iota(jnp.int32, (1, LANES), 1)
    for h in range(N_SSM_HEADS):
        g = h // HEADS_PER_SSM_GROUP
        r0, r1 = h * SSM_HEAD_DIM, (h + 1) * SSM_HEAD_DIM
        decay = jnp.exp(jnp.sum(jnp.where(lane1 == h, last, 0.0), axis=1, keepdims=True))
        hout_ref[0, r0:r1, :] = h0[r0:r1, :] * decay + upd[r0:r1, g * D_STATE:(g + 1) * D_STATE]


def _ssd_sample(xbc, conv_state, z, dt, h0, cw, cb, dtb, alog, dsk, nrm, expand, *, n_tok):
    t = xbc.shape[0]
    n_b = t // n_tok
    row = lambda b: (b, 0)
    const = lambda b: (0, 0)
    vec = lambda n: pl.BlockSpec((1, n), const)
    return pl.pallas_call(
        functools.partial(_ssd_sample_body, n_tok=n_tok),
        grid=(n_b,),
        in_specs=[pl.BlockSpec((n_tok, CONV_DIM), row),
                  pl.BlockSpec((1, CONV_W - 1, CONV_DIM), lambda b: (b, 0, 0)),
                  pl.BlockSpec((n_tok, D_SSM), row), pl.BlockSpec((n_tok, DT_PAD), row),
                  pl.BlockSpec((1, D_SSM, D_STATE), lambda b: (b, 0, 0)),
                  pl.BlockSpec((CONV_W, CONV_DIM), const),
                  vec(CONV_DIM), vec(DT_PAD), vec(DT_PAD), vec(D_SSM), vec(D_SSM),
                  pl.BlockSpec((LANES, D_SSM), const)],
        out_specs=[pl.BlockSpec((n_tok, D_SSM), row),
                   pl.BlockSpec((1, D_SSM, D_STATE), lambda b: (b, 0, 0))],
        out_shape=[jax.ShapeDtypeStruct((t, D_SSM), F32),
                   jax.ShapeDtypeStruct((n_b, D_SSM, D_STATE), F32)],
        scratch_shapes=[pltpu.VMEM((2 * SUBLANES, CONV_DIM), F32),
                        pltpu.VMEM((LANES, D_SSM), F32),
                        pltpu.VMEM((LANES, N_SSM_GROUPS * D_STATE), F32)],
        compiler_params=_params(1),
        name="ssd_sample",
    )(xbc, conv_state, z, dt, h0, cw, cb, dtb, alog, dsk, nrm, expand)


HEADS_PER_HALF = LANES // ATT_HEAD_DIM
N_HALVES = D_GROUP // LANES


def _attn_prompt_body(slope_ref, *refs, dil, group):
    n_in = 5 * N_HALVES
    in_refs, o_refs, l_refs = refs[:n_in], refs[n_in:n_in + N_HALVES], refs[n_in + N_HALVES:]
    i = pl.program_id(0)
    nb = ATT_BLOCK
    rows = pl.ds(pl.program_id(1), nb, stride=dil) if dil > 1 else pl.ds(0, nb)
    qi = lax.broadcasted_iota(jnp.int32, (nb, 2 * nb), 0)
    kj = lax.broadcasted_iota(jnp.int32, (nb, 2 * nb), 1)
    dist = nb + qi - kj
    first_key = jnp.where(i > 0, 0, nb)
    valid = (dist >= 0) & (dist <= nb) & (kj >= first_key)
    dist_f = (dist * dil).astype(F32)
    lane_head = lax.broadcasted_iota(jnp.int32, (nb, LANES), 1) // ATT_HEAD_DIM
    for half in range(N_HALVES):
        q_ref, kp_ref, kc_ref, vp_ref, vc_ref = in_refs[5 * half:5 * half + 5]
        q = q_ref[rows, :]
        k = jnp.concatenate([kp_ref[rows, :], kc_ref[rows, :]], axis=0).astype(BF16)
        v = jnp.concatenate([vp_ref[rows, :], vc_ref[rows, :]], axis=0).astype(BF16)
        out = jnp.zeros((nb, LANES), F32)
        lse = jnp.zeros((nb, LANES), F32)
        for hh in range(HEADS_PER_HALF):
            mine = lane_head == hh
            slope = slope_ref[group * H_PER_DIL + half * HEADS_PER_HALF + hh]
            qh = jnp.where(mine, q, 0.0).astype(BF16)
            s = _dot_nt(qh, k) * (ATT_HEAD_DIM ** -0.5)
            s = jnp.where(valid, s - slope * dist_f, -jnp.inf)
            m = jnp.max(s, axis=1, keepdims=True)
            p = jnp.exp(s - m)
            l = jnp.sum(p, axis=1, keepdims=True)
            pv = _dot(p.astype(BF16), v)
            out = jnp.where(mine, pv / l, out)
            lse = jnp.where(mine, m + jnp.log(l), lse)
        o_refs[half][rows, :] = out
        l_refs[half][rows, :] = lse


def _attn_prompt(slopes, qkv, *, group, dil, seq):
    n_blk = seq // (dil * ATT_BLOCK)
    blk = (dil * ATT_BLOCK, LANES)
    cur = lambda c: pl.BlockSpec(blk, lambda i, r: (i, c))
    prev = lambda c: pl.BlockSpec(blk, lambda i, r: (jnp.maximum(i - 1, 0), c))
    in_specs = [pl.BlockSpec(memory_space=pltpu.SMEM)]
    for half in range(N_HALVES):
        q_col, k_col, v_col = (N_HALVES * (part * N_DIL + group) + half for part in range(3))
        in_specs += [cur(q_col), prev(k_col), cur(k_col), prev(v_col), cur(v_col)]
    out_blk = pl.BlockSpec(blk, lambda i, r: (i, 0))
    res = pl.pallas_call(
        functools.partial(_attn_prompt_body, dil=dil, group=group),
        grid=(n_blk, dil),
        in_specs=in_specs,
        out_specs=[out_blk] * (2 * N_HALVES),
        out_shape=[jax.ShapeDtypeStruct((seq, LANES), F32)] * (2 * N_HALVES),
        compiler_params=_params(2),
        name=f"attn_prompt_g{group}",
    )(slopes, *([qkv] * (5 * N_HALVES)))
    return res[:N_HALVES], res[N_HALVES:]


def _attn_sample_body(slope_ref, qkv_ref, k0_ref, v0_ref, k1_ref, v1_ref, k2_ref, v2_ref, att_ref,
                      *, n_tok):
    n_rows = H_PER_DIL * n_tok
    qkv = qkv_ref[...]
    kv_refs = ((k0_ref, v0_ref), (k1_ref, v1_ref), (k2_ref, v2_ref))
    row2 = lax.broadcasted_iota(jnp.int32, (n_rows, D_GROUP), 0)
    lane2 = lax.broadcasted_iota(jnp.int32, (n_rows, D_GROUP), 1)
    own_head = (lane2 // ATT_HEAD_DIM) == (row2 // n_tok)
    row1 = lax.broadcasted_iota(jnp.int32, (n_rows, 1), 0)
    tok1 = row1 % n_tok
    head1 = row1 // n_tok
    lane_head8 = lax.broadcasted_iota(jnp.int32, (n_tok, D_GROUP), 1) // ATT_HEAD_DIM
    scale = ATT_HEAD_DIM ** -0.5

    outs, lses = [], []
    for g, (win, dil) in enumerate(DILATION_PAIRS):
        k_ref, v_ref = kv_refs[g]
        q = qkv[:, g * D_GROUP:(g + 1) * D_GROUP]
        k_new = qkv[:, D_ATT + g * D_GROUP:D_ATT + (g + 1) * D_GROUP]
        v_new = qkv[:, 2 * D_ATT + g * D_GROUP:2 * D_ATT + (g + 1) * D_GROUP]
        q_rows = jnp.where(own_head, jnp.concatenate([q] * H_PER_DIL, axis=0), 0.0)
        slope_1 = jnp.zeros((n_rows, 1), F32)
        for h in range(H_PER_DIL):
            slope_1 = jnp.where(head1 == h, slope_ref[g * H_PER_DIL + h], slope_1)

        s_buf = _dot(q_rows.astype(BF16), k_ref[0].astype(BF16)) * scale
        pos = lax.broadcasted_iota(jnp.int32, (n_rows, win), 1)
        tok_s = lax.broadcasted_iota(jnp.int32, (n_rows, win), 0) % n_tok
        dist = win + tok_s - pos
        on_grid = (pos >= tok_s) & (jnp.bitwise_and(dist, dil - 1) == 0)
        s_buf = jnp.where(on_grid, s_buf - slope_1 * dist.astype(F32), -jnp.inf)
        m = jnp.max(s_buf, axis=1, keepdims=True)
        s_new = []
        for tp in range(n_tok):
            sn = jnp.sum(q_rows * k_new[tp:tp + 1, :], axis=1, keepdims=True) * scale
            ok = (tok1 >= tp) & (jnp.bitwise_and(tok1 - tp, dil - 1) == 0)
            sn = jnp.where(ok, sn - slope_1 * (tok1 - tp).astype(F32), -jnp.inf)
            s_new.append(sn)
            m = jnp.maximum(m, sn)
        p_buf = jnp.exp(s_buf - m)
        l = jnp.sum(p_buf, axis=1, keepdims=True)
        o_rows = _dot_nt(p_buf.astype(BF16), v_ref[0].astype(BF16))
        for tp in range(n_tok):
            p_new = jnp.exp(s_new[tp] - m)
            l = l + p_new
            o_rows = o_rows + p_new * v_new[tp:tp + 1, :]
        o_rows = o_rows / l
        lse_rows = m + jnp.log(l)
        o = jnp.zeros((n_tok, D_GROUP), F32)
        lse = jnp.zeros((n_tok, D_GROUP), F32)
        for h in range(H_PER_DIL):
            o = jnp.where(lane_head8 == h, o_rows[h * n_tok:(h + 1) * n_tok, :], o)
            lse = jnp.where(lane_head8 == h, lse_rows[h * n_tok:(h + 1) * n_tok, :], lse)
        outs.append(o)
        lses.append(lse)

    m = jnp.maximum(jnp.maximum(lses[0], lses[1]), lses[2])
    e = [jnp.exp(x - m) for x in lses]
    den = e[0] + e[1] + e[2]
    for g in range(N_DIL):
        att_ref[:, g * D_GROUP:(g + 1) * D_GROUP] = outs[g] * (e[g] / den)


def _attn_sample(slopes, qkv, caches, *, n_tok):
    t = qkv.shape[0]
    n_b = t // n_tok
    in_specs = [pl.BlockSpec(memory_space=pltpu.SMEM),
                pl.BlockSpec((n_tok, 3 * D_ATT), lambda b: (b, 0))]
    args = [slopes, qkv]
    for g, (win, dil) in enumerate(DILATION_PAIRS):
        assert dil & (dil - 1) == 0, "dilations must be powers of two"
        for cache in caches[g]:
            assert cache.shape[1] == win, "window buffers must hold a full window"
            args.append(jnp.transpose(cache, (0, 2, 3, 1)).reshape(n_b, D_GROUP, win))
            in_specs.append(pl.BlockSpec((1, D_GROUP, win), lambda b: (b, 0, 0)))
    return pl.pallas_call(
        functools.partial(_attn_sample_body, n_tok=n_tok),
        grid=(n_b,),
        in_specs=in_specs,
        out_specs=pl.BlockSpec((n_tok, D_ATT), lambda b: (b, 0)),
        out_shape=jax.ShapeDtypeStruct((t, D_ATT), F32),
        compiler_params=_params(1),
        name="attn_sample",
    )(*args)


def _outproj_body(*refs, combine):
    n_pieces = N_DIL * N_HALVES
    if combine:
        x_ref, ssm_ref = refs[:2]
        o_refs, l_refs = refs[2:2 + n_pieces], refs[2 + n_pieces:2 + 2 * n_pieces]
        w_ref, out_ref = refs[2 + 2 * n_pieces:]
        att = [None] * n_pieces
        for half in range(N_HALVES):
            idx = [g * N_HALVES + half for g in range(N_DIL)]
            lses = [l_refs[p][...] for p in idx]
            m = jnp.maximum(jnp.maximum(lses[0], lses[1]), lses[2])
            e = [jnp.exp(x - m) for x in lses]
            den = e[0] + e[1] + e[2]
            for p, w in zip(idx, e):
                att[p] = o_refs[p][...] * (w / den)
    else:
        x_ref, ssm_ref, att_ref, w_ref, out_ref = refs
        att = [att_ref[:, p * LANES:(p + 1) * LANES] for p in range(n_pieces)]
    mixed = jnp.concatenate([ssm_ref[...]] + att, axis=1).astype(BF16)
    out_ref[...] = x_ref[...] + _dot(mixed, w_ref[...])


def _outproj(x, ssm, att, w_out, *, tm=256):
    t = x.shape[0]
    row = lambda i: (i, 0)
    combine = isinstance(att, (tuple, list))
    att_args = list(att[0]) + list(att[1]) if combine else [att]
    att_specs = [pl.BlockSpec((tm, a.shape[1]), row) for a in att_args]
    return pl.pallas_call(
        functools.partial(_outproj_body, combine=combine),
        grid=(t // tm,),
        in_specs=[pl.BlockSpec((tm, D_MODEL), row), pl.BlockSpec((tm, D_SSM), row)] + att_specs
                 + [pl.BlockSpec((D_MODEL, D_MODEL), lambda i: (0, 0), pipeline_mode=pl.Buffered(1))],
        out_specs=pl.BlockSpec((tm, D_MODEL), row),
        out_shape=jax.ShapeDtypeStruct((t, D_MODEL), F32),
        compiler_params=_params(1),
        name="outproj_prompt" if combine else "outproj_sample",
    )(x, ssm, *att_args, w_out)


def kernel(x_prompt, x_sample, cache_k0, cache_v0, cache_k1, cache_v1, cache_k2, cache_v2, state_ssm, state_conv, ffn1_norm, ffn1_w_gate, ffn1_w_up, ffn1_w_down, mix_norm, w_in, conv_w, conv_b, dt_bias, a_log, d_skip, ssm_norm, w_out, ffn2_norm, ffn2_w_gate, ffn2_w_up, ffn2_w_down, final_norm):
    assert w_in.shape[0] == 1, "single layer"
    batch, seq, _ = x_prompt.shape
    dec_batch, dec_seq, _ = x_sample.shape
    assert batch == 1 and dec_seq == SUBLANES

    row = lambda v: v.reshape(1, -1).astype(F32)
    pad_heads = lambda v: jnp.pad(v.reshape(1, -1).astype(F32), ((0, 0), (0, DT_PAD - N_SSM_HEADS)))
    w = w_in[0]
    c_dt = D_SSM + CONV_DIM
    c_q = c_dt + N_SSM_HEADS
    w_all = jnp.concatenate(
        [w[:, :c_dt], w[:, c_q:], jnp.pad(w[:, c_dt:c_q], ((0, 0), (0, DT_PAD - N_SSM_HEADS)))],
        axis=1).astype(BF16)
    ffn1 = (row(ffn1_norm[0]), ffn1_w_gate[0].astype(BF16), ffn1_w_up[0].astype(BF16),
            ffn1_w_down[0].astype(BF16))
    ffn2 = (row(ffn2_norm[0]), ffn2_w_gate[0].astype(BF16), ffn2_w_up[0].astype(BF16),
            ffn2_w_down[0].astype(BF16))
    w_o = w_out[0].astype(BF16)
    ssd_params = (conv_w[0].astype(F32), row(conv_b[0]), pad_heads(dt_bias[0]), pad_heads(a_log[0]),
                  row(jnp.repeat(d_skip[0], SSM_HEAD_DIM)), row(ssm_norm[0]))
    slopes = jnp.exp2(-ALIBI_MAX_EXP * jnp.arange(1, N_ATT_HEADS + 1, dtype=F32) / N_ATT_HEADS)
    expand = jnp.asarray(np.arange(D_SSM)[None, :] // SSM_HEAD_DIM == np.arange(LANES)[:, None], BF16)

    def front(x):
        x1 = _ffn(x, *ffn1)
        z, xbc, qkv, dt = _inproj(x1, row(mix_norm[0]), w_all)
        return x1, z, xbc, qkv, dt

    def back(x2):
        return _ffn(x2, *ffn2, row(final_norm))

    x1, z, xbc_p, qkv_p, dt = front(x_prompt.reshape(seq, D_MODEL))
    ssm_p, hfin_p = _ssd_prompt(xbc_p, z, dt, *ssd_params)
    outs, lses = [], []
    for g, (win, dil) in enumerate(DILATION_PAIRS):
        o, lse = _attn_prompt(slopes, qkv_p, group=g, dil=dil, seq=seq)
        outs += o
        lses += lse
    y_prompt = back(_outproj(x1, ssm_p, (outs, lses), w_o)).reshape(1, seq, D_MODEL)

    n_s = dec_batch * dec_seq
    x1, z, xbc_s, qkv_s, dt = front(x_sample.reshape(n_s, D_MODEL))
    ssm_s, h_s = _ssd_sample(xbc_s, state_conv[0], z, dt, state_ssm[0].reshape(dec_batch, D_SSM, D_STATE),
                             *ssd_params, expand, n_tok=dec_seq)
    caches = ((cache_k0[0], cache_v0[0]), (cache_k1[0], cache_v1[0]), (cache_k2[0], cache_v2[0]))
    att_s = _attn_sample(slopes, qkv_s, caches, n_tok=dec_seq)
    y_sample = back(_outproj(x1, ssm_s, att_s, w_o)).reshape(dec_batch, dec_seq, D_MODEL)

    head_shape = (H_PER_DIL, ATT_HEAD_DIM)
    p_kv, s_kv = [], []
    for g, (win, dil) in enumerate(DILATION_PAIRS):
        keep = min(win, seq)
        for base in (D_ATT, 2 * D_ATT):
            c0 = base + g * D_GROUP
            p_kv.append(qkv_p[seq - keep:, c0:c0 + D_GROUP].reshape((1, 1, keep) + head_shape))
            s_kv.append(qkv_s[:, c0:c0 + D_GROUP].reshape((1, dec_batch, dec_seq) + head_shape))
    p_ssm = hfin_p.reshape(1, 1, N_SSM_HEADS, SSM_HEAD_DIM, D_STATE)
    p_conv = xbc_p[seq - (CONV_W - 1):].reshape(1, 1, CONV_W - 1, CONV_DIM)
    s_ssm = h_s.reshape(1, dec_batch, N_SSM_HEADS, SSM_HEAD_DIM, D_STATE)
    s_conv = xbc_s.reshape(dec_batch, dec_seq, CONV_DIM)[:, dec_seq - (CONV_W - 1):].reshape(
        1, dec_batch, CONV_W - 1, CONV_DIM)
    return (y_prompt, y_sample, *p_kv, p_ssm, p_conv, *s_kv, s_ssm, s_conv)
```

```python
import functools

import numpy as np
import jax
import jax.numpy as jnp
from jax import lax
from jax.experimental import pallas as pl
from jax.experimental.pallas import tpu as pltpu

F32 = jnp.float32
BF16 = jnp.bfloat16

D_MODEL = 2048
D_FF = 5632
D_ATT = 768
ATT_HEAD_DIM = 64
H_PER_DIL = 4
D_GROUP = H_PER_DIL * ATT_HEAD_DIM
DILATION_PAIRS = ((128, 1), (512, 4), (2048, 16))
N_DIL = len(DILATION_PAIRS)
ATT_BLOCK = 128
D_SSM = 1280
SSM_HEAD_DIM = 64
N_SSM_HEADS = 20
N_SSM_GROUPS = 4
HEADS_PER_SSM_GROUP = 5
D_SSM_GROUP = D_SSM // N_SSM_GROUPS
D_STATE = 128
CONV_W = 4
CONV_DIM = 2304
SSD_CHUNK = 128
RMS_EPS = 1e-6
ALIBI_MAX_EXP = 8.0
N_ATT_HEADS = 12

LANES = 128
SUBLANES = 8
DT_PAD = LANES
IN_HEAD = D_SSM + CONV_DIM
IN_TAIL = N_SSM_HEADS + 3 * D_ATT
IN_TAIL_PAD = -(-IN_TAIL // LANES) * LANES

VMEM_LIMIT = 56 * 1024 * 1024


def _params(n_grid_dims, vmem=VMEM_LIMIT):
    return pltpu.CompilerParams(dimension_semantics=("arbitrary",) * n_grid_dims,
                                vmem_limit_bytes=vmem)


def _silu(x):
    return x * (0.5 + 0.5 * jnp.tanh(0.5 * x))


def _softplus(x):
    return jnp.maximum(x, 0.0) + jnp.log(1.0 + jnp.exp(-jnp.abs(x)))


def _rms_scale(x):
    return x * lax.rsqrt(jnp.mean(x * x, axis=-1, keepdims=True) + RMS_EPS)


def _dot(a, b):
    return jnp.dot(a, b, preferred_element_type=F32)


def _dot_nt(a, b):
    return lax.dot_general(a, b, (((1,), (1,)), ((), ())), preferred_element_type=F32)


def _dot_tn(a, b):
    return lax.dot_general(a, b, (((0,), (0,)), ((), ())), preferred_element_type=F32)


def _cumsum_rows(x):
    n = x.shape[0]
    rows = lax.broadcasted_iota(jnp.int32, x.shape, 0)
    shift = 1
    while shift < n:
        x = x + jnp.where(rows >= shift, pltpu.roll(x, shift, 0), 0.0)
        shift *= 2
    return x


def _ffn_body(*refs, n_f, final):
    if final:
        x_ref, g_ref, wg_ref, wu_ref, wd_ref, fg_ref, o_ref, h_ref = refs
    else:
        x_ref, g_ref, wg_ref, wu_ref, wd_ref, o_ref, h_ref = refs
    f = pl.program_id(1)

    @pl.when(f == 0)
    def _():
        h_ref[...] = (_rms_scale(x_ref[...]) * g_ref[...]).astype(BF16)
        o_ref[...] = jnp.zeros_like(o_ref)

    h = h_ref[...]
    gate = _dot(h, wg_ref[...])
    up = _dot(h, wu_ref[...])
    act = (_silu(gate) * up).astype(BF16)
    o_ref[...] += _dot(act, wd_ref[...])

    @pl.when(f == n_f - 1)
    def _():
        y = x_ref[...] + 0.5 * o_ref[...]
        if final:
            y = _rms_scale(y) * fg_ref[...]
        o_ref[...] = y


def _ffn(x, gain, wg, wu, wd, final_gain=None, *, tm=512, tf=512):
    t = x.shape[0]
    n_f = D_FF // tf
    final = final_gain is not None
    row = lambda i, f: (i, 0)
    const = lambda i, f: (0, 0)
    in_specs = [pl.BlockSpec((tm, D_MODEL), row), pl.BlockSpec((1, D_MODEL), const),
                pl.BlockSpec((D_MODEL, tf), lambda i, f: (0, f)),
                pl.BlockSpec((D_MODEL, tf), lambda i, f: (0, f)),
                pl.BlockSpec((tf, D_MODEL), lambda i, f: (f, 0))]
    args = [x, gain, wg, wu, wd]
    if final:
        in_specs.append(pl.BlockSpec((1, D_MODEL), const))
        args.append(final_gain)
    return pl.pallas_call(
        functools.partial(_ffn_body, n_f=n_f, final=final),
        grid=(t // tm, n_f),
        in_specs=in_specs,
        out_specs=pl.BlockSpec((tm, D_MODEL), row),
        out_shape=jax.ShapeDtypeStruct((t, D_MODEL), F32),
        scratch_shapes=[pltpu.VMEM((tm, D_MODEL), BF16)],
        compiler_params=_params(2),
        name="ffn_final" if final else "ffn",
    )(*args)


def _inproj_body(x_ref, g_ref, wh_ref, wt_ref, z_ref, xbc_ref, qkv_ref, dt_ref):
    h = (_rms_scale(x_ref[...]) * g_ref[...]).astype(BF16)
    z_ref[...] = _dot_nt(h, wh_ref[0:D_SSM, :])
    xbc_ref[...] = _dot_nt(h, wh_ref[D_SSM:IN_HEAD, :])
    tail = _dot_nt(h, wt_ref[...])
    lane = lax.broadcasted_iota(jnp.int32, (tail.shape[0], DT_PAD), 1)
    dt_ref[...] = jnp.where(lane < N_SSM_HEADS, tail[:, 0:DT_PAD], 0.0)
    qkv_ref[...] = tail[:, N_SSM_HEADS:IN_TAIL]


def _inproj(x, gain, w_head, w_tail, *, tm=256):
    t = x.shape[0]
    row = lambda i: (i, 0)
    const = lambda i: (0, 0)
    widths = (D_SSM, CONV_DIM, 3 * D_ATT, DT_PAD)
    resident = lambda n: pl.BlockSpec((n, D_MODEL), const, pipeline_mode=pl.Buffered(1))
    return pl.pallas_call(
        _inproj_body,
        grid=(t // tm,),
        in_specs=[pl.BlockSpec((tm, D_MODEL), row), pl.BlockSpec((1, D_MODEL), const),
                  resident(IN_HEAD), resident(IN_TAIL_PAD)],
        out_specs=[pl.BlockSpec((tm, w), row) for w in widths],
        out_shape=[jax.ShapeDtypeStruct((t, w), F32) for w in widths],
        compiler_params=_params(1),
        name="inproj",
    )(x, gain, w_head, w_tail)


def _pair_blockdiag(x, low_half):
    zero = jnp.zeros_like(x)
    return jnp.concatenate([jnp.where(low_half, x, zero), jnp.where(low_half, zero, x)],
                           axis=0).astype(BF16)


def _ssd_prompt_body(xbc_ref, z_ref, dt_ref, cw_ref, cb_ref, dtb_ref, alog_ref, dsk_ref, nrm_ref,
                     y_ref, hfin_ref, xf_scr, ht_scr):
    c = pl.program_id(0)
    n_c = pl.num_programs(0)
    cl = SSD_CHUNK

    @pl.when(c == 0)
    def _():
        xf_scr[0:SUBLANES, :] = jnp.zeros((SUBLANES, CONV_DIM), F32)
        ht_scr[...] = jnp.zeros_like(ht_scr)

    x = xbc_ref[...]
    xf_scr[SUBLANES:SUBLANES + cl, :] = x
    conv = cb_ref[...]
    for j in range(CONV_W - 1):
        conv = conv + xf_scr[SUBLANES - 3 + j:SUBLANES - 3 + j + cl, :] * cw_ref[j:j + 1, :]
    conv = conv + x * cw_ref[CONV_W - 1:CONV_W, :]
    xf_scr[0:SUBLANES, :] = x[cl - SUBLANES:cl, :]
    xa = _silu(conv)
    xs = xa[:, :D_SSM]
    bm = xa[:, D_SSM:D_SSM + N_SSM_GROUPS * D_STATE]
    cm = xa[:, D_SSM + N_SSM_GROUPS * D_STATE:]

    dt = _softplus(dt_ref[...] + dtb_ref[...])
    a = -jnp.exp(alog_ref[...])
    acum = _cumsum_rows(dt * a)
    w_end = dt * jnp.exp(acum[cl - 1:cl, :] - acum)
    acum_t = acum.T
    dt_t = dt.T
    w_end_t = w_end.T

    rows = lax.broadcasted_iota(jnp.int32, (cl, cl), 0)
    cols = lax.broadcasted_iota(jnp.int32, (cl, cl), 1)
    causal = rows >= cols
    low_half = cols < SSM_HEAD_DIM

    b_g = [bm[:, g * D_STATE:(g + 1) * D_STATE] for g in range(N_SSM_GROUPS)]
    c_g = [cm[:, g * D_STATE:(g + 1) * D_STATE].astype(BF16) for g in range(N_SSM_GROUPS)]
    b_t = [b.T for b in b_g]
    cb = [_dot_nt(c_g[g], b_g[g].astype(BF16)) for g in range(N_SSM_GROUPS)]

    sumsq = [jnp.zeros((cl, 1), F32) for _ in range(N_SSM_GROUPS)]
    for j in range(N_SSM_HEADS // 2):
        lo, hi = j * LANES, (j + 1) * LANES
        heads = (2 * j, 2 * j + 1)
        groups = tuple(h // HEADS_PER_SSM_GROUP for h in heads)
        xs_p = xs[:, lo:hi]
        x_bd = _pair_blockdiag(xs_p, low_half)
        ht_p = ht_scr[:, lo:hi]
        h_bd = _pair_blockdiag(ht_p, low_half)
        g_mats, w_mats, e_cols = [], [], []
        for h, g in zip(heads, groups):
            row_b = jnp.broadcast_to(acum_t[h:h + 1, :], (cl, cl))
            col_b = row_b.T
            decay = jnp.where(causal, jnp.exp(col_b - row_b), 0.0)
            g_mats.append((cb[g] * decay * dt_t[h:h + 1, :]).astype(BF16))
            w_mats.append((b_t[g] * w_end_t[h:h + 1, :]).astype(BF16))
            e_cols.append(jnp.exp(col_b))
        y_diag = _dot(jnp.concatenate(g_mats, axis=1), x_bd)
        y_off = _dot(jnp.concatenate([c_g[groups[0]], c_g[groups[1]]], axis=1), h_bd)
        y_off = y_off * jnp.where(low_half, e_cols[0], e_cols[1])
        state = _dot(jnp.concatenate(w_mats, axis=1), x_bd)
        chunk_decay = jnp.where(low_half[0:1, :], e_cols[0][cl - 1:cl, :], e_cols[1][cl - 1:cl, :])
        ht_scr[:, lo:hi] = ht_p * chunk_decay + state
        y_p = y_diag + y_off + dsk_ref[:, lo:hi] * xs_p
        y_p = y_p * _silu(z_ref[:, lo:hi])
        y_ref[:, lo:hi] = y_p
        sq = y_p * y_p
        sumsq[groups[0]] = sumsq[groups[0]] + jnp.sum(jnp.where(low_half, sq, 0.0), axis=1, keepdims=True)
        sumsq[groups[1]] = sumsq[groups[1]] + jnp.sum(jnp.where(low_half, 0.0, sq), axis=1, keepdims=True)

    scale = [lax.rsqrt(s / D_SSM_GROUP + RMS_EPS) for s in sumsq]
    for j in range(N_SSM_HEADS // 2):
        lo, hi = j * LANES, (j + 1) * LANES
        g0, g1 = (2 * j) // HEADS_PER_SSM_GROUP, (2 * j + 1) // HEADS_PER_SSM_GROUP
        y_ref[:, lo:hi] = y_ref[:, lo:hi] * jnp.where(low_half, scale[g0], scale[g1]) * nrm_ref[:, lo:hi]

    @pl.when(c == n_c - 1)
    def _():
        for j in range(D_SSM // LANES):
            hfin_ref[j * LANES:(j + 1) * LANES, :] = ht_scr[:, j * LANES:(j + 1) * LANES].T


def _ssd_prompt(xbc, z, dt, cw, cb, dtb, alog, dsk, nrm):
    t = xbc.shape[0]
    cl = SSD_CHUNK
    row = lambda c: (c, 0)
    const = lambda c: (0, 0)
    vec = lambda n: pl.BlockSpec((1, n), const)
    return pl.pallas_call(
        _ssd_prompt_body,
        grid=(t // cl,),
        in_specs=[pl.BlockSpec((cl, CONV_DIM), row), pl.BlockSpec((cl, D_SSM), row),
                  pl.BlockSpec((cl, DT_PAD), row), pl.BlockSpec((CONV_W, CONV_DIM), const),
                  vec(CONV_DIM), vec(DT_PAD), vec(DT_PAD), vec(D_SSM), vec(D_SSM)],
        out_specs=[pl.BlockSpec((cl, D_SSM), row), pl.BlockSpec((D_SSM, D_STATE), const)],
        out_shape=[jax.ShapeDtypeStruct((t, D_SSM), F32),
                   jax.ShapeDtypeStruct((D_SSM, D_STATE), F32)],
        scratch_shapes=[pltpu.VMEM((SUBLANES + cl, CONV_DIM), F32),
                        pltpu.VMEM((D_STATE, D_SSM), F32)],
        compiler_params=_params(1),
        name="ssd_prompt",
    )(xbc, z, dt, cw, cb, dtb, alog, dsk, nrm)


def _expand_heads(x, e_ref):
    hi = x.astype(BF16)
    r1 = x - hi.astype(F32)
    mid = r1.astype(BF16)
    lo = (r1 - mid.astype(F32)).astype(BF16)
    e = e_ref[...]
    return _dot(hi, e) + _dot(mid, e) + _dot(lo, e)


def _ssd_sample_body(xbc_ref, sc_ref, z_ref, dt_ref, h0_ref, cw_ref, cb_ref, dtb_ref, alog_ref,
                     dsk_ref, nrm_ref, e_ref, y_ref, hout_ref, xf_scr, xw_scr, b_scr, *, n_tok):
    step = pl.program_id(0)

    @pl.when(step == 0)
    def _():
        xw_scr[...] = jnp.zeros_like(xw_scr)
        b_scr[...] = jnp.zeros_like(b_scr)

    x = xbc_ref[...]
    xf_scr[SUBLANES - 3:SUBLANES, :] = sc_ref[0]
    xf_scr[SUBLANES:SUBLANES + n_tok, :] = x
    conv = cb_ref[...]
    for j in range(CONV_W - 1):
        conv = conv + xf_scr[SUBLANES - 3 + j:SUBLANES - 3 + j + n_tok, :] * cw_ref[j:j + 1, :]
    conv = conv + x * cw_ref[CONV_W - 1:CONV_W, :]
    xa = _silu(conv)
    xs = xa[:, :D_SSM]
    bm = xa[:, D_SSM:D_SSM + N_SSM_GROUPS * D_STATE]
    cm = xa[:, D_SSM + N_SSM_GROUPS * D_STATE:]

    dt = _softplus(dt_ref[...] + dtb_ref[...])
    a = -jnp.exp(alog_ref[...])
    acum = _cumsum_rows(dt * a)
    last = acum[n_tok - 1:n_tok, :]
    e_acum = jnp.exp(acum)
    w_end = dt * jnp.exp(last - acum)

    rows = lax.broadcasted_iota(jnp.int32, (n_tok, LANES), 0)
    lane = lax.broadcasted_iota(jnp.int32, (n_tok, LANES), 1)
    lane_group = lane // HEADS_PER_SSM_GROUP
    pieces = []
    for s in range(n_tok):
        prod = cm * bm[s:s + 1, :]
        cb_s = jnp.zeros((n_tok, LANES), F32)
        for g in range(N_SSM_GROUPS):
            r = jnp.sum(prod[:, g * D_STATE:(g + 1) * D_STATE], axis=1, keepdims=True)
            cb_s = jnp.where(lane_group == g, r, cb_s)
        g_s = cb_s * jnp.exp(acum - acum[s:s + 1, :]) * dt[s:s + 1, :]
        pieces.append(jnp.where(rows >= s, g_s, 0.0))
    expanded = _expand_heads(jnp.concatenate(pieces + [e_acum, w_end], axis=0), e_ref)
    y_diag = jnp.zeros((n_tok, D_SSM), F32)
    for s in range(n_tok):
        y_diag = y_diag + expanded[s * n_tok:(s + 1) * n_tok, :] * xs[s:s + 1, :]
    e_acum_x = expanded[n_tok * n_tok:n_tok * (n_tok + 1), :]
    w_end_x = expanded[n_tok * (n_tok + 1):n_tok * (n_tok + 2), :]

    h0 = h0_ref[0]
    c_stack = jnp.concatenate([cm[:, g * D_STATE:(g + 1) * D_STATE] for g in range(N_SSM_GROUPS)],
                              axis=0).astype(BF16)
    y_all = _dot_nt(c_stack, h0.astype(BF16))
    col_group = lax.broadcasted_iota(jnp.int32, (n_tok, D_SSM), 1) // D_SSM_GROUP
    y_off = jnp.zeros((n_tok, D_SSM), F32)
    for g in range(N_SSM_GROUPS):
        y_off = jnp.where(col_group == g, y_all[g * n_tok:(g + 1) * n_tok, :], y_off)
    y = y_diag + y_off * e_acum_x + dsk_ref[...] * xs
    y = y * _silu(z_ref[...])
    sq = y * y
    scale = jnp.zeros((n_tok, D_SSM), F32)
    for g in range(N_SSM_GROUPS):
        ms = jnp.sum(jnp.where(col_group == g, sq, 0.0), axis=1, keepdims=True) / D_SSM_GROUP
        scale = jnp.where(col_group == g, lax.rsqrt(ms + RMS_EPS), scale)
    y_ref[...] = y * scale * nrm_ref[...]

    xw_scr[0:n_tok, :] = xs * w_end_x
    b_scr[0:n_tok, :] = bm
    upd = _dot_tn(xw_scr[...].astype(BF16), b_scr[...].astype(BF16))
    lane1 = lax.broadcasted_iota(jnp.int32, (1, LANES), 1)
    for h in range(N_SSM_HEADS):
        g = h // HEADS_PER_SSM_GROUP
        r0, r1 = h * SSM_HEAD_DIM, (h + 1) * SSM_HEAD_DIM
        decay = jnp.exp(jnp.sum(jnp.where(lane1 == h, last, 0.0), axis=1, keepdims=True))
        hout_ref[0, r0:r1, :] = h0[r0:r1, :] * decay + upd[r0:r1, g * D_STATE:(g + 1) * D_STATE]


def _ssd_sample(xbc, conv_state, z, dt, h0, cw, cb, dtb, alog, dsk, nrm, expand, *, n_tok):
    t = xbc.shape[0]
    n_b = t // n_tok
    row = lambda b: (b, 0)
    const = lambda b: (0, 0)
    vec = lambda n: pl.BlockSpec((1, n), const)
    return pl.pallas_call(
        functools.partial(_ssd_sample_body, n_tok=n_tok),
        grid=(n_b,),
        in_specs=[pl.BlockSpec((n_tok, CONV_DIM), row),
                  pl.BlockSpec((1, CONV_W - 1, CONV_DIM), lambda b: (b, 0, 0)),
                  pl.BlockSpec((n_tok, D_SSM), row), pl.BlockSpec((n_tok, DT_PAD), row),
                  pl.BlockSpec((1, D_SSM, D_STATE), lambda b: (b, 0, 0)),
                  pl.BlockSpec((CONV_W, CONV_DIM), const),
                  vec(CONV_DIM), vec(DT_PAD), vec(DT_PAD), vec(D_SSM), vec(D_SSM),
                  pl.BlockSpec((LANES, D_SSM), const)],
        out_specs=[pl.BlockSpec((n_tok, D_SSM), row),
                   pl.BlockSpec((1, D_SSM, D_STATE), lambda b: (b, 0, 0))],
        out_shape=[jax.ShapeDtypeStruct((t, D_SSM), F32),
                   jax.ShapeDtypeStruct((n_b, D_SSM, D_STATE), F32)],
        scratch_shapes=[pltpu.VMEM((2 * SUBLANES, CONV_DIM), F32),
                        pltpu.VMEM((LANES, D_SSM), F32),
                        pltpu.VMEM((LANES, N_SSM_GROUPS * D_STATE), F32)],
        compiler_params=_params(1),
        name="ssd_sample",
    )(xbc, conv_state, z, dt, h0, cw, cb, dtb, alog, dsk, nrm, expand)


HEADS_PER_HALF = LANES // ATT_HEAD_DIM
N_HALVES = D_GROUP // LANES


ATT_UNITS = 4


def _attn_prompt_body(slope_ref, *refs, dil, group):
    n_in = 5 * N_HALVES
    in_refs, o_refs, l_refs = refs[:n_in], refs[n_in:n_in + N_HALVES], refs[n_in + N_HALVES:]
    i = pl.program_id(0)
    nb = ATT_BLOCK
    qi = lax.broadcasted_iota(jnp.int32, (nb, 2 * nb), 0)
    kj = lax.broadcasted_iota(jnp.int32, (nb, 2 * nb), 1)
    dist = nb + qi - kj
    band = (dist >= 0) & (dist <= nb)
    band_first = band & (kj >= jnp.where(i > 0, 0, nb))
    dist_f = (dist * dil).astype(F32)
    lane_head = lax.broadcasted_iota(jnp.int32, (nb, LANES), 1) // ATT_HEAD_DIM
    for u in range(ATT_UNITS):
        if dil == 1:
            rows = pl.ds(u * nb, nb)
            rows_prev = pl.ds((ATT_UNITS - 1) * nb, nb) if u == 0 else pl.ds((u - 1) * nb, nb)
            valid = band_first if u == 0 else band
        else:
            rows = rows_prev = pl.ds(pl.program_id(1) * ATT_UNITS + u, nb, stride=dil)
            valid = band_first
        for half in range(N_HALVES):
            q_ref, kp_ref, kc_ref, vp_ref, vc_ref = in_refs[5 * half:5 * half + 5]
            if dil == 1 and u > 0:
                kp_ref, vp_ref = kc_ref, vc_ref
            q = q_ref[rows, :]
            k = jnp.concatenate([kp_ref[rows_prev, :], kc_ref[rows, :]], axis=0).astype(BF16)
            v = jnp.concatenate([vp_ref[rows_prev, :], vc_ref[rows, :]], axis=0).astype(BF16)
            out = jnp.zeros((nb, LANES), F32)
            lse = jnp.zeros((nb, LANES), F32)
            for hh in range(HEADS_PER_HALF):
                mine = lane_head == hh
                slope = slope_ref[group * H_PER_DIL + half * HEADS_PER_HALF + hh]
                qh = jnp.where(mine, q, 0.0).astype(BF16)
                s = _dot_nt(qh, k) * (ATT_HEAD_DIM ** -0.5)
                s = jnp.where(valid, s - slope * dist_f, -jnp.inf)
                m = jnp.max(s, axis=1, keepdims=True)
                p = jnp.exp(s - m)
                l = jnp.sum(p, axis=1, keepdims=True)
                pv = _dot(p.astype(BF16), v)
                out = jnp.where(mine, pv / l, out)
                lse = jnp.where(mine, m + jnp.log(l), lse)
            o_refs[half][rows, :] = out
            l_refs[half][rows, :] = lse


def _attn_prompt(slopes, qkv, *, group, dil, seq):
    assert dil == 1 or dil % ATT_UNITS == 0
    blk_rows = max(dil, ATT_UNITS) * ATT_BLOCK
    n_blk = seq // blk_rows
    n_res_steps = max(dil // ATT_UNITS, 1)
    blk = (blk_rows, LANES)
    cur = lambda c: pl.BlockSpec(blk, lambda i, r: (i, c))
    prev = lambda c: pl.BlockSpec(blk, lambda i, r: (jnp.maximum(i - 1, 0), c))
    in_specs = [pl.BlockSpec(memory_space=pltpu.SMEM)]
    for half in range(N_HALVES):
        q_col, k_col, v_col = (N_HALVES * (part * N_DIL + group) + half for part in range(3))
        in_specs += [cur(q_col), prev(k_col), cur(k_col), prev(v_col), cur(v_col)]
    out_blk = pl.BlockSpec(blk, lambda i, r: (i, 0))
    res = pl.pallas_call(
        functools.partial(_attn_prompt_body, dil=dil, group=group),
        grid=(n_blk, n_res_steps),
        in_specs=in_specs,
        out_specs=[out_blk] * (2 * N_HALVES),
        out_shape=[jax.ShapeDtypeStruct((seq, LANES), F32)] * (2 * N_HALVES),
        compiler_params=_params(2),
        name=f"attn_prompt_g{group}",
    )(slopes, *([qkv] * (5 * N_HALVES)))
    return res[:N_HALVES], res[N_HALVES:]


def _attn_sample_body(slope_ref, qkv_ref, k0_ref, v0_ref, k1_ref, v1_ref, k2_ref, v2_ref, att_ref,
                      *, n_tok):
    n_rows = H_PER_DIL * n_tok
    qkv = qkv_ref[...]
    kv_refs = ((k0_ref, v0_ref), (k1_ref, v1_ref), (k2_ref, v2_ref))
    row2 = lax.broadcasted_iota(jnp.int32, (n_rows, D_GROUP), 0)
    lane2 = lax.broadcasted_iota(jnp.int32, (n_rows, D_GROUP), 1)
    own_head = (lane2 // ATT_HEAD_DIM) == (row2 // n_tok)
    row1 = lax.broadcasted_iota(jnp.int32, (n_rows, 1), 0)
    tok1 = row1 % n_tok
    head1 = row1 // n_tok
    lane_head8 = lax.broadcasted_iota(jnp.int32, (n_tok, D_GROUP), 1) // ATT_HEAD_DIM
    scale = ATT_HEAD_DIM ** -0.5

    outs, lses = [], []
    for g, (win, dil) in enumerate(DILATION_PAIRS):
        k_ref, v_ref = kv_refs[g]
        q = qkv[:, g * D_GROUP:(g + 1) * D_GROUP]
        k_new = qkv[:, D_ATT + g * D_GROUP:D_ATT + (g + 1) * D_GROUP]
        v_new = qkv[:, 2 * D_ATT + g * D_GROUP:2 * D_ATT + (g + 1) * D_GROUP]
        q_rows = jnp.where(own_head, jnp.concatenate([q] * H_PER_DIL, axis=0), 0.0)
        slope_1 = jnp.zeros((n_rows, 1), F32)
        for h in range(H_PER_DIL):
            slope_1 = jnp.where(head1 == h, slope_ref[g * H_PER_DIL + h], slope_1)

        s_buf = _dot(q_rows.astype(BF16), k_ref[0].astype(BF16)) * scale
        pos = lax.broadcasted_iota(jnp.int32, (n_rows, win), 1)
        tok_s = lax.broadcasted_iota(jnp.int32, (n_rows, win), 0) % n_tok
        dist = win + tok_s - pos
        on_grid = (pos >= tok_s) & (jnp.bitwise_and(dist, dil - 1) == 0)
        s_buf = jnp.where(on_grid, s_buf - slope_1 * dist.astype(F32), -jnp.inf)
        m = jnp.max(s_buf, axis=1, keepdims=True)
        s_new = []
        for tp in range(n_tok):
            sn = jnp.sum(q_rows * k_new[tp:tp + 1, :], axis=1, keepdims=True) * scale
            ok = (tok1 >= tp) & (jnp.bitwise_and(tok1 - tp, dil - 1) == 0)
            sn = jnp.where(ok, sn - slope_1 * (tok1 - tp).astype(F32), -jnp.inf)
            s_new.append(sn)
            m = jnp.maximum(m, sn)
        p_buf = jnp.exp(s_buf - m)
        l = jnp.sum(p_buf, axis=1, keepdims=True)
        o_rows = _dot_nt(p_buf.astype(BF16), v_ref[0].astype(BF16))
        for tp in range(n_tok):
            p_new = jnp.exp(s_new[tp] - m)
            l = l + p_new
            o_rows = o_rows + p_new * v_new[tp:tp + 1, :]
        o_rows = o_rows / l
        lse_rows = m + jnp.log(l)
        o = jnp.zeros((n_tok, D_GROUP), F32)
        lse = jnp.zeros((n_tok, D_GROUP), F32)
        for h in range(H_PER_DIL):
            o = jnp.where(lane_head8 == h, o_rows[h * n_tok:(h + 1) * n_tok, :], o)
            lse = jnp.where(lane_head8 == h, lse_rows[h * n_tok:(h + 1) * n_tok, :], lse)
        outs.append(o)
        lses.append(lse)

    m = jnp.maximum(jnp.maximum(lses[0], lses[1]), lses[2])
    e = [jnp.exp(x - m) for x in lses]
    den = e[0] + e[1] + e[2]
    for g in range(N_DIL):
        att_ref[:, g * D_GROUP:(g + 1) * D_GROUP] = outs[g] * (e[g] / den)


def _attn_sample(slopes, qkv, caches, *, n_tok):
    t = qkv.shape[0]
    n_b = t // n_tok
    in_specs = [pl.BlockSpec(memory_space=pltpu.SMEM),
                pl.BlockSpec((n_tok, 3 * D_ATT), lambda b: (b, 0))]
    args = [slopes, qkv]
    for g, (win, dil) in enumerate(DILATION_PAIRS):
        assert dil & (dil - 1) == 0, "dilations must be powers of two"
        for cache in caches[g]:
            assert cache.shape[1] == win, "window buffers must hold a full window"
            args.append(jnp.transpose(cache, (0, 2, 3, 1)).reshape(n_b, D_GROUP, win))
            in_specs.append(pl.BlockSpec((1, D_GROUP, win), lambda b: (b, 0, 0)))
    return pl.pallas_call(
        functools.partial(_attn_sample_body, n_tok=n_tok),
        grid=(n_b,),
        in_specs=in_specs,
        out_specs=pl.BlockSpec((n_tok, D_ATT), lambda b: (b, 0)),
        out_shape=jax.ShapeDtypeStruct((t, D_ATT), F32),
        compiler_params=_params(1),
        name="attn_sample",
    )(*args)


def _outproj_body(*refs, combine):
    n_pieces = N_DIL * N_HALVES
    if combine:
        x_ref, ssm_ref = refs[:2]
        o_refs, l_refs = refs[2:2 + n_pieces], refs[2 + n_pieces:2 + 2 * n_pieces]
        w_ref, out_ref = refs[2 + 2 * n_pieces:]
        att = [None] * n_pieces
        for half in range(N_HALVES):
            idx = [g * N_HALVES + half for g in range(N_DIL)]
            lses = [l_refs[p][...] for p in idx]
            m = jnp.maximum(jnp.maximum(lses[0], lses[1]), lses[2])
            e = [jnp.exp(x - m) for x in lses]
            den = e[0] + e[1] + e[2]
            for p, w in zip(idx, e):
                att[p] = o_refs[p][...] * (w / den)
    else:
        x_ref, ssm_ref, att_ref, w_ref, out_ref = refs
        att = [att_ref[:, p * LANES:(p + 1) * LANES] for p in range(n_pieces)]
    mixed = jnp.concatenate([ssm_ref[...]] + att, axis=1).astype(BF16)
    out_ref[...] = x_ref[...] + _dot(mixed, w_ref[...])


def _outproj(x, ssm, att, w_out, *, tm=256):
    t = x.shape[0]
    row = lambda i: (i, 0)
    combine = isinstance(att, (tuple, list))
    att_args = list(att[0]) + list(att[1]) if combine else [att]
    att_specs = [pl.BlockSpec((tm, a.shape[1]), row) for a in att_args]
    return pl.pallas_call(
        functools.partial(_outproj_body, combine=combine),
        grid=(t // tm,),
        in_specs=[pl.BlockSpec((tm, D_MODEL), row), pl.BlockSpec((tm, D_SSM), row)] + att_specs
                 + [pl.BlockSpec((D_MODEL, D_MODEL), lambda i: (0, 0), pipeline_mode=pl.Buffered(1))],
        out_specs=pl.BlockSpec((tm, D_MODEL), row),
        out_shape=jax.ShapeDtypeStruct((t, D_MODEL), F32),
        compiler_params=_params(1),
        name="outproj_prompt" if combine else "outproj_sample",
    )(x, ssm, *att_args, w_out)


def kernel(x_prompt, x_sample, cache_k0, cache_v0, cache_k1, cache_v1, cache_k2, cache_v2, state_ssm, state_conv, ffn1_norm, ffn1_w_gate, ffn1_w_up, ffn1_w_down, mix_norm, w_in, conv_w, conv_b, dt_bias, a_log, d_skip, ssm_norm, w_out, ffn2_norm, ffn2_w_gate, ffn2_w_up, ffn2_w_down, final_norm):
    assert w_in.shape[0] == 1, "single layer"
    batch, seq, _ = x_prompt.shape
    dec_batch, dec_seq, _ = x_sample.shape
    assert batch == 1 and dec_seq == SUBLANES

    row = lambda v: v.reshape(1, -1).astype(F32)
    pad_heads = lambda v: jnp.pad(v.reshape(1, -1).astype(F32), ((0, 0), (0, DT_PAD - N_SSM_HEADS)))
    w_in_t = jnp.transpose(w_in[0])
    w_head = w_in_t[:IN_HEAD].astype(BF16)
    w_tail = jnp.pad(w_in_t[IN_HEAD:], ((0, IN_TAIL_PAD - IN_TAIL), (0, 0))).astype(BF16)
    ffn1 = (row(ffn1_norm[0]), ffn1_w_gate[0].astype(BF16), ffn1_w_up[0].astype(BF16),
            ffn1_w_down[0].astype(BF16))
    ffn2 = (row(ffn2_norm[0]), ffn2_w_gate[0].astype(BF16), ffn2_w_up[0].astype(BF16),
            ffn2_w_down[0].astype(BF16))
    w_o = w_out[0].astype(BF16)
    ssd_params = (conv_w[0].astype(F32), row(conv_b[0]), pad_heads(dt_bias[0]), pad_heads(a_log[0]),
                  row(jnp.repeat(d_skip[0], SSM_HEAD_DIM)), row(ssm_norm[0]))
    slopes = jnp.exp2(-ALIBI_MAX_EXP * jnp.arange(1, N_ATT_HEADS + 1, dtype=F32) / N_ATT_HEADS)
    expand = jnp.asarray(np.arange(D_SSM)[None, :] // SSM_HEAD_DIM == np.arange(LANES)[:, None], BF16)

    def front(x):
        x1 = _ffn(x, *ffn1)
        z, xbc, qkv, dt = _inproj(x1, row(mix_norm[0]), w_head, w_tail)
        return x1, z, xbc, qkv, dt

    def back(x2):
        return _ffn(x2, *ffn2, row(final_norm))

    x1, z, xbc_p, qkv_p, dt = front(x_prompt.reshape(seq, D_MODEL))
    ssm_p, hfin_p = _ssd_prompt(xbc_p, z, dt, *ssd_params)
    outs, lses = [], []
    for g, (win, dil) in enumerate(DILATION_PAIRS):
        o, lse = _attn_prompt(slopes, qkv_p, group=g, dil=dil, seq=seq)
        outs += o
        lses += lse
    y_prompt = back(_outproj(x1, ssm_p, (outs, lses), w_o)).reshape(1, seq, D_MODEL)

    n_s = dec_batch * dec_seq
    x1, z, xbc_s, qkv_s, dt = front(x_sample.reshape(n_s, D_MODEL))
    ssm_s, h_s = _ssd_sample(xbc_s, state_conv[0], z, dt, state_ssm[0].reshape(dec_batch, D_SSM, D_STATE),
                             *ssd_params, expand, n_tok=dec_seq)
    caches = ((cache_k0[0], cache_v0[0]), (cache_k1[0], cache_v1[0]), (cache_k2[0], cache_v2[0]))
    att_s = _attn_sample(slopes, qkv_s, caches, n_tok=dec_seq)
    y_sample = back(_outproj(x1, ssm_s, att_s, w_o)).reshape(dec_batch, dec_seq, D_MODEL)

    head_shape = (H_PER_DIL, ATT_HEAD_DIM)
    p_kv, s_kv = [], []
    for g, (win, dil) in enumerate(DILATION_PAIRS):
        keep = min(win, seq)
        for base in (D_ATT, 2 * D_ATT):
            c0 = base + g * D_GROUP
            p_kv.append(qkv_p[seq - keep:, c0:c0 + D_GROUP].reshape((1, 1, keep) + head_shape))
            s_kv.append(qkv_s[:, c0:c0 + D_GROUP].reshape((1, dec_batch, dec_seq) + head_shape))
    p_ssm = hfin_p.reshape(1, 1, N_SSM_HEADS, SSM_HEAD_DIM, D_STATE)
    p_conv = xbc_p[seq - (CONV_W - 1):].reshape(1, 1, CONV_W - 1, CONV_DIM)
    s_ssm = h_s.reshape(1, dec_batch, N_SSM_HEADS, SSM_HEAD_DIM, D_STATE)
    s_conv = xbc_s.reshape(dec_batch, dec_seq, CONV_DIM)[:, dec_seq - (CONV_W - 1):].reshape(
        1, dec_batch, CONV_W - 1, CONV_DIM)
    return (y_prompt, y_sample, *p_kv, p_ssm, p_conv, *s_kv, s_ssm, s_conv)
```

```python
import functools

import numpy as np
import jax
import jax.numpy as jnp
from jax import lax
from jax.experimental import pallas as pl
from jax.experimental.pallas import tpu as pltpu

F32 = jnp.float32
BF16 = jnp.bfloat16

D_MODEL = 2048
D_FF = 5632
D_ATT = 768
ATT_HEAD_DIM = 64
H_PER_DIL = 4
D_GROUP = H_PER_DIL * ATT_HEAD_DIM
DILATION_PAIRS = ((128, 1), (512, 4), (2048, 16))
N_DIL = len(DILATION_PAIRS)
ATT_BLOCK = 128
D_SSM = 1280
SSM_HEAD_DIM = 64
N_SSM_HEADS = 20
N_SSM_GROUPS = 4
HEADS_PER_SSM_GROUP = 5
D_SSM_GROUP = D_SSM // N_SSM_GROUPS
D_STATE = 128
CONV_W = 4
CONV_DIM = 2304
SSD_CHUNK = 128
RMS_EPS = 1e-6
ALIBI_MAX_EXP = 8.0
N_ATT_HEADS = 12

LANES = 128
SUBLANES = 8
DT_PAD = LANES
IN_HEAD = D_SSM + CONV_DIM
IN_TAIL = N_SSM_HEADS + 3 * D_ATT
IN_TAIL_PAD = -(-IN_TAIL // LANES) * LANES

VMEM_LIMIT = 60 * 1024 * 1024


def _params(n_grid_dims, vmem=VMEM_LIMIT):
    return pltpu.CompilerParams(dimension_semantics=("arbitrary",) * n_grid_dims,
                                vmem_limit_bytes=vmem)


def _silu(x):
    return x * (0.5 + 0.5 * jnp.tanh(0.5 * x))


def _softplus(x):
    return jnp.maximum(x, 0.0) + jnp.log(1.0 + jnp.exp(-jnp.abs(x)))


def _rms_scale(x):
    return x * lax.rsqrt(jnp.mean(x * x, axis=-1, keepdims=True) + RMS_EPS)


def _dot(a, b):
    return jnp.dot(a, b, preferred_element_type=F32)


def _dot_nt(a, b):
    return lax.dot_general(a, b, (((1,), (1,)), ((), ())), preferred_element_type=F32)


def _dot_tn(a, b):
    return lax.dot_general(a, b, (((0,), (0,)), ((), ())), preferred_element_type=F32)


def _cumsum_rows(x):
    n = x.shape[0]
    rows = lax.broadcasted_iota(jnp.int32, x.shape, 0)
    shift = 1
    while shift < n:
        x = x + jnp.where(rows >= shift, pltpu.roll(x, shift, 0), 0.0)
        shift *= 2
    return x


def _ffn_body(*refs, n_f, final):
    if final:
        x_ref, g_ref, wg_ref, wu_ref, wd_ref, fg_ref, o_ref, h_ref = refs
    else:
        x_ref, g_ref, wg_ref, wu_ref, wd_ref, o_ref, h_ref = refs
    f = pl.program_id(1)

    @pl.when(f == 0)
    def _():
        h_ref[...] = (_rms_scale(x_ref[...]) * g_ref[...]).astype(BF16)
        o_ref[...] = jnp.zeros_like(o_ref)

    h = h_ref[...]
    gate = _dot(h, wg_ref[...].astype(BF16))
    up = _dot(h, wu_ref[...].astype(BF16))
    act = (_silu(gate) * up).astype(BF16)
    o_ref[...] += _dot(act, wd_ref[...].astype(BF16))

    @pl.when(f == n_f - 1)
    def _():
        y = x_ref[...] + 0.5 * o_ref[...]
        if final:
            y = _rms_scale(y) * fg_ref[...]
        o_ref[...] = y


def _ffn(x, gain, wg, wu, wd, final_gain=None, *, tm=1024, tf=256):
    t = x.shape[0]
    n_f = D_FF // tf
    final = final_gain is not None
    row = lambda i, f: (i, 0)
    const = lambda i, f: (0, 0)
    in_specs = [pl.BlockSpec((tm, D_MODEL), row), pl.BlockSpec((1, D_MODEL), const),
                pl.BlockSpec((D_MODEL, tf), lambda i, f: (0, f)),
                pl.BlockSpec((D_MODEL, tf), lambda i, f: (0, f)),
                pl.BlockSpec((tf, D_MODEL), lambda i, f: (f, 0))]
    args = [x, gain, wg, wu, wd]
    if final:
        in_specs.append(pl.BlockSpec((1, D_MODEL), const))
        args.append(final_gain)
    return pl.pallas_call(
        functools.partial(_ffn_body, n_f=n_f, final=final),
        grid=(t // tm, n_f),
        in_specs=in_specs,
        out_specs=pl.BlockSpec((tm, D_MODEL), row),
        out_shape=jax.ShapeDtypeStruct((t, D_MODEL), F32),
        scratch_shapes=[pltpu.VMEM((tm, D_MODEL), BF16)],
        compiler_params=_params(2),
        name="ffn_final" if final else "ffn",
    )(*args)


def _inproj_body(x_ref, g_ref, wh_ref, wt_ref, z_ref, xbc_ref, qkv_ref, dt_ref):
    h = (_rms_scale(x_ref[...]) * g_ref[...]).astype(BF16)
    z_ref[...] = _dot_nt(h, wh_ref[0:D_SSM, :])
    xbc_ref[...] = _dot_nt(h, wh_ref[D_SSM:IN_HEAD, :])
    tail = _dot_nt(h, wt_ref[...])
    lane = lax.broadcasted_iota(jnp.int32, (tail.shape[0], DT_PAD), 1)
    dt_ref[...] = jnp.where(lane < N_SSM_HEADS, tail[:, 0:DT_PAD], 0.0)
    qkv_ref[...] = tail[:, N_SSM_HEADS:IN_TAIL]


def _inproj(x, gain, w_head, w_tail, *, tm=256):
    t = x.shape[0]
    row = lambda i: (i, 0)
    const = lambda i: (0, 0)
    widths = (D_SSM, CONV_DIM, 3 * D_ATT, DT_PAD)
    resident = lambda n: pl.BlockSpec((n, D_MODEL), const, pipeline_mode=pl.Buffered(1))
    return pl.pallas_call(
        _inproj_body,
        grid=(t // tm,),
        in_specs=[pl.BlockSpec((tm, D_MODEL), row), pl.BlockSpec((1, D_MODEL), const),
                  resident(IN_HEAD), resident(IN_TAIL_PAD)],
        out_specs=[pl.BlockSpec((tm, w), row) for w in widths],
        out_shape=[jax.ShapeDtypeStruct((t, w), F32) for w in widths],
        compiler_params=_params(1),
        name="inproj",
    )(x, gain, w_head, w_tail)


def _pair_blockdiag(x, low_half):
    zero = jnp.zeros_like(x)
    return jnp.concatenate([jnp.where(low_half, x, zero), jnp.where(low_half, zero, x)],
                           axis=0).astype(BF16)


def _ssd_prompt_body(xbc_ref, z_ref, dt_ref, cw_ref, cb_ref, dtb_ref, alog_ref, dsk_ref, nrm_ref,
                     y_ref, hfin_ref, xf_scr, ht_scr):
    c = pl.program_id(0)
    n_c = pl.num_programs(0)
    cl = SSD_CHUNK

    @pl.when(c == 0)
    def _():
        xf_scr[0:SUBLANES, :] = jnp.zeros((SUBLANES, CONV_DIM), F32)
        ht_scr[...] = jnp.zeros_like(ht_scr)

    x = xbc_ref[...]
    xf_scr[SUBLANES:SUBLANES + cl, :] = x
    conv = cb_ref[...]
    for j in range(CONV_W - 1):
        conv = conv + xf_scr[SUBLANES - 3 + j:SUBLANES - 3 + j + cl, :] * cw_ref[j:j + 1, :]
    conv = conv + x * cw_ref[CONV_W - 1:CONV_W, :]
    xf_scr[0:SUBLANES, :] = x[cl - SUBLANES:cl, :]
    xa = _silu(conv)
    xs = xa[:, :D_SSM]
    bm = xa[:, D_SSM:D_SSM + N_SSM_GROUPS * D_STATE]
    cm = xa[:, D_SSM + N_SSM_GROUPS * D_STATE:]

    dt = _softplus(dt_ref[...] + dtb_ref[...])
    a = -jnp.exp(alog_ref[...])
    acum = _cumsum_rows(dt * a)
    w_end = dt * jnp.exp(acum[cl - 1:cl, :] - acum)
    acum_t = acum.T
    dt_t = dt.T
    w_end_t = w_end.T

    rows = lax.broadcasted_iota(jnp.int32, (cl, cl), 0)
    cols = lax.broadcasted_iota(jnp.int32, (cl, cl), 1)
    causal = rows >= cols
    low_half = cols < SSM_HEAD_DIM

    b_g = [bm[:, g * D_STATE:(g + 1) * D_STATE] for g in range(N_SSM_GROUPS)]
    c_g = [cm[:, g * D_STATE:(g + 1) * D_STATE].astype(BF16) for g in range(N_SSM_GROUPS)]
    b_t = [b.T for b in b_g]
    cb = [_dot_nt(c_g[g], b_g[g].astype(BF16)) for g in range(N_SSM_GROUPS)]

    sumsq = [jnp.zeros((cl, 1), F32) for _ in range(N_SSM_GROUPS)]
    for j in range(N_SSM_HEADS // 2):
        lo, hi = j * LANES, (j + 1) * LANES
        heads = (2 * j, 2 * j + 1)
        groups = tuple(h // HEADS_PER_SSM_GROUP for h in heads)
        xs_p = xs[:, lo:hi]
        x_bd = _pair_blockdiag(xs_p, low_half)
        ht_p = ht_scr[:, lo:hi]
        h_bd = _pair_blockdiag(ht_p, low_half)
        g_mats, w_mats, e_cols = [], [], []
        for h, g in zip(heads, groups):
            row_b = jnp.broadcast_to(acum_t[h:h + 1, :], (cl, cl))
            col_b = row_b.T
            decay = jnp.where(causal, jnp.exp(col_b - row_b), 0.0)
            g_mats.append((cb[g] * decay * dt_t[h:h + 1, :]).astype(BF16))
            w_mats.append((b_t[g] * w_end_t[h:h + 1, :]).astype(BF16))
            e_cols.append(jnp.exp(col_b))
        y_diag = _dot(jnp.concatenate(g_mats, axis=1), x_bd)
        y_off = _dot(jnp.concatenate([c_g[groups[0]], c_g[groups[1]]], axis=1), h_bd)
        y_off = y_off * jnp.where(low_half, e_cols[0], e_cols[1])
        state = _dot(jnp.concatenate(w_mats, axis=1), x_bd)
        chunk_decay = jnp.where(low_half[0:1, :], e_cols[0][cl - 1:cl, :], e_cols[1][cl - 1:cl, :])
        ht_scr[:, lo:hi] = ht_p * chunk_decay + state
        y_p = y_diag + y_off + dsk_ref[:, lo:hi] * xs_p
        y_p = y_p * _silu(z_ref[:, lo:hi])
        y_ref[:, lo:hi] = y_p
        sq = y_p * y_p
        sumsq[groups[0]] = sumsq[groups[0]] + jnp.sum(jnp.where(low_half, sq, 0.0), axis=1, keepdims=True)
        sumsq[groups[1]] = sumsq[groups[1]] + jnp.sum(jnp.where(low_half, 0.0, sq), axis=1, keepdims=True)

    scale = [lax.rsqrt(s / D_SSM_GROUP + RMS_EPS) for s in sumsq]
    for j in range(N_SSM_HEADS // 2):
        lo, hi = j * LANES, (j + 1) * LANES
        g0, g1 = (2 * j) // HEADS_PER_SSM_GROUP, (2 * j + 1) // HEADS_PER_SSM_GROUP
        y_ref[:, lo:hi] = y_ref[:, lo:hi] * jnp.where(low_half, scale[g0], scale[g1]) * nrm_ref[:, lo:hi]

    @pl.when(c == n_c - 1)
    def _():
        for j in range(D_SSM // LANES):
            hfin_ref[j * LANES:(j + 1) * LANES, :] = ht_scr[:, j * LANES:(j + 1) * LANES].T


def _ssd_prompt(xbc, z, dt, cw, cb, dtb, alog, dsk, nrm):
    t = xbc.shape[0]
    cl = SSD_CHUNK
    row = lambda c: (c, 0)
    const = lambda c: (0, 0)
    vec = lambda n: pl.BlockSpec((1, n), const)
    return pl.pallas_call(
        _ssd_prompt_body,
        grid=(t // cl,),
        in_specs=[pl.BlockSpec((cl, CONV_DIM), row), pl.BlockSpec((cl, D_SSM), row),
                  pl.BlockSpec((cl, DT_PAD), row), pl.BlockSpec((CONV_W, CONV_DIM), const),
                  vec(CONV_DIM), vec(DT_PAD), vec(DT_PAD), vec(D_SSM), vec(D_SSM)],
        out_specs=[pl.BlockSpec((cl, D_SSM), row), pl.BlockSpec((D_SSM, D_STATE), const)],
        out_shape=[jax.ShapeDtypeStruct((t, D_SSM), F32),
                   jax.ShapeDtypeStruct((D_SSM, D_STATE), F32)],
        scratch_shapes=[pltpu.VMEM((SUBLANES + cl, CONV_DIM), F32),
                        pltpu.VMEM((D_STATE, D_SSM), F32)],
        compiler_params=_params(1),
        name="ssd_prompt",
    )(xbc, z, dt, cw, cb, dtb, alog, dsk, nrm)


def _expand_heads(x, e_ref):
    hi = x.astype(BF16)
    r1 = x - hi.astype(F32)
    mid = r1.astype(BF16)
    lo = (r1 - mid.astype(F32)).astype(BF16)
    e = e_ref[...]
    return _dot(hi, e) + _dot(mid, e) + _dot(lo, e)


def _ssd_sample_body(xbc_ref, sc_ref, z_ref, dt_ref, h0_ref, cw_ref, cb_ref, dtb_ref, alog_ref,
                     dsk_ref, nrm_ref, e_ref, y_ref, hout_ref, xf_scr, xw_scr, b_scr, *, n_tok):
    step = pl.program_id(0)

    @pl.when(step == 0)
    def _():
        xw_scr[...] = jnp.zeros_like(xw_scr)
        b_scr[...] = jnp.zeros_like(b_scr)

    x = xbc_ref[...]
    xf_scr[SUBLANES - 3:SUBLANES, :] = sc_ref[0]
    xf_scr[SUBLANES:SUBLANES + n_tok, :] = x
    conv = cb_ref[...]
    for j in range(CONV_W - 1):
        conv = conv + xf_scr[SUBLANES - 3 + j:SUBLANES - 3 + j + n_tok, :] * cw_ref[j:j + 1, :]
    conv = conv + x * cw_ref[CONV_W - 1:CONV_W, :]
    xa = _silu(conv)
    xs = xa[:, :D_SSM]
    bm = xa[:, D_SSM:D_SSM + N_SSM_GROUPS * D_STATE]
    cm = xa[:, D_SSM + N_SSM_GROUPS * D_STATE:]

    dt = _softplus(dt_ref[...] + dtb_ref[...])
    a = -jnp.exp(alog_ref[...])
    acum = _cumsum_rows(dt * a)
    last = acum[n_tok - 1:n_tok, :]
    e_acum = jnp.exp(acum)
    w_end = dt * jnp.exp(last - acum)

    rows = lax.broadcasted_iota(jnp.int32, (n_tok, LANES), 0)
    lane = lax.broadcasted_iota(jnp.int32, (n_tok, LANES), 1)
    lane_group = lane // HEADS_PER_SSM_GROUP
    pieces = []
    for s in range(n_tok):
        prod = cm * bm[s:s + 1, :]
        cb_s = jnp.zeros((n_tok, LANES), F32)
        for g in range(N_SSM_GROUPS):
            r = jnp.sum(prod[:, g * D_STATE:(g + 1) * D_STATE], axis=1, keepdims=True)
            cb_s = jnp.where(lane_group == g, r, cb_s)
        g_s = cb_s * jnp.exp(acum - acum[s:s + 1, :]) * dt[s:s + 1, :]
        pieces.append(jnp.where(rows >= s, g_s, 0.0))
    expanded = _expand_heads(jnp.concatenate(pieces + [e_acum, w_end], axis=0), e_ref)
    y_diag = jnp.zeros((n_tok, D_SSM), F32)
    for s in range(n_tok):
        y_diag = y_diag + expanded[s * n_tok:(s + 1) * n_tok, :] * xs[s:s + 1, :]
    e_acum_x = expanded[n_tok * n_tok:n_tok * (n_tok + 1), :]
    w_end_x = expanded[n_tok * (n_tok + 1):n_tok * (n_tok + 2), :]

    h0 = h0_ref[0]
    c_stack = jnp.concatenate([cm[:, g * D_STATE:(g + 1) * D_STATE] for g in range(N_SSM_GROUPS)],
                              axis=0).astype(BF16)
    y_all = _dot_nt(c_stack, h0.astype(BF16))
    col_group = lax.broadcasted_iota(jnp.int32, (n_tok, D_SSM), 1) // D_SSM_GROUP
    y_off = jnp.zeros((n_tok, D_SSM), F32)
    for g in range(N_SSM_GROUPS):
        y_off = jnp.where(col_group == g, y_all[g * n_tok:(g + 1) * n_tok, :], y_off)
    y = y_diag + y_off * e_acum_x + dsk_ref[...] * xs
    y = y * _silu(z_ref[...])
    sq = y * y
    scale = jnp.zeros((n_tok, D_SSM), F32)
    for g in range(N_SSM_GROUPS):
        ms = jnp.sum(jnp.where(col_group == g, sq, 0.0), axis=1, keepdims=True) / D_SSM_GROUP
        scale = jnp.where(col_group == g, lax.rsqrt(ms + RMS_EPS), scale)
    y_ref[...] = y * scale * nrm_ref[...]

    xw_scr[0:n_tok, :] = xs * w_end_x
    b_scr[0:n_tok, :] = bm
    upd = _dot_tn(xw_scr[...].astype(BF16), b_scr[...].astype(BF16))
    lane1 = lax.broadcasted_iota(jnp.int32, (1, LANES), 1)
    for h in range(N_SSM_HEADS):
        g = h // HEADS_PER_SSM_GROUP
        r0, r1 = h * SSM_HEAD_DIM, (h + 1) * SSM_HEAD_DIM
        decay = jnp.exp(jnp.sum(jnp.where(lane1 == h, last, 0.0), axis=1, keepdims=True))
        hout_ref[0, r0:r1, :] = h0[r0:r1, :] * decay + upd[r0:r1, g * D_STATE:(g + 1) * D_STATE]


def _ssd_sample(xbc, conv_state, z, dt, h0, cw, cb, dtb, alog, dsk, nrm, expand, *, n_tok):
    t = xbc.shape[0]
    n_b = t // n_tok
    row = lambda b: (b, 0)
    const = lambda b: (0, 0)
    vec = lambda n: pl.BlockSpec((1, n), const)
    return pl.pallas_call(
        functools.partial(_ssd_sample_body, n_tok=n_tok),
        grid=(n_b,),
        in_specs=[pl.BlockSpec((n_tok, CONV_DIM), row),
                  pl.BlockSpec((1, CONV_W - 1, CONV_DIM), lambda b: (b, 0, 0)),
                  pl.BlockSpec((n_tok, D_SSM), row), pl.BlockSpec((n_tok, DT_PAD), row),
                  pl.BlockSpec((1, D_SSM, D_STATE), lambda b: (b, 0, 0)),
                  pl.BlockSpec((CONV_W, CONV_DIM), const),
                  vec(CONV_DIM), vec(DT_PAD), vec(DT_PAD), vec(D_SSM), vec(D_SSM),
                  pl.BlockSpec((LANES, D_SSM), const)],
        out_specs=[pl.BlockSpec((n_tok, D_SSM), row),
                   pl.BlockSpec((1, D_SSM, D_STATE), lambda b: (b, 0, 0))],
        out_shape=[jax.ShapeDtypeStruct((t, D_SSM), F32),
                   jax.ShapeDtypeStruct((n_b, D_SSM, D_STATE), F32)],
        scratch_shapes=[pltpu.VMEM((2 * SUBLANES, CONV_DIM), F32),
                        pltpu.VMEM((LANES, D_SSM), F32),
                        pltpu.VMEM((LANES, N_SSM_GROUPS * D_STATE), F32)],
        compiler_params=_params(1),
        name="ssd_sample",
    )(xbc, conv_state, z, dt, h0, cw, cb, dtb, alog, dsk, nrm, expand)


HEADS_PER_HALF = LANES // ATT_HEAD_DIM
N_HALVES = D_GROUP // LANES


ATT_UNITS = 4


def _attn_prompt_body(slope_ref, *refs, dil, group):
    n_in = 5 * N_HALVES
    in_refs, o_refs, l_refs = refs[:n_in], refs[n_in:n_in + N_HALVES], refs[n_in + N_HALVES:]
    i = pl.program_id(0)
    nb = ATT_BLOCK
    qi = lax.broadcasted_iota(jnp.int32, (nb, 2 * nb), 0)
    kj = lax.broadcasted_iota(jnp.int32, (nb, 2 * nb), 1)
    dist = nb + qi - kj
    band = (dist >= 0) & (dist <= nb)
    band_first = band & (kj >= jnp.where(i > 0, 0, nb))
    dist_f = (dist * dil).astype(F32)
    lane_head = lax.broadcasted_iota(jnp.int32, (nb, LANES), 1) // ATT_HEAD_DIM
    for u in range(ATT_UNITS):
        if dil == 1:
            rows = pl.ds(u * nb, nb)
            rows_prev = pl.ds((ATT_UNITS - 1) * nb, nb) if u == 0 else pl.ds((u - 1) * nb, nb)
            valid = band_first if u == 0 else band
        else:
            rows = rows_prev = pl.ds(pl.program_id(1) * ATT_UNITS + u, nb, stride=dil)
            valid = band_first
        for half in range(N_HALVES):
            q_ref, kp_ref, kc_ref, vp_ref, vc_ref = in_refs[5 * half:5 * half + 5]
            if dil == 1 and u > 0:
                kp_ref, vp_ref = kc_ref, vc_ref
            q = q_ref[rows, :]
            k = jnp.concatenate([kp_ref[rows_prev, :], kc_ref[rows, :]], axis=0).astype(BF16)
            v = jnp.concatenate([vp_ref[rows_prev, :], vc_ref[rows, :]], axis=0).astype(BF16)
            out = jnp.zeros((nb, LANES), F32)
            lse = jnp.zeros((nb, LANES), F32)
            for hh in range(HEADS_PER_HALF):
                mine = lane_head == hh
                slope = slope_ref[group * H_PER_DIL + half * HEADS_PER_HALF + hh]
                qh = jnp.where(mine, q, 0.0).astype(BF16)
                s = _dot_nt(qh, k) * (ATT_HEAD_DIM ** -0.5)
                s = jnp.where(valid, s - slope * dist_f, -jnp.inf)
                m = jnp.max(s, axis=1, keepdims=True)
                p = jnp.exp(s - m)
                l = jnp.sum(p, axis=1, keepdims=True)
                pv = _dot(p.astype(BF16), v)
                out = jnp.where(mine, pv / l, out)
                lse = jnp.where(mine, m + jnp.log(l), lse)
            o_refs[half][rows, :] = out
            l_refs[half][rows, :] = lse


def _attn_prompt(slopes, qkv, *, group, dil, seq):
    assert dil == 1 or dil % ATT_UNITS == 0
    blk_rows = max(dil, ATT_UNITS) * ATT_BLOCK
    n_blk = seq // blk_rows
    n_res_steps = max(dil // ATT_UNITS, 1)
    blk = (blk_rows, LANES)
    cur = lambda c: pl.BlockSpec(blk, lambda i, r: (i, c))
    prev = lambda c: pl.BlockSpec(blk, lambda i, r: (jnp.maximum(i - 1, 0), c))
    in_specs = [pl.BlockSpec(memory_space=pltpu.SMEM)]
    for half in range(N_HALVES):
        q_col, k_col, v_col = (N_HALVES * (part * N_DIL + group) + half for part in range(3))
        in_specs += [cur(q_col), prev(k_col), cur(k_col), prev(v_col), cur(v_col)]
    out_blk = pl.BlockSpec(blk, lambda i, r: (i, 0))
    res = pl.pallas_call(
        functools.partial(_attn_prompt_body, dil=dil, group=group),
        grid=(n_blk, n_res_steps),
        in_specs=in_specs,
        out_specs=[out_blk] * (2 * N_HALVES),
        out_shape=[jax.ShapeDtypeStruct((seq, LANES), F32)] * (2 * N_HALVES),
        compiler_params=_params(2),
        name=f"attn_prompt_g{group}",
    )(slopes, *([qkv] * (5 * N_HALVES)))
    return res[:N_HALVES], res[N_HALVES:]


def _attn_sample_body(slope_ref, qkv_ref, k0_ref, v0_ref, k1_ref, v1_ref, k2_ref, v2_ref, att_ref,
                      *, n_tok, n_seq):
    n_rows = H_PER_DIL * n_tok
    kv_refs = ((k0_ref, v0_ref), (k1_ref, v1_ref), (k2_ref, v2_ref))
    row2 = lax.broadcasted_iota(jnp.int32, (n_rows, D_GROUP), 0)
    lane2 = lax.broadcasted_iota(jnp.int32, (n_rows, D_GROUP), 1)
    own_head = (lane2 // ATT_HEAD_DIM) == (row2 // n_tok)
    row1 = lax.broadcasted_iota(jnp.int32, (n_rows, 1), 0)
    tok1 = row1 % n_tok
    head1 = row1 // n_tok
    lane_head8 = lax.broadcasted_iota(jnp.int32, (n_tok, D_GROUP), 1) // ATT_HEAD_DIM
    scale = ATT_HEAD_DIM ** -0.5
    for b in range(n_seq):
        rows = slice(b * n_tok, (b + 1) * n_tok)
        _attn_sample_one(slope_ref, qkv_ref[rows, :], [(k[b], v[b]) for k, v in kv_refs],
                         att_ref.at[rows, :], n_tok, n_rows, own_head, tok1, head1, lane_head8, scale)


def _attn_sample_one(slope_ref, qkv, kv, att_ref, n_tok, n_rows, own_head, tok1, head1, lane_head8, scale):
    outs, lses = [], []
    for g, (win, dil) in enumerate(DILATION_PAIRS):
        k_buf, v_buf = kv[g]
        q = qkv[:, g * D_GROUP:(g + 1) * D_GROUP]
        k_new = qkv[:, D_ATT + g * D_GROUP:D_ATT + (g + 1) * D_GROUP]
        v_new = qkv[:, 2 * D_ATT + g * D_GROUP:2 * D_ATT + (g + 1) * D_GROUP]
        q_rows = jnp.where(own_head, jnp.concatenate([q] * H_PER_DIL, axis=0), 0.0)
        slope_1 = jnp.zeros((n_rows, 1), F32)
        for h in range(H_PER_DIL):
            slope_1 = jnp.where(head1 == h, slope_ref[g * H_PER_DIL + h], slope_1)

        s_buf = _dot(q_rows.astype(BF16), k_buf.astype(BF16)) * scale
        pos = lax.broadcasted_iota(jnp.int32, (n_rows, win), 1)
        tok_s = lax.broadcasted_iota(jnp.int32, (n_rows, win), 0) % n_tok
        dist = win + tok_s - pos
        on_grid = (pos >= tok_s) & (jnp.bitwise_and(dist, dil - 1) == 0)
        s_buf = jnp.where(on_grid, s_buf - slope_1 * dist.astype(F32), -jnp.inf)
        m = jnp.max(s_buf, axis=1, keepdims=True)
        s_new = []
        for tp in range(n_tok):
            sn = jnp.sum(q_rows * k_new[tp:tp + 1, :], axis=1, keepdims=True) * scale
            ok = (tok1 >= tp) & (jnp.bitwise_and(tok1 - tp, dil - 1) == 0)
            sn = jnp.where(ok, sn - slope_1 * (tok1 - tp).astype(F32), -jnp.inf)
            s_new.append(sn)
            m = jnp.maximum(m, sn)
        p_buf = jnp.exp(s_buf - m)
        l = jnp.sum(p_buf, axis=1, keepdims=True)
        o_rows = _dot_nt(p_buf.astype(BF16), v_buf.astype(BF16))
        for tp in range(n_tok):
            p_new = jnp.exp(s_new[tp] - m)
            l = l + p_new
            o_rows = o_rows + p_new * v_new[tp:tp + 1, :]
        o_rows = o_rows / l
        lse_rows = m + jnp.log(l)
        o = jnp.zeros((n_tok, D_GROUP), F32)
        lse = jnp.zeros((n_tok, D_GROUP), F32)
        for h in range(H_PER_DIL):
            o = jnp.where(lane_head8 == h, o_rows[h * n_tok:(h + 1) * n_tok, :], o)
            lse = jnp.where(lane_head8 == h, lse_rows[h * n_tok:(h + 1) * n_tok, :], lse)
        outs.append(o)
        lses.append(lse)

    m = jnp.maximum(jnp.maximum(lses[0], lses[1]), lses[2])
    e = [jnp.exp(x - m) for x in lses]
    den = e[0] + e[1] + e[2]
    for g in range(N_DIL):
        att_ref[:, g * D_GROUP:(g + 1) * D_GROUP] = outs[g] * (e[g] / den)


def _attn_sample(slopes, qkv, caches, *, n_tok, n_seq=2):
    t = qkv.shape[0]
    n_b = t // n_tok
    in_specs = [pl.BlockSpec(memory_space=pltpu.SMEM),
                pl.BlockSpec((n_seq * n_tok, 3 * D_ATT), lambda b: (b, 0))]
    args = [slopes, qkv]
    for g, (win, dil) in enumerate(DILATION_PAIRS):
        assert dil & (dil - 1) == 0, "dilations must be powers of two"
        for cache in caches[g]:
            assert cache.shape[1] == win, "window buffers must hold a full window"
            args.append(jnp.transpose(cache, (0, 2, 3, 1)).reshape(n_b, D_GROUP, win))
            in_specs.append(pl.BlockSpec((n_seq, D_GROUP, win), lambda b: (b, 0, 0)))
    return pl.pallas_call(
        functools.partial(_attn_sample_body, n_tok=n_tok, n_seq=n_seq),
        grid=(n_b // n_seq,),
        in_specs=in_specs,
        out_specs=pl.BlockSpec((n_seq * n_tok, D_ATT), lambda b: (b, 0)),
        out_shape=jax.ShapeDtypeStruct((t, D_ATT), F32),
        compiler_params=_params(1),
        name="attn_sample",
    )(*args)


def _outproj_body(*refs, combine):
    n_pieces = N_DIL * N_HALVES
    if combine:
        x_ref, ssm_ref = refs[:2]
        o_refs, l_refs = refs[2:2 + n_pieces], refs[2 + n_pieces:2 + 2 * n_pieces]
        w_ref, out_ref = refs[2 + 2 * n_pieces:]
        att = [None] * n_pieces
        for half in range(N_HALVES):
            idx = [g * N_HALVES + half for g in range(N_DIL)]
            lses = [l_refs[p][...] for p in idx]
            m = jnp.maximum(jnp.maximum(lses[0], lses[1]), lses[2])
            e = [jnp.exp(x - m) for x in lses]
            den = e[0] + e[1] + e[2]
            for p, w in zip(idx, e):
                att[p] = o_refs[p][...] * (w / den)
    else:
        x_ref, ssm_ref, att_ref, w_ref, out_ref = refs
        att = [att_ref[:, p * LANES:(p + 1) * LANES] for p in range(n_pieces)]
    mixed = jnp.concatenate([ssm_ref[...]] + att, axis=1).astype(BF16)
    out_ref[...] = x_ref[...] + _dot(mixed, w_ref[...])


def _outproj(x, ssm, att, w_out, *, tm=256):
    t = x.shape[0]
    row = lambda i: (i, 0)
    combine = isinstance(att, (tuple, list))
    att_args = list(att[0]) + list(att[1]) if combine else [att]
    att_specs = [pl.BlockSpec((tm, a.shape[1]), row) for a in att_args]
    return pl.pallas_call(
        functools.partial(_outproj_body, combine=combine),
        grid=(t // tm,),
        in_specs=[pl.BlockSpec((tm, D_MODEL), row), pl.BlockSpec((tm, D_SSM), row)] + att_specs
                 + [pl.BlockSpec((D_MODEL, D_MODEL), lambda i: (0, 0), pipeline_mode=pl.Buffered(1))],
        out_specs=pl.BlockSpec((tm, D_MODEL), row),
        out_shape=jax.ShapeDtypeStruct((t, D_MODEL), F32),
        compiler_params=_params(1),
        name="outproj_prompt" if combine else "outproj_sample",
    )(x, ssm, *att_args, w_out)


def kernel(x_prompt, x_sample, cache_k0, cache_v0, cache_k1, cache_v1, cache_k2, cache_v2, state_ssm, state_conv, ffn1_norm, ffn1_w_gate, ffn1_w_up, ffn1_w_down, mix_norm, w_in, conv_w, conv_b, dt_bias, a_log, d_skip, ssm_norm, w_out, ffn2_norm, ffn2_w_gate, ffn2_w_up, ffn2_w_down, final_norm):
    assert w_in.shape[0] == 1, "single layer"
    batch, seq, _ = x_prompt.shape
    dec_batch, dec_seq, _ = x_sample.shape
    assert batch == 1 and dec_seq == SUBLANES

    row = lambda v: v.reshape(1, -1).astype(F32)
    pad_heads = lambda v: jnp.pad(v.reshape(1, -1).astype(F32), ((0, 0), (0, DT_PAD - N_SSM_HEADS)))
    w_in_t = jnp.transpose(w_in[0])
    w_head = w_in_t[:IN_HEAD].astype(BF16)
    w_tail = jnp.pad(w_in_t[IN_HEAD:], ((0, IN_TAIL_PAD - IN_TAIL), (0, 0))).astype(BF16)
    ffn1 = (row(ffn1_norm[0]), ffn1_w_gate[0], ffn1_w_up[0], ffn1_w_down[0])
    ffn2 = (row(ffn2_norm[0]), ffn2_w_gate[0], ffn2_w_up[0], ffn2_w_down[0])
    w_o = w_out[0].astype(BF16)
    ssd_params = (conv_w[0].astype(F32), row(conv_b[0]), pad_heads(dt_bias[0]), pad_heads(a_log[0]),
                  row(jnp.repeat(d_skip[0], SSM_HEAD_DIM)), row(ssm_norm[0]))
    slopes = jnp.exp2(-ALIBI_MAX_EXP * jnp.arange(1, N_ATT_HEADS + 1, dtype=F32) / N_ATT_HEADS)
    expand = jnp.asarray(np.arange(D_SSM)[None, :] // SSM_HEAD_DIM == np.arange(LANES)[:, None], BF16)

    def front(x):
        x1 = _ffn(x, *ffn1)
        z, xbc, qkv, dt = _inproj(x1, row(mix_norm[0]), w_head, w_tail)
        return x1, z, xbc, qkv, dt

    def back(x2):
        return _ffn(x2, *ffn2, row(final_norm))

    x1, z, xbc_p, qkv_p, dt = front(x_prompt.reshape(seq, D_MODEL))
    ssm_p, hfin_p = _ssd_prompt(xbc_p, z, dt, *ssd_params)
    outs, lses = [], []
    for g, (win, dil) in enumerate(DILATION_PAIRS):
        o, lse = _attn_prompt(slopes, qkv_p, group=g, dil=dil, seq=seq)
        outs += o
        lses += lse
    y_prompt = back(_outproj(x1, ssm_p, (outs, lses), w_o)).reshape(1, seq, D_MODEL)

    n_s = dec_batch * dec_seq
    x1, z, xbc_s, qkv_s, dt = front(x_sample.reshape(n_s, D_MODEL))
    ssm_s, h_s = _ssd_sample(xbc_s, state_conv[0], z, dt, state_ssm[0].reshape(dec_batch, D_SSM, D_STATE),
                             *ssd_params, expand, n_tok=dec_seq)
    caches = ((cache_k0[0], cache_v0[0]), (cache_k1[0], cache_v1[0]), (cache_k2[0], cache_v2[0]))
    att_s = _attn_sample(slopes, qkv_s, caches, n_tok=dec_seq)
    y_sample = back(_outproj(x1, ssm_s, att_s, w_o)).reshape(dec_batch, dec_seq, D_MODEL)

    head_shape = (H_PER_DIL, ATT_HEAD_DIM)
    p_kv, s_kv = [], []
    for g, (win, dil) in enumerate(DILATION_PAIRS):
        keep = min(win, seq)
        for base in (D_ATT, 2 * D_ATT):
            c0 = base + g * D_GROUP
            p_kv.append(qkv_p[seq - keep:, c0:c0 + D_GROUP].reshape((1, 1, keep) + head_shape))
            s_kv.append(qkv_s[:, c0:c0 + D_GROUP].reshape((1, dec_batch, dec_seq) + head_shape))
    p_ssm = hfin_p.reshape(1, 1, N_SSM_HEADS, SSM_HEAD_DIM, D_STATE)
    p_conv = xbc_p[seq - (CONV_W - 1):].reshape(1, 1, CONV_W - 1, CONV_DIM)
    s_ssm = h_s.reshape(1, dec_batch, N_SSM_HEADS, SSM_HEAD_DIM, D_STATE)
    s_conv = xbc_s.reshape(dec_batch, dec_seq, CONV_DIM)[:, dec_seq - (CONV_W - 1):].reshape(
        1, dec_batch, CONV_W - 1, CONV_DIM)
    return (y_prompt, y_sample, *p_kv, p_ssm, p_conv, *s_kv, s_ssm, s_conv)
```

```python
import functools

import numpy as np
import jax
import jax.numpy as jnp
from jax import lax
from jax.experimental import pallas as pl
from jax.experimental.pallas import tpu as pltpu

F32 = jnp.float32
BF16 = jnp.bfloat16

D_MODEL = 2048
D_FF = 5632
D_ATT = 768
ATT_HEAD_DIM = 64
H_PER_DIL = 4
D_GROUP = H_PER_DIL * ATT_HEAD_DIM
DILATION_PAIRS = ((128, 1), (512, 4), (2048, 16))
N_DIL = len(DILATION_PAIRS)
ATT_BLOCK = 128
D_SSM = 1280
SSM_HEAD_DIM = 64
N_SSM_HEADS = 20
N_SSM_GROUPS = 4
HEADS_PER_SSM_GROUP = 5
D_SSM_GROUP = D_SSM // N_SSM_GROUPS
D_STATE = 128
CONV_W = 4
CONV_DIM = 2304
SSD_CHUNK = 128
RMS_EPS = 1e-6
ALIBI_MAX_EXP = 8.0
N_ATT_HEADS = 12

LANES = 128
SUBLANES = 8
DT_PAD = LANES
IN_HEAD = D_SSM + CONV_DIM
IN_TAIL = N_SSM_HEADS + 3 * D_ATT
IN_TAIL_PAD = -(-IN_TAIL // LANES) * LANES

VMEM_LIMIT = 60 * 1024 * 1024


def _params(n_grid_dims, vmem=VMEM_LIMIT):
    return pltpu.CompilerParams(dimension_semantics=("arbitrary",) * n_grid_dims,
                                vmem_limit_bytes=vmem)


def _silu(x):
    return x * (0.5 + 0.5 * jnp.tanh(0.5 * x))


def _softplus(x):
    return jnp.maximum(x, 0.0) + jnp.log(1.0 + jnp.exp(-jnp.abs(x)))


def _rms_scale(x):
    return x * lax.rsqrt(jnp.mean(x * x, axis=-1, keepdims=True) + RMS_EPS)


def _dot(a, b):
    return jnp.dot(a, b, preferred_element_type=F32)


def _dot_nt(a, b):
    return lax.dot_general(a, b, (((1,), (1,)), ((), ())), preferred_element_type=F32)


def _cumsum_rows(x):
    n = x.shape[0]
    rows = lax.broadcasted_iota(jnp.int32, x.shape, 0)
    shift = 1
    while shift < n:
        x = x + jnp.where(rows >= shift, pltpu.roll(x, shift, 0), 0.0)
        shift *= 2
    return x


def _ffn_body(*refs, n_f, final):
    if final:
        x_ref, g_ref, wg_ref, wu_ref, wd_ref, fg_ref, o_ref, h_ref = refs
    else:
        x_ref, g_ref, wg_ref, wu_ref, wd_ref, o_ref, h_ref = refs
    f = pl.program_id(1)

    @pl.when(f == 0)
    def _():
        h_ref[...] = (_rms_scale(x_ref[...]) * g_ref[...]).astype(BF16)
        o_ref[...] = jnp.zeros_like(o_ref)

    h = h_ref[...]
    gate = _dot(h, wg_ref[...].astype(BF16))
    up = _dot(h, wu_ref[...].astype(BF16))
    act = (_silu(gate) * up).astype(BF16)
    o_ref[...] += _dot(act, wd_ref[...].astype(BF16))

    @pl.when(f == n_f - 1)
    def _():
        y = x_ref[...] + 0.5 * o_ref[...]
        if final:
            y = _rms_scale(y) * fg_ref[...]
        o_ref[...] = y


def _ffn(x, gain, wg, wu, wd, final_gain=None, *, tm=1024, tf=256):
    t = x.shape[0]
    n_f = D_FF // tf
    final = final_gain is not None
    row = lambda i, f: (i, 0)
    const = lambda i, f: (0, 0)
    in_specs = [pl.BlockSpec((tm, D_MODEL), row), pl.BlockSpec((1, D_MODEL), const),
                pl.BlockSpec((D_MODEL, tf), lambda i, f: (0, f)),
                pl.BlockSpec((D_MODEL, tf), lambda i, f: (0, f)),
                pl.BlockSpec((tf, D_MODEL), lambda i, f: (f, 0))]
    args = [x, gain, wg, wu, wd]
    if final:
        in_specs.append(pl.BlockSpec((1, D_MODEL), const))
        args.append(final_gain)
    return pl.pallas_call(
        functools.partial(_ffn_body, n_f=n_f, final=final),
        grid=(t // tm, n_f),
        in_specs=in_specs,
        out_specs=pl.BlockSpec((tm, D_MODEL), row),
        out_shape=jax.ShapeDtypeStruct((t, D_MODEL), F32),
        scratch_shapes=[pltpu.VMEM((tm, D_MODEL), BF16)],
        compiler_params=_params(2),
        name="ffn_final" if final else "ffn",
    )(*args)


def _inproj_body(x_ref, g_ref, wh_ref, wt_ref, z_ref, xbc_ref, qkv_ref, dt_ref):
    h = (_rms_scale(x_ref[...]) * g_ref[...]).astype(BF16)
    z_ref[...] = _dot_nt(h, wh_ref[0:D_SSM, :])
    xbc_ref[...] = _dot_nt(h, wh_ref[D_SSM:IN_HEAD, :])
    tail = _dot_nt(h, wt_ref[...])
    lane = lax.broadcasted_iota(jnp.int32, (tail.shape[0], DT_PAD), 1)
    dt_ref[...] = jnp.where(lane < N_SSM_HEADS, tail[:, 0:DT_PAD], 0.0)
    qkv_ref[...] = tail[:, N_SSM_HEADS:IN_TAIL]


def _inproj(x, gain, w_head, w_tail, *, tm=256):
    t = x.shape[0]
    row = lambda i: (i, 0)
    const = lambda i: (0, 0)
    widths = (D_SSM, CONV_DIM, 3 * D_ATT, DT_PAD)
    resident = lambda n: pl.BlockSpec((n, D_MODEL), const, pipeline_mode=pl.Buffered(1))
    return pl.pallas_call(
        _inproj_body,
        grid=(t // tm,),
        in_specs=[pl.BlockSpec((tm, D_MODEL), row), pl.BlockSpec((1, D_MODEL), const),
                  resident(IN_HEAD), resident(IN_TAIL_PAD)],
        out_specs=[pl.BlockSpec((tm, w), row) for w in widths],
        out_shape=[jax.ShapeDtypeStruct((t, w), F32) for w in widths],
        compiler_params=_params(1),
        name="inproj",
    )(x, gain, w_head, w_tail)


def _pair_blockdiag(x, low_half):
    zero = jnp.zeros_like(x)
    return jnp.concatenate([jnp.where(low_half, x, zero), jnp.where(low_half, zero, x)],
                           axis=0).astype(BF16)


def _ssd_prompt_body(xbc_ref, z_ref, dt_ref, cw_ref, cb_ref, dtb_ref, alog_ref, dsk_ref, nrm_ref,
                     y_ref, hfin_ref, xf_scr, ht_scr):
    c = pl.program_id(0)
    n_c = pl.num_programs(0)
    cl = SSD_CHUNK

    @pl.when(c == 0)
    def _():
        xf_scr[0:SUBLANES, :] = jnp.zeros((SUBLANES, CONV_DIM), F32)
        ht_scr[...] = jnp.zeros_like(ht_scr)

    x = xbc_ref[...]
    xf_scr[SUBLANES:SUBLANES + cl, :] = x
    conv = cb_ref[...]
    for j in range(CONV_W - 1):
        conv = conv + xf_scr[SUBLANES - 3 + j:SUBLANES - 3 + j + cl, :] * cw_ref[j:j + 1, :]
    conv = conv + x * cw_ref[CONV_W - 1:CONV_W, :]
    xf_scr[0:SUBLANES, :] = x[cl - SUBLANES:cl, :]
    xa = _silu(conv)
    xs = xa[:, :D_SSM]
    bm = xa[:, D_SSM:D_SSM + N_SSM_GROUPS * D_STATE]
    cm = xa[:, D_SSM + N_SSM_GROUPS * D_STATE:]

    dt = _softplus(dt_ref[...] + dtb_ref[...])
    a = -jnp.exp(alog_ref[...])
    acum = _cumsum_rows(dt * a)
    w_end = dt * jnp.exp(acum[cl - 1:cl, :] - acum)
    acum_t = acum.T
    dt_t = dt.T
    w_end_t = w_end.T

    rows = lax.broadcasted_iota(jnp.int32, (cl, cl), 0)
    cols = lax.broadcasted_iota(jnp.int32, (cl, cl), 1)
    causal = rows >= cols
    low_half = cols < SSM_HEAD_DIM

    b_g = [bm[:, g * D_STATE:(g + 1) * D_STATE] for g in range(N_SSM_GROUPS)]
    c_g = [cm[:, g * D_STATE:(g + 1) * D_STATE].astype(BF16) for g in range(N_SSM_GROUPS)]
    b_t = [b.T for b in b_g]
    cb = [_dot_nt(c_g[g], b_g[g].astype(BF16)) for g in range(N_SSM_GROUPS)]

    sumsq = [jnp.zeros((cl, 1), F32) for _ in range(N_SSM_GROUPS)]
    for j in range(N_SSM_HEADS // 2):
        lo, hi = j * LANES, (j + 1) * LANES
        heads = (2 * j, 2 * j + 1)
        groups = tuple(h // HEADS_PER_SSM_GROUP for h in heads)
        xs_p = xs[:, lo:hi]
        x_bd = _pair_blockdiag(xs_p, low_half)
        ht_p = ht_scr[:, lo:hi]
        h_bd = _pair_blockdiag(ht_p, low_half)
        g_mats, w_mats, e_cols = [], [], []
        for h, g in zip(heads, groups):
            row_b = jnp.broadcast_to(acum_t[h:h + 1, :], (cl, cl))
            col_b = row_b.T
            decay = jnp.where(causal, jnp.exp(col_b - row_b), 0.0)
            g_mats.append((cb[g] * decay * dt_t[h:h + 1, :]).astype(BF16))
            w_mats.append((b_t[g] * w_end_t[h:h + 1, :]).astype(BF16))
            e_cols.append(jnp.exp(col_b))
        y_diag = _dot(jnp.concatenate(g_mats, axis=1), x_bd)
        y_off = _dot(jnp.concatenate([c_g[groups[0]], c_g[groups[1]]], axis=1), h_bd)
        y_off = y_off * jnp.where(low_half, e_cols[0], e_cols[1])
        state = _dot(jnp.concatenate(w_mats, axis=1), x_bd)
        chunk_decay = jnp.where(low_half[0:1, :], e_cols[0][cl - 1:cl, :], e_cols[1][cl - 1:cl, :])
        ht_scr[:, lo:hi] = ht_p * chunk_decay + state
        y_p = y_diag + y_off + dsk_ref[:, lo:hi] * xs_p
        y_p = y_p * _silu(z_ref[:, lo:hi])
        y_ref[:, lo:hi] = y_p
        sq = y_p * y_p
        sumsq[groups[0]] = sumsq[groups[0]] + jnp.sum(jnp.where(low_half, sq, 0.0), axis=1, keepdims=True)
        sumsq[groups[1]] = sumsq[groups[1]] + jnp.sum(jnp.where(low_half, 0.0, sq), axis=1, keepdims=True)

    scale = [lax.rsqrt(s / D_SSM_GROUP + RMS_EPS) for s in sumsq]
    for j in range(N_SSM_HEADS // 2):
        lo, hi = j * LANES, (j + 1) * LANES
        g0, g1 = (2 * j) // HEADS_PER_SSM_GROUP, (2 * j + 1) // HEADS_PER_SSM_GROUP
        y_ref[:, lo:hi] = y_ref[:, lo:hi] * jnp.where(low_half, scale[g0], scale[g1]) * nrm_ref[:, lo:hi]

    @pl.when(c == n_c - 1)
    def _():
        for j in range(D_SSM // LANES):
            hfin_ref[j * LANES:(j + 1) * LANES, :] = ht_scr[:, j * LANES:(j + 1) * LANES].T


def _ssd_prompt(xbc, z, dt, cw, cb, dtb, alog, dsk, nrm):
    t = xbc.shape[0]
    cl = SSD_CHUNK
    row = lambda c: (c, 0)
    const = lambda c: (0, 0)
    vec = lambda n: pl.BlockSpec((1, n), const)
    return pl.pallas_call(
        _ssd_prompt_body,
        grid=(t // cl,),
        in_specs=[pl.BlockSpec((cl, CONV_DIM), row), pl.BlockSpec((cl, D_SSM), row),
                  pl.BlockSpec((cl, DT_PAD), row), pl.BlockSpec((CONV_W, CONV_DIM), const),
                  vec(CONV_DIM), vec(DT_PAD), vec(DT_PAD), vec(D_SSM), vec(D_SSM)],
        out_specs=[pl.BlockSpec((cl, D_SSM), row), pl.BlockSpec((D_SSM, D_STATE), const)],
        out_shape=[jax.ShapeDtypeStruct((t, D_SSM), F32),
                   jax.ShapeDtypeStruct((D_SSM, D_STATE), F32)],
        scratch_shapes=[pltpu.VMEM((SUBLANES + cl, CONV_DIM), F32),
                        pltpu.VMEM((D_STATE, D_SSM), F32)],
        compiler_params=_params(1),
        name="ssd_prompt",
    )(xbc, z, dt, cw, cb, dtb, alog, dsk, nrm)


def _ssd_sample_body(xbc_ref, cs_ref, z_ref, dt_ref, h0_ref, cw_ref, cb_ref, dtb_ref, alog_ref,
                     dsk_ref, nrm_ref, y_ref, hout_ref, hist_scr, xwt_scr, ecd_scr, eexp_scr, yoff_scr,
                     *, n_tok, n_seq):
    n_rows = n_seq * n_tok
    assert n_rows == LANES

    @pl.when(pl.program_id(0) == 0)
    def _():
        hist_scr[...] = jnp.zeros_like(hist_scr)

    x = xbc_ref[...]
    hist_scr[:, 0:CONV_W - 1, :] = cs_ref[...]
    hist = hist_scr[...].reshape(n_rows, CONV_DIM)
    tok = lax.broadcasted_iota(jnp.int32, (n_rows, 1), 0) % n_tok
    conv = cb_ref[...] + x * cw_ref[CONV_W - 1:CONV_W, :]
    for k in range(1, CONV_W):
        back = CONV_W - 1 - k
        from_hist = hist if back == 0 else pltpu.roll(hist, n_rows - back, 0)
        shifted = jnp.where(tok < k, from_hist, pltpu.roll(x, k, 0))
        conv = conv + shifted * cw_ref[back:back + 1, :]
    xa = _silu(conv)
    xs = xa[:, :D_SSM]
    bm = xa[:, D_SSM:D_SSM + N_SSM_GROUPS * D_STATE]
    cm = xa[:, D_SSM + N_SSM_GROUPS * D_STATE:]

    dt = _softplus(dt_ref[...] + dtb_ref[...])
    dta = dt * -jnp.exp(alog_ref[...])
    tok_l = lax.broadcasted_iota(jnp.int32, (n_rows, LANES), 0) % n_tok
    acum = dta
    to_end = jnp.where(tok_l + 1 < n_tok, pltpu.roll(dta, n_rows - 1, 0), 0.0)
    shift = 1
    while shift < n_tok:
        acum = acum + jnp.where(tok_l >= shift, pltpu.roll(acum, shift, 0), 0.0)
        to_end = to_end + jnp.where(tok_l + shift < n_tok, pltpu.roll(to_end, n_rows - shift, 0), 0.0)
        shift *= 2
    w_end = dt * jnp.exp(to_end)
    acum_t = acum.T
    dt_t = dt.T
    w_end_t = w_end.T

    rows = lax.broadcasted_iota(jnp.int32, (n_rows, n_rows), 0)
    cols = lax.broadcasted_iota(jnp.int32, (n_rows, n_rows), 1)
    causal = (rows // n_tok == cols // n_tok) & (rows >= cols)
    low_half = cols < SSM_HEAD_DIM
    b_g = [bm[:, g * D_STATE:(g + 1) * D_STATE].astype(BF16) for g in range(N_SSM_GROUPS)]
    c_g = [cm[:, g * D_STATE:(g + 1) * D_STATE].astype(BF16) for g in range(N_SSM_GROUPS)]
    cb = [_dot_nt(c_g[g], b_g[g]) for g in range(N_SSM_GROUPS)]

    for j in range(N_SSM_HEADS // 2):
        lo, hi = j * LANES, (j + 1) * LANES
        heads = (2 * j, 2 * j + 1)
        xs_p = xs[:, lo:hi]
        g_mats, e_cols, w_cols = [], [], []
        for h in heads:
            g = h // HEADS_PER_SSM_GROUP
            row_b = jnp.broadcast_to(acum_t[h:h + 1, :], (n_rows, n_rows))
            col_b = row_b.T
            decay = jnp.where(causal, jnp.exp(col_b - row_b), 0.0)
            g_mats.append((cb[g] * decay * dt_t[h:h + 1, :]).astype(BF16))
            e_col = jnp.exp(col_b)
            ecd_scr[h] = e_col
            e_cols.append(e_col)
            w_cols.append(jnp.broadcast_to(w_end_t[h:h + 1, :], (n_rows, n_rows)).T)
        y_diag = _dot(jnp.concatenate(g_mats, axis=1), _pair_blockdiag(xs_p, low_half))
        y_ref[:, lo:hi] = y_diag + dsk_ref[:, lo:hi] * xs_p
        eexp_scr[:, lo:hi] = jnp.where(low_half, e_cols[0], e_cols[1])
        xwt_scr[lo:hi, :] = (xs_p * jnp.where(low_half, w_cols[0], w_cols[1])).T

    col_group = lax.broadcasted_iota(jnp.int32, (n_tok, D_SSM), 1) // D_SSM_GROUP
    lane_seq = lax.broadcasted_iota(jnp.int32, (D_SSM_GROUP, n_rows), 1) // n_tok
    for b in range(n_seq):
        r0 = b * n_tok
        h0 = h0_ref[b]
        c_stack = jnp.concatenate([c_g[g][r0:r0 + n_tok, :] for g in range(N_SSM_GROUPS)], axis=0)
        y_all = _dot_nt(c_stack, h0.astype(BF16))
        y_off = jnp.zeros((n_tok, D_SSM), F32)
        for g in range(N_SSM_GROUPS):
            y_off = jnp.where(col_group == g, y_all[g * n_tok:(g + 1) * n_tok, :], y_off)
        yoff_scr[r0:r0 + n_tok, :] = y_off
        for g in range(N_SSM_GROUPS):
            c0 = g * D_SSM_GROUP
            lhs = jnp.where(lane_seq == b, xwt_scr[c0:c0 + D_SSM_GROUP, :], 0.0).astype(BF16)
            upd = _dot(lhs, b_g[g])
            for e in range(HEADS_PER_SSM_GROUP):
                h = g * HEADS_PER_SSM_GROUP + e
                p0 = h * SSM_HEAD_DIM
                decay = ecd_scr[h, r0 + n_tok - 1:r0 + n_tok, :]
                hout_ref[b, p0:p0 + SSM_HEAD_DIM, :] = (
                    h0[p0:p0 + SSM_HEAD_DIM, :] * decay + upd[e * SSM_HEAD_DIM:(e + 1) * SSM_HEAD_DIM, :])

    y = y_ref[...] + yoff_scr[...] * eexp_scr[...]
    y = y * _silu(z_ref[...])
    sq = y * y
    col_group = lax.broadcasted_iota(jnp.int32, (n_rows, D_SSM), 1) // D_SSM_GROUP
    scale = jnp.zeros((n_rows, D_SSM), F32)
    for g in range(N_SSM_GROUPS):
        ms = jnp.sum(jnp.where(col_group == g, sq, 0.0), axis=1, keepdims=True) / D_SSM_GROUP
        scale = jnp.where(col_group == g, lax.rsqrt(ms + RMS_EPS), scale)
    y_ref[...] = y * scale * nrm_ref[...]


def _ssd_sample(xbc, conv_state, z, dt, h0, cw, cb, dtb, alog, dsk, nrm, *, n_tok):
    t = xbc.shape[0]
    n_b = t // n_tok
    n_seq = LANES // n_tok
    rows = n_seq * n_tok
    row = lambda b: (b, 0)
    const = lambda b: (0, 0)
    vec = lambda n: pl.BlockSpec((1, n), const)
    return pl.pallas_call(
        functools.partial(_ssd_sample_body, n_tok=n_tok, n_seq=n_seq),
        grid=(n_b // n_seq,),
        in_specs=[pl.BlockSpec((rows, CONV_DIM), row),
                  pl.BlockSpec((n_seq, CONV_W - 1, CONV_DIM), lambda b: (b, 0, 0)),
                  pl.BlockSpec((rows, D_SSM), row), pl.BlockSpec((rows, DT_PAD), row),
                  pl.BlockSpec((n_seq, D_SSM, D_STATE), lambda b: (b, 0, 0)),
                  pl.BlockSpec((CONV_W, CONV_DIM), const),
                  vec(CONV_DIM), vec(DT_PAD), vec(DT_PAD), vec(D_SSM), vec(D_SSM)],
        out_specs=[pl.BlockSpec((rows, D_SSM), row),
                   pl.BlockSpec((n_seq, D_SSM, D_STATE), lambda b: (b, 0, 0))],
        out_shape=[jax.ShapeDtypeStruct((t, D_SSM), F32),
                   jax.ShapeDtypeStruct((n_b, D_SSM, D_STATE), F32)],
        scratch_shapes=[pltpu.VMEM((n_seq, n_tok, CONV_DIM), F32),
                        pltpu.VMEM((D_SSM, rows), F32),
                        pltpu.VMEM((N_SSM_HEADS, rows, LANES), F32),
                        pltpu.VMEM((rows, D_SSM), F32),
                        pltpu.VMEM((rows, D_SSM), F32)],
        compiler_params=_params(1),
        name="ssd_sample",
    )(xbc, conv_state, z, dt, h0, cw, cb, dtb, alog, dsk, nrm)


HEADS_PER_HALF = LANES // ATT_HEAD_DIM
N_HALVES = D_GROUP // LANES


ATT_UNITS = 4


def _attn_prompt_body(slope_ref, *refs, dil, group):
    n_in = 5 * N_HALVES
    in_refs, o_refs, l_refs = refs[:n_in], refs[n_in:n_in + N_HALVES], refs[n_in + N_HALVES:]
    i = pl.program_id(0)
    nb = ATT_BLOCK
    qi = lax.broadcasted_iota(jnp.int32, (nb, 2 * nb), 0)
    kj = lax.broadcasted_iota(jnp.int32, (nb, 2 * nb), 1)
    dist = nb + qi - kj
    band = (dist >= 0) & (dist <= nb)
    band_first = band & (kj >= jnp.where(i > 0, 0, nb))
    dist_f = (dist * dil).astype(F32)
    lane_head = lax.broadcasted_iota(jnp.int32, (nb, LANES), 1) // ATT_HEAD_DIM
    for u in range(ATT_UNITS):
        if dil == 1:
            rows = pl.ds(u * nb, nb)
            rows_prev = pl.ds((ATT_UNITS - 1) * nb, nb) if u == 0 else pl.ds((u - 1) * nb, nb)
            valid = band_first if u == 0 else band
        else:
            rows = rows_prev = pl.ds(pl.program_id(1) * ATT_UNITS + u, nb, stride=dil)
            valid = band_first
        for half in range(N_HALVES):
            q_ref, kp_ref, kc_ref, vp_ref, vc_ref = in_refs[5 * half:5 * half + 5]
            if dil == 1 and u > 0:
                kp_ref, vp_ref = kc_ref, vc_ref
            q = q_ref[rows, :]
            k = jnp.concatenate([kp_ref[rows_prev, :], kc_ref[rows, :]], axis=0).astype(BF16)
            v = jnp.concatenate([vp_ref[rows_prev, :], vc_ref[rows, :]], axis=0).astype(BF16)
            out = jnp.zeros((nb, LANES), F32)
            lse = jnp.zeros((nb, LANES), F32)
            for hh in range(HEADS_PER_HALF):
                mine = lane_head == hh
                slope = slope_ref[group * H_PER_DIL + half * HEADS_PER_HALF + hh]
                qh = jnp.where(mine, q, 0.0).astype(BF16)
                s = _dot_nt(qh, k) * (ATT_HEAD_DIM ** -0.5)
                s = jnp.where(valid, s - slope * dist_f, -jnp.inf)
                m = jnp.max(s, axis=1, keepdims=True)
                p = jnp.exp(s - m)
                l = jnp.sum(p, axis=1, keepdims=True)
                pv = _dot(p.astype(BF16), v)
                out = jnp.where(mine, pv / l, out)
                lse = jnp.where(mine, m + jnp.log(l), lse)
            o_refs[half][rows, :] = out
            l_refs[half][rows, :] = lse


def _attn_prompt(slopes, qkv, *, group, dil, seq):
    assert dil == 1 or dil % ATT_UNITS == 0
    blk_rows = max(dil, ATT_UNITS) * ATT_BLOCK
    n_blk = seq // blk_rows
    n_res_steps = max(dil // ATT_UNITS, 1)
    blk = (blk_rows, LANES)
    cur = lambda c: pl.BlockSpec(blk, lambda i, r: (i, c))
    prev = lambda c: pl.BlockSpec(blk, lambda i, r: (jnp.maximum(i - 1, 0), c))
    in_specs = [pl.BlockSpec(memory_space=pltpu.SMEM)]
    for half in range(N_HALVES):
        q_col, k_col, v_col = (N_HALVES * (part * N_DIL + group) + half for part in range(3))
        in_specs += [cur(q_col), prev(k_col), cur(k_col), prev(v_col), cur(v_col)]
    out_blk = pl.BlockSpec(blk, lambda i, r: (i, 0))
    res = pl.pallas_call(
        functools.partial(_attn_prompt_body, dil=dil, group=group),
        grid=(n_blk, n_res_steps),
        in_specs=in_specs,
        out_specs=[out_blk] * (2 * N_HALVES),
        out_shape=[jax.ShapeDtypeStruct((seq, LANES), F32)] * (2 * N_HALVES),
        compiler_params=_params(2),
        name=f"attn_prompt_g{group}",
    )(slopes, *([qkv] * (5 * N_HALVES)))
    return res[:N_HALVES], res[N_HALVES:]


def _attn_sample_body(slope_ref, qkv_ref, k0_ref, v0_ref, k1_ref, v1_ref, k2_ref, v2_ref, att_ref,
                      *, n_tok, n_seq):
    n_rows = H_PER_DIL * n_tok
    kv_refs = ((k0_ref, v0_ref), (k1_ref, v1_ref), (k2_ref, v2_ref))
    row2 = lax.broadcasted_iota(jnp.int32, (n_rows, D_GROUP), 0)
    lane2 = lax.broadcasted_iota(jnp.int32, (n_rows, D_GROUP), 1)
    own_head = (lane2 // ATT_HEAD_DIM) == (row2 // n_tok)
    row1 = lax.broadcasted_iota(jnp.int32, (n_rows, 1), 0)
    tok1 = row1 % n_tok
    head1 = row1 // n_tok
    lane_head8 = lax.broadcasted_iota(jnp.int32, (n_tok, D_GROUP), 1) // ATT_HEAD_DIM
    scale = ATT_HEAD_DIM ** -0.5
    for b in range(n_seq):
        rows = slice(b * n_tok, (b + 1) * n_tok)
        _attn_sample_one(slope_ref, qkv_ref[rows, :], [(k[b], v[b]) for k, v in kv_refs],
                         att_ref.at[rows, :], n_tok, n_rows, own_head, tok1, head1, lane_head8, scale)


def _attn_sample_one(slope_ref, qkv, kv, att_ref, n_tok, n_rows, own_head, tok1, head1, lane_head8, scale):
    outs, lses = [], []
    for g, (win, dil) in enumerate(DILATION_PAIRS):
        k_buf, v_buf = kv[g]
        q = qkv[:, g * D_GROUP:(g + 1) * D_GROUP]
        k_new = qkv[:, D_ATT + g * D_GROUP:D_ATT + (g + 1) * D_GROUP]
        v_new = qkv[:, 2 * D_ATT + g * D_GROUP:2 * D_ATT + (g + 1) * D_GROUP]
        q_rows = jnp.where(own_head, jnp.concatenate([q] * H_PER_DIL, axis=0), 0.0)
        slope_1 = jnp.zeros((n_rows, 1), F32)
        for h in range(H_PER_DIL):
            slope_1 = jnp.where(head1 == h, slope_ref[g * H_PER_DIL + h], slope_1)

        s_buf = _dot(q_rows.astype(BF16), k_buf.astype(BF16)) * scale
        pos = lax.broadcasted_iota(jnp.int32, (n_rows, win), 1)
        tok_s = lax.broadcasted_iota(jnp.int32, (n_rows, win), 0) % n_tok
        dist = win + tok_s - pos
        on_grid = (pos >= tok_s) & (jnp.bitwise_and(dist, dil - 1) == 0)
        s_buf = jnp.where(on_grid, s_buf - slope_1 * dist.astype(F32), -jnp.inf)
        m = jnp.max(s_buf, axis=1, keepdims=True)
        s_new = []
        for tp in range(n_tok):
            sn = jnp.sum(q_rows * k_new[tp:tp + 1, :], axis=1, keepdims=True) * scale
            ok = (tok1 >= tp) & (jnp.bitwise_and(tok1 - tp, dil - 1) == 0)
            sn = jnp.where(ok, sn - slope_1 * (tok1 - tp).astype(F32), -jnp.inf)
            s_new.append(sn)
            m = jnp.maximum(m, sn)
        p_buf = jnp.exp(s_buf - m)
        l = jnp.sum(p_buf, axis=1, keepdims=True)
        o_rows = _dot_nt(p_buf.astype(BF16), v_buf.astype(BF16))
        for tp in range(n_tok):
            p_new = jnp.exp(s_new[tp] - m)
            l = l + p_new
            o_rows = o_rows + p_new * v_new[tp:tp + 1, :]
        o_rows = o_rows / l
        lse_rows = m + jnp.log(l)
        o = jnp.zeros((n_tok, D_GROUP), F32)
        lse = jnp.zeros((n_tok, D_GROUP), F32)
        for h in range(H_PER_DIL):
            o = jnp.where(lane_head8 == h, o_rows[h * n_tok:(h + 1) * n_tok, :], o)
            lse = jnp.where(lane_head8 == h, lse_rows[h * n_tok:(h + 1) * n_tok, :], lse)
        outs.append(o)
        lses.append(lse)

    m = jnp.maximum(jnp.maximum(lses[0], lses[1]), lses[2])
    e = [jnp.exp(x - m) for x in lses]
    den = e[0] + e[1] + e[2]
    for g in range(N_DIL):
        att_ref[:, g * D_GROUP:(g + 1) * D_GROUP] = outs[g] * (e[g] / den)


def _attn_sample(slopes, qkv, caches, *, n_tok, n_seq=2):
    t = qkv.shape[0]
    n_b = t // n_tok
    in_specs = [pl.BlockSpec(memory_space=pltpu.SMEM),
                pl.BlockSpec((n_seq * n_tok, 3 * D_ATT), lambda b: (b, 0))]
    args = [slopes, qkv]
    for g, (win, dil) in enumerate(DILATION_PAIRS):
        assert dil & (dil - 1) == 0, "dilations must be powers of two"
        for cache in caches[g]:
            assert cache.shape[1] == win, "window buffers must hold a full window"
            args.append(jnp.transpose(cache, (0, 2, 3, 1)).reshape(n_b, D_GROUP, win))
            in_specs.append(pl.BlockSpec((n_seq, D_GROUP, win), lambda b: (b, 0, 0)))
    return pl.pallas_call(
        functools.partial(_attn_sample_body, n_tok=n_tok, n_seq=n_seq),
        grid=(n_b // n_seq,),
        in_specs=in_specs,
        out_specs=pl.BlockSpec((n_seq * n_tok, D_ATT), lambda b: (b, 0)),
        out_shape=jax.ShapeDtypeStruct((t, D_ATT), F32),
        compiler_params=_params(1),
        name="attn_sample",
    )(*args)


def _outproj_body(*refs, combine):
    n_pieces = N_DIL * N_HALVES
    if combine:
        x_ref, ssm_ref = refs[:2]
        o_refs, l_refs = refs[2:2 + n_pieces], refs[2 + n_pieces:2 + 2 * n_pieces]
        w_ref, out_ref = refs[2 + 2 * n_pieces:]
        att = [None] * n_pieces
        for half in range(N_HALVES):
            idx = [g * N_HALVES + half for g in range(N_DIL)]
            lses = [l_refs[p][...] for p in idx]
            m = jnp.maximum(jnp.maximum(lses[0], lses[1]), lses[2])
            e = [jnp.exp(x - m) for x in lses]
            den = e[0] + e[1] + e[2]
            for p, w in zip(idx, e):
                att[p] = o_refs[p][...] * (w / den)
    else:
        x_ref, ssm_ref, att_ref, w_ref, out_ref = refs
        att = [att_ref[:, p * LANES:(p + 1) * LANES] for p in range(n_pieces)]
    mixed = jnp.concatenate([ssm_ref[...]] + att, axis=1).astype(BF16)
    out_ref[...] = x_ref[...] + _dot(mixed, w_ref[...])


def _outproj(x, ssm, att, w_out, *, tm=512):
    t = x.shape[0]
    row = lambda i: (i, 0)
    combine = isinstance(att, (tuple, list))
    att_args = list(att[0]) + list(att[1]) if combine else [att]
    att_specs = [pl.BlockSpec((tm, a.shape[1]), row) for a in att_args]
    return pl.pallas_call(
        functools.partial(_outproj_body, combine=combine),
        grid=(t // tm,),
        in_specs=[pl.BlockSpec((tm, D_MODEL), row), pl.BlockSpec((tm, D_SSM), row)] + att_specs
                 + [pl.BlockSpec((D_MODEL, D_MODEL), lambda i: (0, 0), pipeline_mode=pl.Buffered(1))],
        out_specs=pl.BlockSpec((tm, D_MODEL), row),
        out_shape=jax.ShapeDtypeStruct((t, D_MODEL), F32),
        compiler_params=_params(1),
        name="outproj_prompt" if combine else "outproj_sample",
    )(x, ssm, *att_args, w_out)


def kernel(x_prompt, x_sample, cache_k0, cache_v0, cache_k1, cache_v1, cache_k2, cache_v2, state_ssm, state_conv, ffn1_norm, ffn1_w_gate, ffn1_w_up, ffn1_w_down, mix_norm, w_in, conv_w, conv_b, dt_bias, a_log, d_skip, ssm_norm, w_out, ffn2_norm, ffn2_w_gate, ffn2_w_up, ffn2_w_down, final_norm):
    assert w_in.shape[0] == 1, "single layer"
    batch, seq, _ = x_prompt.shape
    dec_batch, dec_seq, _ = x_sample.shape
    assert batch == 1 and dec_seq == SUBLANES

    row = lambda v: v.reshape(1, -1).astype(F32)
    pad_heads = lambda v: jnp.pad(v.reshape(1, -1).astype(F32), ((0, 0), (0, DT_PAD - N_SSM_HEADS)))
    w_in_t = jnp.transpose(w_in[0])
    w_head = w_in_t[:IN_HEAD].astype(BF16)
    w_tail = jnp.pad(w_in_t[IN_HEAD:], ((0, IN_TAIL_PAD - IN_TAIL), (0, 0))).astype(BF16)
    ffn1 = (row(ffn1_norm[0]), ffn1_w_gate[0], ffn1_w_up[0], ffn1_w_down[0])
    ffn2 = (row(ffn2_norm[0]), ffn2_w_gate[0], ffn2_w_up[0], ffn2_w_down[0])
    w_o = w_out[0].astype(BF16)
    ssd_params = (conv_w[0].astype(F32), row(conv_b[0]), pad_heads(dt_bias[0]), pad_heads(a_log[0]),
                  row(jnp.repeat(d_skip[0], SSM_HEAD_DIM)), row(ssm_norm[0]))
    slopes = jnp.exp2(-ALIBI_MAX_EXP * jnp.arange(1, N_ATT_HEADS + 1, dtype=F32) / N_ATT_HEADS)

    def front(x):
        x1 = _ffn(x, *ffn1)
        z, xbc, qkv, dt = _inproj(x1, row(mix_norm[0]), w_head, w_tail)
        return x1, z, xbc, qkv, dt

    def back(x2):
        return _ffn(x2, *ffn2, row(final_norm))

    x1, z, xbc_p, qkv_p, dt = front(x_prompt.reshape(seq, D_MODEL))
    ssm_p, hfin_p = _ssd_prompt(xbc_p, z, dt, *ssd_params)
    outs, lses = [], []
    for g, (win, dil) in enumerate(DILATION_PAIRS):
        o, lse = _attn_prompt(slopes, qkv_p, group=g, dil=dil, seq=seq)
        outs += o
        lses += lse
    y_prompt = back(_outproj(x1, ssm_p, (outs, lses), w_o)).reshape(1, seq, D_MODEL)

    n_s = dec_batch * dec_seq
    x1, z, xbc_s, qkv_s, dt = front(x_sample.reshape(n_s, D_MODEL))
    ssm_s, h_s = _ssd_sample(xbc_s, state_conv[0], z, dt, state_ssm[0].reshape(dec_batch, D_SSM, D_STATE),
                             *ssd_params, n_tok=dec_seq)
    caches = ((cache_k0[0], cache_v0[0]), (cache_k1[0], cache_v1[0]), (cache_k2[0], cache_v2[0]))
    att_s = _attn_sample(slopes, qkv_s, caches, n_tok=dec_seq)
    y_sample = back(_outproj(x1, ssm_s, att_s, w_o)).reshape(dec_batch, dec_seq, D_MODEL)

    head_shape = (H_PER_DIL, ATT_HEAD_DIM)
    p_kv, s_kv = [], []
    for g, (win, dil) in enumerate(DILATION_PAIRS):
        keep = min(win, seq)
        for base in (D_ATT, 2 * D_ATT):
            c0 = base + g * D_GROUP
            p_kv.append(qkv_p[seq - keep:, c0:c0 + D_GROUP].reshape((1, 1, keep) + head_shape))
            s_kv.append(qkv_s[:, c0:c0 + D_GROUP].reshape((1, dec_batch, dec_seq) + head_shape))
    p_ssm = hfin_p.reshape(1, 1, N_SSM_HEADS, SSM_HEAD_DIM, D_STATE)
    p_conv = xbc_p[seq - (CONV_W - 1):].reshape(1, 1, CONV_W - 1, CONV_DIM)
    s_ssm = h_s.reshape(1, dec_batch, N_SSM_HEADS, SSM_HEAD_DIM, D_STATE)
    s_conv = xbc_s.reshape(dec_batch, dec_seq, CONV_DIM)[:, dec_seq - (CONV_W - 1):].reshape(
        1, dec_batch, CONV_W - 1, CONV_DIM)
    return (y_prompt, y_sample, *p_kv, p_ssm, p_conv, *s_kv, s_ssm, s_conv)
```

```python
import functools

import numpy as np
import jax
import jax.numpy as jnp
from jax import lax
from jax.experimental import pallas as pl
from jax.experimental.pallas import tpu as pltpu

F32 = jnp.float32
BF16 = jnp.bfloat16

D_MODEL = 2048
D_FF = 5632
D_ATT = 768
ATT_HEAD_DIM = 64
H_PER_DIL = 4
D_GROUP = H_PER_DIL * ATT_HEAD_DIM
DILATION_PAIRS = ((128, 1), (512, 4), (2048, 16))
N_DIL = len(DILATION_PAIRS)
ATT_BLOCK = 128
D_SSM = 1280
SSM_HEAD_DIM = 64
N_SSM_HEADS = 20
N_SSM_GROUPS = 4
HEADS_PER_SSM_GROUP = 5
D_SSM_GROUP = D_SSM // N_SSM_GROUPS
D_STATE = 128
CONV_W = 4
CONV_DIM = 2304
SSD_CHUNK = 128
RMS_EPS = 1e-6
ALIBI_MAX_EXP = 8.0
N_ATT_HEADS = 12

LANES = 128
SUBLANES = 8
DT_PAD = LANES
IN_HEAD = D_SSM + CONV_DIM
IN_TAIL = N_SSM_HEADS + 3 * D_ATT

VMEM_LIMIT = 60 * 1024 * 1024


def _params(n_grid_dims, vmem=VMEM_LIMIT):
    return pltpu.CompilerParams(dimension_semantics=("arbitrary",) * n_grid_dims,
                                vmem_limit_bytes=vmem)


def _silu(x):
    return x * (0.5 + 0.5 * jnp.tanh(0.5 * x))


def _softplus(x):
    return jnp.maximum(x, 0.0) + jnp.log(1.0 + jnp.exp(-jnp.abs(x)))


def _rms_scale(x):
    return x * lax.rsqrt(jnp.mean(x * x, axis=-1, keepdims=True) + RMS_EPS)


def _dot(a, b):
    return jnp.dot(a, b, preferred_element_type=F32)


def _dot_nt(a, b):
    return lax.dot_general(a, b, (((1,), (1,)), ((), ())), preferred_element_type=F32)


def _cumsum_rows(x):
    n = x.shape[0]
    rows = lax.broadcasted_iota(jnp.int32, x.shape, 0)
    shift = 1
    while shift < n:
        x = x + jnp.where(rows >= shift, pltpu.roll(x, shift, 0), 0.0)
        shift *= 2
    return x


FFN_ROW_CHUNKS = 4


def _ffn_body(*refs, n_f, final):
    if final:
        x_ref, g_ref, wg_ref, wu_ref, wd_ref, fg_ref, o_ref, h_ref = refs
    else:
        x_ref, g_ref, wg_ref, wu_ref, wd_ref, o_ref, h_ref = refs
    f = pl.program_id(1)
    tm = x_ref.shape[0]
    assert n_f >= 2 and tm % FFN_ROW_CHUNKS == 0

    def weights():
        return wg_ref[...].astype(BF16), wu_ref[...].astype(BF16), wd_ref[...].astype(BF16)

    def partial_out(h, w):
        gate = _dot(h, w[0])
        up = _dot(h, w[1])
        act = (_silu(gate) * up).astype(BF16)
        return _dot(act, w[2])

    chunks = [slice(c * tm // FFN_ROW_CHUNKS, (c + 1) * tm // FFN_ROW_CHUNKS) for c in range(FFN_ROW_CHUNKS)]

    @pl.when(f == 0)
    def _():
        w = weights()
        for rows in chunks:
            h = (_rms_scale(x_ref[rows, :]) * g_ref[...]).astype(BF16)
            h_ref[rows, :] = h
            o_ref[rows, :] = partial_out(h, w)

    @pl.when((f > 0) & (f < n_f - 1))
    def _():
        o_ref[...] += partial_out(h_ref[...], weights())

    @pl.when(f == n_f - 1)
    def _():
        w = weights()
        for rows in chunks:
            y = x_ref[rows, :] + 0.5 * (o_ref[rows, :] + partial_out(h_ref[rows, :], w))
            if final:
                y = _rms_scale(y) * fg_ref[...]
            o_ref[rows, :] = y


def _ffn(x, gain, wg, wu, wd, final_gain=None, *, tm=1024, tf=256):
    t = x.shape[0]
    n_f = D_FF // tf
    final = final_gain is not None
    row = lambda i, f: (i, 0)
    const = lambda i, f: (0, 0)
    in_specs = [pl.BlockSpec((tm, D_MODEL), row), pl.BlockSpec((1, D_MODEL), const),
                pl.BlockSpec((D_MODEL, tf), lambda i, f: (0, f)),
                pl.BlockSpec((D_MODEL, tf), lambda i, f: (0, f)),
                pl.BlockSpec((tf, D_MODEL), lambda i, f: (f, 0))]
    args = [x, gain, wg, wu, wd]
    if final:
        in_specs.append(pl.BlockSpec((1, D_MODEL), const))
        args.append(final_gain)
    return pl.pallas_call(
        functools.partial(_ffn_body, n_f=n_f, final=final),
        grid=(t // tm, n_f),
        in_specs=in_specs,
        out_specs=pl.BlockSpec((tm, D_MODEL), row),
        out_shape=jax.ShapeDtypeStruct((t, D_MODEL), F32),
        scratch_shapes=[pltpu.VMEM((tm, D_MODEL), BF16)],
        compiler_params=_params(2),
        name="ffn_final" if final else "ffn",
    )(*args)


def _inproj_body(x_ref, g_ref, w_ref, z_ref, xbc_ref, qkv_ref, dt_ref):
    h = (_rms_scale(x_ref[...]) * g_ref[...]).astype(BF16)
    z_ref[...] = _dot_nt(h, w_ref[0:D_SSM, :])
    xbc_ref[...] = _dot_nt(h, w_ref[D_SSM:IN_HEAD, :])
    tail = _dot_nt(h, w_ref[IN_HEAD:IN_HEAD + IN_TAIL, :])
    lane = lax.broadcasted_iota(jnp.int32, (tail.shape[0], DT_PAD), 1)
    dt_ref[...] = jnp.where(lane < N_SSM_HEADS, tail[:, 0:DT_PAD], 0.0)
    qkv_ref[...] = tail[:, N_SSM_HEADS:IN_TAIL]


def _inproj(x, gain, w_t, *, tm=256):
    t = x.shape[0]
    row = lambda i: (i, 0)
    const = lambda i: (0, 0)
    widths = (D_SSM, CONV_DIM, 3 * D_ATT, DT_PAD)
    return pl.pallas_call(
        _inproj_body,
        grid=(t // tm,),
        in_specs=[pl.BlockSpec((tm, D_MODEL), row), pl.BlockSpec((1, D_MODEL), const),
                  pl.BlockSpec((IN_HEAD + IN_TAIL, D_MODEL), const, pipeline_mode=pl.Buffered(1))],
        out_specs=[pl.BlockSpec((tm, w), row) for w in widths],
        out_shape=[jax.ShapeDtypeStruct((t, w), F32) for w in widths],
        compiler_params=_params(1),
        name="inproj",
    )(x, gain, w_t)


def _pair_blockdiag(x, low_half):
    zero = jnp.zeros_like(x)
    return jnp.concatenate([jnp.where(low_half, x, zero), jnp.where(low_half, zero, x)],
                           axis=0).astype(BF16)


def _ssd_prompt_body(xbc_ref, z_ref, dt_ref, cw_ref, cb_ref, dtb_ref, alog_ref, dsk_ref, nrm_ref,
                     y_ref, hfin_ref, xf_scr, ht_scr):
    c = pl.program_id(0)
    n_c = pl.num_programs(0)
    cl = SSD_CHUNK

    @pl.when(c == 0)
    def _():
        xf_scr[0:SUBLANES, :] = jnp.zeros((SUBLANES, CONV_DIM), F32)
        ht_scr[...] = jnp.zeros_like(ht_scr)

    x = xbc_ref[...]
    xf_scr[SUBLANES:SUBLANES + cl, :] = x
    conv = cb_ref[...]
    for j in range(CONV_W - 1):
        conv = conv + xf_scr[SUBLANES - 3 + j:SUBLANES - 3 + j + cl, :] * cw_ref[j:j + 1, :]
    conv = conv + x * cw_ref[CONV_W - 1:CONV_W, :]
    xf_scr[0:SUBLANES, :] = x[cl - SUBLANES:cl, :]
    xa = _silu(conv)
    xs = xa[:, :D_SSM]
    bm = xa[:, D_SSM:D_SSM + N_SSM_GROUPS * D_STATE]
    cm = xa[:, D_SSM + N_SSM_GROUPS * D_STATE:]

    dt = _softplus(dt_ref[...] + dtb_ref[...])
    a = -jnp.exp(alog_ref[...])
    acum = _cumsum_rows(dt * a)
    w_end = dt * jnp.exp(acum[cl - 1:cl, :] - acum)
    acum_t = acum.T
    dt_t = dt.T
    w_end_t = w_end.T

    rows = lax.broadcasted_iota(jnp.int32, (cl, cl), 0)
    cols = lax.broadcasted_iota(jnp.int32, (cl, cl), 1)
    causal = rows >= cols
    low_half = cols < SSM_HEAD_DIM

    b_g = [bm[:, g * D_STATE:(g + 1) * D_STATE] for g in range(N_SSM_GROUPS)]
    c_g = [cm[:, g * D_STATE:(g + 1) * D_STATE].astype(BF16) for g in range(N_SSM_GROUPS)]
    b_t = [b.T for b in b_g]
    cb = [_dot_nt(c_g[g], b_g[g].astype(BF16)) for g in range(N_SSM_GROUPS)]

    sumsq = [jnp.zeros((cl, 1), F32) for _ in range(N_SSM_GROUPS)]
    for j in range(N_SSM_HEADS // 2):
        lo, hi = j * LANES, (j + 1) * LANES
        heads = (2 * j, 2 * j + 1)
        groups = tuple(h // HEADS_PER_SSM_GROUP for h in heads)
        xs_p = xs[:, lo:hi]
        x_bd = _pair_blockdiag(xs_p, low_half)
        ht_p = ht_scr[:, lo:hi]
        h_bd = _pair_blockdiag(ht_p, low_half)
        g_mats, w_mats, e_cols = [], [], []
        for h, g in zip(heads, groups):
            row_b = jnp.broadcast_to(acum_t[h:h + 1, :], (cl, cl))
            col_b = row_b.T
            decay = jnp.where(causal, jnp.exp(col_b - row_b), 0.0)
            g_mats.append((cb[g] * decay * dt_t[h:h + 1, :]).astype(BF16))
            w_mats.append((b_t[g] * w_end_t[h:h + 1, :]).astype(BF16))
            e_cols.append(jnp.exp(col_b))
        y_diag = _dot(jnp.concatenate(g_mats, axis=1), x_bd)
        y_off = _dot(jnp.concatenate([c_g[groups[0]], c_g[groups[1]]], axis=1), h_bd)
        y_off = y_off * jnp.where(low_half, e_cols[0], e_cols[1])
        state = _dot(jnp.concatenate(w_mats, axis=1), x_bd)
        chunk_decay = jnp.where(low_half[0:1, :], e_cols[0][cl - 1:cl, :], e_cols[1][cl - 1:cl, :])
        ht_scr[:, lo:hi] = ht_p * chunk_decay + state
        y_p = y_diag + y_off + dsk_ref[:, lo:hi] * xs_p
        y_p = y_p * _silu(z_ref[:, lo:hi])
        y_ref[:, lo:hi] = y_p
        sq = y_p * y_p
        sumsq[groups[0]] = sumsq[groups[0]] + jnp.sum(jnp.where(low_half, sq, 0.0), axis=1, keepdims=True)
        sumsq[groups[1]] = sumsq[groups[1]] + jnp.sum(jnp.where(low_half, 0.0, sq), axis=1, keepdims=True)

    scale = [lax.rsqrt(s / D_SSM_GROUP + RMS_EPS) for s in sumsq]
    for j in range(N_SSM_HEADS // 2):
        lo, hi = j * LANES, (j + 1) * LANES
        g0, g1 = (2 * j) // HEADS_PER_SSM_GROUP, (2 * j + 1) // HEADS_PER_SSM_GROUP
        y_ref[:, lo:hi] = y_ref[:, lo:hi] * jnp.where(low_half, scale[g0], scale[g1]) * nrm_ref[:, lo:hi]

    @pl.when(c == n_c - 1)
    def _():
        for j in range(D_SSM // LANES):
            hfin_ref[j * LANES:(j + 1) * LANES, :] = ht_scr[:, j * LANES:(j + 1) * LANES].T


def _ssd_prompt(xbc, z, dt, cw, cb, dtb, alog, dsk, nrm):
    t = xbc.shape[0]
    cl = SSD_CHUNK
    row = lambda c: (c, 0)
    const = lambda c: (0, 0)
    vec = lambda n: pl.BlockSpec((1, n), const)
    return pl.pallas_call(
        _ssd_prompt_body,
        grid=(t // cl,),
        in_specs=[pl.BlockSpec((cl, CONV_DIM), row), pl.BlockSpec((cl, D_SSM), row),
                  pl.BlockSpec((cl, DT_PAD), row), pl.BlockSpec((CONV_W, CONV_DIM), const),
                  vec(CONV_DIM), vec(DT_PAD), vec(DT_PAD), vec(D_SSM), vec(D_SSM)],
        out_specs=[pl.BlockSpec((cl, D_SSM), row), pl.BlockSpec((D_SSM, D_STATE), const)],
        out_shape=[jax.ShapeDtypeStruct((t, D_SSM), F32),
                   jax.ShapeDtypeStruct((D_SSM, D_STATE), F32)],
        scratch_shapes=[pltpu.VMEM((SUBLANES + cl, CONV_DIM), F32),
                        pltpu.VMEM((D_STATE, D_SSM), F32)],
        compiler_params=_params(1),
        name="ssd_prompt",
    )(xbc, z, dt, cw, cb, dtb, alog, dsk, nrm)


def _ssd_sample_body(xbc_ref, cs_ref, z_ref, dt_ref, h0_ref, cw_ref, cb_ref, dtb_ref, alog_ref,
                     dsk_ref, nrm_ref, y_ref, hout_ref, hist_scr, xwt_scr, ecd_scr, eexp_scr, yoff_scr,
                     *, n_tok, n_seq):
    n_rows = n_seq * n_tok
    assert n_rows == LANES

    @pl.when(pl.program_id(0) == 0)
    def _():
        hist_scr[...] = jnp.zeros_like(hist_scr)

    x = xbc_ref[...]
    hist_scr[:, 0:CONV_W - 1, :] = cs_ref[...]
    hist = hist_scr[...].reshape(n_rows, CONV_DIM)
    tok = lax.broadcasted_iota(jnp.int32, (n_rows, 1), 0) % n_tok
    conv = cb_ref[...] + x * cw_ref[CONV_W - 1:CONV_W, :]
    for k in range(1, CONV_W):
        back = CONV_W - 1 - k
        from_hist = hist if back == 0 else pltpu.roll(hist, n_rows - back, 0)
        shifted = jnp.where(tok < k, from_hist, pltpu.roll(x, k, 0))
        conv = conv + shifted * cw_ref[back:back + 1, :]
    xa = _silu(conv)
    xs = xa[:, :D_SSM]
    bm = xa[:, D_SSM:D_SSM + N_SSM_GROUPS * D_STATE]
    cm = xa[:, D_SSM + N_SSM_GROUPS * D_STATE:]

    dt = _softplus(dt_ref[...] + dtb_ref[...])
    dta = dt * -jnp.exp(alog_ref[...])
    tok_l = lax.broadcasted_iota(jnp.int32, (n_rows, LANES), 0) % n_tok
    acum = dta
    to_end = jnp.where(tok_l + 1 < n_tok, pltpu.roll(dta, n_rows - 1, 0), 0.0)
    shift = 1
    while shift < n_tok:
        acum = acum + jnp.where(tok_l >= shift, pltpu.roll(acum, shift, 0), 0.0)
        to_end = to_end + jnp.where(tok_l + shift < n_tok, pltpu.roll(to_end, n_rows - shift, 0), 0.0)
        shift *= 2
    w_end = dt * jnp.exp(to_end)
    acum_t = acum.T
    dt_t = dt.T
    w_end_t = w_end.T

    rows = lax.broadcasted_iota(jnp.int32, (n_rows, n_rows), 0)
    cols = lax.broadcasted_iota(jnp.int32, (n_rows, n_rows), 1)
    causal = (rows // n_tok == cols // n_tok) & (rows >= cols)
    low_half = cols < SSM_HEAD_DIM
    b_g = [bm[:, g * D_STATE:(g + 1) * D_STATE].astype(BF16) for g in range(N_SSM_GROUPS)]
    c_g = [cm[:, g * D_STATE:(g + 1) * D_STATE].astype(BF16) for g in range(N_SSM_GROUPS)]
    cb = [_dot_nt(c_g[g], b_g[g]) for g in range(N_SSM_GROUPS)]

    for j in range(N_SSM_HEADS // 2):
        lo, hi = j * LANES, (j + 1) * LANES
        heads = (2 * j, 2 * j + 1)
        xs_p = xs[:, lo:hi]
        g_mats, e_cols, w_cols = [], [], []
        for h in heads:
            g = h // HEADS_PER_SSM_GROUP
            row_b = jnp.broadcast_to(acum_t[h:h + 1, :], (n_rows, n_rows))
            col_b = row_b.T
            decay = jnp.where(causal, jnp.exp(col_b - row_b), 0.0)
            g_mats.append((cb[g] * decay * dt_t[h:h + 1, :]).astype(BF16))
            e_col = jnp.exp(col_b)
            ecd_scr[h] = e_col
            e_cols.append(e_col)
            w_cols.append(jnp.broadcast_to(w_end_t[h:h + 1, :], (n_rows, n_rows)).T)
        y_diag = _dot(jnp.concatenate(g_mats, axis=1), _pair_blockdiag(xs_p, low_half))
        y_ref[:, lo:hi] = y_diag + dsk_ref[:, lo:hi] * xs_p
        eexp_scr[:, lo:hi] = jnp.where(low_half, e_cols[0], e_cols[1])
        xwt_scr[lo:hi, :] = (xs_p * jnp.where(low_half, w_cols[0], w_cols[1])).T

    col_group = lax.broadcasted_iota(jnp.int32, (n_tok, D_SSM), 1) // D_SSM_GROUP
    lane_seq = lax.broadcasted_iota(jnp.int32, (D_SSM_GROUP, n_rows), 1) // n_tok
    for b in range(n_seq):
        r0 = b * n_tok
        h0 = h0_ref[b]
        c_stack = jnp.concatenate([c_g[g][r0:r0 + n_tok, :] for g in range(N_SSM_GROUPS)], axis=0)
        y_all = _dot_nt(c_stack, h0.astype(BF16))
        y_off = jnp.zeros((n_tok, D_SSM), F32)
        for g in range(N_SSM_GROUPS):
            y_off = jnp.where(col_group == g, y_all[g * n_tok:(g + 1) * n_tok, :], y_off)
        yoff_scr[r0:r0 + n_tok, :] = y_off
        for g in range(N_SSM_GROUPS):
            c0 = g * D_SSM_GROUP
            lhs = jnp.where(lane_seq == b, xwt_scr[c0:c0 + D_SSM_GROUP, :], 0.0).astype(BF16)
            upd = _dot(lhs, b_g[g])
            for e in range(HEADS_PER_SSM_GROUP):
                h = g * HEADS_PER_SSM_GROUP + e
                p0 = h * SSM_HEAD_DIM
                decay = ecd_scr[h, r0 + n_tok - 1:r0 + n_tok, :]
                hout_ref[b, p0:p0 + SSM_HEAD_DIM, :] = (
                    h0[p0:p0 + SSM_HEAD_DIM, :] * decay + upd[e * SSM_HEAD_DIM:(e + 1) * SSM_HEAD_DIM, :])

    y = y_ref[...] + yoff_scr[...] * eexp_scr[...]
    y = y * _silu(z_ref[...])
    sq = y * y
    col_group = lax.broadcasted_iota(jnp.int32, (n_rows, D_SSM), 1) // D_SSM_GROUP
    scale = jnp.zeros((n_rows, D_SSM), F32)
    for g in range(N_SSM_GROUPS):
        ms = jnp.sum(jnp.where(col_group == g, sq, 0.0), axis=1, keepdims=True) / D_SSM_GROUP
        scale = jnp.where(col_group == g, lax.rsqrt(ms + RMS_EPS), scale)
    y_ref[...] = y * scale * nrm_ref[...]


def _ssd_sample(xbc, conv_state, z, dt, h0, cw, cb, dtb, alog, dsk, nrm, *, n_tok):
    t = xbc.shape[0]
    n_b = t // n_tok
    n_seq = LANES // n_tok
    rows = n_seq * n_tok
    row = lambda b: (b, 0)
    const = lambda b: (0, 0)
    vec = lambda n: pl.BlockSpec((1, n), const)
    return pl.pallas_call(
        functools.partial(_ssd_sample_body, n_tok=n_tok, n_seq=n_seq),
        grid=(n_b // n_seq,),
        in_specs=[pl.BlockSpec((rows, CONV_DIM), row),
                  pl.BlockSpec((n_seq, CONV_W - 1, CONV_DIM), lambda b: (b, 0, 0)),
                  pl.BlockSpec((rows, D_SSM), row), pl.BlockSpec((rows, DT_PAD), row),
                  pl.BlockSpec((n_seq, D_SSM, D_STATE), lambda b: (b, 0, 0)),
                  pl.BlockSpec((CONV_W, CONV_DIM), const),
                  vec(CONV_DIM), vec(DT_PAD), vec(DT_PAD), vec(D_SSM), vec(D_SSM)],
        out_specs=[pl.BlockSpec((rows, D_SSM), row),
                   pl.BlockSpec((n_seq, D_SSM, D_STATE), lambda b: (b, 0, 0))],
        out_shape=[jax.ShapeDtypeStruct((t, D_SSM), F32),
                   jax.ShapeDtypeStruct((n_b, D_SSM, D_STATE), F32)],
        scratch_shapes=[pltpu.VMEM((n_seq, n_tok, CONV_DIM), F32),
                        pltpu.VMEM((D_SSM, rows), F32),
                        pltpu.VMEM((N_SSM_HEADS, rows, LANES), F32),
                        pltpu.VMEM((rows, D_SSM), F32),
                        pltpu.VMEM((rows, D_SSM), F32)],
        compiler_params=_params(1),
        name="ssd_sample",
    )(xbc, conv_state, z, dt, h0, cw, cb, dtb, alog, dsk, nrm)


HEADS_PER_HALF = LANES // ATT_HEAD_DIM
N_HALVES = D_GROUP // LANES


ATT_UNITS = 4


def _attn_prompt_body(slope_ref, *refs, dil, group):
    n_in = 5 * N_HALVES
    in_refs, o_refs, l_refs = refs[:n_in], refs[n_in:n_in + N_HALVES], refs[n_in + N_HALVES:]
    i = pl.program_id(0)
    nb = ATT_BLOCK
    qi = lax.broadcasted_iota(jnp.int32, (nb, 2 * nb), 0)
    kj = lax.broadcasted_iota(jnp.int32, (nb, 2 * nb), 1)
    dist = nb + qi - kj
    band = (dist >= 0) & (dist <= nb)
    band_first = band & (kj >= jnp.where(i > 0, 0, nb))
    dist_f = (dist * dil).astype(F32)
    lane_head = lax.broadcasted_iota(jnp.int32, (nb, LANES), 1) // ATT_HEAD_DIM
    for u in range(ATT_UNITS):
        if dil == 1:
            rows = pl.ds(u * nb, nb)
            rows_prev = pl.ds((ATT_UNITS - 1) * nb, nb) if u == 0 else pl.ds((u - 1) * nb, nb)
            valid = band_first if u == 0 else band
        else:
            rows = rows_prev = pl.ds(pl.program_id(1) * ATT_UNITS + u, nb, stride=dil)
            valid = band_first
        for half in range(N_HALVES):
            q_ref, kp_ref, kc_ref, vp_ref, vc_ref = in_refs[5 * half:5 * half + 5]
            if dil == 1 and u > 0:
                kp_ref, vp_ref = kc_ref, vc_ref
            q = q_ref[rows, :]
            k = jnp.concatenate([kp_ref[rows_prev, :], kc_ref[rows, :]], axis=0).astype(BF16)
            v = jnp.concatenate([vp_ref[rows_prev, :], vc_ref[rows, :]], axis=0).astype(BF16)
            out = jnp.zeros((nb, LANES), F32)
            lse = jnp.zeros((nb, LANES), F32)
            for hh in range(HEADS_PER_HALF):
                mine = lane_head == hh
                slope = slope_ref[group * H_PER_DIL + half * HEADS_PER_HALF + hh]
                qh = jnp.where(mine, q, 0.0).astype(BF16)
                s = _dot_nt(qh, k) * (ATT_HEAD_DIM ** -0.5)
                s = jnp.where(valid, s - slope * dist_f, -jnp.inf)
                m = jnp.max(s, axis=1, keepdims=True)
                p = jnp.exp(s - m)
                l = jnp.sum(p, axis=1, keepdims=True)
                pv = _dot(p.astype(BF16), v)
                out = jnp.where(mine, pv / l, out)
                lse = jnp.where(mine, m + jnp.log(l), lse)
            o_refs[half][rows, :] = out
            l_refs[half][rows, :] = lse


def _attn_prompt(slopes, qkv, *, group, dil, seq):
    assert dil == 1 or dil % ATT_UNITS == 0
    blk_rows = max(dil, ATT_UNITS) * ATT_BLOCK
    n_blk = seq // blk_rows
    n_res_steps = max(dil // ATT_UNITS, 1)
    blk = (blk_rows, LANES)
    cur = lambda c: pl.BlockSpec(blk, lambda i, r: (i, c))
    prev = lambda c: pl.BlockSpec(blk, lambda i, r: (jnp.maximum(i - 1, 0), c))
    in_specs = [pl.BlockSpec(memory_space=pltpu.SMEM)]
    for half in range(N_HALVES):
        q_col, k_col, v_col = (N_HALVES * (part * N_DIL + group) + half for part in range(3))
        in_specs += [cur(q_col), prev(k_col), cur(k_col), prev(v_col), cur(v_col)]
    out_blk = pl.BlockSpec(blk, lambda i, r: (i, 0))
    res = pl.pallas_call(
        functools.partial(_attn_prompt_body, dil=dil, group=group),
        grid=(n_blk, n_res_steps),
        in_specs=in_specs,
        out_specs=[out_blk] * (2 * N_HALVES),
        out_shape=[jax.ShapeDtypeStruct((seq, LANES), F32)] * (2 * N_HALVES),
        compiler_params=_params(2),
        name=f"attn_prompt_g{group}",
    )(slopes, *([qkv] * (5 * N_HALVES)))
    return res[:N_HALVES], res[N_HALVES:]


def _attn_sample_body(slope_ref, qkv_ref, k0_ref, v0_ref, k1_ref, v1_ref, k2_ref, v2_ref, att_ref,
                      *, n_tok, n_seq):
    n_rows = H_PER_DIL * n_tok
    kv_refs = ((k0_ref, v0_ref), (k1_ref, v1_ref), (k2_ref, v2_ref))
    row2 = lax.broadcasted_iota(jnp.int32, (n_rows, D_GROUP), 0)
    lane2 = lax.broadcasted_iota(jnp.int32, (n_rows, D_GROUP), 1)
    own_head = (lane2 // ATT_HEAD_DIM) == (row2 // n_tok)
    row1 = lax.broadcasted_iota(jnp.int32, (n_rows, 1), 0)
    tok1 = row1 % n_tok
    head1 = row1 // n_tok
    lane_head8 = lax.broadcasted_iota(jnp.int32, (n_tok, D_GROUP), 1) // ATT_HEAD_DIM
    scale = ATT_HEAD_DIM ** -0.5
    for b in range(n_seq):
        rows = slice(b * n_tok, (b + 1) * n_tok)
        _attn_sample_one(slope_ref, qkv_ref[rows, :], [(k[b], v[b]) for k, v in kv_refs],
                         att_ref.at[rows, :], n_tok, n_rows, own_head, tok1, head1, lane_head8, scale)


def _attn_sample_one(slope_ref, qkv, kv, att_ref, n_tok, n_rows, own_head, tok1, head1, lane_head8, scale):
    outs, lses = [], []
    for g, (win, dil) in enumerate(DILATION_PAIRS):
        k_buf, v_buf = kv[g]
        q = qkv[:, g * D_GROUP:(g + 1) * D_GROUP]
        k_new = qkv[:, D_ATT + g * D_GROUP:D_ATT + (g + 1) * D_GROUP]
        v_new = qkv[:, 2 * D_ATT + g * D_GROUP:2 * D_ATT + (g + 1) * D_GROUP]
        q_rows = jnp.where(own_head, jnp.concatenate([q] * H_PER_DIL, axis=0), 0.0)
        slope_1 = jnp.zeros((n_rows, 1), F32)
        for h in range(H_PER_DIL):
            slope_1 = jnp.where(head1 == h, slope_ref[g * H_PER_DIL + h], slope_1)

        s_buf = _dot(q_rows.astype(BF16), k_buf.astype(BF16)) * scale
        pos = lax.broadcasted_iota(jnp.int32, (n_rows, win), 1)
        tok_s = lax.broadcasted_iota(jnp.int32, (n_rows, win), 0) % n_tok
        dist = win + tok_s - pos
        on_grid = (pos >= tok_s) & (jnp.bitwise_and(dist, dil - 1) == 0)
        s_buf = jnp.where(on_grid, s_buf - slope_1 * dist.astype(F32), -jnp.inf)
        m = jnp.max(s_buf, axis=1, keepdims=True)
        s_new = []
        for tp in range(n_tok):
            sn = jnp.sum(q_rows * k_new[tp:tp + 1, :], axis=1, keepdims=True) * scale
            ok = (tok1 >= tp) & (jnp.bitwise_and(tok1 - tp, dil - 1) == 0)
            sn = jnp.where(ok, sn - slope_1 * (tok1 - tp).astype(F32), -jnp.inf)
            s_new.append(sn)
            m = jnp.maximum(m, sn)
        p_buf = jnp.exp(s_buf - m)
        l = jnp.sum(p_buf, axis=1, keepdims=True)
        o_rows = _dot_nt(p_buf.astype(BF16), v_buf.astype(BF16))
        for tp in range(n_tok):
            p_new = jnp.exp(s_new[tp] - m)
            l = l + p_new
            o_rows = o_rows + p_new * v_new[tp:tp + 1, :]
        o_rows = o_rows / l
        lse_rows = m + jnp.log(l)
        o = jnp.zeros((n_tok, D_GROUP), F32)
        lse = jnp.zeros((n_tok, D_GROUP), F32)
        for h in range(H_PER_DIL):
            o = jnp.where(lane_head8 == h, o_rows[h * n_tok:(h + 1) * n_tok, :], o)
            lse = jnp.where(lane_head8 == h, lse_rows[h * n_tok:(h + 1) * n_tok, :], lse)
        outs.append(o)
        lses.append(lse)

    m = jnp.maximum(jnp.maximum(lses[0], lses[1]), lses[2])
    e = [jnp.exp(x - m) for x in lses]
    den = e[0] + e[1] + e[2]
    for g in range(N_DIL):
        att_ref[:, g * D_GROUP:(g + 1) * D_GROUP] = outs[g] * (e[g] / den)


def _attn_sample(slopes, qkv, caches, *, n_tok, n_seq=2):
    t = qkv.shape[0]
    n_b = t // n_tok
    in_specs = [pl.BlockSpec(memory_space=pltpu.SMEM),
                pl.BlockSpec((n_seq * n_tok, 3 * D_ATT), lambda b: (b, 0))]
    args = [slopes, qkv]
    for g, (win, dil) in enumerate(DILATION_PAIRS):
        assert dil & (dil - 1) == 0, "dilations must be powers of two"
        for cache in caches[g]:
            assert cache.shape[1] == win, "window buffers must hold a full window"
            args.append(jnp.transpose(cache, (0, 2, 3, 1)).reshape(n_b, D_GROUP, win))
            in_specs.append(pl.BlockSpec((n_seq, D_GROUP, win), lambda b: (b, 0, 0)))
    return pl.pallas_call(
        functools.partial(_attn_sample_body, n_tok=n_tok, n_seq=n_seq),
        grid=(n_b // n_seq,),
        in_specs=in_specs,
        out_specs=pl.BlockSpec((n_seq * n_tok, D_ATT), lambda b: (b, 0)),
        out_shape=jax.ShapeDtypeStruct((t, D_ATT), F32),
        compiler_params=_params(1),
        name="attn_sample",
    )(*args)


def _outproj_body(*refs, combine):
    n_pieces = N_DIL * N_HALVES
    if combine:
        x_ref, ssm_ref = refs[:2]
        o_refs, l_refs = refs[2:2 + n_pieces], refs[2 + n_pieces:2 + 2 * n_pieces]
        w_ref, out_ref = refs[2 + 2 * n_pieces:]
        att = [None] * n_pieces
        for half in range(N_HALVES):
            idx = [g * N_HALVES + half for g in range(N_DIL)]
            lses = [l_refs[p][...] for p in idx]
            m = jnp.maximum(jnp.maximum(lses[0], lses[1]), lses[2])
            e = [jnp.exp(x - m) for x in lses]
            den = e[0] + e[1] + e[2]
            for p, w in zip(idx, e):
                att[p] = o_refs[p][...] * (w / den)
    else:
        x_ref, ssm_ref, att_ref, w_ref, out_ref = refs
        att = [att_ref[:, p * LANES:(p + 1) * LANES] for p in range(n_pieces)]
    mixed = jnp.concatenate([ssm_ref[...]] + att, axis=1).astype(BF16)
    out_ref[...] = x_ref[...] + _dot(mixed, w_ref[...])


def _outproj(x, ssm, att, w_out, *, tm=512):
    t = x.shape[0]
    row = lambda i: (i, 0)
    combine = isinstance(att, (tuple, list))
    att_args = list(att[0]) + list(att[1]) if combine else [att]
    att_specs = [pl.BlockSpec((tm, a.shape[1]), row) for a in att_args]
    return pl.pallas_call(
        functools.partial(_outproj_body, combine=combine),
        grid=(t // tm,),
        in_specs=[pl.BlockSpec((tm, D_MODEL), row), pl.BlockSpec((tm, D_SSM), row)] + att_specs
                 + [pl.BlockSpec((D_MODEL, D_MODEL), lambda i: (0, 0), pipeline_mode=pl.Buffered(1))],
        out_specs=pl.BlockSpec((tm, D_MODEL), row),
        out_shape=jax.ShapeDtypeStruct((t, D_MODEL), F32),
        compiler_params=_params(1),
        name="outproj_prompt" if combine else "outproj_sample",
    )(x, ssm, *att_args, w_out)


def kernel(x_prompt, x_sample, cache_k0, cache_v0, cache_k1, cache_v1, cache_k2, cache_v2, state_ssm, state_conv, ffn1_norm, ffn1_w_gate, ffn1_w_up, ffn1_w_down, mix_norm, w_in, conv_w, conv_b, dt_bias, a_log, d_skip, ssm_norm, w_out, ffn2_norm, ffn2_w_gate, ffn2_w_up, ffn2_w_down, final_norm):
    assert w_in.shape[0] == 1, "single layer"
    batch, seq, _ = x_prompt.shape
    dec_batch, dec_seq, _ = x_sample.shape
    assert batch == 1 and dec_seq == SUBLANES

    row = lambda v: v.reshape(1, -1).astype(F32)
    pad_heads = lambda v: jnp.pad(v.reshape(1, -1).astype(F32), ((0, 0), (0, DT_PAD - N_SSM_HEADS)))
    w_in_t = jnp.transpose(w_in[0]).astype(BF16)
    ffn1 = (row(ffn1_norm[0]), ffn1_w_gate[0], ffn1_w_up[0], ffn1_w_down[0])
    ffn2 = (row(ffn2_norm[0]), ffn2_w_gate[0], ffn2_w_up[0], ffn2_w_down[0])
    w_o = w_out[0].astype(BF16)
    ssd_params = (conv_w[0].astype(F32), row(conv_b[0]), pad_heads(dt_bias[0]), pad_heads(a_log[0]),
                  row(jnp.repeat(d_skip[0], SSM_HEAD_DIM)), row(ssm_norm[0]))
    slopes = jnp.exp2(-ALIBI_MAX_EXP * jnp.arange(1, N_ATT_HEADS + 1, dtype=F32) / N_ATT_HEADS)

    def front(x):
        x1 = _ffn(x, *ffn1)
        z, xbc, qkv, dt = _inproj(x1, row(mix_norm[0]), w_in_t)
        return x1, z, xbc, qkv, dt

    def back(x2):
        return _ffn(x2, *ffn2, row(final_norm))

    x1, z, xbc_p, qkv_p, dt = front(x_prompt.reshape(seq, D_MODEL))
    ssm_p, hfin_p = _ssd_prompt(xbc_p, z, dt, *ssd_params)
    outs, lses = [], []
    for g, (win, dil) in enumerate(DILATION_PAIRS):
        o, lse = _attn_prompt(slopes, qkv_p, group=g, dil=dil, seq=seq)
        outs += o
        lses += lse
    y_prompt = back(_outproj(x1, ssm_p, (outs, lses), w_o)).reshape(1, seq, D_MODEL)

    n_s = dec_batch * dec_seq
    x1, z, xbc_s, qkv_s, dt = front(x_sample.reshape(n_s, D_MODEL))
    ssm_s, h_s = _ssd_sample(xbc_s, state_conv[0], z, dt, state_ssm[0].reshape(dec_batch, D_SSM, D_STATE),
                             *ssd_params, n_tok=dec_seq)
    caches = ((cache_k0[0], cache_v0[0]), (cache_k1[0], cache_v1[0]), (cache_k2[0], cache_v2[0]))
    att_s = _attn_sample(slopes, qkv_s, caches, n_tok=dec_seq)
    y_sample = back(_outproj(x1, ssm_s, att_s, w_o)).reshape(dec_batch, dec_seq, D_MODEL)

    head_shape = (H_PER_DIL, ATT_HEAD_DIM)
    p_kv, s_kv = [], []
    for g, (win, dil) in enumerate(DILATION_PAIRS):
        keep = min(win, seq)
        for base in (D_ATT, 2 * D_ATT):
            c0 = base + g * D_GROUP
            p_kv.append(qkv_p[seq - keep:, c0:c0 + D_GROUP].reshape((1, 1, keep) + head_shape))
            s_kv.append(qkv_s[:, c0:c0 + D_GROUP].reshape((1, dec_batch, dec_seq) + head_shape))
    p_ssm = hfin_p.reshape(1, 1, N_SSM_HEADS, SSM_HEAD_DIM, D_STATE)
    p_conv = xbc_p[seq - (CONV_W - 1):].reshape(1, 1, CONV_W - 1, CONV_DIM)
    s_ssm = h_s.reshape(1, dec_batch, N_SSM_HEADS, SSM_HEAD_DIM, D_STATE)
    s_conv = xbc_s.reshape(dec_batch, dec_seq, CONV_DIM)[:, dec_seq - (CONV_W - 1):].reshape(
        1, dec_batch, CONV_W - 1, CONV_DIM)
    return (y_prompt, y_sample, *p_kv, p_ssm, p_conv, *s_kv, s_ssm, s_conv)
```

```python
import functools

import numpy as np
import jax
import jax.numpy as jnp
from jax import lax
from jax.experimental import pallas as pl
from jax.experimental.pallas import tpu as pltpu

F32 = jnp.float32
BF16 = jnp.bfloat16

D_MODEL = 2048
D_FF = 5632
D_ATT = 768
ATT_HEAD_DIM = 64
H_PER_DIL = 4
D_GROUP = H_PER_DIL * ATT_HEAD_DIM
DILATION_PAIRS = ((128, 1), (512, 4), (2048, 16))
N_DIL = len(DILATION_PAIRS)
ATT_BLOCK = 128
D_SSM = 1280
SSM_HEAD_DIM = 64
N_SSM_HEADS = 20
N_SSM_GROUPS = 4
HEADS_PER_SSM_GROUP = 5
D_SSM_GROUP = D_SSM // N_SSM_GROUPS
D_STATE = 128
CONV_W = 4
CONV_DIM = 2304
SSD_CHUNK = 128
RMS_EPS = 1e-6
ALIBI_MAX_EXP = 8.0
N_ATT_HEADS = 12

LANES = 128
SUBLANES = 8
DT_PAD = LANES
IN_HEAD = D_SSM + CONV_DIM
IN_TAIL = N_SSM_HEADS + 3 * D_ATT

VMEM_LIMIT = 60 * 1024 * 1024


def _params(n_grid_dims, vmem=VMEM_LIMIT):
    return pltpu.CompilerParams(dimension_semantics=("arbitrary",) * n_grid_dims,
                                vmem_limit_bytes=vmem)


def _silu(x):
    return x * (0.5 + 0.5 * jnp.tanh(0.5 * x))


def _softplus(x):
    return jnp.maximum(x, 0.0) + jnp.log(1.0 + jnp.exp(-jnp.abs(x)))


def _rms_scale(x):
    return x * lax.rsqrt(jnp.mean(x * x, axis=-1, keepdims=True) + RMS_EPS)


def _dot(a, b):
    return jnp.dot(a, b, preferred_element_type=F32)


def _dot_nt(a, b):
    return lax.dot_general(a, b, (((1,), (1,)), ((), ())), preferred_element_type=F32)


def _cumsum_rows(x):
    n = x.shape[0]
    rows = lax.broadcasted_iota(jnp.int32, x.shape, 0)
    shift = 1
    while shift < n:
        x = x + jnp.where(rows >= shift, pltpu.roll(x, shift, 0), 0.0)
        shift *= 2
    return x


FFN_ROW_CHUNKS = 4


def _ffn_body(*refs, n_f, final):
    if final:
        x_ref, g_ref, wg_ref, wu_ref, wd_ref, fg_ref, o_ref, h_ref = refs
    else:
        x_ref, g_ref, wg_ref, wu_ref, wd_ref, o_ref, h_ref = refs
    f = pl.program_id(1)
    tm = x_ref.shape[0]
    assert n_f >= 2 and tm % FFN_ROW_CHUNKS == 0

    def weights():
        return wg_ref[...].astype(BF16), wu_ref[...].astype(BF16), wd_ref[...].astype(BF16)

    def partial_out(h, w):
        gate = _dot(h, w[0])
        up = _dot(h, w[1])
        act = (_silu(gate) * up).astype(BF16)
        return _dot(act, w[2])

    chunks = [slice(c * tm // FFN_ROW_CHUNKS, (c + 1) * tm // FFN_ROW_CHUNKS) for c in range(FFN_ROW_CHUNKS)]

    @pl.when(f == 0)
    def _():
        w = weights()
        for rows in chunks:
            h = (_rms_scale(x_ref[rows, :]) * g_ref[...]).astype(BF16)
            h_ref[rows, :] = h
            o_ref[rows, :] = partial_out(h, w)

    @pl.when((f > 0) & (f < n_f - 1))
    def _():
        o_ref[...] += partial_out(h_ref[...], weights())

    @pl.when(f == n_f - 1)
    def _():
        w = weights()
        for rows in chunks:
            y = x_ref[rows, :] + 0.5 * (o_ref[rows, :] + partial_out(h_ref[rows, :], w))
            if final:
                y = _rms_scale(y) * fg_ref[...]
            o_ref[rows, :] = y


def _ffn(x, gain, wg, wu, wd, final_gain=None, *, tm=1024, tf=256):
    t = x.shape[0]
    n_f = D_FF // tf
    final = final_gain is not None
    row = lambda i, f: (i, 0)
    const = lambda i, f: (0, 0)
    in_specs = [pl.BlockSpec((tm, D_MODEL), row), pl.BlockSpec((1, D_MODEL), const),
                pl.BlockSpec((D_MODEL, tf), lambda i, f: (0, f)),
                pl.BlockSpec((D_MODEL, tf), lambda i, f: (0, f)),
                pl.BlockSpec((tf, D_MODEL), lambda i, f: (f, 0))]
    args = [x, gain, wg, wu, wd]
    if final:
        in_specs.append(pl.BlockSpec((1, D_MODEL), const))
        args.append(final_gain)
    return pl.pallas_call(
        functools.partial(_ffn_body, n_f=n_f, final=final),
        grid=(t // tm, n_f),
        in_specs=in_specs,
        out_specs=pl.BlockSpec((tm, D_MODEL), row),
        out_shape=jax.ShapeDtypeStruct((t, D_MODEL), F32),
        scratch_shapes=[pltpu.VMEM((tm, D_MODEL), BF16)],
        compiler_params=_params(2),
        name="ffn_final" if final else "ffn",
    )(*args)


def _inproj_body(x_ref, g_ref, w_ref, z_ref, xbc_ref, qkv_ref, dt_ref):
    h = (_rms_scale(x_ref[...]) * g_ref[...]).astype(BF16)
    z_ref[...] = _dot_nt(h, w_ref[0:D_SSM, :])
    xbc_ref[...] = _dot_nt(h, w_ref[D_SSM:IN_HEAD, :])
    tail = _dot_nt(h, w_ref[IN_HEAD:IN_HEAD + IN_TAIL, :])
    lane = lax.broadcasted_iota(jnp.int32, (tail.shape[0], DT_PAD), 1)
    dt_ref[...] = jnp.where(lane < N_SSM_HEADS, tail[:, 0:DT_PAD], 0.0)
    qkv_ref[...] = tail[:, N_SSM_HEADS:IN_TAIL]


def _inproj(x, gain, w_t, *, tm=256):
    t = x.shape[0]
    row = lambda i: (i, 0)
    const = lambda i: (0, 0)
    widths = (D_SSM, CONV_DIM, 3 * D_ATT, DT_PAD)
    return pl.pallas_call(
        _inproj_body,
        grid=(t // tm,),
        in_specs=[pl.BlockSpec((tm, D_MODEL), row), pl.BlockSpec((1, D_MODEL), const),
                  pl.BlockSpec((IN_HEAD + IN_TAIL, D_MODEL), const, pipeline_mode=pl.Buffered(1))],
        out_specs=[pl.BlockSpec((tm, w), row) for w in widths],
        out_shape=[jax.ShapeDtypeStruct((t, w), F32) for w in widths],
        compiler_params=_params(1),
        name="inproj",
    )(x, gain, w_t)


def _pair_blockdiag(x, low_half):
    zero = jnp.zeros_like(x)
    return jnp.concatenate([jnp.where(low_half, x, zero), jnp.where(low_half, zero, x)],
                           axis=0).astype(BF16)


def _ssd_chunk(xbc_ref, z_ref, dt_ref, cw_ref, cb_ref, dtb_ref, alog_ref, dsk_ref, nrm_ref,
               y_ref, xf_scr, ht_scr):
    cl = SSD_CHUNK
    x = xbc_ref[...]
    xf_scr[SUBLANES:SUBLANES + cl, :] = x
    conv = cb_ref[...]
    for j in range(CONV_W - 1):
        conv = conv + xf_scr[SUBLANES - 3 + j:SUBLANES - 3 + j + cl, :] * cw_ref[j:j + 1, :]
    conv = conv + x * cw_ref[CONV_W - 1:CONV_W, :]
    xf_scr[0:SUBLANES, :] = x[cl - SUBLANES:cl, :]
    xa = _silu(conv)
    xs = xa[:, :D_SSM]
    bm = xa[:, D_SSM:D_SSM + N_SSM_GROUPS * D_STATE]
    cm = xa[:, D_SSM + N_SSM_GROUPS * D_STATE:]

    dt = _softplus(dt_ref[...] + dtb_ref[...])
    a = -jnp.exp(alog_ref[...])
    acum = _cumsum_rows(dt * a)
    w_end = dt * jnp.exp(acum[cl - 1:cl, :] - acum)
    acum_t = acum.T
    dt_t = dt.T
    w_end_t = w_end.T

    rows = lax.broadcasted_iota(jnp.int32, (cl, cl), 0)
    cols = lax.broadcasted_iota(jnp.int32, (cl, cl), 1)
    causal = rows >= cols
    low_half = cols < SSM_HEAD_DIM

    b_g = [bm[:, g * D_STATE:(g + 1) * D_STATE] for g in range(N_SSM_GROUPS)]
    c_g = [cm[:, g * D_STATE:(g + 1) * D_STATE].astype(BF16) for g in range(N_SSM_GROUPS)]
    b_t = [b.T for b in b_g]
    cb = [_dot_nt(c_g[g], b_g[g].astype(BF16)) for g in range(N_SSM_GROUPS)]

    sumsq = [jnp.zeros((cl, 1), F32) for _ in range(N_SSM_GROUPS)]
    for j in range(N_SSM_HEADS // 2):
        lo, hi = j * LANES, (j + 1) * LANES
        heads = (2 * j, 2 * j + 1)
        groups = tuple(h // HEADS_PER_SSM_GROUP for h in heads)
        xs_p = xs[:, lo:hi]
        x_bd = _pair_blockdiag(xs_p, low_half)
        ht_p = ht_scr[:, lo:hi]
        h_bd = _pair_blockdiag(ht_p, low_half)
        g_mats, w_mats, e_cols = [], [], []
        for h, g in zip(heads, groups):
            row_b = jnp.broadcast_to(acum_t[h:h + 1, :], (cl, cl))
            col_b = row_b.T
            decay = jnp.where(causal, jnp.exp(col_b - row_b), 0.0)
            g_mats.append((cb[g] * decay * dt_t[h:h + 1, :]).astype(BF16))
            w_mats.append((b_t[g] * w_end_t[h:h + 1, :]).astype(BF16))
            e_cols.append(jnp.exp(col_b))
        y_diag = _dot(jnp.concatenate(g_mats, axis=1), x_bd)
        y_off = _dot(jnp.concatenate([c_g[groups[0]], c_g[groups[1]]], axis=1), h_bd)
        y_off = y_off * jnp.where(low_half, e_cols[0], e_cols[1])
        state = _dot(jnp.concatenate(w_mats, axis=1), x_bd)
        chunk_decay = jnp.where(low_half[0:1, :], e_cols[0][cl - 1:cl, :], e_cols[1][cl - 1:cl, :])
        ht_scr[:, lo:hi] = ht_p * chunk_decay + state
        y_p = y_diag + y_off + dsk_ref[:, lo:hi] * xs_p
        y_p = y_p * _silu(z_ref[:, lo:hi])
        y_ref[:, lo:hi] = y_p
        sq = y_p * y_p
        sumsq[groups[0]] = sumsq[groups[0]] + jnp.sum(jnp.where(low_half, sq, 0.0), axis=1, keepdims=True)
        sumsq[groups[1]] = sumsq[groups[1]] + jnp.sum(jnp.where(low_half, 0.0, sq), axis=1, keepdims=True)

    scale = [lax.rsqrt(s / D_SSM_GROUP + RMS_EPS) for s in sumsq]
    for j in range(N_SSM_HEADS // 2):
        lo, hi = j * LANES, (j + 1) * LANES
        g0, g1 = (2 * j) // HEADS_PER_SSM_GROUP, (2 * j + 1) // HEADS_PER_SSM_GROUP
        y_ref[:, lo:hi] = y_ref[:, lo:hi] * jnp.where(low_half, scale[g0], scale[g1]) * nrm_ref[:, lo:hi]


def _inproj_ssd_body(x_ref, g_ref, w_ref, cw_ref, cb_ref, dtb_ref, alog_ref, dsk_ref, nrm_ref,
                     xbc_ref, qkv_ref, y_ref, hfin_ref,
                     xbc_a, z_a, dt_a, xbc_b, z_b, dt_b, xf_scr, ht_scr, *, n_blocks):
    s = pl.program_id(0)
    tm = x_ref.shape[0]
    bufs = ((xbc_a, z_a, dt_a), (xbc_b, z_b, dt_b))

    @pl.when(s <= 1)
    def _():
        xf_scr[0:SUBLANES, :] = jnp.zeros((SUBLANES, CONV_DIM), F32)
        ht_scr[...] = jnp.zeros_like(ht_scr)

    @pl.when(s == 0)
    def _():
        for ref in bufs[1]:
            ref[...] = jnp.zeros_like(ref)

    def step(parity):
        xbc_w, z_w, dt_w = bufs[parity]
        xbc_r, z_r, dt_r = bufs[1 - parity]
        h = (_rms_scale(x_ref[...]) * g_ref[...]).astype(BF16)
        z_w[...] = _dot_nt(h, w_ref[0:D_SSM, :])
        xbc = _dot_nt(h, w_ref[D_SSM:IN_HEAD, :])
        xbc_w[...] = xbc
        xbc_ref[...] = xbc
        tail = _dot_nt(h, w_ref[IN_HEAD:IN_HEAD + IN_TAIL, :])
        lane = lax.broadcasted_iota(jnp.int32, (tm, DT_PAD), 1)
        dt_w[...] = jnp.where(lane < N_SSM_HEADS, tail[:, 0:DT_PAD], 0.0)
        qkv_ref[...] = tail[:, N_SSM_HEADS:IN_TAIL]
        for c in range(tm // SSD_CHUNK):
            rows = slice(c * SSD_CHUNK, (c + 1) * SSD_CHUNK)
            _ssd_chunk(xbc_r.at[rows, :], z_r.at[rows, :], dt_r.at[rows, :], cw_ref, cb_ref, dtb_ref,
                       alog_ref, dsk_ref, nrm_ref, y_ref.at[rows, :], xf_scr, ht_scr)

    @pl.when(s % 2 == 0)
    def _():
        step(0)

    @pl.when(s % 2 == 1)
    def _():
        step(1)

    @pl.when(s == n_blocks)
    def _():
        for j in range(D_SSM // LANES):
            hfin_ref[j * LANES:(j + 1) * LANES, :] = ht_scr[:, j * LANES:(j + 1) * LANES].T


def _inproj_ssd(x, gain, w_t, cw, cb, dtb, alog, dsk, nrm, *, tm=256):
    t = x.shape[0]
    n_blocks = t // tm
    cur = lambda s: (jnp.minimum(s, n_blocks - 1), 0)
    prev = lambda s: (jnp.maximum(s - 1, 0), 0)
    const = lambda s: (0, 0)
    vec = lambda n: pl.BlockSpec((1, n), const)
    buf = lambda n: pltpu.VMEM((tm, n), F32)
    return pl.pallas_call(
        functools.partial(_inproj_ssd_body, n_blocks=n_blocks),
        grid=(n_blocks + 1,),
        in_specs=[pl.BlockSpec((tm, D_MODEL), cur), vec(D_MODEL),
                  pl.BlockSpec((IN_HEAD + IN_TAIL, D_MODEL), const, pipeline_mode=pl.Buffered(1)),
                  pl.BlockSpec((CONV_W, CONV_DIM), const),
                  vec(CONV_DIM), vec(DT_PAD), vec(DT_PAD), vec(D_SSM), vec(D_SSM)],
        out_specs=[pl.BlockSpec((tm, CONV_DIM), cur), pl.BlockSpec((tm, 3 * D_ATT), cur),
                   pl.BlockSpec((tm, D_SSM), prev), pl.BlockSpec((D_SSM, D_STATE), const)],
        out_shape=[jax.ShapeDtypeStruct((t, CONV_DIM), F32), jax.ShapeDtypeStruct((t, 3 * D_ATT), F32),
                   jax.ShapeDtypeStruct((t, D_SSM), F32), jax.ShapeDtypeStruct((D_SSM, D_STATE), F32)],
        scratch_shapes=[buf(CONV_DIM), buf(D_SSM), buf(DT_PAD), buf(CONV_DIM), buf(D_SSM), buf(DT_PAD),
                        pltpu.VMEM((SUBLANES + SSD_CHUNK, CONV_DIM), F32),
                        pltpu.VMEM((D_STATE, D_SSM), F32)],
        compiler_params=_params(1),
        name="inproj_ssd",
    )(x, gain, w_t, cw, cb, dtb, alog, dsk, nrm)


def _ssd_sample_body(xbc_ref, cs_ref, z_ref, dt_ref, h0_ref, cw_ref, cb_ref, dtb_ref, alog_ref,
                     dsk_ref, nrm_ref, y_ref, hout_ref, hist_scr, xwt_scr, ecd_scr, eexp_scr, yoff_scr,
                     *, n_tok, n_seq):
    n_rows = n_seq * n_tok
    assert n_rows == LANES

    @pl.when(pl.program_id(0) == 0)
    def _():
        hist_scr[...] = jnp.zeros_like(hist_scr)

    x = xbc_ref[...]
    hist_scr[:, 0:CONV_W - 1, :] = cs_ref[...]
    hist = hist_scr[...].reshape(n_rows, CONV_DIM)
    tok = lax.broadcasted_iota(jnp.int32, (n_rows, 1), 0) % n_tok
    conv = cb_ref[...] + x * cw_ref[CONV_W - 1:CONV_W, :]
    for k in range(1, CONV_W):
        back = CONV_W - 1 - k
        from_hist = hist if back == 0 else pltpu.roll(hist, n_rows - back, 0)
        shifted = jnp.where(tok < k, from_hist, pltpu.roll(x, k, 0))
        conv = conv + shifted * cw_ref[back:back + 1, :]
    xa = _silu(conv)
    xs = xa[:, :D_SSM]
    bm = xa[:, D_SSM:D_SSM + N_SSM_GROUPS * D_STATE]
    cm = xa[:, D_SSM + N_SSM_GROUPS * D_STATE:]

    dt = _softplus(dt_ref[...] + dtb_ref[...])
    dta = dt * -jnp.exp(alog_ref[...])
    tok_l = lax.broadcasted_iota(jnp.int32, (n_rows, LANES), 0) % n_tok
    acum = dta
    to_end = jnp.where(tok_l + 1 < n_tok, pltpu.roll(dta, n_rows - 1, 0), 0.0)
    shift = 1
    while shift < n_tok:
        acum = acum + jnp.where(tok_l >= shift, pltpu.roll(acum, shift, 0), 0.0)
        to_end = to_end + jnp.where(tok_l + shift < n_tok, pltpu.roll(to_end, n_rows - shift, 0), 0.0)
        shift *= 2
    w_end = dt * jnp.exp(to_end)
    acum_t = acum.T
    dt_t = dt.T
    w_end_t = w_end.T

    rows = lax.broadcasted_iota(jnp.int32, (n_rows, n_rows), 0)
    cols = lax.broadcasted_iota(jnp.int32, (n_rows, n_rows), 1)
    causal = (rows // n_tok == cols // n_tok) & (rows >= cols)
    low_half = cols < SSM_HEAD_DIM
    b_g = [bm[:, g * D_STATE:(g + 1) * D_STATE].astype(BF16) for g in range(N_SSM_GROUPS)]
    c_g = [cm[:, g * D_STATE:(g + 1) * D_STATE].astype(BF16) for g in range(N_SSM_GROUPS)]
    cb = [_dot_nt(c_g[g], b_g[g]) for g in range(N_SSM_GROUPS)]

    for j in range(N_SSM_HEADS // 2):
        lo, hi = j * LANES, (j + 1) * LANES
        heads = (2 * j, 2 * j + 1)
        xs_p = xs[:, lo:hi]
        g_mats, e_cols, w_cols = [], [], []
        for h in heads:
            g = h // HEADS_PER_SSM_GROUP
            row_b = jnp.broadcast_to(acum_t[h:h + 1, :], (n_rows, n_rows))
            col_b = row_b.T
            decay = jnp.where(causal, jnp.exp(col_b - row_b), 0.0)
            g_mats.append((cb[g] * decay * dt_t[h:h + 1, :]).astype(BF16))
            e_col = jnp.exp(col_b)
            ecd_scr[h] = e_col
            e_cols.append(e_col)
            w_cols.append(jnp.broadcast_to(w_end_t[h:h + 1, :], (n_rows, n_rows)).T)
        y_diag = _dot(jnp.concatenate(g_mats, axis=1), _pair_blockdiag(xs_p, low_half))
        y_ref[:, lo:hi] = y_diag + dsk_ref[:, lo:hi] * xs_p
        eexp_scr[:, lo:hi] = jnp.where(low_half, e_cols[0], e_cols[1])
        xwt_scr[lo:hi, :] = (xs_p * jnp.where(low_half, w_cols[0], w_cols[1])).T

    col_group = lax.broadcasted_iota(jnp.int32, (n_tok, D_SSM), 1) // D_SSM_GROUP
    lane_seq = lax.broadcasted_iota(jnp.int32, (D_SSM_GROUP, n_rows), 1) // n_tok
    for b in range(n_seq):
        r0 = b * n_tok
        h0 = h0_ref[b]
        c_stack = jnp.concatenate([c_g[g][r0:r0 + n_tok, :] for g in range(N_SSM_GROUPS)], axis=0)
        y_all = _dot_nt(c_stack, h0.astype(BF16))
        y_off = jnp.zeros((n_tok, D_SSM), F32)
        for g in range(N_SSM_GROUPS):
            y_off = jnp.where(col_group == g, y_all[g * n_tok:(g + 1) * n_tok, :], y_off)
        yoff_scr[r0:r0 + n_tok, :] = y_off
        for g in range(N_SSM_GROUPS):
            c0 = g * D_SSM_GROUP
            lhs = jnp.where(lane_seq == b, xwt_scr[c0:c0 + D_SSM_GROUP, :], 0.0).astype(BF16)
            upd = _dot(lhs, b_g[g])
            for e in range(HEADS_PER_SSM_GROUP):
                h = g * HEADS_PER_SSM_GROUP + e
                p0 = h * SSM_HEAD_DIM
                decay = ecd_scr[h, r0 + n_tok - 1:r0 + n_tok, :]
                hout_ref[b, p0:p0 + SSM_HEAD_DIM, :] = (
                    h0[p0:p0 + SSM_HEAD_DIM, :] * decay + upd[e * SSM_HEAD_DIM:(e + 1) * SSM_HEAD_DIM, :])

    y = y_ref[...] + yoff_scr[...] * eexp_scr[...]
    y = y * _silu(z_ref[...])
    sq = y * y
    col_group = lax.broadcasted_iota(jnp.int32, (n_rows, D_SSM), 1) // D_SSM_GROUP
    scale = jnp.zeros((n_rows, D_SSM), F32)
    for g in range(N_SSM_GROUPS):
        ms = jnp.sum(jnp.where(col_group == g, sq, 0.0), axis=1, keepdims=True) / D_SSM_GROUP
        scale = jnp.where(col_group == g, lax.rsqrt(ms + RMS_EPS), scale)
    y_ref[...] = y * scale * nrm_ref[...]


def _ssd_sample(xbc, conv_state, z, dt, h0, cw, cb, dtb, alog, dsk, nrm, *, n_tok):
    t = xbc.shape[0]
    n_b = t // n_tok
    n_seq = LANES // n_tok
    rows = n_seq * n_tok
    row = lambda b: (b, 0)
    const = lambda b: (0, 0)
    vec = lambda n: pl.BlockSpec((1, n), const)
    return pl.pallas_call(
        functools.partial(_ssd_sample_body, n_tok=n_tok, n_seq=n_seq),
        grid=(n_b // n_seq,),
        in_specs=[pl.BlockSpec((rows, CONV_DIM), row),
                  pl.BlockSpec((n_seq, CONV_W - 1, CONV_DIM), lambda b: (b, 0, 0)),
                  pl.BlockSpec((rows, D_SSM), row), pl.BlockSpec((rows, DT_PAD), row),
                  pl.BlockSpec((n_seq, D_SSM, D_STATE), lambda b: (b, 0, 0)),
                  pl.BlockSpec((CONV_W, CONV_DIM), const),
                  vec(CONV_DIM), vec(DT_PAD), vec(DT_PAD), vec(D_SSM), vec(D_SSM)],
        out_specs=[pl.BlockSpec((rows, D_SSM), row),
                   pl.BlockSpec((n_seq, D_SSM, D_STATE), lambda b: (b, 0, 0))],
        out_shape=[jax.ShapeDtypeStruct((t, D_SSM), F32),
                   jax.ShapeDtypeStruct((n_b, D_SSM, D_STATE), F32)],
        scratch_shapes=[pltpu.VMEM((n_seq, n_tok, CONV_DIM), F32),
                        pltpu.VMEM((D_SSM, rows), F32),
                        pltpu.VMEM((N_SSM_HEADS, rows, LANES), F32),
                        pltpu.VMEM((rows, D_SSM), F32),
                        pltpu.VMEM((rows, D_SSM), F32)],
        compiler_params=_params(1),
        name="ssd_sample",
    )(xbc, conv_state, z, dt, h0, cw, cb, dtb, alog, dsk, nrm)


HEADS_PER_HALF = LANES // ATT_HEAD_DIM
N_HALVES = D_GROUP // LANES


ATT_UNITS = 4


def _attn_prompt_body(slope_ref, *refs, dil, group):
    n_in = 5 * N_HALVES
    in_refs, o_refs, l_refs = refs[:n_in], refs[n_in:n_in + N_HALVES], refs[n_in + N_HALVES:]
    i = pl.program_id(0)
    nb = ATT_BLOCK
    qi = lax.broadcasted_iota(jnp.int32, (nb, 2 * nb), 0)
    kj = lax.broadcasted_iota(jnp.int32, (nb, 2 * nb), 1)
    dist = nb + qi - kj
    band = (dist >= 0) & (dist <= nb)
    band_first = band & (kj >= jnp.where(i > 0, 0, nb))
    dist_f = (dist * dil).astype(F32)
    lane_head = lax.broadcasted_iota(jnp.int32, (nb, LANES), 1) // ATT_HEAD_DIM
    for u in range(ATT_UNITS):
        if dil == 1:
            rows = pl.ds(u * nb, nb)
            rows_prev = pl.ds((ATT_UNITS - 1) * nb, nb) if u == 0 else pl.ds((u - 1) * nb, nb)
            valid = band_first if u == 0 else band
        else:
            rows = rows_prev = pl.ds(pl.program_id(1) * ATT_UNITS + u, nb, stride=dil)
            valid = band_first
        for half in range(N_HALVES):
            q_ref, kp_ref, kc_ref, vp_ref, vc_ref = in_refs[5 * half:5 * half + 5]
            if dil == 1 and u > 0:
                kp_ref, vp_ref = kc_ref, vc_ref
            q = q_ref[rows, :]
            k = jnp.concatenate([kp_ref[rows_prev, :], kc_ref[rows, :]], axis=0).astype(BF16)
            v = jnp.concatenate([vp_ref[rows_prev, :], vc_ref[rows, :]], axis=0).astype(BF16)
            out = jnp.zeros((nb, LANES), F32)
            lse = jnp.zeros((nb, LANES), F32)
            for hh in range(HEADS_PER_HALF):
                mine = lane_head == hh
                slope = slope_ref[group * H_PER_DIL + half * HEADS_PER_HALF + hh]
                qh = jnp.where(mine, q, 0.0).astype(BF16)
                s = _dot_nt(qh, k) * (ATT_HEAD_DIM ** -0.5)
                s = jnp.where(valid, s - slope * dist_f, -jnp.inf)
                m = jnp.max(s, axis=1, keepdims=True)
                p = jnp.exp(s - m)
                l = jnp.sum(p, axis=1, keepdims=True)
                pv = _dot(p.astype(BF16), v)
                out = jnp.where(mine, pv / l, out)
                lse = jnp.where(mine, m + jnp.log(l), lse)
            o_refs[half][rows, :] = out
            l_refs[half][rows, :] = lse


def _attn_prompt(slopes, qkv, *, group, dil, seq):
    assert dil == 1 or dil % ATT_UNITS == 0
    blk_rows = max(dil, ATT_UNITS) * ATT_BLOCK
    n_blk = seq // blk_rows
    n_res_steps = max(dil // ATT_UNITS, 1)
    blk = (blk_rows, LANES)
    cur = lambda c: pl.BlockSpec(blk, lambda i, r: (i, c))
    prev = lambda c: pl.BlockSpec(blk, lambda i, r: (jnp.maximum(i - 1, 0), c))
    in_specs = [pl.BlockSpec(memory_space=pltpu.SMEM)]
    for half in range(N_HALVES):
        q_col, k_col, v_col = (N_HALVES * (part * N_DIL + group) + half for part in range(3))
        in_specs += [cur(q_col), prev(k_col), cur(k_col), prev(v_col), cur(v_col)]
    out_blk = pl.BlockSpec(blk, lambda i, r: (i, 0))
    res = pl.pallas_call(
        functools.partial(_attn_prompt_body, dil=dil, group=group),
        grid=(n_blk, n_res_steps),
        in_specs=in_specs,
        out_specs=[out_blk] * (2 * N_HALVES),
        out_shape=[jax.ShapeDtypeStruct((seq, LANES), F32)] * (2 * N_HALVES),
        compiler_params=_params(2),
        name=f"attn_prompt_g{group}",
    )(slopes, *([qkv] * (5 * N_HALVES)))
    return res[:N_HALVES], res[N_HALVES:]


def _attn_sample_body(slope_ref, qkv_ref, k0_ref, v0_ref, k1_ref, v1_ref, k2_ref, v2_ref, att_ref,
                      *, n_tok, n_seq):
    n_rows = H_PER_DIL * n_tok
    kv_refs = ((k0_ref, v0_ref), (k1_ref, v1_ref), (k2_ref, v2_ref))
    row2 = lax.broadcasted_iota(jnp.int32, (n_rows, D_GROUP), 0)
    lane2 = lax.broadcasted_iota(jnp.int32, (n_rows, D_GROUP), 1)
    own_head = (lane2 // ATT_HEAD_DIM) == (row2 // n_tok)
    row1 = lax.broadcasted_iota(jnp.int32, (n_rows, 1), 0)
    tok1 = row1 % n_tok
    head1 = row1 // n_tok
    lane_head8 = lax.broadcasted_iota(jnp.int32, (n_tok, D_GROUP), 1) // ATT_HEAD_DIM
    scale = ATT_HEAD_DIM ** -0.5
    for b in range(n_seq):
        rows = slice(b * n_tok, (b + 1) * n_tok)
        _attn_sample_one(slope_ref, qkv_ref[rows, :], [(k[b], v[b]) for k, v in kv_refs],
                         att_ref.at[rows, :], n_tok, n_rows, own_head, tok1, head1, lane_head8, scale)


def _attn_sample_one(slope_ref, qkv, kv, att_ref, n_tok, n_rows, own_head, tok1, head1, lane_head8, scale):
    outs, lses = [], []
    for g, (win, dil) in enumerate(DILATION_PAIRS):
        k_buf, v_buf = kv[g]
        q = qkv[:, g * D_GROUP:(g + 1) * D_GROUP]
        k_new = qkv[:, D_ATT + g * D_GROUP:D_ATT + (g + 1) * D_GROUP]
        v_new = qkv[:, 2 * D_ATT + g * D_GROUP:2 * D_ATT + (g + 1) * D_GROUP]
        q_rows = jnp.where(own_head, jnp.concatenate([q] * H_PER_DIL, axis=0), 0.0)
        slope_1 = jnp.zeros((n_rows, 1), F32)
        for h in range(H_PER_DIL):
            slope_1 = jnp.where(head1 == h, slope_ref[g * H_PER_DIL + h], slope_1)

        s_buf = _dot(q_rows.astype(BF16), k_buf.astype(BF16)) * scale
        pos = lax.broadcasted_iota(jnp.int32, (n_rows, win), 1)
        tok_s = lax.broadcasted_iota(jnp.int32, (n_rows, win), 0) % n_tok
        dist = win + tok_s - pos
        on_grid = (pos >= tok_s) & (jnp.bitwise_and(dist, dil - 1) == 0)
        s_buf = jnp.where(on_grid, s_buf - slope_1 * dist.astype(F32), -jnp.inf)
        m = jnp.max(s_buf, axis=1, keepdims=True)
        s_new = []
        for tp in range(n_tok):
            sn = jnp.sum(q_rows * k_new[tp:tp + 1, :], axis=1, keepdims=True) * scale
            ok = (tok1 >= tp) & (jnp.bitwise_and(tok1 - tp, dil - 1) == 0)
            sn = jnp.where(ok, sn - slope_1 * (tok1 - tp).astype(F32), -jnp.inf)
            s_new.append(sn)
            m = jnp.maximum(m, sn)
        p_buf = jnp.exp(s_buf - m)
        l = jnp.sum(p_buf, axis=1, keepdims=True)
        o_rows = _dot_nt(p_buf.astype(BF16), v_buf.astype(BF16))
        for tp in range(n_tok):
            p_new = jnp.exp(s_new[tp] - m)
            l = l + p_new
            o_rows = o_rows + p_new * v_new[tp:tp + 1, :]
        o_rows = o_rows / l
        lse_rows = m + jnp.log(l)
        o = jnp.zeros((n_tok, D_GROUP), F32)
        lse = jnp.zeros((n_tok, D_GROUP), F32)
        for h in range(H_PER_DIL):
            o = jnp.where(lane_head8 == h, o_rows[h * n_tok:(h + 1) * n_tok, :], o)
            lse = jnp.where(lane_head8 == h, lse_rows[h * n_tok:(h + 1) * n_tok, :], lse)
        outs.append(o)
        lses.append(lse)

    m = jnp.maximum(jnp.maximum(lses[0], lses[1]), lses[2])
    e = [jnp.exp(x - m) for x in lses]
    den = e[0] + e[1] + e[2]
    for g in range(N_DIL):
        att_ref[:, g * D_GROUP:(g + 1) * D_GROUP] = outs[g] * (e[g] / den)


def _attn_sample(slopes, qkv, caches, *, n_tok, n_seq=4):
    t = qkv.shape[0]
    n_b = t // n_tok
    in_specs = [pl.BlockSpec(memory_space=pltpu.SMEM),
                pl.BlockSpec((n_seq * n_tok, 3 * D_ATT), lambda b: (b, 0))]
    args = [slopes, qkv]
    for g, (win, dil) in enumerate(DILATION_PAIRS):
        assert dil & (dil - 1) == 0, "dilations must be powers of two"
        for cache in caches[g]:
            assert cache.shape[1] == win, "window buffers must hold a full window"
            args.append(jnp.transpose(cache, (0, 2, 3, 1)).reshape(n_b, D_GROUP, win))
            in_specs.append(pl.BlockSpec((n_seq, D_GROUP, win), lambda b: (b, 0, 0)))
    return pl.pallas_call(
        functools.partial(_attn_sample_body, n_tok=n_tok, n_seq=n_seq),
        grid=(n_b // n_seq,),
        in_specs=in_specs,
        out_specs=pl.BlockSpec((n_seq * n_tok, D_ATT), lambda b: (b, 0)),
        out_shape=jax.ShapeDtypeStruct((t, D_ATT), F32),
        compiler_params=_params(1),
        name="attn_sample",
    )(*args)


def _outproj_body(*refs, combine):
    n_pieces = N_DIL * N_HALVES
    if combine:
        x_ref, ssm_ref = refs[:2]
        o_refs, l_refs = refs[2:2 + n_pieces], refs[2 + n_pieces:2 + 2 * n_pieces]
        w_ref, out_ref = refs[2 + 2 * n_pieces:]
        att = [None] * n_pieces
        for half in range(N_HALVES):
            idx = [g * N_HALVES + half for g in range(N_DIL)]
            lses = [l_refs[p][...] for p in idx]
            m = jnp.maximum(jnp.maximum(lses[0], lses[1]), lses[2])
            e = [jnp.exp(x - m) for x in lses]
            den = e[0] + e[1] + e[2]
            for p, w in zip(idx, e):
                att[p] = o_refs[p][...] * (w / den)
    else:
        x_ref, ssm_ref, att_ref, w_ref, out_ref = refs
        att = [att_ref[:, p * LANES:(p + 1) * LANES] for p in range(n_pieces)]
    mixed = jnp.concatenate([ssm_ref[...]] + att, axis=1).astype(BF16)
    out_ref[...] = x_ref[...] + _dot(mixed, w_ref[...])


def _outproj(x, ssm, att, w_out, *, tm=512):
    t = x.shape[0]
    row = lambda i: (i, 0)
    combine = isinstance(att, (tuple, list))
    att_args = list(att[0]) + list(att[1]) if combine else [att]
    att_specs = [pl.BlockSpec((tm, a.shape[1]), row) for a in att_args]
    return pl.pallas_call(
        functools.partial(_outproj_body, combine=combine),
        grid=(t // tm,),
        in_specs=[pl.BlockSpec((tm, D_MODEL), row), pl.BlockSpec((tm, D_SSM), row)] + att_specs
                 + [pl.BlockSpec((D_MODEL, D_MODEL), lambda i: (0, 0), pipeline_mode=pl.Buffered(1))],
        out_specs=pl.BlockSpec((tm, D_MODEL), row),
        out_shape=jax.ShapeDtypeStruct((t, D_MODEL), F32),
        compiler_params=_params(1),
        name="outproj_prompt" if combine else "outproj_sample",
    )(x, ssm, *att_args, w_out)


def kernel(x_prompt, x_sample, cache_k0, cache_v0, cache_k1, cache_v1, cache_k2, cache_v2, state_ssm, state_conv, ffn1_norm, ffn1_w_gate, ffn1_w_up, ffn1_w_down, mix_norm, w_in, conv_w, conv_b, dt_bias, a_log, d_skip, ssm_norm, w_out, ffn2_norm, ffn2_w_gate, ffn2_w_up, ffn2_w_down, final_norm):
    assert w_in.shape[0] == 1, "single layer"
    batch, seq, _ = x_prompt.shape
    dec_batch, dec_seq, _ = x_sample.shape
    assert batch == 1 and dec_seq == SUBLANES

    row = lambda v: v.reshape(1, -1).astype(F32)
    pad_heads = lambda v: jnp.pad(v.reshape(1, -1).astype(F32), ((0, 0), (0, DT_PAD - N_SSM_HEADS)))
    w_in_t = jnp.transpose(w_in[0]).astype(BF16)
    ffn1 = (row(ffn1_norm[0]), ffn1_w_gate[0], ffn1_w_up[0], ffn1_w_down[0])
    ffn2 = (row(ffn2_norm[0]), ffn2_w_gate[0], ffn2_w_up[0], ffn2_w_down[0])
    w_o = w_out[0].astype(BF16)
    ssd_params = (conv_w[0].astype(F32), row(conv_b[0]), pad_heads(dt_bias[0]), pad_heads(a_log[0]),
                  row(jnp.repeat(d_skip[0], SSM_HEAD_DIM)), row(ssm_norm[0]))
    slopes = jnp.exp2(-ALIBI_MAX_EXP * jnp.arange(1, N_ATT_HEADS + 1, dtype=F32) / N_ATT_HEADS)

    def back(x2):
        return _ffn(x2, *ffn2, row(final_norm))

    x1 = _ffn(x_prompt.reshape(seq, D_MODEL), *ffn1)
    xbc_p, qkv_p, ssm_p, hfin_p = _inproj_ssd(x1, row(mix_norm[0]), w_in_t, *ssd_params)
    outs, lses = [], []
    for g, (win, dil) in enumerate(DILATION_PAIRS):
        o, lse = _attn_prompt(slopes, qkv_p, group=g, dil=dil, seq=seq)
        outs += o
        lses += lse
    y_prompt = back(_outproj(x1, ssm_p, (outs, lses), w_o)).reshape(1, seq, D_MODEL)

    n_s = dec_batch * dec_seq
    x1 = _ffn(x_sample.reshape(n_s, D_MODEL), *ffn1)
    z, xbc_s, qkv_s, dt = _inproj(x1, row(mix_norm[0]), w_in_t)
    ssm_s, h_s = _ssd_sample(xbc_s, state_conv[0], z, dt, state_ssm[0].reshape(dec_batch, D_SSM, D_STATE),
                             *ssd_params, n_tok=dec_seq)
    caches = ((cache_k0[0], cache_v0[0]), (cache_k1[0], cache_v1[0]), (cache_k2[0], cache_v2[0]))
    att_s = _attn_sample(slopes, qkv_s, caches, n_tok=dec_seq)
    y_sample = back(_outproj(x1, ssm_s, att_s, w_o)).reshape(dec_batch, dec_seq, D_MODEL)

    head_shape = (H_PER_DIL, ATT_HEAD_DIM)
    p_kv, s_kv = [], []
    for g, (win, dil) in enumerate(DILATION_PAIRS):
        keep = min(win, seq)
        for base in (D_ATT, 2 * D_ATT):
            c0 = base + g * D_GROUP
            p_kv.append(qkv_p[seq - keep:, c0:c0 + D_GROUP].reshape((1, 1, keep) + head_shape))
            s_kv.append(qkv_s[:, c0:c0 + D_GROUP].reshape((1, dec_batch, dec_seq) + head_shape))
    p_ssm = hfin_p.reshape(1, 1, N_SSM_HEADS, SSM_HEAD_DIM, D_STATE)
    p_conv = xbc_p[seq - (CONV_W - 1):].reshape(1, 1, CONV_W - 1, CONV_DIM)
    s_ssm = h_s.reshape(1, dec_batch, N_SSM_HEADS, SSM_HEAD_DIM, D_STATE)
    s_conv = xbc_s.reshape(dec_batch, dec_seq, CONV_DIM)[:, dec_seq - (CONV_W - 1):].reshape(
        1, dec_batch, CONV_W - 1, CONV_DIM)
    return (y_prompt, y_sample, *p_kv, p_ssm, p_conv, *s_kv, s_ssm, s_conv)
```

```python
import functools

import numpy as np
import jax
import jax.numpy as jnp
from jax import lax
from jax.experimental import pallas as pl
from jax.experimental.pallas import tpu as pltpu

F32 = jnp.float32
BF16 = jnp.bfloat16

D_MODEL = 2048
D_FF = 5632
D_ATT = 768
ATT_HEAD_DIM = 64
H_PER_DIL = 4
D_GROUP = H_PER_DIL * ATT_HEAD_DIM
DILATION_PAIRS = ((128, 1), (512, 4), (2048, 16))
N_DIL = len(DILATION_PAIRS)
ATT_BLOCK = 128
D_SSM = 1280
SSM_HEAD_DIM = 64
N_SSM_HEADS = 20
N_SSM_GROUPS = 4
HEADS_PER_SSM_GROUP = 5
D_SSM_GROUP = D_SSM // N_SSM_GROUPS
D_STATE = 128
CONV_W = 4
CONV_DIM = 2304
SSD_CHUNK = 128
RMS_EPS = 1e-6
ALIBI_MAX_EXP = 8.0
N_ATT_HEADS = 12

LANES = 128
SUBLANES = 8
DT_PAD = LANES
IN_HEAD = D_SSM + CONV_DIM
IN_TAIL = N_SSM_HEADS + 3 * D_ATT

VMEM_LIMIT = 60 * 1024 * 1024


def _params(n_grid_dims, vmem=VMEM_LIMIT):
    return pltpu.CompilerParams(dimension_semantics=("arbitrary",) * n_grid_dims,
                                vmem_limit_bytes=vmem)


def _silu(x):
    return x * (0.5 + 0.5 * jnp.tanh(0.5 * x))


def _softplus(x):
    return jnp.maximum(x, 0.0) + jnp.log(1.0 + jnp.exp(-jnp.abs(x)))


def _rms_scale(x):
    return x * lax.rsqrt(jnp.mean(x * x, axis=-1, keepdims=True) + RMS_EPS)


def _dot(a, b):
    return jnp.dot(a, b, preferred_element_type=F32)


def _dot_nt(a, b):
    return lax.dot_general(a, b, (((1,), (1,)), ((), ())), preferred_element_type=F32)


def _cumsum_rows(x):
    n = x.shape[0]
    rows = lax.broadcasted_iota(jnp.int32, x.shape, 0)
    shift = 1
    while shift < n:
        x = x + jnp.where(rows >= shift, pltpu.roll(x, shift, 0), 0.0)
        shift *= 2
    return x


FFN_ROW_CHUNKS = 4


def _ffn_body(*refs, n_f, final):
    if final:
        x_ref, g_ref, wg_ref, wu_ref, wd_ref, fg_ref, o_ref, h_ref = refs
    else:
        x_ref, g_ref, wg_ref, wu_ref, wd_ref, o_ref, h_ref = refs
    f = pl.program_id(1)
    tm = x_ref.shape[0]
    assert n_f >= 2 and tm % FFN_ROW_CHUNKS == 0

    def weights():
        return wg_ref[...].astype(BF16), wu_ref[...].astype(BF16), wd_ref[...].astype(BF16)

    def partial_out(h, w):
        gate = _dot(h, w[0])
        up = _dot(h, w[1])
        act = (_silu(gate) * up).astype(BF16)
        return _dot(act, w[2])

    chunks = [slice(c * tm // FFN_ROW_CHUNKS, (c + 1) * tm // FFN_ROW_CHUNKS) for c in range(FFN_ROW_CHUNKS)]

    @pl.when(f == 0)
    def _():
        w = weights()
        for rows in chunks:
            h = (_rms_scale(x_ref[rows, :]) * g_ref[...]).astype(BF16)
            h_ref[rows, :] = h
            o_ref[rows, :] = partial_out(h, w)

    @pl.when((f > 0) & (f < n_f - 1))
    def _():
        o_ref[...] += partial_out(h_ref[...], weights())

    @pl.when(f == n_f - 1)
    def _():
        w = weights()
        for rows in chunks:
            y = x_ref[rows, :] + 0.5 * (o_ref[rows, :] + partial_out(h_ref[rows, :], w))
            if final:
                y = _rms_scale(y) * fg_ref[...]
            o_ref[rows, :] = y


def _ffn(x, gain, wg, wu, wd, final_gain=None, *, tm=1024, tf=256):
    t = x.shape[0]
    n_f = D_FF // tf
    final = final_gain is not None
    row = lambda i, f: (i, 0)
    const = lambda i, f: (0, 0)
    in_specs = [pl.BlockSpec((tm, D_MODEL), row), pl.BlockSpec((1, D_MODEL), const),
                pl.BlockSpec((D_MODEL, tf), lambda i, f: (0, f)),
                pl.BlockSpec((D_MODEL, tf), lambda i, f: (0, f)),
                pl.BlockSpec((tf, D_MODEL), lambda i, f: (f, 0))]
    args = [x, gain, wg, wu, wd]
    if final:
        in_specs.append(pl.BlockSpec((1, D_MODEL), const))
        args.append(final_gain)
    return pl.pallas_call(
        functools.partial(_ffn_body, n_f=n_f, final=final),
        grid=(t // tm, n_f),
        in_specs=in_specs,
        out_specs=pl.BlockSpec((tm, D_MODEL), row),
        out_shape=jax.ShapeDtypeStruct((t, D_MODEL), F32),
        scratch_shapes=[pltpu.VMEM((tm, D_MODEL), BF16)],
        compiler_params=_params(2),
        name="ffn_final" if final else "ffn",
    )(*args)


def _inproj_body(x_ref, g_ref, w_ref, z_ref, xbc_ref, qkv_ref, dt_ref):
    h = (_rms_scale(x_ref[...]) * g_ref[...]).astype(BF16)
    z_ref[...] = _dot_nt(h, w_ref[0:D_SSM, :])
    xbc_ref[...] = _dot_nt(h, w_ref[D_SSM:IN_HEAD, :])
    tail = _dot_nt(h, w_ref[IN_HEAD:IN_HEAD + IN_TAIL, :])
    lane = lax.broadcasted_iota(jnp.int32, (tail.shape[0], DT_PAD), 1)
    dt_ref[...] = jnp.where(lane < N_SSM_HEADS, tail[:, 0:DT_PAD], 0.0)
    qkv_ref[...] = tail[:, N_SSM_HEADS:IN_TAIL]


def _inproj(x, gain, w_t, *, tm=256):
    t = x.shape[0]
    row = lambda i: (i, 0)
    const = lambda i: (0, 0)
    widths = (D_SSM, CONV_DIM, 3 * D_ATT, DT_PAD)
    return pl.pallas_call(
        _inproj_body,
        grid=(t // tm,),
        in_specs=[pl.BlockSpec((tm, D_MODEL), row), pl.BlockSpec((1, D_MODEL), const),
                  pl.BlockSpec((IN_HEAD + IN_TAIL, D_MODEL), const, pipeline_mode=pl.Buffered(1))],
        out_specs=[pl.BlockSpec((tm, w), row) for w in widths],
        out_shape=[jax.ShapeDtypeStruct((t, w), F32) for w in widths],
        compiler_params=_params(1),
        name="inproj",
    )(x, gain, w_t)


def _pair_blockdiag(x, low_half):
    zero = jnp.zeros_like(x)
    return jnp.concatenate([jnp.where(low_half, x, zero), jnp.where(low_half, zero, x)],
                           axis=0).astype(BF16)


def _ssd_chunk(xbc_ref, z_ref, dt_ref, cw_ref, cb_ref, dtb_ref, alog_ref, dsk_ref, nrm_ref,
               y_ref, xf_scr, ht_scr):
    cl = SSD_CHUNK
    x = xbc_ref[...]
    xf_scr[SUBLANES:SUBLANES + cl, :] = x
    conv = cb_ref[...]
    for j in range(CONV_W - 1):
        conv = conv + xf_scr[SUBLANES - 3 + j:SUBLANES - 3 + j + cl, :] * cw_ref[j:j + 1, :]
    conv = conv + x * cw_ref[CONV_W - 1:CONV_W, :]
    xf_scr[0:SUBLANES, :] = x[cl - SUBLANES:cl, :]
    xa = _silu(conv)
    xs = xa[:, :D_SSM]
    bm = xa[:, D_SSM:D_SSM + N_SSM_GROUPS * D_STATE]
    cm = xa[:, D_SSM + N_SSM_GROUPS * D_STATE:]

    dt = _softplus(dt_ref[...] + dtb_ref[...])
    a = -jnp.exp(alog_ref[...])
    acum = _cumsum_rows(dt * a)
    w_end = dt * jnp.exp(acum[cl - 1:cl, :] - acum)
    acum_t = acum.T
    dt_t = dt.T
    w_end_t = w_end.T

    rows = lax.broadcasted_iota(jnp.int32, (cl, cl), 0)
    cols = lax.broadcasted_iota(jnp.int32, (cl, cl), 1)
    causal = rows >= cols
    low_half = cols < SSM_HEAD_DIM

    b_g = [bm[:, g * D_STATE:(g + 1) * D_STATE] for g in range(N_SSM_GROUPS)]
    c_g = [cm[:, g * D_STATE:(g + 1) * D_STATE].astype(BF16) for g in range(N_SSM_GROUPS)]
    b_t = [b.T for b in b_g]
    cb = [_dot_nt(c_g[g], b_g[g].astype(BF16)) for g in range(N_SSM_GROUPS)]

    sumsq = [jnp.zeros((cl, 1), F32) for _ in range(N_SSM_GROUPS)]
    for j in range(N_SSM_HEADS // 2):
        lo, hi = j * LANES, (j + 1) * LANES
        heads = (2 * j, 2 * j + 1)
        groups = tuple(h // HEADS_PER_SSM_GROUP for h in heads)
        xs_p = xs[:, lo:hi]
        x_bd = _pair_blockdiag(xs_p, low_half)
        ht_p = ht_scr[:, lo:hi]
        h_bd = _pair_blockdiag(ht_p, low_half)
        g_mats, w_mats, e_cols = [], [], []
        for h, g in zip(heads, groups):
            row_b = jnp.broadcast_to(acum_t[h:h + 1, :], (cl, cl))
            col_b = row_b.T
            decay = jnp.where(causal, jnp.exp(col_b - row_b), 0.0)
            g_mats.append((cb[g] * decay * dt_t[h:h + 1, :]).astype(BF16))
            w_mats.append((b_t[g] * w_end_t[h:h + 1, :]).astype(BF16))
            e_cols.append(jnp.exp(col_b))
        y_diag = _dot(jnp.concatenate(g_mats, axis=1), x_bd)
        y_off = _dot(jnp.concatenate([c_g[groups[0]], c_g[groups[1]]], axis=1), h_bd)
        y_off = y_off * jnp.where(low_half, e_cols[0], e_cols[1])
        state = _dot(jnp.concatenate(w_mats, axis=1), x_bd)
        chunk_decay = jnp.where(low_half[0:1, :], e_cols[0][cl - 1:cl, :], e_cols[1][cl - 1:cl, :])
        ht_scr[:, lo:hi] = ht_p * chunk_decay + state
        y_p = y_diag + y_off + dsk_ref[:, lo:hi] * xs_p
        y_p = y_p * _silu(z_ref[:, lo:hi])
        y_ref[:, lo:hi] = y_p
        sq = y_p * y_p
        sumsq[groups[0]] = sumsq[groups[0]] + jnp.sum(jnp.where(low_half, sq, 0.0), axis=1, keepdims=True)
        sumsq[groups[1]] = sumsq[groups[1]] + jnp.sum(jnp.where(low_half, 0.0, sq), axis=1, keepdims=True)

    scale = [lax.rsqrt(s / D_SSM_GROUP + RMS_EPS) for s in sumsq]
    for j in range(N_SSM_HEADS // 2):
        lo, hi = j * LANES, (j + 1) * LANES
        g0, g1 = (2 * j) // HEADS_PER_SSM_GROUP, (2 * j + 1) // HEADS_PER_SSM_GROUP
        y_ref[:, lo:hi] = y_ref[:, lo:hi] * jnp.where(low_half, scale[g0], scale[g1]) * nrm_ref[:, lo:hi]


def _inproj_ssd_body(x_ref, g_ref, w_ref, cw_ref, cb_ref, dtb_ref, alog_ref, dsk_ref, nrm_ref,
                     xbc_ref, qkv_ref, y_ref, hfin_ref,
                     xbc_a, z_a, dt_a, xbc_b, z_b, dt_b, xf_scr, ht_scr, *, n_blocks):
    s = pl.program_id(0)
    tm = x_ref.shape[0]
    bufs = ((xbc_a, z_a, dt_a), (xbc_b, z_b, dt_b))

    @pl.when(s <= 1)
    def _():
        xf_scr[0:SUBLANES, :] = jnp.zeros((SUBLANES, CONV_DIM), F32)
        ht_scr[...] = jnp.zeros_like(ht_scr)

    @pl.when(s == 0)
    def _():
        for ref in bufs[1]:
            ref[...] = jnp.zeros_like(ref)

    def step(parity):
        xbc_w, z_w, dt_w = bufs[parity]
        xbc_r, z_r, dt_r = bufs[1 - parity]
        h = (_rms_scale(x_ref[...]) * g_ref[...]).astype(BF16)
        z_w[...] = _dot_nt(h, w_ref[0:D_SSM, :])
        xbc = _dot_nt(h, w_ref[D_SSM:IN_HEAD, :])
        xbc_w[...] = xbc
        xbc_ref[...] = xbc
        tail = _dot_nt(h, w_ref[IN_HEAD:IN_HEAD + IN_TAIL, :])
        lane = lax.broadcasted_iota(jnp.int32, (tm, DT_PAD), 1)
        dt_w[...] = jnp.where(lane < N_SSM_HEADS, tail[:, 0:DT_PAD], 0.0)
        qkv_ref[...] = tail[:, N_SSM_HEADS:IN_TAIL]
        for c in range(tm // SSD_CHUNK):
            rows = slice(c * SSD_CHUNK, (c + 1) * SSD_CHUNK)
            _ssd_chunk(xbc_r.at[rows, :], z_r.at[rows, :], dt_r.at[rows, :], cw_ref, cb_ref, dtb_ref,
                       alog_ref, dsk_ref, nrm_ref, y_ref.at[rows, :], xf_scr, ht_scr)

    @pl.when(s % 2 == 0)
    def _():
        step(0)

    @pl.when(s % 2 == 1)
    def _():
        step(1)

    @pl.when(s == n_blocks)
    def _():
        for j in range(D_SSM // LANES):
            hfin_ref[j * LANES:(j + 1) * LANES, :] = ht_scr[:, j * LANES:(j + 1) * LANES].T


def _inproj_ssd(x, gain, w_t, cw, cb, dtb, alog, dsk, nrm, *, tm=256):
    t = x.shape[0]
    n_blocks = t // tm
    cur = lambda s: (jnp.minimum(s, n_blocks - 1), 0)
    prev = lambda s: (jnp.maximum(s - 1, 0), 0)
    const = lambda s: (0, 0)
    vec = lambda n: pl.BlockSpec((1, n), const)
    buf = lambda n: pltpu.VMEM((tm, n), F32)
    return pl.pallas_call(
        functools.partial(_inproj_ssd_body, n_blocks=n_blocks),
        grid=(n_blocks + 1,),
        in_specs=[pl.BlockSpec((tm, D_MODEL), cur), vec(D_MODEL),
                  pl.BlockSpec((IN_HEAD + IN_TAIL, D_MODEL), const, pipeline_mode=pl.Buffered(1)),
                  pl.BlockSpec((CONV_W, CONV_DIM), const),
                  vec(CONV_DIM), vec(DT_PAD), vec(DT_PAD), vec(D_SSM), vec(D_SSM)],
        out_specs=[pl.BlockSpec((tm, CONV_DIM), cur), pl.BlockSpec((tm, 3 * D_ATT), cur),
                   pl.BlockSpec((tm, D_SSM), prev), pl.BlockSpec((D_SSM, D_STATE), const)],
        out_shape=[jax.ShapeDtypeStruct((t, CONV_DIM), F32), jax.ShapeDtypeStruct((t, 3 * D_ATT), F32),
                   jax.ShapeDtypeStruct((t, D_SSM), F32), jax.ShapeDtypeStruct((D_SSM, D_STATE), F32)],
        scratch_shapes=[buf(CONV_DIM), buf(D_SSM), buf(DT_PAD), buf(CONV_DIM), buf(D_SSM), buf(DT_PAD),
                        pltpu.VMEM((SUBLANES + SSD_CHUNK, CONV_DIM), F32),
                        pltpu.VMEM((D_STATE, D_SSM), F32)],
        compiler_params=_params(1),
        name="inproj_ssd",
    )(x, gain, w_t, cw, cb, dtb, alog, dsk, nrm)


def _ssd_sample_body(xbc_ref, cs_ref, z_ref, dt_ref, h0_ref, cw_ref, cb_ref, dtb_ref, alog_ref,
                     dsk_ref, nrm_ref, y_ref, hout_ref, hist_scr, xwt_scr, ecd_scr, eexp_scr, yoff_scr,
                     *, n_tok, n_seq):
    n_rows = n_seq * n_tok
    assert n_rows == LANES

    @pl.when(pl.program_id(0) == 0)
    def _():
        hist_scr[...] = jnp.zeros_like(hist_scr)

    x = xbc_ref[...]
    hist_scr[:, 0:CONV_W - 1, :] = cs_ref[...]
    hist = hist_scr[...].reshape(n_rows, CONV_DIM)
    tok = lax.broadcasted_iota(jnp.int32, (n_rows, 1), 0) % n_tok
    conv = cb_ref[...] + x * cw_ref[CONV_W - 1:CONV_W, :]
    for k in range(1, CONV_W):
        back = CONV_W - 1 - k
        from_hist = hist if back == 0 else pltpu.roll(hist, n_rows - back, 0)
        shifted = jnp.where(tok < k, from_hist, pltpu.roll(x, k, 0))
        conv = conv + shifted * cw_ref[back:back + 1, :]
    xa = _silu(conv)
    xs = xa[:, :D_SSM]
    bm = xa[:, D_SSM:D_SSM + N_SSM_GROUPS * D_STATE]
    cm = xa[:, D_SSM + N_SSM_GROUPS * D_STATE:]

    dt = _softplus(dt_ref[...] + dtb_ref[...])
    dta = dt * -jnp.exp(alog_ref[...])
    tok_l = lax.broadcasted_iota(jnp.int32, (n_rows, LANES), 0) % n_tok
    acum = dta
    to_end = jnp.where(tok_l + 1 < n_tok, pltpu.roll(dta, n_rows - 1, 0), 0.0)
    shift = 1
    while shift < n_tok:
        acum = acum + jnp.where(tok_l >= shift, pltpu.roll(acum, shift, 0), 0.0)
        to_end = to_end + jnp.where(tok_l + shift < n_tok, pltpu.roll(to_end, n_rows - shift, 0), 0.0)
        shift *= 2
    w_end = dt * jnp.exp(to_end)
    acum_t = acum.T
    dt_t = dt.T
    w_end_t = w_end.T

    rows = lax.broadcasted_iota(jnp.int32, (n_rows, n_rows), 0)
    cols = lax.broadcasted_iota(jnp.int32, (n_rows, n_rows), 1)
    causal = (rows // n_tok == cols // n_tok) & (rows >= cols)
    low_half = cols < SSM_HEAD_DIM
    b_g = [bm[:, g * D_STATE:(g + 1) * D_STATE].astype(BF16) for g in range(N_SSM_GROUPS)]
    c_g = [cm[:, g * D_STATE:(g + 1) * D_STATE].astype(BF16) for g in range(N_SSM_GROUPS)]
    cb = [_dot_nt(c_g[g], b_g[g]) for g in range(N_SSM_GROUPS)]

    for j in range(N_SSM_HEADS // 2):
        lo, hi = j * LANES, (j + 1) * LANES
        heads = (2 * j, 2 * j + 1)
        xs_p = xs[:, lo:hi]
        g_mats, e_cols, w_cols = [], [], []
        for h in heads:
            g = h // HEADS_PER_SSM_GROUP
            row_b = jnp.broadcast_to(acum_t[h:h + 1, :], (n_rows, n_rows))
            col_b = row_b.T
            decay = jnp.where(causal, jnp.exp(col_b - row_b), 0.0)
            g_mats.append((cb[g] * decay * dt_t[h:h + 1, :]).astype(BF16))
            e_col = jnp.exp(col_b)
            ecd_scr[h] = e_col
            e_cols.append(e_col)
            w_cols.append(jnp.broadcast_to(w_end_t[h:h + 1, :], (n_rows, n_rows)).T)
        y_diag = _dot(jnp.concatenate(g_mats, axis=1), _pair_blockdiag(xs_p, low_half))
        y_ref[:, lo:hi] = y_diag + dsk_ref[:, lo:hi] * xs_p
        eexp_scr[:, lo:hi] = jnp.where(low_half, e_cols[0], e_cols[1])
        xwt_scr[lo:hi, :] = (xs_p * jnp.where(low_half, w_cols[0], w_cols[1])).T

    col_group = lax.broadcasted_iota(jnp.int32, (n_tok, D_SSM), 1) // D_SSM_GROUP
    lane_seq = lax.broadcasted_iota(jnp.int32, (D_SSM_GROUP, n_rows), 1) // n_tok
    for b in range(n_seq):
        r0 = b * n_tok
        h0 = h0_ref[b]
        c_stack = jnp.concatenate([c_g[g][r0:r0 + n_tok, :] for g in range(N_SSM_GROUPS)], axis=0)
        y_all = _dot_nt(c_stack, h0.astype(BF16))
        y_off = jnp.zeros((n_tok, D_SSM), F32)
        for g in range(N_SSM_GROUPS):
            y_off = jnp.where(col_group == g, y_all[g * n_tok:(g + 1) * n_tok, :], y_off)
        yoff_scr[r0:r0 + n_tok, :] = y_off
        for g in range(N_SSM_GROUPS):
            c0 = g * D_SSM_GROUP
            lhs = jnp.where(lane_seq == b, xwt_scr[c0:c0 + D_SSM_GROUP, :], 0.0).astype(BF16)
            upd = _dot(lhs, b_g[g])
            for e in range(HEADS_PER_SSM_GROUP):
                h = g * HEADS_PER_SSM_GROUP + e
                p0 = h * SSM_HEAD_DIM
                decay = ecd_scr[h, r0 + n_tok - 1:r0 + n_tok, :]
                hout_ref[b, p0:p0 + SSM_HEAD_DIM, :] = (
                    h0[p0:p0 + SSM_HEAD_DIM, :] * decay + upd[e * SSM_HEAD_DIM:(e + 1) * SSM_HEAD_DIM, :])

    y = y_ref[...] + yoff_scr[...] * eexp_scr[...]
    y = y * _silu(z_ref[...])
    sq = y * y
    col_group = lax.broadcasted_iota(jnp.int32, (n_rows, D_SSM), 1) // D_SSM_GROUP
    scale = jnp.zeros((n_rows, D_SSM), F32)
    for g in range(N_SSM_GROUPS):
        ms = jnp.sum(jnp.where(col_group == g, sq, 0.0), axis=1, keepdims=True) / D_SSM_GROUP
        scale = jnp.where(col_group == g, lax.rsqrt(ms + RMS_EPS), scale)
    y_ref[...] = y * scale * nrm_ref[...]


def _ssd_sample(xbc, conv_state, z, dt, h0, cw, cb, dtb, alog, dsk, nrm, *, n_tok):
    t = xbc.shape[0]
    n_b = t // n_tok
    n_seq = LANES // n_tok
    rows = n_seq * n_tok
    row = lambda b: (b, 0)
    const = lambda b: (0, 0)
    vec = lambda n: pl.BlockSpec((1, n), const)
    return pl.pallas_call(
        functools.partial(_ssd_sample_body, n_tok=n_tok, n_seq=n_seq),
        grid=(n_b // n_seq,),
        in_specs=[pl.BlockSpec((rows, CONV_DIM), row),
                  pl.BlockSpec((n_seq, CONV_W - 1, CONV_DIM), lambda b: (b, 0, 0)),
                  pl.BlockSpec((rows, D_SSM), row), pl.BlockSpec((rows, DT_PAD), row),
                  pl.BlockSpec((n_seq, D_SSM, D_STATE), lambda b: (b, 0, 0)),
                  pl.BlockSpec((CONV_W, CONV_DIM), const),
                  vec(CONV_DIM), vec(DT_PAD), vec(DT_PAD), vec(D_SSM), vec(D_SSM)],
        out_specs=[pl.BlockSpec((rows, D_SSM), row),
                   pl.BlockSpec((n_seq, D_SSM, D_STATE), lambda b: (b, 0, 0))],
        out_shape=[jax.ShapeDtypeStruct((t, D_SSM), F32),
                   jax.ShapeDtypeStruct((n_b, D_SSM, D_STATE), F32)],
        scratch_shapes=[pltpu.VMEM((n_seq, n_tok, CONV_DIM), F32),
                        pltpu.VMEM((D_SSM, rows), F32),
                        pltpu.VMEM((N_SSM_HEADS, rows, LANES), F32),
                        pltpu.VMEM((rows, D_SSM), F32),
                        pltpu.VMEM((rows, D_SSM), F32)],
        compiler_params=_params(1),
        name="ssd_sample",
    )(xbc, conv_state, z, dt, h0, cw, cb, dtb, alog, dsk, nrm)


HEADS_PER_HALF = LANES // ATT_HEAD_DIM
N_HALVES = D_GROUP // LANES


ATT_UNITS = 4


def _attn_prompt_body(slope_ref, *refs, dil, group):
    n_in = 5 * N_HALVES
    in_refs, o_refs, l_refs = refs[:n_in], refs[n_in:n_in + N_HALVES], refs[n_in + N_HALVES:]
    i = pl.program_id(0)
    nb = ATT_BLOCK
    qi = lax.broadcasted_iota(jnp.int32, (nb, 2 * nb), 0)
    kj = lax.broadcasted_iota(jnp.int32, (nb, 2 * nb), 1)
    dist = nb + qi - kj
    band = (dist >= 0) & (dist <= nb)
    band_first = band & (kj >= jnp.where(i > 0, 0, nb))
    dist_f = (dist * dil).astype(F32)
    alibi = [-slope_ref[group * H_PER_DIL + h] * dist_f for h in range(H_PER_DIL)]
    bias_first = [jnp.where(band_first, a, -jnp.inf) for a in alibi]
    bias_inner = [jnp.where(band, a, -jnp.inf) for a in alibi] if dil == 1 else bias_first
    lane_head = lax.broadcasted_iota(jnp.int32, (nb, LANES), 1) // ATT_HEAD_DIM
    for u in range(ATT_UNITS):
        if dil == 1:
            rows = pl.ds(u * nb, nb)
            rows_prev = pl.ds((ATT_UNITS - 1) * nb, nb) if u == 0 else pl.ds((u - 1) * nb, nb)
            bias = bias_first if u == 0 else bias_inner
        else:
            rows = rows_prev = pl.ds(pl.program_id(1) * ATT_UNITS + u, nb, stride=dil)
            bias = bias_first
        for half in range(N_HALVES):
            q_ref, kp_ref, kc_ref, vp_ref, vc_ref = in_refs[5 * half:5 * half + 5]
            if dil == 1 and u > 0:
                kp_ref, vp_ref = kc_ref, vc_ref
            q = q_ref[rows, :] * (ATT_HEAD_DIM ** -0.5)
            k = jnp.concatenate([kp_ref[rows_prev, :], kc_ref[rows, :]], axis=0).astype(BF16)
            v = jnp.concatenate([vp_ref[rows_prev, :], vc_ref[rows, :]], axis=0).astype(BF16)
            out = jnp.zeros((nb, LANES), F32)
            lse = jnp.zeros((nb, LANES), F32)
            for hh in range(HEADS_PER_HALF):
                mine = lane_head == hh
                qh = jnp.where(mine, q, 0.0).astype(BF16)
                s = _dot_nt(qh, k) + bias[half * HEADS_PER_HALF + hh]
                m = jnp.max(s, axis=1, keepdims=True)
                p = jnp.exp(s - m)
                l = jnp.sum(p, axis=1, keepdims=True)
                pv = _dot(p.astype(BF16), v)
                out = jnp.where(mine, pv / l, out)
                lse = jnp.where(mine, m + jnp.log(l), lse)
            o_refs[half][rows, :] = out
            l_refs[half][rows, :] = lse


def _attn_prompt(slopes, qkv, *, group, dil, seq):
    assert dil == 1 or dil % ATT_UNITS == 0
    blk_rows = max(dil, ATT_UNITS) * ATT_BLOCK
    n_blk = seq // blk_rows
    n_res_steps = max(dil // ATT_UNITS, 1)
    blk = (blk_rows, LANES)
    cur = lambda c: pl.BlockSpec(blk, lambda i, r: (i, c))
    prev = lambda c: pl.BlockSpec(blk, lambda i, r: (jnp.maximum(i - 1, 0), c))
    in_specs = [pl.BlockSpec(memory_space=pltpu.SMEM)]
    for half in range(N_HALVES):
        q_col, k_col, v_col = (N_HALVES * (part * N_DIL + group) + half for part in range(3))
        in_specs += [cur(q_col), prev(k_col), cur(k_col), prev(v_col), cur(v_col)]
    out_blk = pl.BlockSpec(blk, lambda i, r: (i, 0))
    res = pl.pallas_call(
        functools.partial(_attn_prompt_body, dil=dil, group=group),
        grid=(n_blk, n_res_steps),
        in_specs=in_specs,
        out_specs=[out_blk] * (2 * N_HALVES),
        out_shape=[jax.ShapeDtypeStruct((seq, LANES), F32)] * (2 * N_HALVES),
        compiler_params=_params(2),
        name=f"attn_prompt_g{group}",
    )(slopes, *([qkv] * (5 * N_HALVES)))
    return res[:N_HALVES], res[N_HALVES:]


def _attn_sample_body(slope_ref, qkv_ref, k0_ref, v0_ref, k1_ref, v1_ref, k2_ref, v2_ref, att_ref,
                      *, n_tok, n_seq):
    n_rows = H_PER_DIL * n_tok
    kv_refs = ((k0_ref, v0_ref), (k1_ref, v1_ref), (k2_ref, v2_ref))
    row2 = lax.broadcasted_iota(jnp.int32, (n_rows, D_GROUP), 0)
    lane2 = lax.broadcasted_iota(jnp.int32, (n_rows, D_GROUP), 1)
    own_head = (lane2 // ATT_HEAD_DIM) == (row2 // n_tok)
    row1 = lax.broadcasted_iota(jnp.int32, (n_rows, 1), 0)
    tok1 = row1 % n_tok
    head1 = row1 // n_tok
    lane_head8 = lax.broadcasted_iota(jnp.int32, (n_tok, D_GROUP), 1) // ATT_HEAD_DIM
    scale = ATT_HEAD_DIM ** -0.5
    for b in range(n_seq):
        rows = slice(b * n_tok, (b + 1) * n_tok)
        _attn_sample_one(slope_ref, qkv_ref[rows, :], [(k[b], v[b]) for k, v in kv_refs],
                         att_ref.at[rows, :], n_tok, n_rows, own_head, tok1, head1, lane_head8, scale)


def _attn_sample_one(slope_ref, qkv, kv, att_ref, n_tok, n_rows, own_head, tok1, head1, lane_head8, scale):
    outs, lses = [], []
    for g, (win, dil) in enumerate(DILATION_PAIRS):
        k_buf, v_buf = kv[g]
        q = qkv[:, g * D_GROUP:(g + 1) * D_GROUP]
        k_new = qkv[:, D_ATT + g * D_GROUP:D_ATT + (g + 1) * D_GROUP]
        v_new = qkv[:, 2 * D_ATT + g * D_GROUP:2 * D_ATT + (g + 1) * D_GROUP]
        q_rows = jnp.where(own_head, jnp.concatenate([q] * H_PER_DIL, axis=0), 0.0)
        slope_1 = jnp.zeros((n_rows, 1), F32)
        for h in range(H_PER_DIL):
            slope_1 = jnp.where(head1 == h, slope_ref[g * H_PER_DIL + h], slope_1)

        s_buf = _dot(q_rows.astype(BF16), k_buf.astype(BF16)) * scale
        pos = lax.broadcasted_iota(jnp.int32, (n_rows, win), 1)
        tok_s = lax.broadcasted_iota(jnp.int32, (n_rows, win), 0) % n_tok
        dist = win + tok_s - pos
        on_grid = (pos >= tok_s) & (jnp.bitwise_and(dist, dil - 1) == 0)
        s_buf = jnp.where(on_grid, s_buf - slope_1 * dist.astype(F32), -jnp.inf)
        m = jnp.max(s_buf, axis=1, keepdims=True)
        s_new = []
        for tp in range(n_tok):
            sn = jnp.sum(q_rows * k_new[tp:tp + 1, :], axis=1, keepdims=True) * scale
            ok = (tok1 >= tp) & (jnp.bitwise_and(tok1 - tp, dil - 1) == 0)
            sn = jnp.where(ok, sn - slope_1 * (tok1 - tp).astype(F32), -jnp.inf)
            s_new.append(sn)
            m = jnp.maximum(m, sn)
        p_buf = jnp.exp(s_buf - m)
        l = jnp.sum(p_buf, axis=1, keepdims=True)
        o_rows = _dot_nt(p_buf.astype(BF16), v_buf.astype(BF16))
        for tp in range(n_tok):
            p_new = jnp.exp(s_new[tp] - m)
            l = l + p_new
            o_rows = o_rows + p_new * v_new[tp:tp + 1, :]
        o_rows = o_rows / l
        lse_rows = m + jnp.log(l)
        o = jnp.zeros((n_tok, D_GROUP), F32)
        lse = jnp.zeros((n_tok, D_GROUP), F32)
        for h in range(H_PER_DIL):
            o = jnp.where(lane_head8 == h, o_rows[h * n_tok:(h + 1) * n_tok, :], o)
            lse = jnp.where(lane_head8 == h, lse_rows[h * n_tok:(h + 1) * n_tok, :], lse)
        outs.append(o)
        lses.append(lse)

    m = jnp.maximum(jnp.maximum(lses[0], lses[1]), lses[2])
    e = [jnp.exp(x - m) for x in lses]
    den = e[0] + e[1] + e[2]
    for g in range(N_DIL):
        att_ref[:, g * D_GROUP:(g + 1) * D_GROUP] = outs[g] * (e[g] / den)


def _attn_sample(slopes, qkv, caches, *, n_tok, n_seq=4):
    t = qkv.shape[0]
    n_b = t // n_tok
    in_specs = [pl.BlockSpec(memory_space=pltpu.SMEM),
                pl.BlockSpec((n_seq * n_tok, 3 * D_ATT), lambda b: (b, 0))]
    args = [slopes, qkv]
    for g, (win, dil) in enumerate(DILATION_PAIRS):
        assert dil & (dil - 1) == 0, "dilations must be powers of two"
        for cache in caches[g]:
            assert cache.shape[1] == win, "window buffers must hold a full window"
            args.append(jnp.transpose(cache, (0, 2, 3, 1)).reshape(n_b, D_GROUP, win))
            in_specs.append(pl.BlockSpec((n_seq, D_GROUP, win), lambda b: (b, 0, 0)))
    return pl.pallas_call(
        functools.partial(_attn_sample_body, n_tok=n_tok, n_seq=n_seq),
        grid=(n_b // n_seq,),
        in_specs=in_specs,
        out_specs=pl.BlockSpec((n_seq * n_tok, D_ATT), lambda b: (b, 0)),
        out_shape=jax.ShapeDtypeStruct((t, D_ATT), F32),
        compiler_params=_params(1),
        name="attn_sample",
    )(*args)


def _outproj_body(*refs, combine):
    n_pieces = N_DIL * N_HALVES
    if combine:
        x_ref, ssm_ref = refs[:2]
        o_refs, l_refs = refs[2:2 + n_pieces], refs[2 + n_pieces:2 + 2 * n_pieces]
        w_ref, out_ref = refs[2 + 2 * n_pieces:]
        att = [None] * n_pieces
        for half in range(N_HALVES):
            idx = [g * N_HALVES + half for g in range(N_DIL)]
            lses = [l_refs[p][...] for p in idx]
            m = jnp.maximum(jnp.maximum(lses[0], lses[1]), lses[2])
            e = [jnp.exp(x - m) for x in lses]
            den = e[0] + e[1] + e[2]
            for p, w in zip(idx, e):
                att[p] = o_refs[p][...] * (w / den)
    else:
        x_ref, ssm_ref, att_ref, w_ref, out_ref = refs
        att = [att_ref[:, p * LANES:(p + 1) * LANES] for p in range(n_pieces)]
    mixed = jnp.concatenate([ssm_ref[...]] + att, axis=1).astype(BF16)
    out_ref[...] = x_ref[...] + _dot(mixed, w_ref[...].astype(BF16))


def _outproj(x, ssm, att, w_out, *, tm=512):
    t = x.shape[0]
    row = lambda i: (i, 0)
    combine = isinstance(att, (tuple, list))
    att_args = list(att[0]) + list(att[1]) if combine else [att]
    att_specs = [pl.BlockSpec((tm, a.shape[1]), row) for a in att_args]
    return pl.pallas_call(
        functools.partial(_outproj_body, combine=combine),
        grid=(t // tm,),
        in_specs=[pl.BlockSpec((tm, D_MODEL), row), pl.BlockSpec((tm, D_SSM), row)] + att_specs
                 + [pl.BlockSpec((D_MODEL, D_MODEL), lambda i: (0, 0), pipeline_mode=pl.Buffered(1))],
        out_specs=pl.BlockSpec((tm, D_MODEL), row),
        out_shape=jax.ShapeDtypeStruct((t, D_MODEL), F32),
        compiler_params=_params(1),
        name="outproj_prompt" if combine else "outproj_sample",
    )(x, ssm, *att_args, w_out)


def kernel(x_prompt, x_sample, cache_k0, cache_v0, cache_k1, cache_v1, cache_k2, cache_v2, state_ssm, state_conv, ffn1_norm, ffn1_w_gate, ffn1_w_up, ffn1_w_down, mix_norm, w_in, conv_w, conv_b, dt_bias, a_log, d_skip, ssm_norm, w_out, ffn2_norm, ffn2_w_gate, ffn2_w_up, ffn2_w_down, final_norm):
    assert w_in.shape[0] == 1, "single layer"
    batch, seq, _ = x_prompt.shape
    dec_batch, dec_seq, _ = x_sample.shape
    assert batch == 1 and dec_seq == SUBLANES

    row = lambda v: v.reshape(1, -1).astype(F32)
    pad_heads = lambda v: jnp.pad(v.reshape(1, -1).astype(F32), ((0, 0), (0, DT_PAD - N_SSM_HEADS)))
    w_in_t = jnp.transpose(w_in[0]).astype(BF16)
    ffn1 = (row(ffn1_norm[0]), ffn1_w_gate[0], ffn1_w_up[0], ffn1_w_down[0])
    ffn2 = (row(ffn2_norm[0]), ffn2_w_gate[0], ffn2_w_up[0], ffn2_w_down[0])
    w_o = w_out[0]
    ssd_params = (conv_w[0].astype(F32), row(conv_b[0]), pad_heads(dt_bias[0]), pad_heads(a_log[0]),
                  row(jnp.repeat(d_skip[0], SSM_HEAD_DIM)), row(ssm_norm[0]))
    slopes = jnp.exp2(-ALIBI_MAX_EXP * jnp.arange(1, N_ATT_HEADS + 1, dtype=F32) / N_ATT_HEADS)

    def back(x2):
        return _ffn(x2, *ffn2, row(final_norm))

    x1 = _ffn(x_prompt.reshape(seq, D_MODEL), *ffn1)
    xbc_p, qkv_p, ssm_p, hfin_p = _inproj_ssd(x1, row(mix_norm[0]), w_in_t, *ssd_params)
    outs, lses = [], []
    for g, (win, dil) in enumerate(DILATION_PAIRS):
        o, lse = _attn_prompt(slopes, qkv_p, group=g, dil=dil, seq=seq)
        outs += o
        lses += lse
    y_prompt = back(_outproj(x1, ssm_p, (outs, lses), w_o)).reshape(1, seq, D_MODEL)

    n_s = dec_batch * dec_seq
    x1 = _ffn(x_sample.reshape(n_s, D_MODEL), *ffn1)
    z, xbc_s, qkv_s, dt = _inproj(x1, row(mix_norm[0]), w_in_t)
    ssm_s, h_s = _ssd_sample(xbc_s, state_conv[0], z, dt, state_ssm[0].reshape(dec_batch, D_SSM, D_STATE),
                             *ssd_params, n_tok=dec_seq)
    caches = ((cache_k0[0], cache_v0[0]), (cache_k1[0], cache_v1[0]), (cache_k2[0], cache_v2[0]))
    att_s = _attn_sample(slopes, qkv_s, caches, n_tok=dec_seq)
    y_sample = back(_outproj(x1, ssm_s, att_s, w_o)).reshape(dec_batch, dec_seq, D_MODEL)

    head_shape = (H_PER_DIL, ATT_HEAD_DIM)
    p_kv, s_kv = [], []
    for g, (win, dil) in enumerate(DILATION_PAIRS):
        keep = min(win, seq)
        for base in (D_ATT, 2 * D_ATT):
            c0 = base + g * D_GROUP
            p_kv.append(qkv_p[seq - keep:, c0:c0 + D_GROUP].reshape((1, 1, keep) + head_shape))
            s_kv.append(qkv_s[:, c0:c0 + D_GROUP].reshape((1, dec_batch, dec_seq) + head_shape))
    p_ssm = hfin_p.reshape(1, 1, N_SSM_HEADS, SSM_HEAD_DIM, D_STATE)
    p_conv = xbc_p[seq - (CONV_W - 1):].reshape(1, 1, CONV_W - 1, CONV_DIM)
    s_ssm = h_s.reshape(1, dec_batch, N_SSM_HEADS, SSM_HEAD_DIM, D_STATE)
    s_conv = xbc_s.reshape(dec_batch, dec_seq, CONV_DIM)[:, dec_seq - (CONV_W - 1):].reshape(
        1, dec_batch, CONV_W - 1, CONV_DIM)
    return (y_prompt, y_sample, *p_kv, p_ssm, p_conv, *s_kv, s_ssm, s_conv)
```

```python
import functools

import numpy as np
import jax
import jax.numpy as jnp
from jax import lax
from jax.experimental import pallas as pl
from jax.experimental.pallas import tpu as pltpu

F32 = jnp.float32
BF16 = jnp.bfloat16

D_MODEL = 2048
D_FF = 5632
D_ATT = 768
ATT_HEAD_DIM = 64
H_PER_DIL = 4
D_GROUP = H_PER_DIL * ATT_HEAD_DIM
DILATION_PAIRS = ((128, 1), (512, 4), (2048, 16))
N_DIL = len(DILATION_PAIRS)
ATT_BLOCK = 128
D_SSM = 1280
SSM_HEAD_DIM = 64
N_SSM_HEADS = 20
N_SSM_GROUPS = 4
HEADS_PER_SSM_GROUP = 5
D_SSM_GROUP = D_SSM // N_SSM_GROUPS
D_STATE = 128
CONV_W = 4
CONV_DIM = 2304
SSD_CHUNK = 128
RMS_EPS = 1e-6
ALIBI_MAX_EXP = 8.0
N_ATT_HEADS = 12

LANES = 128
SUBLANES = 8
DT_PAD = LANES
IN_HEAD = D_SSM + CONV_DIM
IN_TAIL = N_SSM_HEADS + 3 * D_ATT
N_QKV_BLOCKS = 3 * D_ATT // LANES

VMEM_LIMIT = 60 * 1024 * 1024


def _params(n_grid_dims, vmem=VMEM_LIMIT):
    return pltpu.CompilerParams(dimension_semantics=("arbitrary",) * n_grid_dims,
                                vmem_limit_bytes=vmem)


def _silu(x):
    return x * (0.5 + 0.5 * jnp.tanh(0.5 * x))


def _softplus(x):
    return jnp.maximum(x, 0.0) + jnp.log(1.0 + jnp.exp(-jnp.abs(x)))


def _rms_scale(x):
    return x * lax.rsqrt(jnp.mean(x * x, axis=-1, keepdims=True) + RMS_EPS)


def _dot(a, b):
    return jnp.dot(a, b, preferred_element_type=F32)


def _dot_nt(a, b):
    return lax.dot_general(a, b, (((1,), (1,)), ((), ())), preferred_element_type=F32)


def _cumsum_rows(x):
    n = x.shape[0]
    rows = lax.broadcasted_iota(jnp.int32, x.shape, 0)
    shift = 1
    while shift < n:
        x = x + jnp.where(rows >= shift, pltpu.roll(x, shift, 0), 0.0)
        shift *= 2
    return x


FFN_ROW_CHUNKS = 4


def _ffn_body(*refs, n_f, final):
    if final:
        x_ref, g_ref, wg_ref, wu_ref, wd_ref, fg_ref, o_ref, h_ref = refs
    else:
        x_ref, g_ref, wg_ref, wu_ref, wd_ref, o_ref, h_ref = refs
    f = pl.program_id(1)
    tm = x_ref.shape[0]
    assert n_f >= 2 and tm % FFN_ROW_CHUNKS == 0

    def weights():
        return wg_ref[...].astype(BF16), wu_ref[...].astype(BF16), wd_ref[...].astype(BF16)

    def partial_out(h, w):
        gate = _dot(h, w[0])
        up = _dot(h, w[1])
        act = (_silu(gate) * up).astype(BF16)
        return _dot(act, w[2])

    chunks = [slice(c * tm // FFN_ROW_CHUNKS, (c + 1) * tm // FFN_ROW_CHUNKS) for c in range(FFN_ROW_CHUNKS)]

    @pl.when(f == 0)
    def _():
        w = weights()
        for rows in chunks:
            h = (_rms_scale(x_ref[rows, :]) * g_ref[...]).astype(BF16)
            h_ref[rows, :] = h
            o_ref[rows, :] = partial_out(h, w)

    @pl.when((f > 0) & (f < n_f - 1))
    def _():
        o_ref[...] += partial_out(h_ref[...], weights())

    @pl.when(f == n_f - 1)
    def _():
        w = weights()
        for rows in chunks:
            y = x_ref[rows, :] + 0.5 * (o_ref[rows, :] + partial_out(h_ref[rows, :], w))
            if final:
                y = _rms_scale(y) * fg_ref[...]
            o_ref[rows, :] = y


def _ffn(x, gain, wg, wu, wd, final_gain=None, *, tm=1024, tf=256):
    t = x.shape[0]
    n_f = D_FF // tf
    final = final_gain is not None
    row = lambda i, f: (i, 0)
    const = lambda i, f: (0, 0)
    in_specs = [pl.BlockSpec((tm, D_MODEL), row), pl.BlockSpec((1, D_MODEL), const),
                pl.BlockSpec((D_MODEL, tf), lambda i, f: (0, f)),
                pl.BlockSpec((D_MODEL, tf), lambda i, f: (0, f)),
                pl.BlockSpec((tf, D_MODEL), lambda i, f: (f, 0))]
    args = [x, gain, wg, wu, wd]
    if final:
        in_specs.append(pl.BlockSpec((1, D_MODEL), const))
        args.append(final_gain)
    return pl.pallas_call(
        functools.partial(_ffn_body, n_f=n_f, final=final),
        grid=(t // tm, n_f),
        in_specs=in_specs,
        out_specs=pl.BlockSpec((tm, D_MODEL), row),
        out_shape=jax.ShapeDtypeStruct((t, D_MODEL), F32),
        scratch_shapes=[pltpu.VMEM((tm, D_MODEL), BF16)],
        compiler_params=_params(2),
        name="ffn_final" if final else "ffn",
    )(*args)


def _inproj_body(x_ref, g_ref, w_ref, z_ref, xbc_ref, qkv_ref, dt_ref):
    h = (_rms_scale(x_ref[...]) * g_ref[...]).astype(BF16)
    z_ref[...] = _dot_nt(h, w_ref[0:D_SSM, :])
    xbc_ref[...] = _dot_nt(h, w_ref[D_SSM:IN_HEAD, :])
    tail = _dot_nt(h, w_ref[IN_HEAD:IN_HEAD + IN_TAIL, :])
    lane = lax.broadcasted_iota(jnp.int32, (tail.shape[0], DT_PAD), 1)
    dt_ref[...] = jnp.where(lane < N_SSM_HEADS, tail[:, 0:DT_PAD], 0.0)
    qkv_ref[...] = tail[:, N_SSM_HEADS:IN_TAIL]


def _inproj(x, gain, w_t, *, tm=256):
    t = x.shape[0]
    row = lambda i: (i, 0)
    const = lambda i: (0, 0)
    widths = (D_SSM, CONV_DIM, 3 * D_ATT, DT_PAD)
    return pl.pallas_call(
        _inproj_body,
        grid=(t // tm,),
        in_specs=[pl.BlockSpec((tm, D_MODEL), row), pl.BlockSpec((1, D_MODEL), const),
                  pl.BlockSpec((IN_HEAD + IN_TAIL, D_MODEL), const, pipeline_mode=pl.Buffered(1))],
        out_specs=[pl.BlockSpec((tm, w), row) for w in widths],
        out_shape=[jax.ShapeDtypeStruct((t, w), F32) for w in widths],
        compiler_params=_params(1),
        name="inproj",
    )(x, gain, w_t)


def _pair_blockdiag(x, low_half):
    zero = jnp.zeros_like(x)
    return jnp.concatenate([jnp.where(low_half, x, zero), jnp.where(low_half, zero, x)],
                           axis=0).astype(BF16)


def _ssd_chunk(xbc_ref, z_ref, dt_ref, cw_ref, cb_ref, dtb_ref, alog_ref, dsk_ref, nrm_ref,
               y_ref, xf_scr, ht_scr):
    cl = SSD_CHUNK
    x = xbc_ref[...]
    xf_scr[SUBLANES:SUBLANES + cl, :] = x
    conv = cb_ref[...]
    for j in range(CONV_W - 1):
        conv = conv + xf_scr[SUBLANES - 3 + j:SUBLANES - 3 + j + cl, :] * cw_ref[j:j + 1, :]
    conv = conv + x * cw_ref[CONV_W - 1:CONV_W, :]
    xf_scr[0:SUBLANES, :] = x[cl - SUBLANES:cl, :]
    xa = _silu(conv)
    xs = xa[:, :D_SSM]
    bm = xa[:, D_SSM:D_SSM + N_SSM_GROUPS * D_STATE]
    cm = xa[:, D_SSM + N_SSM_GROUPS * D_STATE:]

    dt = _softplus(dt_ref[...] + dtb_ref[...])
    a = -jnp.exp(alog_ref[...])
    acum = _cumsum_rows(dt * a)
    w_end = dt * jnp.exp(acum[cl - 1:cl, :] - acum)
    acum_t = acum.T
    dt_t = dt.T
    w_end_t = w_end.T

    rows = lax.broadcasted_iota(jnp.int32, (cl, cl), 0)
    cols = lax.broadcasted_iota(jnp.int32, (cl, cl), 1)
    causal = rows >= cols
    low_half = cols < SSM_HEAD_DIM

    b_g = [bm[:, g * D_STATE:(g + 1) * D_STATE] for g in range(N_SSM_GROUPS)]
    c_g = [cm[:, g * D_STATE:(g + 1) * D_STATE].astype(BF16) for g in range(N_SSM_GROUPS)]
    b_t = [b.T for b in b_g]
    cb = [_dot_nt(c_g[g], b_g[g].astype(BF16)) for g in range(N_SSM_GROUPS)]

    sumsq = [jnp.zeros((cl, 1), F32) for _ in range(N_SSM_GROUPS)]
    for j in range(N_SSM_HEADS // 2):
        lo, hi = j * LANES, (j + 1) * LANES
        heads = (2 * j, 2 * j + 1)
        groups = tuple(h // HEADS_PER_SSM_GROUP for h in heads)
        xs_p = xs[:, lo:hi]
        x_bd = _pair_blockdiag(xs_p, low_half)
        ht_p = ht_scr[:, lo:hi]
        h_bd = _pair_blockdiag(ht_p, low_half)
        g_mats, w_mats, e_cols = [], [], []
        for h, g in zip(heads, groups):
            row_b = jnp.broadcast_to(acum_t[h:h + 1, :], (cl, cl))
            col_b = row_b.T
            decay = jnp.where(causal, jnp.exp(col_b - row_b), 0.0)
            g_mats.append((cb[g] * decay * dt_t[h:h + 1, :]).astype(BF16))
            w_mats.append((b_t[g] * w_end_t[h:h + 1, :]).astype(BF16))
            e_cols.append(jnp.exp(col_b))
        y_diag = _dot(jnp.concatenate(g_mats, axis=1), x_bd)
        y_off = _dot(jnp.concatenate([c_g[groups[0]], c_g[groups[1]]], axis=1), h_bd)
        y_off = y_off * jnp.where(low_half, e_cols[0], e_cols[1])
        state = _dot(jnp.concatenate(w_mats, axis=1), x_bd)
        chunk_decay = jnp.where(low_half[0:1, :], e_cols[0][cl - 1:cl, :], e_cols[1][cl - 1:cl, :])
        ht_scr[:, lo:hi] = ht_p * chunk_decay + state
        y_p = y_diag + y_off + dsk_ref[:, lo:hi] * xs_p
        y_p = y_p * _silu(z_ref[:, lo:hi])
        y_ref[:, lo:hi] = y_p
        sq = y_p * y_p
        sumsq[groups[0]] = sumsq[groups[0]] + jnp.sum(jnp.where(low_half, sq, 0.0), axis=1, keepdims=True)
        sumsq[groups[1]] = sumsq[groups[1]] + jnp.sum(jnp.where(low_half, 0.0, sq), axis=1, keepdims=True)

    scale = [lax.rsqrt(s / D_SSM_GROUP + RMS_EPS) for s in sumsq]
    for j in range(N_SSM_HEADS // 2):
        lo, hi = j * LANES, (j + 1) * LANES
        g0, g1 = (2 * j) // HEADS_PER_SSM_GROUP, (2 * j + 1) // HEADS_PER_SSM_GROUP
        y_ref[:, lo:hi] = y_ref[:, lo:hi] * jnp.where(low_half, scale[g0], scale[g1]) * nrm_ref[:, lo:hi]


def _inproj_ssd_body(x_ref, g_ref, w_ref, cw_ref, cb_ref, dtb_ref, alog_ref, dsk_ref, nrm_ref,
                     xbc_ref, qkv_ref, y_ref, hfin_ref,
                     xbc_a, z_a, dt_a, xbc_b, z_b, dt_b, xf_scr, ht_scr, *, n_blocks):
    s = pl.program_id(0)
    tm = x_ref.shape[0]
    bufs = ((xbc_a, z_a, dt_a), (xbc_b, z_b, dt_b))

    @pl.when(s <= 1)
    def _():
        xf_scr[0:SUBLANES, :] = jnp.zeros((SUBLANES, CONV_DIM), F32)
        ht_scr[...] = jnp.zeros_like(ht_scr)

    @pl.when(s == 0)
    def _():
        for ref in bufs[1]:
            ref[...] = jnp.zeros_like(ref)

    def step(parity):
        xbc_w, z_w, dt_w = bufs[parity]
        xbc_r, z_r, dt_r = bufs[1 - parity]
        h = (_rms_scale(x_ref[...]) * g_ref[...]).astype(BF16)
        z_w[...] = _dot_nt(h, w_ref[0:D_SSM, :])
        xbc = _dot_nt(h, w_ref[D_SSM:IN_HEAD, :])
        xbc_w[...] = xbc
        xbc_ref[...] = xbc
        tail = _dot_nt(h, w_ref[IN_HEAD:IN_HEAD + IN_TAIL, :])
        lane = lax.broadcasted_iota(jnp.int32, (tm, DT_PAD), 1)
        dt_w[...] = jnp.where(lane < N_SSM_HEADS, tail[:, 0:DT_PAD], 0.0)
        qkv = tail[:, N_SSM_HEADS:IN_TAIL]
        for j in range(N_QKV_BLOCKS):
            qkv_ref[j] = qkv[:, j * LANES:(j + 1) * LANES]
        for c in range(tm // SSD_CHUNK):
            rows = slice(c * SSD_CHUNK, (c + 1) * SSD_CHUNK)
            _ssd_chunk(xbc_r.at[rows, :], z_r.at[rows, :], dt_r.at[rows, :], cw_ref, cb_ref, dtb_ref,
                       alog_ref, dsk_ref, nrm_ref, y_ref.at[rows, :], xf_scr, ht_scr)

    @pl.when(s % 2 == 0)
    def _():
        step(0)

    @pl.when(s % 2 == 1)
    def _():
        step(1)

    @pl.when(s == n_blocks)
    def _():
        for j in range(D_SSM // LANES):
            hfin_ref[j * LANES:(j + 1) * LANES, :] = ht_scr[:, j * LANES:(j + 1) * LANES].T


def _inproj_ssd(x, gain, w_t, cw, cb, dtb, alog, dsk, nrm, *, tm=256):
    t = x.shape[0]
    n_blocks = t // tm
    cur = lambda s: (jnp.minimum(s, n_blocks - 1), 0)
    prev = lambda s: (jnp.maximum(s - 1, 0), 0)
    const = lambda s: (0, 0)
    vec = lambda n: pl.BlockSpec((1, n), const)
    buf = lambda n: pltpu.VMEM((tm, n), F32)
    return pl.pallas_call(
        functools.partial(_inproj_ssd_body, n_blocks=n_blocks),
        grid=(n_blocks + 1,),
        in_specs=[pl.BlockSpec((tm, D_MODEL), cur), vec(D_MODEL),
                  pl.BlockSpec((IN_HEAD + IN_TAIL, D_MODEL), const, pipeline_mode=pl.Buffered(1)),
                  pl.BlockSpec((CONV_W, CONV_DIM), const),
                  vec(CONV_DIM), vec(DT_PAD), vec(DT_PAD), vec(D_SSM), vec(D_SSM)],
        out_specs=[pl.BlockSpec((tm, CONV_DIM), cur),
                   pl.BlockSpec((N_QKV_BLOCKS, tm, LANES), lambda s: (0, jnp.minimum(s, n_blocks - 1), 0)),
                   pl.BlockSpec((tm, D_SSM), prev), pl.BlockSpec((D_SSM, D_STATE), const)],
        out_shape=[jax.ShapeDtypeStruct((t, CONV_DIM), F32),
                   jax.ShapeDtypeStruct((N_QKV_BLOCKS, t, LANES), F32),
                   jax.ShapeDtypeStruct((t, D_SSM), F32), jax.ShapeDtypeStruct((D_SSM, D_STATE), F32)],
        scratch_shapes=[buf(CONV_DIM), buf(D_SSM), buf(DT_PAD), buf(CONV_DIM), buf(D_SSM), buf(DT_PAD),
                        pltpu.VMEM((SUBLANES + SSD_CHUNK, CONV_DIM), F32),
                        pltpu.VMEM((D_STATE, D_SSM), F32)],
        compiler_params=_params(1),
        name="inproj_ssd",
    )(x, gain, w_t, cw, cb, dtb, alog, dsk, nrm)


def _ssd_sample_body(xbc_ref, cs_ref, z_ref, dt_ref, h0_ref, cw_ref, cb_ref, dtb_ref, alog_ref,
                     dsk_ref, nrm_ref, y_ref, hout_ref, hist_scr, xwt_scr, ecd_scr, eexp_scr, yoff_scr,
                     *, n_tok, n_seq):
    n_rows = n_seq * n_tok
    assert n_rows == LANES

    @pl.when(pl.program_id(0) == 0)
    def _():
        hist_scr[...] = jnp.zeros_like(hist_scr)

    x = xbc_ref[...]
    hist_scr[:, 0:CONV_W - 1, :] = cs_ref[...]
    hist = hist_scr[...].reshape(n_rows, CONV_DIM)
    tok = lax.broadcasted_iota(jnp.int32, (n_rows, 1), 0) % n_tok
    conv = cb_ref[...] + x * cw_ref[CONV_W - 1:CONV_W, :]
    for k in range(1, CONV_W):
        back = CONV_W - 1 - k
        from_hist = hist if back == 0 else pltpu.roll(hist, n_rows - back, 0)
        shifted = jnp.where(tok < k, from_hist, pltpu.roll(x, k, 0))
        conv = conv + shifted * cw_ref[back:back + 1, :]
    xa = _silu(conv)
    xs = xa[:, :D_SSM]
    bm = xa[:, D_SSM:D_SSM + N_SSM_GROUPS * D_STATE]
    cm = xa[:, D_SSM + N_SSM_GROUPS * D_STATE:]

    dt = _softplus(dt_ref[...] + dtb_ref[...])
    dta = dt * -jnp.exp(alog_ref[...])
    tok_l = lax.broadcasted_iota(jnp.int32, (n_rows, LANES), 0) % n_tok
    acum = dta
    to_end = jnp.where(tok_l + 1 < n_tok, pltpu.roll(dta, n_rows - 1, 0), 0.0)
    shift = 1
    while shift < n_tok:
        acum = acum + jnp.where(tok_l >= shift, pltpu.roll(acum, shift, 0), 0.0)
        to_end = to_end + jnp.where(tok_l + shift < n_tok, pltpu.roll(to_end, n_rows - shift, 0), 0.0)
        shift *= 2
    w_end = dt * jnp.exp(to_end)
    acum_t = acum.T
    dt_t = dt.T
    w_end_t = w_end.T

    rows = lax.broadcasted_iota(jnp.int32, (n_rows, n_rows), 0)
    cols = lax.broadcasted_iota(jnp.int32, (n_rows, n_rows), 1)
    causal = (rows // n_tok == cols // n_tok) & (rows >= cols)
    low_half = cols < SSM_HEAD_DIM
    b_g = [bm[:, g * D_STATE:(g + 1) * D_STATE].astype(BF16) for g in range(N_SSM_GROUPS)]
    c_g = [cm[:, g * D_STATE:(g + 1) * D_STATE].astype(BF16) for g in range(N_SSM_GROUPS)]
    cb = [_dot_nt(c_g[g], b_g[g]) for g in range(N_SSM_GROUPS)]

    for j in range(N_SSM_HEADS // 2):
        lo, hi = j * LANES, (j + 1) * LANES
        heads = (2 * j, 2 * j + 1)
        xs_p = xs[:, lo:hi]
        g_mats, e_cols, w_cols = [], [], []
        for h in heads:
            g = h // HEADS_PER_SSM_GROUP
            row_b = jnp.broadcast_to(acum_t[h:h + 1, :], (n_rows, n_rows))
            col_b = row_b.T
            decay = jnp.where(causal, jnp.exp(col_b - row_b), 0.0)
            g_mats.append((cb[g] * decay * dt_t[h:h + 1, :]).astype(BF16))
            e_col = jnp.exp(col_b)
            ecd_scr[h] = e_col
            e_cols.append(e_col)
            w_cols.append(jnp.broadcast_to(w_end_t[h:h + 1, :], (n_rows, n_rows)).T)
        y_diag = _dot(jnp.concatenate(g_mats, axis=1), _pair_blockdiag(xs_p, low_half))
        y_ref[:, lo:hi] = y_diag + dsk_ref[:, lo:hi] * xs_p
        eexp_scr[:, lo:hi] = jnp.where(low_half, e_cols[0], e_cols[1])
        xwt_scr[lo:hi, :] = (xs_p * jnp.where(low_half, w_cols[0], w_cols[1])).T

    col_group = lax.broadcasted_iota(jnp.int32, (n_tok, D_SSM), 1) // D_SSM_GROUP
    lane_seq = lax.broadcasted_iota(jnp.int32, (D_SSM_GROUP, n_rows), 1) // n_tok
    for b in range(n_seq):
        r0 = b * n_tok
        h0 = h0_ref[b]
        c_stack = jnp.concatenate([c_g[g][r0:r0 + n_tok, :] for g in range(N_SSM_GROUPS)], axis=0)
        y_all = _dot_nt(c_stack, h0.astype(BF16))
        y_off = jnp.zeros((n_tok, D_SSM), F32)
        for g in range(N_SSM_GROUPS):
            y_off = jnp.where(col_group == g, y_all[g * n_tok:(g + 1) * n_tok, :], y_off)
        yoff_scr[r0:r0 + n_tok, :] = y_off
        for g in range(N_SSM_GROUPS):
            c0 = g * D_SSM_GROUP
            lhs = jnp.where(lane_seq == b, xwt_scr[c0:c0 + D_SSM_GROUP, :], 0.0).astype(BF16)
            upd = _dot(lhs, b_g[g])
            for e in range(HEADS_PER_SSM_GROUP):
                h = g * HEADS_PER_SSM_GROUP + e
                p0 = h * SSM_HEAD_DIM
                decay = ecd_scr[h, r0 + n_tok - 1:r0 + n_tok, :]
                hout_ref[b, p0:p0 + SSM_HEAD_DIM, :] = (
                    h0[p0:p0 + SSM_HEAD_DIM, :] * decay + upd[e * SSM_HEAD_DIM:(e + 1) * SSM_HEAD_DIM, :])

    y = y_ref[...] + yoff_scr[...] * eexp_scr[...]
    y = y * _silu(z_ref[...])
    sq = y * y
    col_group = lax.broadcasted_iota(jnp.int32, (n_rows, D_SSM), 1) // D_SSM_GROUP
    scale = jnp.zeros((n_rows, D_SSM), F32)
    for g in range(N_SSM_GROUPS):
        ms = jnp.sum(jnp.where(col_group == g, sq, 0.0), axis=1, keepdims=True) / D_SSM_GROUP
        scale = jnp.where(col_group == g, lax.rsqrt(ms + RMS_EPS), scale)
    y_ref[...] = y * scale * nrm_ref[...]


def _ssd_sample(xbc, conv_state, z, dt, h0, cw, cb, dtb, alog, dsk, nrm, *, n_tok):
    t = xbc.shape[0]
    n_b = t // n_tok
    n_seq = LANES // n_tok
    rows = n_seq * n_tok
    row = lambda b: (b, 0)
    const = lambda b: (0, 0)
    vec = lambda n: pl.BlockSpec((1, n), const)
    return pl.pallas_call(
        functools.partial(_ssd_sample_body, n_tok=n_tok, n_seq=n_seq),
        grid=(n_b // n_seq,),
        in_specs=[pl.BlockSpec((rows, CONV_DIM), row),
                  pl.BlockSpec((n_seq, CONV_W - 1, CONV_DIM), lambda b: (b, 0, 0)),
                  pl.BlockSpec((rows, D_SSM), row), pl.BlockSpec((rows, DT_PAD), row),
                  pl.BlockSpec((n_seq, D_SSM, D_STATE), lambda b: (b, 0, 0)),
                  pl.BlockSpec((CONV_W, CONV_DIM), const),
                  vec(CONV_DIM), vec(DT_PAD), vec(DT_PAD), vec(D_SSM), vec(D_SSM)],
        out_specs=[pl.BlockSpec((rows, D_SSM), row),
                   pl.BlockSpec((n_seq, D_SSM, D_STATE), lambda b: (b, 0, 0))],
        out_shape=[jax.ShapeDtypeStruct((t, D_SSM), F32),
                   jax.ShapeDtypeStruct((n_b, D_SSM, D_STATE), F32)],
        scratch_shapes=[pltpu.VMEM((n_seq, n_tok, CONV_DIM), F32),
                        pltpu.VMEM((D_SSM, rows), F32),
                        pltpu.VMEM((N_SSM_HEADS, rows, LANES), F32),
                        pltpu.VMEM((rows, D_SSM), F32),
                        pltpu.VMEM((rows, D_SSM), F32)],
        compiler_params=_params(1),
        name="ssd_sample",
    )(xbc, conv_state, z, dt, h0, cw, cb, dtb, alog, dsk, nrm)


HEADS_PER_HALF = LANES // ATT_HEAD_DIM
N_HALVES = D_GROUP // LANES


ATT_UNITS = 4


def _attn_prompt_body(slope_ref, *refs, dil, group):
    n_in = 5 * N_HALVES
    in_refs, o_refs, l_refs = refs[:n_in], refs[n_in:n_in + N_HALVES], refs[n_in + N_HALVES:]
    i = pl.program_id(0)
    nb = ATT_BLOCK
    qi = lax.broadcasted_iota(jnp.int32, (nb, 2 * nb), 0)
    kj = lax.broadcasted_iota(jnp.int32, (nb, 2 * nb), 1)
    dist = nb + qi - kj
    band = (dist >= 0) & (dist <= nb)
    band_first = band & (kj >= jnp.where(i > 0, 0, nb))
    dist_f = (dist * dil).astype(F32)
    alibi = [-slope_ref[group * H_PER_DIL + h] * dist_f for h in range(H_PER_DIL)]
    bias_first = [jnp.where(band_first, a, -jnp.inf) for a in alibi]
    bias_inner = [jnp.where(band, a, -jnp.inf) for a in alibi] if dil == 1 else bias_first
    lane_head = lax.broadcasted_iota(jnp.int32, (nb, LANES), 1) // ATT_HEAD_DIM
    for u in range(ATT_UNITS):
        if dil == 1:
            rows = pl.ds(u * nb, nb)
            rows_prev = pl.ds((ATT_UNITS - 1) * nb, nb) if u == 0 else pl.ds((u - 1) * nb, nb)
            bias = bias_first if u == 0 else bias_inner
        else:
            rows = rows_prev = pl.ds(pl.program_id(1) * ATT_UNITS + u, nb, stride=dil)
            bias = bias_first
        for half in range(N_HALVES):
            q_ref, kp_ref, kc_ref, vp_ref, vc_ref = in_refs[5 * half:5 * half + 5]
            if dil == 1 and u > 0:
                kp_ref, vp_ref = kc_ref, vc_ref
            q = q_ref[rows, :] * (ATT_HEAD_DIM ** -0.5)
            k = jnp.concatenate([kp_ref[rows_prev, :], kc_ref[rows, :]], axis=0).astype(BF16)
            v = jnp.concatenate([vp_ref[rows_prev, :], vc_ref[rows, :]], axis=0).astype(BF16)
            out = jnp.zeros((nb, LANES), F32)
            lse = jnp.zeros((nb, LANES), F32)
            for hh in range(HEADS_PER_HALF):
                mine = lane_head == hh
                qh = jnp.where(mine, q, 0.0).astype(BF16)
                s = _dot_nt(qh, k) + bias[half * HEADS_PER_HALF + hh]
                m = jnp.max(s, axis=1, keepdims=True)
                p = jnp.exp(s - m)
                l = jnp.sum(p, axis=1, keepdims=True)
                pv = _dot(p.astype(BF16), v)
                out = jnp.where(mine, pv / l, out)
                lse = jnp.where(mine, m + jnp.log(l), lse)
            o_refs[half][rows, :] = out
            l_refs[half][rows, :] = lse


def _attn_prompt(slopes, qkv, *, group, dil, seq):
    assert dil == 1 or dil % ATT_UNITS == 0
    blk_rows = max(dil, ATT_UNITS) * ATT_BLOCK
    n_blk = seq // blk_rows
    n_res_steps = max(dil // ATT_UNITS, 1)
    blk = (blk_rows, LANES)
    cur = lambda c: pl.BlockSpec((None,) + blk, lambda i, r: (c, i, 0))
    prev = lambda c: pl.BlockSpec((None,) + blk, lambda i, r: (c, jnp.maximum(i - 1, 0), 0))
    in_specs = [pl.BlockSpec(memory_space=pltpu.SMEM)]
    for half in range(N_HALVES):
        q_col, k_col, v_col = (N_HALVES * (part * N_DIL + group) + half for part in range(3))
        in_specs += [cur(q_col), prev(k_col), cur(k_col), prev(v_col), cur(v_col)]
    out_blk = pl.BlockSpec(blk, lambda i, r: (i, 0))
    res = pl.pallas_call(
        functools.partial(_attn_prompt_body, dil=dil, group=group),
        grid=(n_blk, n_res_steps),
        in_specs=in_specs,
        out_specs=[out_blk] * (2 * N_HALVES),
        out_shape=[jax.ShapeDtypeStruct((seq, LANES), F32)] * (2 * N_HALVES),
        compiler_params=_params(2),
        name=f"attn_prompt_g{group}",
    )(slopes, *([qkv] * (5 * N_HALVES)))
    return res[:N_HALVES], res[N_HALVES:]


def _attn_sample_body(slope_ref, qkv_ref, k0_ref, v0_ref, k1_ref, v1_ref, k2_ref, v2_ref, att_ref,
                      *, n_tok, n_seq):
    n_rows = H_PER_DIL * n_tok
    kv_refs = ((k0_ref, v0_ref), (k1_ref, v1_ref), (k2_ref, v2_ref))
    row2 = lax.broadcasted_iota(jnp.int32, (n_rows, D_GROUP), 0)
    lane2 = lax.broadcasted_iota(jnp.int32, (n_rows, D_GROUP), 1)
    own_head = (lane2 // ATT_HEAD_DIM) == (row2 // n_tok)
    row1 = lax.broadcasted_iota(jnp.int32, (n_rows, 1), 0)
    tok1 = row1 % n_tok
    head1 = row1 // n_tok
    lane_head8 = lax.broadcasted_iota(jnp.int32, (n_tok, D_GROUP), 1) // ATT_HEAD_DIM
    scale = ATT_HEAD_DIM ** -0.5
    for b in range(n_seq):
        rows = slice(b * n_tok, (b + 1) * n_tok)
        _attn_sample_one(slope_ref, qkv_ref[rows, :], [(k[b], v[b]) for k, v in kv_refs],
                         att_ref.at[rows, :], n_tok, n_rows, own_head, tok1, head1, lane_head8, scale)


def _attn_sample_one(slope_ref, qkv, kv, att_ref, n_tok, n_rows, own_head, tok1, head1, lane_head8, scale):
    outs, lses = [], []
    for g, (win, dil) in enumerate(DILATION_PAIRS):
        k_buf, v_buf = kv[g]
        q = qkv[:, g * D_GROUP:(g + 1) * D_GROUP]
        k_new = qkv[:, D_ATT + g * D_GROUP:D_ATT + (g + 1) * D_GROUP]
        v_new = qkv[:, 2 * D_ATT + g * D_GROUP:2 * D_ATT + (g + 1) * D_GROUP]
        q_rows = jnp.where(own_head, jnp.concatenate([q] * H_PER_DIL, axis=0), 0.0)
        slope_1 = jnp.zeros((n_rows, 1), F32)
        for h in range(H_PER_DIL):
            slope_1 = jnp.where(head1 == h, slope_ref[g * H_PER_DIL + h], slope_1)

        s_buf = _dot(q_rows.astype(BF16), k_buf.astype(BF16)) * scale
        pos = lax.broadcasted_iota(jnp.int32, (n_rows, win), 1)
        tok_s = lax.broadcasted_iota(jnp.int32, (n_rows, win), 0) % n_tok
        dist = win + tok_s - pos
        on_grid = (pos >= tok_s) & (jnp.bitwise_and(dist, dil - 1) == 0)
        s_buf = jnp.where(on_grid, s_buf - slope_1 * dist.astype(F32), -jnp.inf)
        m = jnp.max(s_buf, axis=1, keepdims=True)
        s_new = []
        for tp in range(n_tok):
            sn = jnp.sum(q_rows * k_new[tp:tp + 1, :], axis=1, keepdims=True) * scale
            ok = (tok1 >= tp) & (jnp.bitwise_and(tok1 - tp, dil - 1) == 0)
            sn = jnp.where(ok, sn - slope_1 * (tok1 - tp).astype(F32), -jnp.inf)
            s_new.append(sn)
            m = jnp.maximum(m, sn)
        p_buf = jnp.exp(s_buf - m)
        l = jnp.sum(p_buf, axis=1, keepdims=True)
        o_rows = _dot_nt(p_buf.astype(BF16), v_buf.astype(BF16))
        for tp in range(n_tok):
            p_new = jnp.exp(s_new[tp] - m)
            l = l + p_new
            o_rows = o_rows + p_new * v_new[tp:tp + 1, :]
        o_rows = o_rows / l
        lse_rows = m + jnp.log(l)
        o = jnp.zeros((n_tok, D_GROUP), F32)
        lse = jnp.zeros((n_tok, D_GROUP), F32)
        for h in range(H_PER_DIL):
            o = jnp.where(lane_head8 == h, o_rows[h * n_tok:(h + 1) * n_tok, :], o)
            lse = jnp.where(lane_head8 == h, lse_rows[h * n_tok:(h + 1) * n_tok, :], lse)
        outs.append(o)
        lses.append(lse)

    m = jnp.maximum(jnp.maximum(lses[0], lses[1]), lses[2])
    e = [jnp.exp(x - m) for x in lses]
    den = e[0] + e[1] + e[2]
    for g in range(N_DIL):
        att_ref[:, g * D_GROUP:(g + 1) * D_GROUP] = outs[g] * (e[g] / den)


def _attn_sample(slopes, qkv, caches, *, n_tok, n_seq=4):
    t = qkv.shape[0]
    n_b = t // n_tok
    in_specs = [pl.BlockSpec(memory_space=pltpu.SMEM),
                pl.BlockSpec((n_seq * n_tok, 3 * D_ATT), lambda b: (b, 0))]
    args = [slopes, qkv]
    for g, (win, dil) in enumerate(DILATION_PAIRS):
        assert dil & (dil - 1) == 0, "dilations must be powers of two"
        for cache in caches[g]:
            assert cache.shape[1] == win, "window buffers must hold a full window"
            args.append(jnp.transpose(cache, (0, 2, 3, 1)).reshape(n_b, D_GROUP, win))
            in_specs.append(pl.BlockSpec((n_seq, D_GROUP, win), lambda b: (b, 0, 0)))
    return pl.pallas_call(
        functools.partial(_attn_sample_body, n_tok=n_tok, n_seq=n_seq),
        grid=(n_b // n_seq,),
        in_specs=in_specs,
        out_specs=pl.BlockSpec((n_seq * n_tok, D_ATT), lambda b: (b, 0)),
        out_shape=jax.ShapeDtypeStruct((t, D_ATT), F32),
        compiler_params=_params(1),
        name="attn_sample",
    )(*args)


def _outproj_body(*refs, combine):
    n_pieces = N_DIL * N_HALVES
    if combine:
        x_ref, ssm_ref = refs[:2]
        o_refs, l_refs = refs[2:2 + n_pieces], refs[2 + n_pieces:2 + 2 * n_pieces]
        w_ref, out_ref = refs[2 + 2 * n_pieces:]
        att = [None] * n_pieces
        for half in range(N_HALVES):
            idx = [g * N_HALVES + half for g in range(N_DIL)]
            lses = [l_refs[p][...] for p in idx]
            m = jnp.maximum(jnp.maximum(lses[0], lses[1]), lses[2])
            e = [jnp.exp(x - m) for x in lses]
            den = e[0] + e[1] + e[2]
            for p, w in zip(idx, e):
                att[p] = o_refs[p][...] * (w / den)
    else:
        x_ref, ssm_ref, att_ref, w_ref, out_ref = refs
        att = [att_ref[:, p * LANES:(p + 1) * LANES] for p in range(n_pieces)]
    mixed = jnp.concatenate([ssm_ref[...]] + att, axis=1).astype(BF16)
    out_ref[...] = x_ref[...] + _dot(mixed, w_ref[...].astype(BF16))


def _outproj(x, ssm, att, w_out, *, tm=512):
    t = x.shape[0]
    row = lambda i: (i, 0)
    combine = isinstance(att, (tuple, list))
    att_args = list(att[0]) + list(att[1]) if combine else [att]
    att_specs = [pl.BlockSpec((tm, a.shape[1]), row) for a in att_args]
    return pl.pallas_call(
        functools.partial(_outproj_body, combine=combine),
        grid=(t // tm,),
        in_specs=[pl.BlockSpec((tm, D_MODEL), row), pl.BlockSpec((tm, D_SSM), row)] + att_specs
                 + [pl.BlockSpec((D_MODEL, D_MODEL), lambda i: (0, 0), pipeline_mode=pl.Buffered(1))],
        out_specs=pl.BlockSpec((tm, D_MODEL), row),
        out_shape=jax.ShapeDtypeStruct((t, D_MODEL), F32),
        compiler_params=_params(1),
        name="outproj_prompt" if combine else "outproj_sample",
    )(x, ssm, *att_args, w_out)


def kernel(x_prompt, x_sample, cache_k0, cache_v0, cache_k1, cache_v1, cache_k2, cache_v2, state_ssm, state_conv, ffn1_norm, ffn1_w_gate, ffn1_w_up, ffn1_w_down, mix_norm, w_in, conv_w, conv_b, dt_bias, a_log, d_skip, ssm_norm, w_out, ffn2_norm, ffn2_w_gate, ffn2_w_up, ffn2_w_down, final_norm):
    assert w_in.shape[0] == 1, "single layer"
    batch, seq, _ = x_prompt.shape
    dec_batch, dec_seq, _ = x_sample.shape
    assert batch == 1 and dec_seq == SUBLANES

    row = lambda v: v.reshape(1, -1).astype(F32)
    pad_heads = lambda v: jnp.pad(v.reshape(1, -1).astype(F32), ((0, 0), (0, DT_PAD - N_SSM_HEADS)))
    w_in_t = jnp.transpose(w_in[0]).astype(BF16)
    ffn1 = (row(ffn1_norm[0]), ffn1_w_gate[0], ffn1_w_up[0], ffn1_w_down[0])
    ffn2 = (row(ffn2_norm[0]), ffn2_w_gate[0], ffn2_w_up[0], ffn2_w_down[0])
    w_o = w_out[0]
    ssd_params = (conv_w[0].astype(F32), row(conv_b[0]), pad_heads(dt_bias[0]), pad_heads(a_log[0]),
                  row(jnp.repeat(d_skip[0], SSM_HEAD_DIM)), row(ssm_norm[0]))
    slopes = jnp.exp2(-ALIBI_MAX_EXP * jnp.arange(1, N_ATT_HEADS + 1, dtype=F32) / N_ATT_HEADS)

    def back(x2):
        return _ffn(x2, *ffn2, row(final_norm))

    x1 = _ffn(x_prompt.reshape(seq, D_MODEL), *ffn1)
    xbc_p, qkv_p, ssm_p, hfin_p = _inproj_ssd(x1, row(mix_norm[0]), w_in_t, *ssd_params)
    outs, lses = [], []
    for g, (win, dil) in enumerate(DILATION_PAIRS):
        o, lse = _attn_prompt(slopes, qkv_p, group=g, dil=dil, seq=seq)
        outs += o
        lses += lse
    y_prompt = back(_outproj(x1, ssm_p, (outs, lses), w_o)).reshape(1, seq, D_MODEL)

    n_s = dec_batch * dec_seq
    x1 = _ffn(x_sample.reshape(n_s, D_MODEL), *ffn1)
    z, xbc_s, qkv_s, dt = _inproj(x1, row(mix_norm[0]), w_in_t)
    ssm_s, h_s = _ssd_sample(xbc_s, state_conv[0], z, dt, state_ssm[0].reshape(dec_batch, D_SSM, D_STATE),
                             *ssd_params, n_tok=dec_seq)
    caches = ((cache_k0[0], cache_v0[0]), (cache_k1[0], cache_v1[0]), (cache_k2[0], cache_v2[0]))
    att_s = _attn_sample(slopes, qkv_s, caches, n_tok=dec_seq)
    y_sample = back(_outproj(x1, ssm_s, att_s, w_o)).reshape(dec_batch, dec_seq, D_MODEL)

    head_shape = (H_PER_DIL, ATT_HEAD_DIM)
    p_kv, s_kv = [], []
    for g, (win, dil) in enumerate(DILATION_PAIRS):
        keep = min(win, seq)
        for base in (D_ATT, 2 * D_ATT):
            c0 = base + g * D_GROUP
            blocks = [qkv_p[c0 // LANES + half, seq - keep:] for half in range(N_HALVES)]
            p_kv.append(jnp.concatenate(blocks, axis=1).reshape((1, 1, keep) + head_shape))
            s_kv.append(qkv_s[:, c0:c0 + D_GROUP].reshape((1, dec_batch, dec_seq) + head_shape))
    p_ssm = hfin_p.reshape(1, 1, N_SSM_HEADS, SSM_HEAD_DIM, D_STATE)
    p_conv = xbc_p[seq - (CONV_W - 1):].reshape(1, 1, CONV_W - 1, CONV_DIM)
    s_ssm = h_s.reshape(1, dec_batch, N_SSM_HEADS, SSM_HEAD_DIM, D_STATE)
    s_conv = xbc_s.reshape(dec_batch, dec_seq, CONV_DIM)[:, dec_seq - (CONV_W - 1):].reshape(
        1, dec_batch, CONV_W - 1, CONV_DIM)
    return (y_prompt, y_sample, *p_kv, p_ssm, p_conv, *s_kv, s_ssm, s_conv)
```

```python
import functools

import numpy as np
import jax
import jax.numpy as jnp
from jax import lax
from jax.experimental import pallas as pl
from jax.experimental.pallas import tpu as pltpu

F32 = jnp.float32
BF16 = jnp.bfloat16

D_MODEL = 2048
D_FF = 5632
D_ATT = 768
ATT_HEAD_DIM = 64
H_PER_DIL = 4
D_GROUP = H_PER_DIL * ATT_HEAD_DIM
DILATION_PAIRS = ((128, 1), (512, 4), (2048, 16))
N_DIL = len(DILATION_PAIRS)
ATT_BLOCK = 128
D_SSM = 1280
SSM_HEAD_DIM = 64
N_SSM_HEADS = 20
N_SSM_GROUPS = 4
HEADS_PER_SSM_GROUP = 5
D_SSM_GROUP = D_SSM // N_SSM_GROUPS
D_STATE = 128
CONV_W = 4
CONV_DIM = 2304
SSD_CHUNK = 128
RMS_EPS = 1e-6
ALIBI_MAX_EXP = 8.0
N_ATT_HEADS = 12

LANES = 128
SUBLANES = 8
DT_PAD = LANES
IN_HEAD = D_SSM + CONV_DIM
IN_TAIL = N_SSM_HEADS + 3 * D_ATT
N_QKV_BLOCKS = 3 * D_ATT // LANES

VMEM_LIMIT = 60 * 1024 * 1024


def _params(n_grid_dims, vmem=VMEM_LIMIT):
    return pltpu.CompilerParams(dimension_semantics=("arbitrary",) * n_grid_dims,
                                vmem_limit_bytes=vmem)


def _silu(x):
    return x * (0.5 + 0.5 * jnp.tanh(0.5 * x))


def _softplus(x):
    return jnp.maximum(x, 0.0) + jnp.log(1.0 + jnp.exp(-jnp.abs(x)))


def _rms_scale(x):
    return x * lax.rsqrt(jnp.mean(x * x, axis=-1, keepdims=True) + RMS_EPS)


def _dot(a, b):
    return jnp.dot(a, b, preferred_element_type=F32)


def _dot_nt(a, b):
    return lax.dot_general(a, b, (((1,), (1,)), ((), ())), preferred_element_type=F32)


def _cumsum_rows(x):
    n = x.shape[0]
    rows = lax.broadcasted_iota(jnp.int32, x.shape, 0)
    shift = 1
    while shift < n:
        x = x + jnp.where(rows >= shift, pltpu.roll(x, shift, 0), 0.0)
        shift *= 2
    return x


FFN_ROW_CHUNKS = 4


def _ffn_body(*refs, n_f, final, hosted):
    refs = list(refs)
    x_ref, g_ref, wg_ref, wu_ref, wd_ref = refs[:5]
    del refs[:5]
    fg_ref = refs.pop(0) if final else None
    if hosted:
        slope_ref, qkv_ref, k0_ref, v0_ref, k1_ref, v1_ref, k2_ref, v2_ref = refs[:8]
        del refs[:8]
        o_ref, att_ref, h_ref = refs
    else:
        o_ref, h_ref = refs
    f = pl.program_id(1)
    tm = x_ref.shape[0]
    assert n_f >= 2 and tm % FFN_ROW_CHUNKS == 0

    def guest_scores():
        if hosted:
            k_bufs = [k0_ref[0], k1_ref[0], k2_ref[0]]
            return _attn_sample_scores(slope_ref, qkv_ref[...], k_bufs, qkv_ref.shape[0])

    def guest_outputs(state):
        if hosted:
            v_bufs = [v0_ref[0], v1_ref[0], v2_ref[0]]
            _attn_sample_outputs(state, v_bufs, att_ref, qkv_ref.shape[0])

    def weights():
        return wg_ref[...].astype(BF16), wu_ref[...].astype(BF16), wd_ref[...].astype(BF16)

    def partial_out(h, w, with_guest=False):
        gate = _dot(h, w[0])
        state = guest_scores() if with_guest else None
        up = _dot(h, w[1])
        if with_guest:
            guest_outputs(state)
        act = (_silu(gate) * up).astype(BF16)
        return _dot(act, w[2])

    chunks = [slice(c * tm // FFN_ROW_CHUNKS, (c + 1) * tm // FFN_ROW_CHUNKS) for c in range(FFN_ROW_CHUNKS)]

    @pl.when(f == 0)
    def _():
        w = weights()
        for c, rows in enumerate(chunks):
            h = (_rms_scale(x_ref[rows, :]) * g_ref[...]).astype(BF16)
            h_ref[rows, :] = h
            o_ref[rows, :] = partial_out(h, w, with_guest=(c == 1))

    @pl.when((f > 0) & (f < n_f - 1))
    def _():
        o_ref[...] += partial_out(h_ref[...], weights(), with_guest=True)

    @pl.when(f == n_f - 1)
    def _():
        w = weights()
        for c, rows in enumerate(chunks):
            y = x_ref[rows, :] + 0.5 * (o_ref[rows, :] + partial_out(h_ref[rows, :], w, with_guest=(c == 1)))
            if final:
                y = _rms_scale(y) * fg_ref[...]
            o_ref[rows, :] = y


def _ffn(x, gain, wg, wu, wd, final_gain=None, *, host=None, tm=1024, tf=256):
    t = x.shape[0]
    n_f = D_FF // tf
    final = final_gain is not None
    hosted = host is not None
    row = lambda i, f: (i, 0)
    const = lambda i, f: (0, 0)
    tile_mode = dict(pipeline_mode=pl.Buffered(1)) if hosted else {}
    in_specs = [pl.BlockSpec((tm, D_MODEL), row, **tile_mode), pl.BlockSpec((1, D_MODEL), const),
                pl.BlockSpec((D_MODEL, tf), lambda i, f: (0, f)),
                pl.BlockSpec((D_MODEL, tf), lambda i, f: (0, f)),
                pl.BlockSpec((tf, D_MODEL), lambda i, f: (f, 0))]
    args = [x, gain, wg, wu, wd]
    if final:
        in_specs.append(pl.BlockSpec((1, D_MODEL), const))
        args.append(final_gain)
    out_specs = pl.BlockSpec((tm, D_MODEL), row, **tile_mode)
    out_shape = jax.ShapeDtypeStruct((t, D_MODEL), F32)
    if hosted:
        slopes, qkv, caches, n_tok = host
        n_b = qkv.shape[0] // n_tok
        assert n_b <= (t // tm) * n_f, "one sequence per grid step"
        seq = lambda i, f: jnp.minimum(i * n_f + f, n_b - 1)
        in_specs += [pl.BlockSpec(memory_space=pltpu.SMEM),
                     pl.BlockSpec((n_tok, 3 * D_ATT), lambda i, f: (seq(i, f), 0))]
        in_specs += [pl.BlockSpec((1,) + c.shape[1:], lambda i, f: (seq(i, f), 0, 0)) for c in caches]
        args += [slopes, qkv, *caches]
        out_specs = [out_specs, pl.BlockSpec((n_tok, D_ATT), lambda i, f: (seq(i, f), 0))]
        out_shape = [out_shape, jax.ShapeDtypeStruct((qkv.shape[0], D_ATT), F32)]
    return pl.pallas_call(
        functools.partial(_ffn_body, n_f=n_f, final=final, hosted=hosted),
        grid=(t // tm, n_f),
        in_specs=in_specs,
        out_specs=out_specs,
        out_shape=out_shape,
        scratch_shapes=[pltpu.VMEM((tm, D_MODEL), BF16)],
        compiler_params=_params(2),
        name=("ffn_final" if final else "ffn") + ("_host" if hosted else ""),
    )(*args)


def _inproj_body(x_ref, g_ref, w_ref, z_ref, xbc_ref, qkv_ref, dt_ref):
    h = (_rms_scale(x_ref[...]) * g_ref[...]).astype(BF16)
    z_ref[...] = _dot_nt(h, w_ref[0:D_SSM, :])
    xbc_ref[...] = _dot_nt(h, w_ref[D_SSM:IN_HEAD, :])
    tail = _dot_nt(h, w_ref[IN_HEAD:IN_HEAD + IN_TAIL, :])
    lane = lax.broadcasted_iota(jnp.int32, (tail.shape[0], DT_PAD), 1)
    dt_ref[...] = jnp.where(lane < N_SSM_HEADS, tail[:, 0:DT_PAD], 0.0)
    qkv_ref[...] = tail[:, N_SSM_HEADS:IN_TAIL]


def _inproj(x, gain, w_t, *, tm=256):
    t = x.shape[0]
    row = lambda i: (i, 0)
    const = lambda i: (0, 0)
    widths = (D_SSM, CONV_DIM, 3 * D_ATT, DT_PAD)
    return pl.pallas_call(
        _inproj_body,
        grid=(t // tm,),
        in_specs=[pl.BlockSpec((tm, D_MODEL), row), pl.BlockSpec((1, D_MODEL), const),
                  pl.BlockSpec((IN_HEAD + IN_TAIL, D_MODEL), const, pipeline_mode=pl.Buffered(1))],
        out_specs=[pl.BlockSpec((tm, w), row) for w in widths],
        out_shape=[jax.ShapeDtypeStruct((t, w), F32) for w in widths],
        compiler_params=_params(1),
        name="inproj",
    )(x, gain, w_t)


def _pair_blockdiag(x, low_half):
    zero = jnp.zeros_like(x)
    return jnp.concatenate([jnp.where(low_half, x, zero), jnp.where(low_half, zero, x)],
                           axis=0).astype(BF16)


def _ssd_chunk(xbc_ref, z_ref, dt_ref, cw_ref, cb_ref, dtb_ref, alog_ref, dsk_ref, nrm_ref,
               y_ref, xf_scr, ht_scr):
    cl = SSD_CHUNK
    x = xbc_ref[...]
    xf_scr[SUBLANES:SUBLANES + cl, :] = x
    conv = cb_ref[...]
    for j in range(CONV_W - 1):
        conv = conv + xf_scr[SUBLANES - 3 + j:SUBLANES - 3 + j + cl, :] * cw_ref[j:j + 1, :]
    conv = conv + x * cw_ref[CONV_W - 1:CONV_W, :]
    xf_scr[0:SUBLANES, :] = x[cl - SUBLANES:cl, :]
    xa = _silu(conv)
    xs = xa[:, :D_SSM]
    bm = xa[:, D_SSM:D_SSM + N_SSM_GROUPS * D_STATE]
    cm = xa[:, D_SSM + N_SSM_GROUPS * D_STATE:]

    dt = _softplus(dt_ref[...] + dtb_ref[...])
    a = -jnp.exp(alog_ref[...])
    acum = _cumsum_rows(dt * a)
    w_end = dt * jnp.exp(acum[cl - 1:cl, :] - acum)
    acum_t = acum.T
    dt_t = dt.T
    w_end_t = w_end.T

    rows = lax.broadcasted_iota(jnp.int32, (cl, cl), 0)
    cols = lax.broadcasted_iota(jnp.int32, (cl, cl), 1)
    causal = rows >= cols
    low_half = cols < SSM_HEAD_DIM

    b_g = [bm[:, g * D_STATE:(g + 1) * D_STATE] for g in range(N_SSM_GROUPS)]
    c_g = [cm[:, g * D_STATE:(g + 1) * D_STATE].astype(BF16) for g in range(N_SSM_GROUPS)]
    b_t = [b.T for b in b_g]
    cb = [_dot_nt(c_g[g], b_g[g].astype(BF16)) for g in range(N_SSM_GROUPS)]

    sumsq = [jnp.zeros((cl, 1), F32) for _ in range(N_SSM_GROUPS)]
    for j in range(N_SSM_HEADS // 2):
        lo, hi = j * LANES, (j + 1) * LANES
        heads = (2 * j, 2 * j + 1)
        groups = tuple(h // HEADS_PER_SSM_GROUP for h in heads)
        xs_p = xs[:, lo:hi]
        x_bd = _pair_blockdiag(xs_p, low_half)
        ht_p = ht_scr[:, lo:hi]
        h_bd = _pair_blockdiag(ht_p, low_half)
        g_mats, w_mats, e_cols = [], [], []
        for h, g in zip(heads, groups):
            row_b = jnp.broadcast_to(acum_t[h:h + 1, :], (cl, cl))
            col_b = row_b.T
            decay = jnp.where(causal, jnp.exp(col_b - row_b), 0.0)
            g_mats.append((cb[g] * decay * dt_t[h:h + 1, :]).astype(BF16))
            w_mats.append((b_t[g] * w_end_t[h:h + 1, :]).astype(BF16))
            e_cols.append(jnp.exp(col_b))
        y_diag = _dot(jnp.concatenate(g_mats, axis=1), x_bd)
        y_off = _dot(jnp.concatenate([c_g[groups[0]], c_g[groups[1]]], axis=1), h_bd)
        y_off = y_off * jnp.where(low_half, e_cols[0], e_cols[1])
        state = _dot(jnp.concatenate(w_mats, axis=1), x_bd)
        chunk_decay = jnp.where(low_half[0:1, :], e_cols[0][cl - 1:cl, :], e_cols[1][cl - 1:cl, :])
        ht_scr[:, lo:hi] = ht_p * chunk_decay + state
        y_p = y_diag + y_off + dsk_ref[:, lo:hi] * xs_p
        y_p = y_p * _silu(z_ref[:, lo:hi])
        y_ref[:, lo:hi] = y_p
        sq = y_p * y_p
        sumsq[groups[0]] = sumsq[groups[0]] + jnp.sum(jnp.where(low_half, sq, 0.0), axis=1, keepdims=True)
        sumsq[groups[1]] = sumsq[groups[1]] + jnp.sum(jnp.where(low_half, 0.0, sq), axis=1, keepdims=True)

    scale = [lax.rsqrt(s / D_SSM_GROUP + RMS_EPS) for s in sumsq]
    for j in range(N_SSM_HEADS // 2):
        lo, hi = j * LANES, (j + 1) * LANES
        g0, g1 = (2 * j) // HEADS_PER_SSM_GROUP, (2 * j + 1) // HEADS_PER_SSM_GROUP
        y_ref[:, lo:hi] = y_ref[:, lo:hi] * jnp.where(low_half, scale[g0], scale[g1]) * nrm_ref[:, lo:hi]


def _inproj_ssd_body(x_ref, g_ref, w_ref, cw_ref, cb_ref, dtb_ref, alog_ref, dsk_ref, nrm_ref,
                     xbc_ref, qkv_ref, y_ref, hfin_ref,
                     xbc_a, z_a, dt_a, xbc_b, z_b, dt_b, xf_scr, ht_scr, *, n_blocks):
    s = pl.program_id(0)
    tm = x_ref.shape[0]
    bufs = ((xbc_a, z_a, dt_a), (xbc_b, z_b, dt_b))

    @pl.when(s <= 1)
    def _():
        xf_scr[0:SUBLANES, :] = jnp.zeros((SUBLANES, CONV_DIM), F32)
        ht_scr[...] = jnp.zeros_like(ht_scr)

    @pl.when(s == 0)
    def _():
        for ref in bufs[1]:
            ref[...] = jnp.zeros_like(ref)

    def step(parity):
        xbc_w, z_w, dt_w = bufs[parity]
        xbc_r, z_r, dt_r = bufs[1 - parity]
        h = (_rms_scale(x_ref[...]) * g_ref[...]).astype(BF16)
        z_w[...] = _dot_nt(h, w_ref[0:D_SSM, :])
        xbc = _dot_nt(h, w_ref[D_SSM:IN_HEAD, :])
        xbc_w[...] = xbc
        xbc_ref[...] = xbc
        tail = _dot_nt(h, w_ref[IN_HEAD:IN_HEAD + IN_TAIL, :])
        lane = lax.broadcasted_iota(jnp.int32, (tm, DT_PAD), 1)
        dt_w[...] = jnp.where(lane < N_SSM_HEADS, tail[:, 0:DT_PAD], 0.0)
        qkv = tail[:, N_SSM_HEADS:IN_TAIL]
        for j in range(N_QKV_BLOCKS):
            qkv_ref[j] = qkv[:, j * LANES:(j + 1) * LANES]
        for c in range(tm // SSD_CHUNK):
            rows = slice(c * SSD_CHUNK, (c + 1) * SSD_CHUNK)
            _ssd_chunk(xbc_r.at[rows, :], z_r.at[rows, :], dt_r.at[rows, :], cw_ref, cb_ref, dtb_ref,
                       alog_ref, dsk_ref, nrm_ref, y_ref.at[rows, :], xf_scr, ht_scr)

    @pl.when(s % 2 == 0)
    def _():
        step(0)

    @pl.when(s % 2 == 1)
    def _():
        step(1)

    @pl.when(s == n_blocks)
    def _():
        for j in range(D_SSM // LANES):
            hfin_ref[j * LANES:(j + 1) * LANES, :] = ht_scr[:, j * LANES:(j + 1) * LANES].T


def _inproj_ssd(x, gain, w_t, cw, cb, dtb, alog, dsk, nrm, *, tm=256):
    t = x.shape[0]
    n_blocks = t // tm
    cur = lambda s: (jnp.minimum(s, n_blocks - 1), 0)
    prev = lambda s: (jnp.maximum(s - 1, 0), 0)
    const = lambda s: (0, 0)
    vec = lambda n: pl.BlockSpec((1, n), const)
    buf = lambda n: pltpu.VMEM((tm, n), F32)
    return pl.pallas_call(
        functools.partial(_inproj_ssd_body, n_blocks=n_blocks),
        grid=(n_blocks + 1,),
        in_specs=[pl.BlockSpec((tm, D_MODEL), cur), vec(D_MODEL),
                  pl.BlockSpec((IN_HEAD + IN_TAIL, D_MODEL), const, pipeline_mode=pl.Buffered(1)),
                  pl.BlockSpec((CONV_W, CONV_DIM), const),
                  vec(CONV_DIM), vec(DT_PAD), vec(DT_PAD), vec(D_SSM), vec(D_SSM)],
        out_specs=[pl.BlockSpec((tm, CONV_DIM), cur),
                   pl.BlockSpec((N_QKV_BLOCKS, tm, LANES), lambda s: (0, jnp.minimum(s, n_blocks - 1), 0)),
                   pl.BlockSpec((tm, D_SSM), prev), pl.BlockSpec((D_SSM, D_STATE), const)],
        out_shape=[jax.ShapeDtypeStruct((t, CONV_DIM), F32),
                   jax.ShapeDtypeStruct((N_QKV_BLOCKS, t, LANES), F32),
                   jax.ShapeDtypeStruct((t, D_SSM), F32), jax.ShapeDtypeStruct((D_SSM, D_STATE), F32)],
        scratch_shapes=[buf(CONV_DIM), buf(D_SSM), buf(DT_PAD), buf(CONV_DIM), buf(D_SSM), buf(DT_PAD),
                        pltpu.VMEM((SUBLANES + SSD_CHUNK, CONV_DIM), F32),
                        pltpu.VMEM((D_STATE, D_SSM), F32)],
        compiler_params=_params(1),
        name="inproj_ssd",
    )(x, gain, w_t, cw, cb, dtb, alog, dsk, nrm)


def _ssd_sample_body(xbc_ref, cs_ref, z_ref, dt_ref, h0_ref, cw_ref, cb_ref, dtb_ref, alog_ref,
                     dsk_ref, nrm_ref, y_ref, hout_ref, hist_scr, xwt_scr, ecd_scr, eexp_scr, yoff_scr,
                     *, n_tok, n_seq):
    n_rows = n_seq * n_tok
    assert n_rows == LANES

    @pl.when(pl.program_id(0) == 0)
    def _():
        hist_scr[...] = jnp.zeros_like(hist_scr)

    x = xbc_ref[...]
    hist_scr[:, 0:CONV_W - 1, :] = cs_ref[...]
    hist = hist_scr[...].reshape(n_rows, CONV_DIM)
    tok = lax.broadcasted_iota(jnp.int32, (n_rows, 1), 0) % n_tok
    conv = cb_ref[...] + x * cw_ref[CONV_W - 1:CONV_W, :]
    for k in range(1, CONV_W):
        back = CONV_W - 1 - k
        from_hist = hist if back == 0 else pltpu.roll(hist, n_rows - back, 0)
        shifted = jnp.where(tok < k, from_hist, pltpu.roll(x, k, 0))
        conv = conv + shifted * cw_ref[back:back + 1, :]
    xa = _silu(conv)
    xs = xa[:, :D_SSM]
    bm = xa[:, D_SSM:D_SSM + N_SSM_GROUPS * D_STATE]
    cm = xa[:, D_SSM + N_SSM_GROUPS * D_STATE:]

    dt = _softplus(dt_ref[...] + dtb_ref[...])
    dta = dt * -jnp.exp(alog_ref[...])
    tok_l = lax.broadcasted_iota(jnp.int32, (n_rows, LANES), 0) % n_tok
    acum = dta
    to_end = jnp.where(tok_l + 1 < n_tok, pltpu.roll(dta, n_rows - 1, 0), 0.0)
    shift = 1
    while shift < n_tok:
        acum = acum + jnp.where(tok_l >= shift, pltpu.roll(acum, shift, 0), 0.0)
        to_end = to_end + jnp.where(tok_l + shift < n_tok, pltpu.roll(to_end, n_rows - shift, 0), 0.0)
        shift *= 2
    w_end = dt * jnp.exp(to_end)
    acum_t = acum.T
    dt_t = dt.T
    w_end_t = w_end.T

    rows = lax.broadcasted_iota(jnp.int32, (n_rows, n_rows), 0)
    cols = lax.broadcasted_iota(jnp.int32, (n_rows, n_rows), 1)
    causal = (rows // n_tok == cols // n_tok) & (rows >= cols)
    low_half = cols < SSM_HEAD_DIM
    b_g = [bm[:, g * D_STATE:(g + 1) * D_STATE].astype(BF16) for g in range(N_SSM_GROUPS)]
    c_g = [cm[:, g * D_STATE:(g + 1) * D_STATE].astype(BF16) for g in range(N_SSM_GROUPS)]
    cb = [_dot_nt(c_g[g], b_g[g]) for g in range(N_SSM_GROUPS)]

    for j in range(N_SSM_HEADS // 2):
        lo, hi = j * LANES, (j + 1) * LANES
        heads = (2 * j, 2 * j + 1)
        xs_p = xs[:, lo:hi]
        g_mats, e_cols, w_cols = [], [], []
        for h in heads:
            g = h // HEADS_PER_SSM_GROUP
            row_b = jnp.broadcast_to(acum_t[h:h + 1, :], (n_rows, n_rows))
            col_b = row_b.T
            decay = jnp.where(causal, jnp.exp(col_b - row_b), 0.0)
            g_mats.append((cb[g] * decay * dt_t[h:h + 1, :]).astype(BF16))
            e_col = jnp.exp(col_b)
            ecd_scr[h] = e_col
            e_cols.append(e_col)
            w_cols.append(jnp.broadcast_to(w_end_t[h:h + 1, :], (n_rows, n_rows)).T)
        y_diag = _dot(jnp.concatenate(g_mats, axis=1), _pair_blockdiag(xs_p, low_half))
        y_ref[:, lo:hi] = y_diag + dsk_ref[:, lo:hi] * xs_p
        eexp_scr[:, lo:hi] = jnp.where(low_half, e_cols[0], e_cols[1])
        xwt_scr[lo:hi, :] = (xs_p * jnp.where(low_half, w_cols[0], w_cols[1])).T

    col_group = lax.broadcasted_iota(jnp.int32, (n_tok, D_SSM), 1) // D_SSM_GROUP
    lane_seq = lax.broadcasted_iota(jnp.int32, (D_SSM_GROUP, n_rows), 1) // n_tok
    for b in range(n_seq):
        r0 = b * n_tok
        h0 = h0_ref[b]
        c_stack = jnp.concatenate([c_g[g][r0:r0 + n_tok, :] for g in range(N_SSM_GROUPS)], axis=0)
        y_all = _dot_nt(c_stack, h0.astype(BF16))
        y_off = jnp.zeros((n_tok, D_SSM), F32)
        for g in range(N_SSM_GROUPS):
            y_off = jnp.where(col_group == g, y_all[g * n_tok:(g + 1) * n_tok, :], y_off)
        yoff_scr[r0:r0 + n_tok, :] = y_off
        for g in range(N_SSM_GROUPS):
            c0 = g * D_SSM_GROUP
            lhs = jnp.where(lane_seq == b, xwt_scr[c0:c0 + D_SSM_GROUP, :], 0.0).astype(BF16)
            upd = _dot(lhs, b_g[g])
            for e in range(HEADS_PER_SSM_GROUP):
                h = g * HEADS_PER_SSM_GROUP + e
                p0 = h * SSM_HEAD_DIM
                decay = ecd_scr[h, r0 + n_tok - 1:r0 + n_tok, :]
                hout_ref[b, p0:p0 + SSM_HEAD_DIM, :] = (
                    h0[p0:p0 + SSM_HEAD_DIM, :] * decay + upd[e * SSM_HEAD_DIM:(e + 1) * SSM_HEAD_DIM, :])

    y = y_ref[...] + yoff_scr[...] * eexp_scr[...]
    y = y * _silu(z_ref[...])
    sq = y * y
    col_group = lax.broadcasted_iota(jnp.int32, (n_rows, D_SSM), 1) // D_SSM_GROUP
    scale = jnp.zeros((n_rows, D_SSM), F32)
    for g in range(N_SSM_GROUPS):
        ms = jnp.sum(jnp.where(col_group == g, sq, 0.0), axis=1, keepdims=True) / D_SSM_GROUP
        scale = jnp.where(col_group == g, lax.rsqrt(ms + RMS_EPS), scale)
    y_ref[...] = y * scale * nrm_ref[...]


def _ssd_sample(xbc, conv_state, z, dt, h0, cw, cb, dtb, alog, dsk, nrm, *, n_tok):
    t = xbc.shape[0]
    n_b = t // n_tok
    n_seq = LANES // n_tok
    rows = n_seq * n_tok
    row = lambda b: (b, 0)
    const = lambda b: (0, 0)
    vec = lambda n: pl.BlockSpec((1, n), const)
    return pl.pallas_call(
        functools.partial(_ssd_sample_body, n_tok=n_tok, n_seq=n_seq),
        grid=(n_b // n_seq,),
        in_specs=[pl.BlockSpec((rows, CONV_DIM), row),
                  pl.BlockSpec((n_seq, CONV_W - 1, CONV_DIM), lambda b: (b, 0, 0)),
                  pl.BlockSpec((rows, D_SSM), row), pl.BlockSpec((rows, DT_PAD), row),
                  pl.BlockSpec((n_seq, D_SSM, D_STATE), lambda b: (b, 0, 0)),
                  pl.BlockSpec((CONV_W, CONV_DIM), const),
                  vec(CONV_DIM), vec(DT_PAD), vec(DT_PAD), vec(D_SSM), vec(D_SSM)],
        out_specs=[pl.BlockSpec((rows, D_SSM), row),
                   pl.BlockSpec((n_seq, D_SSM, D_STATE), lambda b: (b, 0, 0))],
        out_shape=[jax.ShapeDtypeStruct((t, D_SSM), F32),
                   jax.ShapeDtypeStruct((n_b, D_SSM, D_STATE), F32)],
        scratch_shapes=[pltpu.VMEM((n_seq, n_tok, CONV_DIM), F32),
                        pltpu.VMEM((D_SSM, rows), F32),
                        pltpu.VMEM((N_SSM_HEADS, rows, LANES), F32),
                        pltpu.VMEM((rows, D_SSM), F32),
                        pltpu.VMEM((rows, D_SSM), F32)],
        compiler_params=_params(1),
        name="ssd_sample",
    )(xbc, conv_state, z, dt, h0, cw, cb, dtb, alog, dsk, nrm)


HEADS_PER_HALF = LANES // ATT_HEAD_DIM
N_HALVES = D_GROUP // LANES


ATT_UNITS = 4


def _attn_prompt_body(slope_ref, *refs, dil, group):
    n_in = 5 * N_HALVES
    in_refs, o_refs, l_refs = refs[:n_in], refs[n_in:n_in + N_HALVES], refs[n_in + N_HALVES:]
    i = pl.program_id(0)
    nb = ATT_BLOCK
    qi = lax.broadcasted_iota(jnp.int32, (nb, 2 * nb), 0)
    kj = lax.broadcasted_iota(jnp.int32, (nb, 2 * nb), 1)
    dist = nb + qi - kj
    band = (dist >= 0) & (dist <= nb)
    band_first = band & (kj >= jnp.where(i > 0, 0, nb))
    dist_f = (dist * dil).astype(F32)
    alibi = [-slope_ref[group * H_PER_DIL + h] * dist_f for h in range(H_PER_DIL)]
    bias_first = [jnp.where(band_first, a, -jnp.inf) for a in alibi]
    bias_inner = [jnp.where(band, a, -jnp.inf) for a in alibi] if dil == 1 else bias_first
    lane_head = lax.broadcasted_iota(jnp.int32, (nb, LANES), 1) // ATT_HEAD_DIM
    for u in range(ATT_UNITS):
        if dil == 1:
            rows = pl.ds(u * nb, nb)
            rows_prev = pl.ds((ATT_UNITS - 1) * nb, nb) if u == 0 else pl.ds((u - 1) * nb, nb)
            bias = bias_first if u == 0 else bias_inner
        else:
            rows = rows_prev = pl.ds(pl.program_id(1) * ATT_UNITS + u, nb, stride=dil)
            bias = bias_first
        for half in range(N_HALVES):
            q_ref, kp_ref, kc_ref, vp_ref, vc_ref = in_refs[5 * half:5 * half + 5]
            if dil == 1 and u > 0:
                kp_ref, vp_ref = kc_ref, vc_ref
            q = q_ref[rows, :] * (ATT_HEAD_DIM ** -0.5)
            k = jnp.concatenate([kp_ref[rows_prev, :], kc_ref[rows, :]], axis=0).astype(BF16)
            v = jnp.concatenate([vp_ref[rows_prev, :], vc_ref[rows, :]], axis=0).astype(BF16)
            out = jnp.zeros((nb, LANES), F32)
            lse = jnp.zeros((nb, LANES), F32)
            for hh in range(HEADS_PER_HALF):
                mine = lane_head == hh
                qh = jnp.where(mine, q, 0.0).astype(BF16)
                s = _dot_nt(qh, k) + bias[half * HEADS_PER_HALF + hh]
                m = jnp.max(s, axis=1, keepdims=True)
                p = jnp.exp(s - m)
                l = jnp.sum(p, axis=1, keepdims=True)
                pv = _dot(p.astype(BF16), v)
                out = jnp.where(mine, pv / l, out)
                lse = jnp.where(mine, m + jnp.log(l), lse)
            o_refs[half][rows, :] = out
            l_refs[half][rows, :] = lse


def _attn_prompt(slopes, qkv, *, group, dil, seq):
    assert dil == 1 or dil % ATT_UNITS == 0
    blk_rows = max(dil, ATT_UNITS) * ATT_BLOCK
    n_blk = seq // blk_rows
    n_res_steps = max(dil // ATT_UNITS, 1)
    blk = (blk_rows, LANES)
    cur = lambda c: pl.BlockSpec((None,) + blk, lambda i, r: (c, i, 0))
    prev = lambda c: pl.BlockSpec((None,) + blk, lambda i, r: (c, jnp.maximum(i - 1, 0), 0))
    in_specs = [pl.BlockSpec(memory_space=pltpu.SMEM)]
    for half in range(N_HALVES):
        q_col, k_col, v_col = (N_HALVES * (part * N_DIL + group) + half for part in range(3))
        in_specs += [cur(q_col), prev(k_col), cur(k_col), prev(v_col), cur(v_col)]
    out_blk = pl.BlockSpec(blk, lambda i, r: (i, 0))
    res = pl.pallas_call(
        functools.partial(_attn_prompt_body, dil=dil, group=group),
        grid=(n_blk, n_res_steps),
        in_specs=in_specs,
        out_specs=[out_blk] * (2 * N_HALVES),
        out_shape=[jax.ShapeDtypeStruct((seq, LANES), F32)] * (2 * N_HALVES),
        compiler_params=_params(2),
        name=f"attn_prompt_g{group}",
    )(slopes, *([qkv] * (5 * N_HALVES)))
    return res[:N_HALVES], res[N_HALVES:]


def _window_buffers(caches):
    out = []
    for (win, dil), pair in zip(DILATION_PAIRS, caches):
        assert dil & (dil - 1) == 0, "dilations must be powers of two"
        for cache in pair:
            assert cache.shape[1] == win, "window buffers must hold a full window"
            out.append(jnp.transpose(cache, (0, 2, 3, 1)).reshape(cache.shape[0], D_GROUP, win))
    return out


def _attn_sample_scores(slope_ref, qkv, k_bufs, n_tok):
    n_rows = H_PER_DIL * n_tok
    row2 = lax.broadcasted_iota(jnp.int32, (n_rows, D_GROUP), 0)
    lane2 = lax.broadcasted_iota(jnp.int32, (n_rows, D_GROUP), 1)
    own_head = (lane2 // ATT_HEAD_DIM) == (row2 // n_tok)
    row1 = lax.broadcasted_iota(jnp.int32, (n_rows, 1), 0)
    tok1 = row1 % n_tok
    head1 = row1 // n_tok
    scale = ATT_HEAD_DIM ** -0.5
    state = []
    for g, (win, dil) in enumerate(DILATION_PAIRS):
        q = qkv[:, g * D_GROUP:(g + 1) * D_GROUP]
        k_new = qkv[:, D_ATT + g * D_GROUP:D_ATT + (g + 1) * D_GROUP]
        v_new = qkv[:, 2 * D_ATT + g * D_GROUP:2 * D_ATT + (g + 1) * D_GROUP]
        q_rows = jnp.where(own_head, jnp.concatenate([q] * H_PER_DIL, axis=0), 0.0)
        slope_1 = jnp.zeros((n_rows, 1), F32)
        for h in range(H_PER_DIL):
            slope_1 = jnp.where(head1 == h, slope_ref[g * H_PER_DIL + h], slope_1)

        s_buf = _dot(q_rows.astype(BF16), k_bufs[g].astype(BF16)) * scale
        pos = lax.broadcasted_iota(jnp.int32, (n_rows, win), 1)
        tok_s = lax.broadcasted_iota(jnp.int32, (n_rows, win), 0) % n_tok
        dist = win + tok_s - pos
        on_grid = (pos >= tok_s) & (jnp.bitwise_and(dist, dil - 1) == 0)
        s_buf = jnp.where(on_grid, s_buf - slope_1 * dist.astype(F32), -jnp.inf)
        m = jnp.max(s_buf, axis=1, keepdims=True)
        s_new = []
        for tp in range(n_tok):
            sn = jnp.sum(q_rows * k_new[tp:tp + 1, :], axis=1, keepdims=True) * scale
            ok = (tok1 >= tp) & (jnp.bitwise_and(tok1 - tp, dil - 1) == 0)
            sn = jnp.where(ok, sn - slope_1 * (tok1 - tp).astype(F32), -jnp.inf)
            s_new.append(sn)
            m = jnp.maximum(m, sn)
        state.append((s_buf, s_new, m, v_new))
    return state


def _attn_sample_outputs(state, v_bufs, att_ref, n_tok):
    lane_head8 = lax.broadcasted_iota(jnp.int32, (n_tok, D_GROUP), 1) // ATT_HEAD_DIM
    outs, lses = [], []
    for g, (s_buf, s_new, m, v_new) in enumerate(state):
        p_buf = jnp.exp(s_buf - m)
        l = jnp.sum(p_buf, axis=1, keepdims=True)
        o_rows = _dot_nt(p_buf.astype(BF16), v_bufs[g].astype(BF16))
        for tp in range(n_tok):
            p_new = jnp.exp(s_new[tp] - m)
            l = l + p_new
            o_rows = o_rows + p_new * v_new[tp:tp + 1, :]
        o_rows = o_rows / l
        lse_rows = m + jnp.log(l)
        o = jnp.zeros((n_tok, D_GROUP), F32)
        lse = jnp.zeros((n_tok, D_GROUP), F32)
        for h in range(H_PER_DIL):
            o = jnp.where(lane_head8 == h, o_rows[h * n_tok:(h + 1) * n_tok, :], o)
            lse = jnp.where(lane_head8 == h, lse_rows[h * n_tok:(h + 1) * n_tok, :], lse)
        outs.append(o)
        lses.append(lse)

    m = jnp.maximum(jnp.maximum(lses[0], lses[1]), lses[2])
    e = [jnp.exp(x - m) for x in lses]
    den = e[0] + e[1] + e[2]
    for g in range(N_DIL):
        att_ref[:, g * D_GROUP:(g + 1) * D_GROUP] = outs[g] * (e[g] / den)


def _outproj_body(*refs, combine):
    n_pieces = N_DIL * N_HALVES
    if combine:
        x_ref, ssm_ref = refs[:2]
        o_refs, l_refs = refs[2:2 + n_pieces], refs[2 + n_pieces:2 + 2 * n_pieces]
        w_ref, out_ref = refs[2 + 2 * n_pieces:]
        att = [None] * n_pieces
        for half in range(N_HALVES):
            idx = [g * N_HALVES + half for g in range(N_DIL)]
            lses = [l_refs[p][...] for p in idx]
            m = jnp.maximum(jnp.maximum(lses[0], lses[1]), lses[2])
            e = [jnp.exp(x - m) for x in lses]
            den = e[0] + e[1] + e[2]
            for p, w in zip(idx, e):
                att[p] = o_refs[p][...] * (w / den)
    else:
        x_ref, ssm_ref, att_ref, w_ref, out_ref = refs
        att = [att_ref[:, p * LANES:(p + 1) * LANES] for p in range(n_pieces)]
    mixed = jnp.concatenate([ssm_ref[...]] + att, axis=1).astype(BF16)
    out_ref[...] = x_ref[...] + _dot(mixed, w_ref[...].astype(BF16))


def _outproj(x, ssm, att, w_out, *, tm=512):
    t = x.shape[0]
    row = lambda i: (i, 0)
    combine = isinstance(att, (tuple, list))
    att_args = list(att[0]) + list(att[1]) if combine else [att]
    att_specs = [pl.BlockSpec((tm, a.shape[1]), row) for a in att_args]
    return pl.pallas_call(
        functools.partial(_outproj_body, combine=combine),
        grid=(t // tm,),
        in_specs=[pl.BlockSpec((tm, D_MODEL), row), pl.BlockSpec((tm, D_SSM), row)] + att_specs
                 + [pl.BlockSpec((D_MODEL, D_MODEL), lambda i: (0, 0), pipeline_mode=pl.Buffered(1))],
        out_specs=pl.BlockSpec((tm, D_MODEL), row),
        out_shape=jax.ShapeDtypeStruct((t, D_MODEL), F32),
        compiler_params=_params(1),
        name="outproj_prompt" if combine else "outproj_sample",
    )(x, ssm, *att_args, w_out)


def kernel(x_prompt, x_sample, cache_k0, cache_v0, cache_k1, cache_v1, cache_k2, cache_v2, state_ssm, state_conv, ffn1_norm, ffn1_w_gate, ffn1_w_up, ffn1_w_down, mix_norm, w_in, conv_w, conv_b, dt_bias, a_log, d_skip, ssm_norm, w_out, ffn2_norm, ffn2_w_gate, ffn2_w_up, ffn2_w_down, final_norm):
    assert w_in.shape[0] == 1, "single layer"
    batch, seq, _ = x_prompt.shape
    dec_batch, dec_seq, _ = x_sample.shape
    assert batch == 1 and dec_seq == SUBLANES

    row = lambda v: v.reshape(1, -1).astype(F32)
    pad_heads = lambda v: jnp.pad(v.reshape(1, -1).astype(F32), ((0, 0), (0, DT_PAD - N_SSM_HEADS)))
    w_in_t = jnp.transpose(w_in[0]).astype(BF16)
    ffn1 = (row(ffn1_norm[0]), ffn1_w_gate[0], ffn1_w_up[0], ffn1_w_down[0])
    ffn2 = (row(ffn2_norm[0]), ffn2_w_gate[0], ffn2_w_up[0], ffn2_w_down[0])
    w_o = w_out[0]
    ssd_params = (conv_w[0].astype(F32), row(conv_b[0]), pad_heads(dt_bias[0]), pad_heads(a_log[0]),
                  row(jnp.repeat(d_skip[0], SSM_HEAD_DIM)), row(ssm_norm[0]))
    slopes = jnp.exp2(-ALIBI_MAX_EXP * jnp.arange(1, N_ATT_HEADS + 1, dtype=F32) / N_ATT_HEADS)

    def back(x2):
        return _ffn(x2, *ffn2, row(final_norm))

    n_s = dec_batch * dec_seq
    x1_s = _ffn(x_sample.reshape(n_s, D_MODEL), *ffn1)
    z, xbc_s, qkv_s, dt = _inproj(x1_s, row(mix_norm[0]), w_in_t)
    ssm_s, h_s = _ssd_sample(xbc_s, state_conv[0], z, dt, state_ssm[0].reshape(dec_batch, D_SSM, D_STATE),
                             *ssd_params, n_tok=dec_seq)
    caches = ((cache_k0[0], cache_v0[0]), (cache_k1[0], cache_v1[0]), (cache_k2[0], cache_v2[0]))

    x1, att_s = _ffn(x_prompt.reshape(seq, D_MODEL), *ffn1,
                     host=(slopes, qkv_s, _window_buffers(caches), dec_seq))
    xbc_p, qkv_p, ssm_p, hfin_p = _inproj_ssd(x1, row(mix_norm[0]), w_in_t, *ssd_params)
    outs, lses = [], []
    for g, (win, dil) in enumerate(DILATION_PAIRS):
        o, lse = _attn_prompt(slopes, qkv_p, group=g, dil=dil, seq=seq)
        outs += o
        lses += lse
    y_prompt = back(_outproj(x1, ssm_p, (outs, lses), w_o)).reshape(1, seq, D_MODEL)

    y_sample = back(_outproj(x1_s, ssm_s, att_s, w_o)).reshape(dec_batch, dec_seq, D_MODEL)

    head_shape = (H_PER_DIL, ATT_HEAD_DIM)
    p_kv, s_kv = [], []
    for g, (win, dil) in enumerate(DILATION_PAIRS):
        keep = min(win, seq)
        for base in (D_ATT, 2 * D_ATT):
            c0 = base + g * D_GROUP
            blocks = [qkv_p[c0 // LANES + half, seq - keep:] for half in range(N_HALVES)]
            p_kv.append(jnp.concatenate(blocks, axis=1).reshape((1, 1, keep) + head_shape))
            s_kv.append(qkv_s[:, c0:c0 + D_GROUP].reshape((1, dec_batch, dec_seq) + head_shape))
    p_ssm = hfin_p.reshape(1, 1, N_SSM_HEADS, SSM_HEAD_DIM, D_STATE)
    p_conv = xbc_p[seq - (CONV_W - 1):].reshape(1, 1, CONV_W - 1, CONV_DIM)
    s_ssm = h_s.reshape(1, dec_batch, N_SSM_HEADS, SSM_HEAD_DIM, D_STATE)
    s_conv = xbc_s.reshape(dec_batch, dec_seq, CONV_DIM)[:, dec_seq - (CONV_W - 1):].reshape(
        1, dec_batch, CONV_W - 1, CONV_DIM)
    return (y_prompt, y_sample, *p_kv, p_ssm, p_conv, *s_kv, s_ssm, s_conv)
```

```python
import functools

import numpy as np
import jax
import jax.numpy as jnp
from jax import lax
from jax.experimental import pallas as pl
from jax.experimental.pallas import tpu as pltpu

F32 = jnp.float32
BF16 = jnp.bfloat16

D_MODEL = 2048
D_FF = 5632
D_ATT = 768
ATT_HEAD_DIM = 64
H_PER_DIL = 4
D_GROUP = H_PER_DIL * ATT_HEAD_DIM
DILATION_PAIRS = ((128, 1), (512, 4), (2048, 16))
N_DIL = len(DILATION_PAIRS)
ATT_BLOCK = 128
D_SSM = 1280
SSM_HEAD_DIM = 64
N_SSM_HEADS = 20
N_SSM_GROUPS = 4
HEADS_PER_SSM_GROUP = 5
D_SSM_GROUP = D_SSM // N_SSM_GROUPS
D_STATE = 128
CONV_W = 4
CONV_DIM = 2304
SSD_CHUNK = 128
RMS_EPS = 1e-6
ALIBI_MAX_EXP = 8.0
N_ATT_HEADS = 12

LANES = 128
SUBLANES = 8
DT_PAD = LANES
IN_HEAD = D_SSM + CONV_DIM
IN_TAIL = N_SSM_HEADS + 3 * D_ATT
N_QKV_BLOCKS = 3 * D_ATT // LANES

VMEM_LIMIT = 60 * 1024 * 1024


def _params(n_grid_dims, vmem=VMEM_LIMIT):
    return pltpu.CompilerParams(dimension_semantics=("arbitrary",) * n_grid_dims,
                                vmem_limit_bytes=vmem)


def _silu(x):
    return x * (0.5 + 0.5 * jnp.tanh(0.5 * x))


def _softplus(x):
    return jnp.maximum(x, 0.0) + jnp.log(1.0 + jnp.exp(-jnp.abs(x)))


def _rms_scale(x):
    return x * lax.rsqrt(jnp.mean(x * x, axis=-1, keepdims=True) + RMS_EPS)


def _dot(a, b):
    return jnp.dot(a, b, preferred_element_type=F32)


def _dot_nt(a, b):
    return lax.dot_general(a, b, (((1,), (1,)), ((), ())), preferred_element_type=F32)


def _cumsum_rows(x):
    n = x.shape[0]
    rows = lax.broadcasted_iota(jnp.int32, x.shape, 0)
    shift = 1
    while shift < n:
        x = x + jnp.where(rows >= shift, pltpu.roll(x, shift, 0), 0.0)
        shift *= 2
    return x


FFN_ROW_CHUNKS = 4


def _ffn_body(*refs, n_f, final, n_guest):
    hosted = n_guest > 0
    refs = list(refs)
    x_ref, g_ref, wg_ref, wu_ref, wd_ref = refs[:5]
    del refs[:5]
    fg_ref = refs.pop(0) if final else None
    if hosted:
        slope_ref, qkv_ref, k0_ref, v0_ref, k1_ref, v1_ref, k2_ref, v2_ref = refs[:8]
        del refs[:8]
        o_ref, att_ref, h_ref = refs
    else:
        o_ref, h_ref = refs
    f = pl.program_id(1)
    tm = x_ref.shape[0]
    assert n_f >= 2 and tm % FFN_ROW_CHUNKS == 0

    def guest_scores():
        if hosted:
            k_bufs = [k0_ref[0], k1_ref[0], k2_ref[0]]
            return _attn_sample_scores(slope_ref, qkv_ref[...], k_bufs, qkv_ref.shape[0])

    def guest_outputs(state):
        if hosted:
            v_bufs = [v0_ref[0], v1_ref[0], v2_ref[0]]
            _attn_sample_outputs(state, v_bufs, att_ref, qkv_ref.shape[0])

    def weights():
        return wg_ref[...].astype(BF16), wu_ref[...].astype(BF16), wd_ref[...].astype(BF16)

    def partial_out(h, w, with_guest=False):
        gate = _dot(h, w[0])
        state = guest_scores() if with_guest else None
        up = _dot(h, w[1])
        if with_guest:
            guest_outputs(state)
        act = (_silu(gate) * up).astype(BF16)
        return _dot(act, w[2])

    chunks = [slice(c * tm // FFN_ROW_CHUNKS, (c + 1) * tm // FFN_ROW_CHUNKS) for c in range(FFN_ROW_CHUNKS)]

    @pl.when(f == 0)
    def _():
        w = weights()
        for c, rows in enumerate(chunks):
            h = (_rms_scale(x_ref[rows, :]) * g_ref[...]).astype(BF16)
            h_ref[rows, :] = h
            o_ref[rows, :] = partial_out(h, w, with_guest=(c == 1))

    middle = (f > 0) & (f < n_f - 1)
    guest_left = pl.program_id(0) * n_f + f < n_guest

    if hosted:
        @pl.when(middle & guest_left)
        def _():
            o_ref[...] += partial_out(h_ref[...], weights(), with_guest=True)

    @pl.when(middle & jnp.logical_not(guest_left))
    def _():
        o_ref[...] += partial_out(h_ref[...], weights())

    @pl.when(f == n_f - 1)
    def _():
        w = weights()
        for c, rows in enumerate(chunks):
            y = x_ref[rows, :] + 0.5 * (o_ref[rows, :] + partial_out(h_ref[rows, :], w, with_guest=(c == 1)))
            if final:
                y = _rms_scale(y) * fg_ref[...]
            o_ref[rows, :] = y


def _ffn(x, gain, wg, wu, wd, final_gain=None, *, host=None, tm=1024, tf=256):
    t = x.shape[0]
    n_f = D_FF // tf
    final = final_gain is not None
    hosted = host is not None
    row = lambda i, f: (i, 0)
    const = lambda i, f: (0, 0)
    tile_mode = dict(pipeline_mode=pl.Buffered(1)) if hosted else {}
    n_guest = 0
    in_specs = [pl.BlockSpec((tm, D_MODEL), row), pl.BlockSpec((1, D_MODEL), const),
                pl.BlockSpec((D_MODEL, tf), lambda i, f: (0, f)),
                pl.BlockSpec((D_MODEL, tf), lambda i, f: (0, f)),
                pl.BlockSpec((tf, D_MODEL), lambda i, f: (f, 0))]
    args = [x, gain, wg, wu, wd]
    if final:
        in_specs.append(pl.BlockSpec((1, D_MODEL), const))
        args.append(final_gain)
    out_specs = pl.BlockSpec((tm, D_MODEL), row, **tile_mode)
    out_shape = jax.ShapeDtypeStruct((t, D_MODEL), F32)
    if hosted:
        slopes, qkv, caches, n_tok = host
        n_guest = n_b = qkv.shape[0] // n_tok
        assert n_b <= (t // tm) * n_f, "one sequence per grid step"
        seq = lambda i, f: jnp.minimum(i * n_f + f, n_b - 1)
        in_specs += [pl.BlockSpec(memory_space=pltpu.SMEM),
                     pl.BlockSpec((n_tok, 3 * D_ATT), lambda i, f: (seq(i, f), 0))]
        in_specs += [pl.BlockSpec((1,) + c.shape[1:], lambda i, f: (seq(i, f), 0, 0)) for c in caches]
        args += [slopes, qkv, *caches]
        out_specs = [out_specs, pl.BlockSpec((n_tok, D_ATT), lambda i, f: (seq(i, f), 0))]
        out_shape = [out_shape, jax.ShapeDtypeStruct((qkv.shape[0], D_ATT), F32)]
    return pl.pallas_call(
        functools.partial(_ffn_body, n_f=n_f, final=final, n_guest=n_guest),
        grid=(t // tm, n_f),
        in_specs=in_specs,
        out_specs=out_specs,
        out_shape=out_shape,
        scratch_shapes=[pltpu.VMEM((tm, D_MODEL), BF16)],
        compiler_params=_params(2),
        name=("ffn_final" if final else "ffn") + ("_host" if hosted else ""),
    )(*args)


def _inproj_body(x_ref, g_ref, w_ref, z_ref, xbc_ref, qkv_ref, dt_ref):
    h = (_rms_scale(x_ref[...]) * g_ref[...]).astype(BF16)
    z_ref[...] = _dot_nt(h, w_ref[0:D_SSM, :])
    xbc_ref[...] = _dot_nt(h, w_ref[D_SSM:IN_HEAD, :])
    tail = _dot_nt(h, w_ref[IN_HEAD:IN_HEAD + IN_TAIL, :])
    lane = lax.broadcasted_iota(jnp.int32, (tail.shape[0], DT_PAD), 1)
    dt_ref[...] = jnp.where(lane < N_SSM_HEADS, tail[:, 0:DT_PAD], 0.0)
    qkv_ref[...] = tail[:, N_SSM_HEADS:IN_TAIL]


def _inproj(x, gain, w_t, *, tm=256):
    t = x.shape[0]
    row = lambda i: (i, 0)
    const = lambda i: (0, 0)
    widths = (D_SSM, CONV_DIM, 3 * D_ATT, DT_PAD)
    return pl.pallas_call(
        _inproj_body,
        grid=(t // tm,),
        in_specs=[pl.BlockSpec((tm, D_MODEL), row), pl.BlockSpec((1, D_MODEL), const),
                  pl.BlockSpec((IN_HEAD + IN_TAIL, D_MODEL), const, pipeline_mode=pl.Buffered(1))],
        out_specs=[pl.BlockSpec((tm, w), row) for w in widths],
        out_shape=[jax.ShapeDtypeStruct((t, w), F32) for w in widths],
        compiler_params=_params(1),
        name="inproj",
    )(x, gain, w_t)


def _pair_blockdiag(x, low_half):
    zero = jnp.zeros_like(x)
    return jnp.concatenate([jnp.where(low_half, x, zero), jnp.where(low_half, zero, x)],
                           axis=0).astype(BF16)


def _ssd_chunk(xbc_ref, z_ref, dt_ref, cw_ref, cb_ref, dtb_ref, alog_ref, dsk_ref, nrm_ref,
               y_ref, xf_scr, ht_scr):
    cl = SSD_CHUNK
    x = xbc_ref[...]
    xf_scr[SUBLANES:SUBLANES + cl, :] = x
    conv = cb_ref[...]
    for j in range(CONV_W - 1):
        conv = conv + xf_scr[SUBLANES - 3 + j:SUBLANES - 3 + j + cl, :] * cw_ref[j:j + 1, :]
    conv = conv + x * cw_ref[CONV_W - 1:CONV_W, :]
    xf_scr[0:SUBLANES, :] = x[cl - SUBLANES:cl, :]
    xa = _silu(conv)
    xs = xa[:, :D_SSM]
    bm = xa[:, D_SSM:D_SSM + N_SSM_GROUPS * D_STATE]
    cm = xa[:, D_SSM + N_SSM_GROUPS * D_STATE:]

    dt = _softplus(dt_ref[...] + dtb_ref[...])
    a = -jnp.exp(alog_ref[...])
    acum = _cumsum_rows(dt * a)
    w_end = dt * jnp.exp(acum[cl - 1:cl, :] - acum)
    acum_t = acum.T
    dt_t = dt.T
    w_end_t = w_end.T

    rows = lax.broadcasted_iota(jnp.int32, (cl, cl), 0)
    cols = lax.broadcasted_iota(jnp.int32, (cl, cl), 1)
    causal = rows >= cols
    low_half = cols < SSM_HEAD_DIM

    b_g = [bm[:, g * D_STATE:(g + 1) * D_STATE] for g in range(N_SSM_GROUPS)]
    c_g = [cm[:, g * D_STATE:(g + 1) * D_STATE].astype(BF16) for g in range(N_SSM_GROUPS)]
    b_t = [b.T for b in b_g]
    cb = [_dot_nt(c_g[g], b_g[g].astype(BF16)) for g in range(N_SSM_GROUPS)]

    sumsq = [jnp.zeros((cl, 1), F32) for _ in range(N_SSM_GROUPS)]
    for j in range(N_SSM_HEADS // 2):
        lo, hi = j * LANES, (j + 1) * LANES
        heads = (2 * j, 2 * j + 1)
        groups = tuple(h // HEADS_PER_SSM_GROUP for h in heads)
        xs_p = xs[:, lo:hi]
        x_bd = _pair_blockdiag(xs_p, low_half)
        ht_p = ht_scr[:, lo:hi]
        h_bd = _pair_blockdiag(ht_p, low_half)
        g_mats, w_mats, e_cols = [], [], []
        for h, g in zip(heads, groups):
            row_b = jnp.broadcast_to(acum_t[h:h + 1, :], (cl, cl))
            col_b = row_b.T
            decay = jnp.where(causal, jnp.exp(col_b - row_b), 0.0)
            g_mats.append((cb[g] * decay * dt_t[h:h + 1, :]).astype(BF16))
            w_mats.append((b_t[g] * w_end_t[h:h + 1, :]).astype(BF16))
            e_cols.append(jnp.exp(col_b))
        y_diag = _dot(jnp.concatenate(g_mats, axis=1), x_bd)
        y_off = _dot(jnp.concatenate([c_g[groups[0]], c_g[groups[1]]], axis=1), h_bd)
        y_off = y_off * jnp.where(low_half, e_cols[0], e_cols[1])
        state = _dot(jnp.concatenate(w_mats, axis=1), x_bd)
        chunk_decay = jnp.where(low_half[0:1, :], e_cols[0][cl - 1:cl, :], e_cols[1][cl - 1:cl, :])
        ht_scr[:, lo:hi] = ht_p * chunk_decay + state
        y_p = y_diag + y_off + dsk_ref[:, lo:hi] * xs_p
        y_p = y_p * _silu(z_ref[:, lo:hi])
        y_ref[:, lo:hi] = y_p
        sq = y_p * y_p
        sumsq[groups[0]] = sumsq[groups[0]] + jnp.sum(jnp.where(low_half, sq, 0.0), axis=1, keepdims=True)
        sumsq[groups[1]] = sumsq[groups[1]] + jnp.sum(jnp.where(low_half, 0.0, sq), axis=1, keepdims=True)

    scale = [lax.rsqrt(s / D_SSM_GROUP + RMS_EPS) for s in sumsq]
    for j in range(N_SSM_HEADS // 2):
        lo, hi = j * LANES, (j + 1) * LANES
        g0, g1 = (2 * j) // HEADS_PER_SSM_GROUP, (2 * j + 1) // HEADS_PER_SSM_GROUP
        y_ref[:, lo:hi] = y_ref[:, lo:hi] * jnp.where(low_half, scale[g0], scale[g1]) * nrm_ref[:, lo:hi]


def _inproj_ssd_body(x_ref, g_ref, w_ref, cw_ref, cb_ref, dtb_ref, alog_ref, dsk_ref, nrm_ref,
                     xbc_ref, qkv_ref, y_ref, hfin_ref,
                     xbc_a, z_a, dt_a, xbc_b, z_b, dt_b, xf_scr, ht_scr, *, n_blocks):
    s = pl.program_id(0)
    tm = x_ref.shape[0]
    bufs = ((xbc_a, z_a, dt_a), (xbc_b, z_b, dt_b))

    @pl.when(s <= 1)
    def _():
        xf_scr[0:SUBLANES, :] = jnp.zeros((SUBLANES, CONV_DIM), F32)
        ht_scr[...] = jnp.zeros_like(ht_scr)

    @pl.when(s == 0)
    def _():
        for ref in bufs[1]:
            ref[...] = jnp.zeros_like(ref)

    def step(parity):
        xbc_w, z_w, dt_w = bufs[parity]
        xbc_r, z_r, dt_r = bufs[1 - parity]
        h = (_rms_scale(x_ref[...]) * g_ref[...]).astype(BF16)
        z_w[...] = _dot_nt(h, w_ref[0:D_SSM, :])
        xbc = _dot_nt(h, w_ref[D_SSM:IN_HEAD, :])
        xbc_w[...] = xbc
        xbc_ref[...] = xbc
        tail = _dot_nt(h, w_ref[IN_HEAD:IN_HEAD + IN_TAIL, :])
        lane = lax.broadcasted_iota(jnp.int32, (tm, DT_PAD), 1)
        dt_w[...] = jnp.where(lane < N_SSM_HEADS, tail[:, 0:DT_PAD], 0.0)
        qkv = tail[:, N_SSM_HEADS:IN_TAIL]
        for j in range(N_QKV_BLOCKS):
            qkv_ref[j] = qkv[:, j * LANES:(j + 1) * LANES]
        for c in range(tm // SSD_CHUNK):
            rows = slice(c * SSD_CHUNK, (c + 1) * SSD_CHUNK)
            _ssd_chunk(xbc_r.at[rows, :], z_r.at[rows, :], dt_r.at[rows, :], cw_ref, cb_ref, dtb_ref,
                       alog_ref, dsk_ref, nrm_ref, y_ref.at[rows, :], xf_scr, ht_scr)

    @pl.when(s % 2 == 0)
    def _():
        step(0)

    @pl.when(s % 2 == 1)
    def _():
        step(1)

    @pl.when(s == n_blocks)
    def _():
        for j in range(D_SSM // LANES):
            hfin_ref[j * LANES:(j + 1) * LANES, :] = ht_scr[:, j * LANES:(j + 1) * LANES].T


def _inproj_ssd(x, gain, w_t, cw, cb, dtb, alog, dsk, nrm, *, tm=256):
    t = x.shape[0]
    n_blocks = t // tm
    cur = lambda s: (jnp.minimum(s, n_blocks - 1), 0)
    prev = lambda s: (jnp.maximum(s - 1, 0), 0)
    const = lambda s: (0, 0)
    vec = lambda n: pl.BlockSpec((1, n), const)
    buf = lambda n: pltpu.VMEM((tm, n), F32)
    return pl.pallas_call(
        functools.partial(_inproj_ssd_body, n_blocks=n_blocks),
        grid=(n_blocks + 1,),
        in_specs=[pl.BlockSpec((tm, D_MODEL), cur), vec(D_MODEL),
                  pl.BlockSpec((IN_HEAD + IN_TAIL, D_MODEL), const, pipeline_mode=pl.Buffered(1)),
                  pl.BlockSpec((CONV_W, CONV_DIM), const),
                  vec(CONV_DIM), vec(DT_PAD), vec(DT_PAD), vec(D_SSM), vec(D_SSM)],
        out_specs=[pl.BlockSpec((tm, CONV_DIM), cur),
                   pl.BlockSpec((N_QKV_BLOCKS, tm, LANES), lambda s: (0, jnp.minimum(s, n_blocks - 1), 0)),
                   pl.BlockSpec((tm, D_SSM), prev), pl.BlockSpec((D_SSM, D_STATE), const)],
        out_shape=[jax.ShapeDtypeStruct((t, CONV_DIM), F32),
                   jax.ShapeDtypeStruct((N_QKV_BLOCKS, t, LANES), F32),
                   jax.ShapeDtypeStruct((t, D_SSM), F32), jax.ShapeDtypeStruct((D_SSM, D_STATE), F32)],
        scratch_shapes=[buf(CONV_DIM), buf(D_SSM), buf(DT_PAD), buf(CONV_DIM), buf(D_SSM), buf(DT_PAD),
                        pltpu.VMEM((SUBLANES + SSD_CHUNK, CONV_DIM), F32),
                        pltpu.VMEM((D_STATE, D_SSM), F32)],
        compiler_params=_params(1),
        name="inproj_ssd",
    )(x, gain, w_t, cw, cb, dtb, alog, dsk, nrm)


def _ssd_sample_body(xbc_ref, cs_ref, z_ref, dt_ref, h0_ref, cw_ref, cb_ref, dtb_ref, alog_ref,
                     dsk_ref, nrm_ref, y_ref, hout_ref, hist_scr, xwt_scr, ecd_scr, eexp_scr, yoff_scr,
                     *, n_tok, n_seq):
    n_rows = n_seq * n_tok
    assert n_rows == LANES

    @pl.when(pl.program_id(0) == 0)
    def _():
        hist_scr[...] = jnp.zeros_like(hist_scr)

    x = xbc_ref[...]
    hist_scr[:, 0:CONV_W - 1, :] = cs_ref[...]
    hist = hist_scr[...].reshape(n_rows, CONV_DIM)
    tok = lax.broadcasted_iota(jnp.int32, (n_rows, 1), 0) % n_tok
    conv = cb_ref[...] + x * cw_ref[CONV_W - 1:CONV_W, :]
    for k in range(1, CONV_W):
        back = CONV_W - 1 - k
        from_hist = hist if back == 0 else pltpu.roll(hist, n_rows - back, 0)
        shifted = jnp.where(tok < k, from_hist, pltpu.roll(x, k, 0))
        conv = conv + shifted * cw_ref[back:back + 1, :]
    xa = _silu(conv)
    xs = xa[:, :D_SSM]
    bm = xa[:, D_SSM:D_SSM + N_SSM_GROUPS * D_STATE]
    cm = xa[:, D_SSM + N_SSM_GROUPS * D_STATE:]

    dt = _softplus(dt_ref[...] + dtb_ref[...])
    dta = dt * -jnp.exp(alog_ref[...])
    tok_l = lax.broadcasted_iota(jnp.int32, (n_rows, LANES), 0) % n_tok
    acum = dta
    to_end = jnp.where(tok_l + 1 < n_tok, pltpu.roll(dta, n_rows - 1, 0), 0.0)
    shift = 1
    while shift < n_tok:
        acum = acum + jnp.where(tok_l >= shift, pltpu.roll(acum, shift, 0), 0.0)
        to_end = to_end + jnp.where(tok_l + shift < n_tok, pltpu.roll(to_end, n_rows - shift, 0), 0.0)
        shift *= 2
    w_end = dt * jnp.exp(to_end)
    acum_t = acum.T
    dt_t = dt.T
    w_end_t = w_end.T

    rows = lax.broadcasted_iota(jnp.int32, (n_rows, n_rows), 0)
    cols = lax.broadcasted_iota(jnp.int32, (n_rows, n_rows), 1)
    causal = (rows // n_tok == cols // n_tok) & (rows >= cols)
    low_half = cols < SSM_HEAD_DIM
    b_g = [bm[:, g * D_STATE:(g + 1) * D_STATE].astype(BF16) for g in range(N_SSM_GROUPS)]
    c_g = [cm[:, g * D_STATE:(g + 1) * D_STATE].astype(BF16) for g in range(N_SSM_GROUPS)]
    cb = [_dot_nt(c_g[g], b_g[g]) for g in range(N_SSM_GROUPS)]

    for j in range(N_SSM_HEADS // 2):
        lo, hi = j * LANES, (j + 1) * LANES
        heads = (2 * j, 2 * j + 1)
        xs_p = xs[:, lo:hi]
        g_mats, e_cols, w_cols = [], [], []
        for h in heads:
            g = h // HEADS_PER_SSM_GROUP
            row_b = jnp.broadcast_to(acum_t[h:h + 1, :], (n_rows, n_rows))
            col_b = row_b.T
            decay = jnp.where(causal, jnp.exp(col_b - row_b), 0.0)
            g_mats.append((cb[g] * decay * dt_t[h:h + 1, :]).astype(BF16))
            e_col = jnp.exp(col_b)
            ecd_scr[h] = e_col
            e_cols.append(e_col)
            w_cols.append(jnp.broadcast_to(w_end_t[h:h + 1, :], (n_rows, n_rows)).T)
        y_diag = _dot(jnp.concatenate(g_mats, axis=1), _pair_blockdiag(xs_p, low_half))
        y_ref[:, lo:hi] = y_diag + dsk_ref[:, lo:hi] * xs_p
        eexp_scr[:, lo:hi] = jnp.where(low_half, e_cols[0], e_cols[1])
        xwt_scr[lo:hi, :] = (xs_p * jnp.where(low_half, w_cols[0], w_cols[1])).T

    col_group = lax.broadcasted_iota(jnp.int32, (n_tok, D_SSM), 1) // D_SSM_GROUP
    lane_seq = lax.broadcasted_iota(jnp.int32, (D_SSM_GROUP, n_rows), 1) // n_tok
    for b in range(n_seq):
        r0 = b * n_tok
        h0 = h0_ref[b]
        c_stack = jnp.concatenate([c_g[g][r0:r0 + n_tok, :] for g in range(N_SSM_GROUPS)], axis=0)
        y_all = _dot_nt(c_stack, h0.astype(BF16))
        y_off = jnp.zeros((n_tok, D_SSM), F32)
        for g in range(N_SSM_GROUPS):
            y_off = jnp.where(col_group == g, y_all[g * n_tok:(g + 1) * n_tok, :], y_off)
        yoff_scr[r0:r0 + n_tok, :] = y_off
        for g in range(N_SSM_GROUPS):
            c0 = g * D_SSM_GROUP
            lhs = jnp.where(lane_seq == b, xwt_scr[c0:c0 + D_SSM_GROUP, :], 0.0).astype(BF16)
            upd = _dot(lhs, b_g[g])
            for e in range(HEADS_PER_SSM_GROUP):
                h = g * HEADS_PER_SSM_GROUP + e
                p0 = h * SSM_HEAD_DIM
                decay = ecd_scr[h, r0 + n_tok - 1:r0 + n_tok, :]
                hout_ref[b, p0:p0 + SSM_HEAD_DIM, :] = (
                    h0[p0:p0 + SSM_HEAD_DIM, :] * decay + upd[e * SSM_HEAD_DIM:(e + 1) * SSM_HEAD_DIM, :])

    y = y_ref[...] + yoff_scr[...] * eexp_scr[...]
    y = y * _silu(z_ref[...])
    sq = y * y
    col_group = lax.broadcasted_iota(jnp.int32, (n_rows, D_SSM), 1) // D_SSM_GROUP
    scale = jnp.zeros((n_rows, D_SSM), F32)
    for g in range(N_SSM_GROUPS):
        ms = jnp.sum(jnp.where(col_group == g, sq, 0.0), axis=1, keepdims=True) / D_SSM_GROUP
        scale = jnp.where(col_group == g, lax.rsqrt(ms + RMS_EPS), scale)
    y_ref[...] = y * scale * nrm_ref[...]


def _ssd_sample(xbc, conv_state, z, dt, h0, cw, cb, dtb, alog, dsk, nrm, *, n_tok):
    t = xbc.shape[0]
    n_b = t // n_tok
    n_seq = LANES // n_tok
    rows = n_seq * n_tok
    row = lambda b: (b, 0)
    const = lambda b: (0, 0)
    vec = lambda n: pl.BlockSpec((1, n), const)
    return pl.pallas_call(
        functools.partial(_ssd_sample_body, n_tok=n_tok, n_seq=n_seq),
        grid=(n_b // n_seq,),
        in_specs=[pl.BlockSpec((rows, CONV_DIM), row),
                  pl.BlockSpec((n_seq, CONV_W - 1, CONV_DIM), lambda b: (b, 0, 0)),
                  pl.BlockSpec((rows, D_SSM), row), pl.BlockSpec((rows, DT_PAD), row),
                  pl.BlockSpec((n_seq, D_SSM, D_STATE), lambda b: (b, 0, 0)),
                  pl.BlockSpec((CONV_W, CONV_DIM), const),
                  vec(CONV_DIM), vec(DT_PAD), vec(DT_PAD), vec(D_SSM), vec(D_SSM)],
        out_specs=[pl.BlockSpec((rows, D_SSM), row),
                   pl.BlockSpec((n_seq, D_SSM, D_STATE), lambda b: (b, 0, 0))],
        out_shape=[jax.ShapeDtypeStruct((t, D_SSM), F32),
                   jax.ShapeDtypeStruct((n_b, D_SSM, D_STATE), F32)],
        scratch_shapes=[pltpu.VMEM((n_seq, n_tok, CONV_DIM), F32),
                        pltpu.VMEM((D_SSM, rows), F32),
                        pltpu.VMEM((N_SSM_HEADS, rows, LANES), F32),
                        pltpu.VMEM((rows, D_SSM), F32),
                        pltpu.VMEM((rows, D_SSM), F32)],
        compiler_params=_params(1),
        name="ssd_sample",
    )(xbc, conv_state, z, dt, h0, cw, cb, dtb, alog, dsk, nrm)


HEADS_PER_HALF = LANES // ATT_HEAD_DIM
N_HALVES = D_GROUP // LANES


ATT_UNITS = 4


def _attn_prompt_body(slope_ref, *refs, dil, group):
    n_in = 5 * N_HALVES
    in_refs, o_refs, l_refs = refs[:n_in], refs[n_in:n_in + N_HALVES], refs[n_in + N_HALVES:]
    i = pl.program_id(0)
    nb = ATT_BLOCK
    qi = lax.broadcasted_iota(jnp.int32, (nb, 2 * nb), 0)
    kj = lax.broadcasted_iota(jnp.int32, (nb, 2 * nb), 1)
    dist = nb + qi - kj
    band = (dist >= 0) & (dist <= nb)
    band_first = band & (kj >= jnp.where(i > 0, 0, nb))
    dist_f = (dist * dil).astype(F32)
    alibi = [-slope_ref[group * H_PER_DIL + h] * dist_f for h in range(H_PER_DIL)]
    bias_first = [jnp.where(band_first, a, -jnp.inf) for a in alibi]
    bias_inner = [jnp.where(band, a, -jnp.inf) for a in alibi] if dil == 1 else bias_first
    lane_head = lax.broadcasted_iota(jnp.int32, (nb, LANES), 1) // ATT_HEAD_DIM
    for u in range(ATT_UNITS):
        if dil == 1:
            rows = pl.ds(u * nb, nb)
            rows_prev = pl.ds((ATT_UNITS - 1) * nb, nb) if u == 0 else pl.ds((u - 1) * nb, nb)
            bias = bias_first if u == 0 else bias_inner
        else:
            rows = rows_prev = pl.ds(pl.program_id(1) * ATT_UNITS + u, nb, stride=dil)
            bias = bias_first
        for half in range(N_HALVES):
            q_ref, kp_ref, kc_ref, vp_ref, vc_ref = in_refs[5 * half:5 * half + 5]
            if dil == 1 and u > 0:
                kp_ref, vp_ref = kc_ref, vc_ref
            q = q_ref[rows, :] * (ATT_HEAD_DIM ** -0.5)
            k = jnp.concatenate([kp_ref[rows_prev, :], kc_ref[rows, :]], axis=0).astype(BF16)
            v = jnp.concatenate([vp_ref[rows_prev, :], vc_ref[rows, :]], axis=0).astype(BF16)
            out = jnp.zeros((nb, LANES), F32)
            lse = jnp.zeros((nb, LANES), F32)
            for hh in range(HEADS_PER_HALF):
                mine = lane_head == hh
                qh = jnp.where(mine, q, 0.0).astype(BF16)
                s = _dot_nt(qh, k) + bias[half * HEADS_PER_HALF + hh]
                m = jnp.max(s, axis=1, keepdims=True)
                p = jnp.exp(s - m)
                l = jnp.sum(p, axis=1, keepdims=True)
                pv = _dot(p.astype(BF16), v)
                out = jnp.where(mine, pv / l, out)
                lse = jnp.where(mine, m + jnp.log(l), lse)
            o_refs[half][rows, :] = out
            l_refs[half][rows, :] = lse


def _attn_prompt(slopes, qkv, *, group, dil, seq):
    assert dil == 1 or dil % ATT_UNITS == 0
    blk_rows = max(dil, ATT_UNITS) * ATT_BLOCK
    n_blk = seq // blk_rows
    n_res_steps = max(dil // ATT_UNITS, 1)
    blk = (blk_rows, LANES)
    cur = lambda c: pl.BlockSpec((None,) + blk, lambda i, r: (c, i, 0))
    prev = lambda c: pl.BlockSpec((None,) + blk, lambda i, r: (c, jnp.maximum(i - 1, 0), 0))
    in_specs = [pl.BlockSpec(memory_space=pltpu.SMEM)]
    for half in range(N_HALVES):
        q_col, k_col, v_col = (N_HALVES * (part * N_DIL + group) + half for part in range(3))
        in_specs += [cur(q_col), prev(k_col), cur(k_col), prev(v_col), cur(v_col)]
    out_blk = pl.BlockSpec(blk, lambda i, r: (i, 0))
    res = pl.pallas_call(
        functools.partial(_attn_prompt_body, dil=dil, group=group),
        grid=(n_blk, n_res_steps),
        in_specs=in_specs,
        out_specs=[out_blk] * (2 * N_HALVES),
        out_shape=[jax.ShapeDtypeStruct((seq, LANES), F32)] * (2 * N_HALVES),
        compiler_params=_params(2),
        name=f"attn_prompt_g{group}",
    )(slopes, *([qkv] * (5 * N_HALVES)))
    return res[:N_HALVES], res[N_HALVES:]


def _window_buffers(caches):
    out = []
    for (win, dil), pair in zip(DILATION_PAIRS, caches):
        assert dil & (dil - 1) == 0, "dilations must be powers of two"
        for cache in pair:
            assert cache.shape[1] == win, "window buffers must hold a full window"
            out.append(jnp.transpose(cache, (0, 2, 3, 1)).reshape(cache.shape[0], D_GROUP, win))
    return out


def _attn_sample_scores(slope_ref, qkv, k_bufs, n_tok):
    n_rows = H_PER_DIL * n_tok
    row2 = lax.broadcasted_iota(jnp.int32, (n_rows, D_GROUP), 0)
    lane2 = lax.broadcasted_iota(jnp.int32, (n_rows, D_GROUP), 1)
    own_head = (lane2 // ATT_HEAD_DIM) == (row2 // n_tok)
    row1 = lax.broadcasted_iota(jnp.int32, (n_rows, 1), 0)
    tok1 = row1 % n_tok
    head1 = row1 // n_tok
    scale = ATT_HEAD_DIM ** -0.5
    state = []
    for g, (win, dil) in enumerate(DILATION_PAIRS):
        q = qkv[:, g * D_GROUP:(g + 1) * D_GROUP]
        k_new = qkv[:, D_ATT + g * D_GROUP:D_ATT + (g + 1) * D_GROUP]
        v_new = qkv[:, 2 * D_ATT + g * D_GROUP:2 * D_ATT + (g + 1) * D_GROUP]
        q_rows = jnp.where(own_head, jnp.concatenate([q] * H_PER_DIL, axis=0), 0.0)
        slope_1 = jnp.zeros((n_rows, 1), F32)
        for h in range(H_PER_DIL):
            slope_1 = jnp.where(head1 == h, slope_ref[g * H_PER_DIL + h], slope_1)

        s_buf = _dot(q_rows.astype(BF16), k_bufs[g].astype(BF16)) * scale
        pos = lax.broadcasted_iota(jnp.int32, (n_rows, win), 1)
        tok_s = lax.broadcasted_iota(jnp.int32, (n_rows, win), 0) % n_tok
        dist = win + tok_s - pos
        on_grid = (pos >= tok_s) & (jnp.bitwise_and(dist, dil - 1) == 0)
        s_buf = jnp.where(on_grid, s_buf - slope_1 * dist.astype(F32), -jnp.inf)
        m = jnp.max(s_buf, axis=1, keepdims=True)
        s_new = []
        for tp in range(n_tok):
            sn = jnp.sum(q_rows * k_new[tp:tp + 1, :], axis=1, keepdims=True) * scale
            ok = (tok1 >= tp) & (jnp.bitwise_and(tok1 - tp, dil - 1) == 0)
            sn = jnp.where(ok, sn - slope_1 * (tok1 - tp).astype(F32), -jnp.inf)
            s_new.append(sn)
            m = jnp.maximum(m, sn)
        state.append((s_buf, s_new, m, v_new))
    return state


def _attn_sample_outputs(state, v_bufs, att_ref, n_tok):
    lane_head8 = lax.broadcasted_iota(jnp.int32, (n_tok, D_GROUP), 1) // ATT_HEAD_DIM
    outs, lses = [], []
    for g, (s_buf, s_new, m, v_new) in enumerate(state):
        p_buf = jnp.exp(s_buf - m)
        l = jnp.sum(p_buf, axis=1, keepdims=True)
        o_rows = _dot_nt(p_buf.astype(BF16), v_bufs[g].astype(BF16))
        for tp in range(n_tok):
            p_new = jnp.exp(s_new[tp] - m)
            l = l + p_new
            o_rows = o_rows + p_new * v_new[tp:tp + 1, :]
        o_rows = o_rows / l
        lse_rows = m + jnp.log(l)
        o = jnp.zeros((n_tok, D_GROUP), F32)
        lse = jnp.zeros((n_tok, D_GROUP), F32)
        for h in range(H_PER_DIL):
            o = jnp.where(lane_head8 == h, o_rows[h * n_tok:(h + 1) * n_tok, :], o)
            lse = jnp.where(lane_head8 == h, lse_rows[h * n_tok:(h + 1) * n_tok, :], lse)
        outs.append(o)
        lses.append(lse)

    m = jnp.maximum(jnp.maximum(lses[0], lses[1]), lses[2])
    e = [jnp.exp(x - m) for x in lses]
    den = e[0] + e[1] + e[2]
    for g in range(N_DIL):
        att_ref[:, g * D_GROUP:(g + 1) * D_GROUP] = outs[g] * (e[g] / den)


def _outproj_body(*refs, combine):
    n_pieces = N_DIL * N_HALVES
    if combine:
        x_ref, ssm_ref = refs[:2]
        o_refs, l_refs = refs[2:2 + n_pieces], refs[2 + n_pieces:2 + 2 * n_pieces]
        w_ref, out_ref = refs[2 + 2 * n_pieces:]
        att = [None] * n_pieces
        for half in range(N_HALVES):
            idx = [g * N_HALVES + half for g in range(N_DIL)]
            lses = [l_refs[p][...] for p in idx]
            m = jnp.maximum(jnp.maximum(lses[0], lses[1]), lses[2])
            e = [jnp.exp(x - m) for x in lses]
            den = e[0] + e[1] + e[2]
            for p, w in zip(idx, e):
                att[p] = o_refs[p][...] * (w / den)
    else:
        x_ref, ssm_ref, att_ref, w_ref, out_ref = refs
        att = [att_ref[:, p * LANES:(p + 1) * LANES] for p in range(n_pieces)]
    mixed = jnp.concatenate([ssm_ref[...]] + att, axis=1).astype(BF16)
    out_ref[...] = x_ref[...] + _dot(mixed, w_ref[...].astype(BF16))


def _outproj(x, ssm, att, w_out, *, tm=512):
    t = x.shape[0]
    row = lambda i: (i, 0)
    combine = isinstance(att, (tuple, list))
    att_args = list(att[0]) + list(att[1]) if combine else [att]
    att_specs = [pl.BlockSpec((tm, a.shape[1]), row) for a in att_args]
    return pl.pallas_call(
        functools.partial(_outproj_body, combine=combine),
        grid=(t // tm,),
        in_specs=[pl.BlockSpec((tm, D_MODEL), row), pl.BlockSpec((tm, D_SSM), row)] + att_specs
                 + [pl.BlockSpec((D_MODEL, D_MODEL), lambda i: (0, 0), pipeline_mode=pl.Buffered(1))],
        out_specs=pl.BlockSpec((tm, D_MODEL), row),
        out_shape=jax.ShapeDtypeStruct((t, D_MODEL), F32),
        compiler_params=_params(1),
        name="outproj_prompt" if combine else "outproj_sample",
    )(x, ssm, *att_args, w_out)


def kernel(x_prompt, x_sample, cache_k0, cache_v0, cache_k1, cache_v1, cache_k2, cache_v2, state_ssm, state_conv, ffn1_norm, ffn1_w_gate, ffn1_w_up, ffn1_w_down, mix_norm, w_in, conv_w, conv_b, dt_bias, a_log, d_skip, ssm_norm, w_out, ffn2_norm, ffn2_w_gate, ffn2_w_up, ffn2_w_down, final_norm):
    assert w_in.shape[0] == 1, "single layer"
    batch, seq, _ = x_prompt.shape
    dec_batch, dec_seq, _ = x_sample.shape
    assert batch == 1 and dec_seq == SUBLANES

    row = lambda v: v.reshape(1, -1).astype(F32)
    pad_heads = lambda v: jnp.pad(v.reshape(1, -1).astype(F32), ((0, 0), (0, DT_PAD - N_SSM_HEADS)))
    w_in_t = jnp.transpose(w_in[0]).astype(BF16)
    ffn1 = (row(ffn1_norm[0]), ffn1_w_gate[0], ffn1_w_up[0], ffn1_w_down[0])
    ffn2 = (row(ffn2_norm[0]), ffn2_w_gate[0], ffn2_w_up[0], ffn2_w_down[0])
    w_o = w_out[0]
    ssd_params = (conv_w[0].astype(F32), row(conv_b[0]), pad_heads(dt_bias[0]), pad_heads(a_log[0]),
                  row(jnp.repeat(d_skip[0], SSM_HEAD_DIM)), row(ssm_norm[0]))
    slopes = jnp.exp2(-ALIBI_MAX_EXP * jnp.arange(1, N_ATT_HEADS + 1, dtype=F32) / N_ATT_HEADS)

    def back(x2):
        return _ffn(x2, *ffn2, row(final_norm))

    n_s = dec_batch * dec_seq
    x1_s = _ffn(x_sample.reshape(n_s, D_MODEL), *ffn1)
    z, xbc_s, qkv_s, dt = _inproj(x1_s, row(mix_norm[0]), w_in_t)
    ssm_s, h_s = _ssd_sample(xbc_s, state_conv[0], z, dt, state_ssm[0].reshape(dec_batch, D_SSM, D_STATE),
                             *ssd_params, n_tok=dec_seq)
    caches = ((cache_k0[0], cache_v0[0]), (cache_k1[0], cache_v1[0]), (cache_k2[0], cache_v2[0]))

    x1, att_s = _ffn(x_prompt.reshape(seq, D_MODEL), *ffn1,
                     host=(slopes, qkv_s, _window_buffers(caches), dec_seq))
    xbc_p, qkv_p, ssm_p, hfin_p = _inproj_ssd(x1, row(mix_norm[0]), w_in_t, *ssd_params)
    outs, lses = [], []
    for g, (win, dil) in enumerate(DILATION_PAIRS):
        o, lse = _attn_prompt(slopes, qkv_p, group=g, dil=dil, seq=seq)
        outs += o
        lses += lse
    y_prompt = back(_outproj(x1, ssm_p, (outs, lses), w_o)).reshape(1, seq, D_MODEL)

    y_sample = back(_outproj(x1_s, ssm_s, att_s, w_o)).reshape(dec_batch, dec_seq, D_MODEL)

    head_shape = (H_PER_DIL, ATT_HEAD_DIM)
    p_kv, s_kv = [], []
    for g, (win, dil) in enumerate(DILATION_PAIRS):
        keep = min(win, seq)
        for base in (D_ATT, 2 * D_ATT):
            c0 = base + g * D_GROUP
            blocks = [qkv_p[c0 // LANES + half, seq - keep:] for half in range(N_HALVES)]
            p_kv.append(jnp.concatenate(blocks, axis=1).reshape((1, 1, keep) + head_shape))
            s_kv.append(qkv_s[:, c0:c0 + D_GROUP].reshape((1, dec_batch, dec_seq) + head_shape))
    p_ssm = hfin_p.reshape(1, 1, N_SSM_HEADS, SSM_HEAD_DIM, D_STATE)
    p_conv = xbc_p[seq - (CONV_W - 1):].reshape(1, 1, CONV_W - 1, CONV_DIM)
    s_ssm = h_s.reshape(1, dec_batch, N_SSM_HEADS, SSM_HEAD_DIM, D_STATE)
    s_conv = xbc_s.reshape(dec_batch, dec_seq, CONV_DIM)[:, dec_seq - (CONV_W - 1):].reshape(
        1, dec_batch, CONV_W - 1, CONV_DIM)
    return (y_prompt, y_sample, *p_kv, p_ssm, p_conv, *s_kv, s_ssm, s_conv)
```

```python
import functools

import numpy as np
import jax
import jax.numpy as jnp
from jax import lax
from jax.experimental import pallas as pl
from jax.experimental.pallas import tpu as pltpu

F32 = jnp.float32
BF16 = jnp.bfloat16

D_MODEL = 2048
D_FF = 5632
D_ATT = 768
ATT_HEAD_DIM = 64
H_PER_DIL = 4
D_GROUP = H_PER_DIL * ATT_HEAD_DIM
DILATION_PAIRS = ((128, 1), (512, 4), (2048, 16))
N_DIL = len(DILATION_PAIRS)
ATT_BLOCK = 128
D_SSM = 1280
SSM_HEAD_DIM = 64
N_SSM_HEADS = 20
N_SSM_GROUPS = 4
HEADS_PER_SSM_GROUP = 5
D_SSM_GROUP = D_SSM // N_SSM_GROUPS
D_STATE = 128
CONV_W = 4
CONV_DIM = 2304
SSD_CHUNK = 128
RMS_EPS = 1e-6
ALIBI_MAX_EXP = 8.0
N_ATT_HEADS = 12

LANES = 128
SUBLANES = 8
DT_PAD = LANES
IN_HEAD = D_SSM + CONV_DIM
IN_TAIL = N_SSM_HEADS + 3 * D_ATT
N_QKV_BLOCKS = 3 * D_ATT // LANES

VMEM_LIMIT = 60 * 1024 * 1024


def _params(n_grid_dims, vmem=VMEM_LIMIT):
    return pltpu.CompilerParams(dimension_semantics=("arbitrary",) * n_grid_dims,
                                vmem_limit_bytes=vmem)


def _silu(x):
    return x * (0.5 + 0.5 * jnp.tanh(0.5 * x))


def _softplus(x):
    return jnp.maximum(x, 0.0) + jnp.log(1.0 + jnp.exp(-jnp.abs(x)))


def _rms_scale(x):
    return x * lax.rsqrt(jnp.mean(x * x, axis=-1, keepdims=True) + RMS_EPS)


def _dot(a, b):
    return jnp.dot(a, b, preferred_element_type=F32)


def _dot_nt(a, b):
    return lax.dot_general(a, b, (((1,), (1,)), ((), ())), preferred_element_type=F32)


def _cumsum_rows(x):
    n = x.shape[0]
    rows = lax.broadcasted_iota(jnp.int32, x.shape, 0)
    shift = 1
    while shift < n:
        x = x + jnp.where(rows >= shift, pltpu.roll(x, shift, 0), 0.0)
        shift *= 2
    return x


FFN_ROW_CHUNKS = 4


def _ffn_body(*refs, n_f, final, n_guest):
    hosted = n_guest > 0
    refs = list(refs)
    x_ref, g_ref, wg_ref, wu_ref, wd_ref = refs[:5]
    del refs[:5]
    fg_ref = refs.pop(0) if final else None
    if hosted:
        slope_ref, qkv_ref, k0_ref, v0_ref, k1_ref, v1_ref, k2_ref, v2_ref = refs[:8]
        del refs[:8]
        o_ref, att_ref, h_ref = refs
    else:
        o_ref, h_ref = refs
    f = pl.program_id(1)
    tm = x_ref.shape[0]
    assert n_f >= 2 and tm % FFN_ROW_CHUNKS == 0

    def guest_scores():
        if hosted:
            k_bufs = [k0_ref[0], k1_ref[0], k2_ref[0]]
            return _attn_sample_scores(slope_ref, qkv_ref[...], k_bufs, qkv_ref.shape[0])

    def guest_outputs(state):
        if hosted:
            v_bufs = [v0_ref[0], v1_ref[0], v2_ref[0]]
            _attn_sample_outputs(state, v_bufs, att_ref, qkv_ref.shape[0])

    def weights():
        return wg_ref[...].astype(BF16), wu_ref[...].astype(BF16), wd_ref[...].astype(BF16)

    def partial_out(h, w, with_guest=False):
        gate = _dot(h, w[0])
        state = guest_scores() if with_guest else None
        up = _dot(h, w[1])
        if with_guest:
            guest_outputs(state)
        act = (_silu(gate) * up).astype(BF16)
        return _dot(act, w[2])

    chunks = [slice(c * tm // FFN_ROW_CHUNKS, (c + 1) * tm // FFN_ROW_CHUNKS) for c in range(FFN_ROW_CHUNKS)]

    @pl.when(f == 0)
    def _():
        w = weights()
        for c, rows in enumerate(chunks):
            h = (_rms_scale(x_ref[rows, :]) * g_ref[...]).astype(BF16)
            h_ref[rows, :] = h
            o_ref[rows, :] = partial_out(h, w, with_guest=(c == 1))

    middle = (f > 0) & (f < n_f - 1)
    guest_left = pl.program_id(0) * n_f + f < n_guest

    if hosted:
        @pl.when(middle & guest_left)
        def _():
            o_ref[...] += partial_out(h_ref[...], weights(), with_guest=True)

    @pl.when(middle & jnp.logical_not(guest_left))
    def _():
        o_ref[...] += partial_out(h_ref[...], weights())

    @pl.when(f == n_f - 1)
    def _():
        w = weights()
        for c, rows in enumerate(chunks):
            y = x_ref[rows, :] + 0.5 * (o_ref[rows, :] + partial_out(h_ref[rows, :], w, with_guest=(c == 1)))
            if final:
                y = _rms_scale(y) * fg_ref[...]
            o_ref[rows, :] = y


def _ffn(x, gain, wg, wu, wd, final_gain=None, *, host=None, tm=1024, tf=256):
    t = x.shape[0]
    n_f = D_FF // tf
    final = final_gain is not None
    hosted = host is not None
    row = lambda i, f: (i, 0)
    const = lambda i, f: (0, 0)
    tile_mode = dict(pipeline_mode=pl.Buffered(1)) if hosted else {}
    n_guest = 0
    in_specs = [pl.BlockSpec((tm, D_MODEL), row), pl.BlockSpec((1, D_MODEL), const),
                pl.BlockSpec((D_MODEL, tf), lambda i, f: (0, f)),
                pl.BlockSpec((D_MODEL, tf), lambda i, f: (0, f)),
                pl.BlockSpec((tf, D_MODEL), lambda i, f: (f, 0))]
    args = [x, gain, wg, wu, wd]
    if final:
        in_specs.append(pl.BlockSpec((1, D_MODEL), const))
        args.append(final_gain)
    out_specs = pl.BlockSpec((tm, D_MODEL), row, **tile_mode)
    out_shape = jax.ShapeDtypeStruct((t, D_MODEL), F32)
    if hosted:
        slopes, qkv, caches, n_tok = host
        n_guest = n_b = qkv.shape[0] // n_tok
        assert n_b <= (t // tm) * n_f, "one sequence per grid step"
        seq = lambda i, f: jnp.minimum(i * n_f + f, n_b - 1)
        in_specs += [pl.BlockSpec(memory_space=pltpu.SMEM),
                     pl.BlockSpec((n_tok, 3 * D_ATT), lambda i, f: (seq(i, f), 0))]
        in_specs += [pl.BlockSpec((1,) + c.shape[1:], lambda i, f: (seq(i, f), 0, 0)) for c in caches]
        args += [slopes, qkv, *caches]
        out_specs = [out_specs, pl.BlockSpec((n_tok, D_ATT), lambda i, f: (seq(i, f), 0))]
        out_shape = [out_shape, jax.ShapeDtypeStruct((qkv.shape[0], D_ATT), F32)]
    return pl.pallas_call(
        functools.partial(_ffn_body, n_f=n_f, final=final, n_guest=n_guest),
        grid=(t // tm, n_f),
        in_specs=in_specs,
        out_specs=out_specs,
        out_shape=out_shape,
        scratch_shapes=[pltpu.VMEM((tm, D_MODEL), BF16)],
        compiler_params=_params(2),
        name=("ffn_final" if final else "ffn") + ("_host" if hosted else ""),
    )(*args)


def _inproj_body(x_ref, g_ref, w_ref, z_ref, xbc_ref, qkv_ref, dt_ref):
    h = (_rms_scale(x_ref[...]) * g_ref[...]).astype(BF16)
    z_ref[...] = _dot_nt(h, w_ref[0:D_SSM, :])
    xbc_ref[...] = _dot_nt(h, w_ref[D_SSM:IN_HEAD, :])
    tail = _dot_nt(h, w_ref[IN_HEAD:IN_HEAD + IN_TAIL, :])
    lane = lax.broadcasted_iota(jnp.int32, (tail.shape[0], DT_PAD), 1)
    dt_ref[...] = jnp.where(lane < N_SSM_HEADS, tail[:, 0:DT_PAD], 0.0)
    qkv_ref[...] = tail[:, N_SSM_HEADS:IN_TAIL]


def _inproj(x, gain, w_t, *, tm=256):
    t = x.shape[0]
    row = lambda i: (i, 0)
    const = lambda i: (0, 0)
    widths = (D_SSM, CONV_DIM, 3 * D_ATT, DT_PAD)
    return pl.pallas_call(
        _inproj_body,
        grid=(t // tm,),
        in_specs=[pl.BlockSpec((tm, D_MODEL), row), pl.BlockSpec((1, D_MODEL), const),
                  pl.BlockSpec((IN_HEAD + IN_TAIL, D_MODEL), const, pipeline_mode=pl.Buffered(1))],
        out_specs=[pl.BlockSpec((tm, w), row) for w in widths],
        out_shape=[jax.ShapeDtypeStruct((t, w), F32) for w in widths],
        compiler_params=_params(1),
        name="inproj",
    )(x, gain, w_t)


def _pair_blockdiag(x, low_half):
    zero = jnp.zeros_like(x)
    return jnp.concatenate([jnp.where(low_half, x, zero), jnp.where(low_half, zero, x)],
                           axis=0).astype(BF16)


def _ssd_chunk_phases(xbc_ref, z_ref, dt_ref, cw_ref, cb_ref, dtb_ref, alog_ref, dsk_ref, nrm_ref,
                      y_ref, xf_scr, ht_scr):
    cl = SSD_CHUNK
    st = {}

    def prologue():
        x = xbc_ref[...]
        xf_scr[SUBLANES:SUBLANES + cl, :] = x
        conv = cb_ref[...]
        for j in range(CONV_W - 1):
            conv = conv + xf_scr[SUBLANES - 3 + j:SUBLANES - 3 + j + cl, :] * cw_ref[j:j + 1, :]
        conv = conv + x * cw_ref[CONV_W - 1:CONV_W, :]
        xf_scr[0:SUBLANES, :] = x[cl - SUBLANES:cl, :]
        xa = _silu(conv)
        st["xs"] = xa[:, :D_SSM]
        bm = xa[:, D_SSM:D_SSM + N_SSM_GROUPS * D_STATE]
        cm = xa[:, D_SSM + N_SSM_GROUPS * D_STATE:]

        dt = _softplus(dt_ref[...] + dtb_ref[...])
        a = -jnp.exp(alog_ref[...])
        acum = _cumsum_rows(dt * a)
        w_end = dt * jnp.exp(acum[cl - 1:cl, :] - acum)
        st["acum_t"] = acum.T
        st["dt_t"] = dt.T
        st["w_end_t"] = w_end.T

        rows = lax.broadcasted_iota(jnp.int32, (cl, cl), 0)
        cols = lax.broadcasted_iota(jnp.int32, (cl, cl), 1)
        st["causal"] = rows >= cols
        st["low_half"] = cols < SSM_HEAD_DIM

        b_g = [bm[:, g * D_STATE:(g + 1) * D_STATE] for g in range(N_SSM_GROUPS)]
        st["c_g"] = [cm[:, g * D_STATE:(g + 1) * D_STATE].astype(BF16) for g in range(N_SSM_GROUPS)]
        st["b_t"] = [b.T for b in b_g]
        st["cb"] = [_dot_nt(st["c_g"][g], b_g[g].astype(BF16)) for g in range(N_SSM_GROUPS)]
        st["sumsq"] = [jnp.zeros((cl, 1), F32) for _ in range(N_SSM_GROUPS)]

    def pair(j):
        lo, hi = j * LANES, (j + 1) * LANES
        heads = (2 * j, 2 * j + 1)
        groups = tuple(h // HEADS_PER_SSM_GROUP for h in heads)
        low_half, c_g = st["low_half"], st["c_g"]
        xs_p = st["xs"][:, lo:hi]
        x_bd = _pair_blockdiag(xs_p, low_half)
        ht_p = ht_scr[:, lo:hi]
        h_bd = _pair_blockdiag(ht_p, low_half)
        g_mats, w_mats, e_cols = [], [], []
        for h, g in zip(heads, groups):
            row_b = jnp.broadcast_to(st["acum_t"][h:h + 1, :], (cl, cl))
            col_b = row_b.T
            decay = jnp.where(st["causal"], jnp.exp(col_b - row_b), 0.0)
            g_mats.append((st["cb"][g] * decay * st["dt_t"][h:h + 1, :]).astype(BF16))
            w_mats.append((st["b_t"][g] * st["w_end_t"][h:h + 1, :]).astype(BF16))
            e_cols.append(jnp.exp(col_b))
        y_diag = _dot(jnp.concatenate(g_mats, axis=1), x_bd)
        y_off = _dot(jnp.concatenate([c_g[groups[0]], c_g[groups[1]]], axis=1), h_bd)
        y_off = y_off * jnp.where(low_half, e_cols[0], e_cols[1])
        state = _dot(jnp.concatenate(w_mats, axis=1), x_bd)
        chunk_decay = jnp.where(low_half[0:1, :], e_cols[0][cl - 1:cl, :], e_cols[1][cl - 1:cl, :])
        ht_scr[:, lo:hi] = ht_p * chunk_decay + state
        y_p = y_diag + y_off + dsk_ref[:, lo:hi] * xs_p
        y_p = y_p * _silu(z_ref[:, lo:hi])
        y_ref[:, lo:hi] = y_p
        sq = y_p * y_p
        sumsq = st["sumsq"]
        sumsq[groups[0]] = sumsq[groups[0]] + jnp.sum(jnp.where(low_half, sq, 0.0), axis=1, keepdims=True)
        sumsq[groups[1]] = sumsq[groups[1]] + jnp.sum(jnp.where(low_half, 0.0, sq), axis=1, keepdims=True)

    def epilogue():
        scale = [lax.rsqrt(s / D_SSM_GROUP + RMS_EPS) for s in st["sumsq"]]
        for j in range(N_SSM_HEADS // 2):
            lo, hi = j * LANES, (j + 1) * LANES
            g0, g1 = (2 * j) // HEADS_PER_SSM_GROUP, (2 * j + 1) // HEADS_PER_SSM_GROUP
            y_ref[:, lo:hi] = (y_ref[:, lo:hi] * jnp.where(st["low_half"], scale[g0], scale[g1])
                               * nrm_ref[:, lo:hi])

    return [prologue] + [functools.partial(pair, j) for j in range(N_SSM_HEADS // 2)] + [epilogue]


PROJ_TILE = 256


def _inproj_ssd_body(x_ref, g_ref, w_ref, cw_ref, cb_ref, dtb_ref, alog_ref, dsk_ref, nrm_ref,
                     xbc_ref, qkv_ref, y_ref, hfin_ref,
                     xbc_a, z_a, dt_a, xbc_b, z_b, dt_b, xf_scr, ht_scr, *, n_blocks):
    s = pl.program_id(0)
    tm = x_ref.shape[0]
    bufs = ((xbc_a, z_a, dt_a), (xbc_b, z_b, dt_b))

    @pl.when(s <= 1)
    def _():
        xf_scr[0:SUBLANES, :] = jnp.zeros((SUBLANES, CONV_DIM), F32)
        ht_scr[...] = jnp.zeros_like(ht_scr)

    @pl.when(s == 0)
    def _():
        for ref in bufs[1]:
            ref[...] = jnp.zeros_like(ref)

    def step(parity):
        xbc_w, z_w, dt_w = bufs[parity]
        xbc_r, z_r, dt_r = bufs[1 - parity]
        h = (_rms_scale(x_ref[...]) * g_ref[...]).astype(BF16)

        def z_tile(c0, c1):
            z_w[:, c0:c1] = _dot_nt(h, w_ref[c0:c1, :])

        def xbc_tile(c0, c1):
            xbc = _dot_nt(h, w_ref[D_SSM + c0:D_SSM + c1, :])
            xbc_w[:, c0:c1] = xbc
            xbc_ref[:, c0:c1] = xbc

        tail_parts = []

        def tail_tile(c0, c1):
            tail_parts.append(_dot_nt(h, w_ref[IN_HEAD + c0:IN_HEAD + c1, :]))
            if c1 == IN_TAIL:
                tail = jnp.concatenate(tail_parts, axis=1)
                lane = lax.broadcasted_iota(jnp.int32, (tm, DT_PAD), 1)
                dt_w[...] = jnp.where(lane < N_SSM_HEADS, tail[:, 0:DT_PAD], 0.0)
                qkv = tail[:, N_SSM_HEADS:IN_TAIL]
                for j in range(N_QKV_BLOCKS):
                    qkv_ref[j] = qkv[:, j * LANES:(j + 1) * LANES]

        tiles = [functools.partial(z_tile, c, c + PROJ_TILE) for c in range(0, D_SSM, PROJ_TILE)]
        tiles += [functools.partial(xbc_tile, c, c + PROJ_TILE) for c in range(0, CONV_DIM, PROJ_TILE)]
        n_tail = IN_TAIL // PROJ_TILE
        tail_cuts = [c * PROJ_TILE for c in range(n_tail)] + [IN_TAIL]
        tiles += [functools.partial(tail_tile, c0, c1) for c0, c1 in zip(tail_cuts[:-1], tail_cuts[1:])]
        phases = []
        for c in range(tm // SSD_CHUNK):
            rows = slice(c * SSD_CHUNK, (c + 1) * SSD_CHUNK)
            phases += _ssd_chunk_phases(xbc_r.at[rows, :], z_r.at[rows, :], dt_r.at[rows, :], cw_ref,
                                        cb_ref, dtb_ref, alog_ref, dsk_ref, nrm_ref, y_ref.at[rows, :],
                                        xf_scr, ht_scr)
        done = 0
        for i, tile in enumerate(tiles):
            tile()
            upto = (i + 1) * len(phases) // len(tiles)
            for phase in phases[done:upto]:
                phase()
            done = upto

    @pl.when(s % 2 == 0)
    def _():
        step(0)

    @pl.when(s % 2 == 1)
    def _():
        step(1)

    @pl.when(s == n_blocks)
    def _():
        for j in range(D_SSM // LANES):
            hfin_ref[j * LANES:(j + 1) * LANES, :] = ht_scr[:, j * LANES:(j + 1) * LANES].T


def _inproj_ssd(x, gain, w_t, cw, cb, dtb, alog, dsk, nrm, *, tm=256):
    t = x.shape[0]
    n_blocks = t // tm
    cur = lambda s: (jnp.minimum(s, n_blocks - 1), 0)
    prev = lambda s: (jnp.maximum(s - 1, 0), 0)
    const = lambda s: (0, 0)
    vec = lambda n: pl.BlockSpec((1, n), const)
    buf = lambda n: pltpu.VMEM((tm, n), F32)
    return pl.pallas_call(
        functools.partial(_inproj_ssd_body, n_blocks=n_blocks),
        grid=(n_blocks + 1,),
        in_specs=[pl.BlockSpec((tm, D_MODEL), cur), vec(D_MODEL),
                  pl.BlockSpec((IN_HEAD + IN_TAIL, D_MODEL), const, pipeline_mode=pl.Buffered(1)),
                  pl.BlockSpec((CONV_W, CONV_DIM), const),
                  vec(CONV_DIM), vec(DT_PAD), vec(DT_PAD), vec(D_SSM), vec(D_SSM)],
        out_specs=[pl.BlockSpec((tm, CONV_DIM), cur),
                   pl.BlockSpec((N_QKV_BLOCKS, tm, LANES), lambda s: (0, jnp.minimum(s, n_blocks - 1), 0)),
                   pl.BlockSpec((tm, D_SSM), prev), pl.BlockSpec((D_SSM, D_STATE), const)],
        out_shape=[jax.ShapeDtypeStruct((t, CONV_DIM), F32),
                   jax.ShapeDtypeStruct((N_QKV_BLOCKS, t, LANES), F32),
                   jax.ShapeDtypeStruct((t, D_SSM), F32), jax.ShapeDtypeStruct((D_SSM, D_STATE), F32)],
        scratch_shapes=[buf(CONV_DIM), buf(D_SSM), buf(DT_PAD), buf(CONV_DIM), buf(D_SSM), buf(DT_PAD),
                        pltpu.VMEM((SUBLANES + SSD_CHUNK, CONV_DIM), F32),
                        pltpu.VMEM((D_STATE, D_SSM), F32)],
        compiler_params=_params(1),
        name="inproj_ssd",
    )(x, gain, w_t, cw, cb, dtb, alog, dsk, nrm)


def _ssd_sample_body(xbc_ref, cs_ref, z_ref, dt_ref, h0_ref, cw_ref, cb_ref, dtb_ref, alog_ref,
                     dsk_ref, nrm_ref, y_ref, hout_ref, hist_scr, xwt_scr, ecd_scr, eexp_scr, yoff_scr,
                     *, n_tok, n_seq):
    n_rows = n_seq * n_tok
    assert n_rows == LANES

    @pl.when(pl.program_id(0) == 0)
    def _():
        hist_scr[...] = jnp.zeros_like(hist_scr)

    x = xbc_ref[...]
    hist_scr[:, 0:CONV_W - 1, :] = cs_ref[...]
    hist = hist_scr[...].reshape(n_rows, CONV_DIM)
    tok = lax.broadcasted_iota(jnp.int32, (n_rows, 1), 0) % n_tok
    conv = cb_ref[...] + x * cw_ref[CONV_W - 1:CONV_W, :]
    for k in range(1, CONV_W):
        back = CONV_W - 1 - k
        from_hist = hist if back == 0 else pltpu.roll(hist, n_rows - back, 0)
        shifted = jnp.where(tok < k, from_hist, pltpu.roll(x, k, 0))
        conv = conv + shifted * cw_ref[back:back + 1, :]
    xa = _silu(conv)
    xs = xa[:, :D_SSM]
    bm = xa[:, D_SSM:D_SSM + N_SSM_GROUPS * D_STATE]
    cm = xa[:, D_SSM + N_SSM_GROUPS * D_STATE:]

    dt = _softplus(dt_ref[...] + dtb_ref[...])
    dta = dt * -jnp.exp(alog_ref[...])
    tok_l = lax.broadcasted_iota(jnp.int32, (n_rows, LANES), 0) % n_tok
    acum = dta
    to_end = jnp.where(tok_l + 1 < n_tok, pltpu.roll(dta, n_rows - 1, 0), 0.0)
    shift = 1
    while shift < n_tok:
        acum = acum + jnp.where(tok_l >= shift, pltpu.roll(acum, shift, 0), 0.0)
        to_end = to_end + jnp.where(tok_l + shift < n_tok, pltpu.roll(to_end, n_rows - shift, 0), 0.0)
        shift *= 2
    w_end = dt * jnp.exp(to_end)
    acum_t = acum.T
    dt_t = dt.T
    w_end_t = w_end.T

    rows = lax.broadcasted_iota(jnp.int32, (n_rows, n_rows), 0)
    cols = lax.broadcasted_iota(jnp.int32, (n_rows, n_rows), 1)
    causal = (rows // n_tok == cols // n_tok) & (rows >= cols)
    low_half = cols < SSM_HEAD_DIM
    b_g = [bm[:, g * D_STATE:(g + 1) * D_STATE].astype(BF16) for g in range(N_SSM_GROUPS)]
    c_g = [cm[:, g * D_STATE:(g + 1) * D_STATE].astype(BF16) for g in range(N_SSM_GROUPS)]
    cb = [_dot_nt(c_g[g], b_g[g]) for g in range(N_SSM_GROUPS)]

    for j in range(N_SSM_HEADS // 2):
        lo, hi = j * LANES, (j + 1) * LANES
        heads = (2 * j, 2 * j + 1)
        xs_p = xs[:, lo:hi]
        g_mats, e_cols, w_cols = [], [], []
        for h in heads:
            g = h // HEADS_PER_SSM_GROUP
            row_b = jnp.broadcast_to(acum_t[h:h + 1, :], (n_rows, n_rows))
            col_b = row_b.T
            decay = jnp.where(causal, jnp.exp(col_b - row_b), 0.0)
            g_mats.append((cb[g] * decay * dt_t[h:h + 1, :]).astype(BF16))
            e_col = jnp.exp(col_b)
            ecd_scr[h] = e_col
            e_cols.append(e_col)
            w_cols.append(jnp.broadcast_to(w_end_t[h:h + 1, :], (n_rows, n_rows)).T)
        y_diag = _dot(jnp.concatenate(g_mats, axis=1), _pair_blockdiag(xs_p, low_half))
        y_ref[:, lo:hi] = y_diag + dsk_ref[:, lo:hi] * xs_p
        eexp_scr[:, lo:hi] = jnp.where(low_half, e_cols[0], e_cols[1])
        xwt_scr[lo:hi, :] = (xs_p * jnp.where(low_half, w_cols[0], w_cols[1])).T

    col_group = lax.broadcasted_iota(jnp.int32, (n_tok, D_SSM), 1) // D_SSM_GROUP
    lane_seq = lax.broadcasted_iota(jnp.int32, (D_SSM_GROUP, n_rows), 1) // n_tok
    for b in range(n_seq):
        r0 = b * n_tok
        h0 = h0_ref[b]
        c_stack = jnp.concatenate([c_g[g][r0:r0 + n_tok, :] for g in range(N_SSM_GROUPS)], axis=0)
        y_all = _dot_nt(c_stack, h0.astype(BF16))
        y_off = jnp.zeros((n_tok, D_SSM), F32)
        for g in range(N_SSM_GROUPS):
            y_off = jnp.where(col_group == g, y_all[g * n_tok:(g + 1) * n_tok, :], y_off)
        yoff_scr[r0:r0 + n_tok, :] = y_off
        for g in range(N_SSM_GROUPS):
            c0 = g * D_SSM_GROUP
            lhs = jnp.where(lane_seq == b, xwt_scr[c0:c0 + D_SSM_GROUP, :], 0.0).astype(BF16)
            upd = _dot(lhs, b_g[g])
            for e in range(HEADS_PER_SSM_GROUP):
                h = g * HEADS_PER_SSM_GROUP + e
                p0 = h * SSM_HEAD_DIM
                decay = ecd_scr[h, r0 + n_tok - 1:r0 + n_tok, :]
                hout_ref[b, p0:p0 + SSM_HEAD_DIM, :] = (
                    h0[p0:p0 + SSM_HEAD_DIM, :] * decay + upd[e * SSM_HEAD_DIM:(e + 1) * SSM_HEAD_DIM, :])

    y = y_ref[...] + yoff_scr[...] * eexp_scr[...]
    y = y * _silu(z_ref[...])
    sq = y * y
    col_group = lax.broadcasted_iota(jnp.int32, (n_rows, D_SSM), 1) // D_SSM_GROUP
    scale = jnp.zeros((n_rows, D_SSM), F32)
    for g in range(N_SSM_GROUPS):
        ms = jnp.sum(jnp.where(col_group == g, sq, 0.0), axis=1, keepdims=True) / D_SSM_GROUP
        scale = jnp.where(col_group == g, lax.rsqrt(ms + RMS_EPS), scale)
    y_ref[...] = y * scale * nrm_ref[...]


def _ssd_sample(xbc, conv_state, z, dt, h0, cw, cb, dtb, alog, dsk, nrm, *, n_tok):
    t = xbc.shape[0]
    n_b = t // n_tok
    n_seq = LANES // n_tok
    rows = n_seq * n_tok
    row = lambda b: (b, 0)
    const = lambda b: (0, 0)
    vec = lambda n: pl.BlockSpec((1, n), const)
    return pl.pallas_call(
        functools.partial(_ssd_sample_body, n_tok=n_tok, n_seq=n_seq),
        grid=(n_b // n_seq,),
        in_specs=[pl.BlockSpec((rows, CONV_DIM), row),
                  pl.BlockSpec((n_seq, CONV_W - 1, CONV_DIM), lambda b: (b, 0, 0)),
                  pl.BlockSpec((rows, D_SSM), row), pl.BlockSpec((rows, DT_PAD), row),
                  pl.BlockSpec((n_seq, D_SSM, D_STATE), lambda b: (b, 0, 0)),
                  pl.BlockSpec((CONV_W, CONV_DIM), const),
                  vec(CONV_DIM), vec(DT_PAD), vec(DT_PAD), vec(D_SSM), vec(D_SSM)],
        out_specs=[pl.BlockSpec((rows, D_SSM), row),
                   pl.BlockSpec((n_seq, D_SSM, D_STATE), lambda b: (b, 0, 0))],
        out_shape=[jax.ShapeDtypeStruct((t, D_SSM), F32),
                   jax.ShapeDtypeStruct((n_b, D_SSM, D_STATE), F32)],
        scratch_shapes=[pltpu.VMEM((n_seq, n_tok, CONV_DIM), F32),
                        pltpu.VMEM((D_SSM, rows), F32),
                        pltpu.VMEM((N_SSM_HEADS, rows, LANES), F32),
                        pltpu.VMEM((rows, D_SSM), F32),
                        pltpu.VMEM((rows, D_SSM), F32)],
        compiler_params=_params(1),
        name="ssd_sample",
    )(xbc, conv_state, z, dt, h0, cw, cb, dtb, alog, dsk, nrm)


HEADS_PER_HALF = LANES // ATT_HEAD_DIM
N_HALVES = D_GROUP // LANES


ATT_UNITS = 4


def _attn_prompt_body(slope_ref, *refs, dil, group):
    n_in = 5 * N_HALVES
    in_refs, o_refs, l_refs = refs[:n_in], refs[n_in:n_in + N_HALVES], refs[n_in + N_HALVES:]
    i = pl.program_id(0)
    nb = ATT_BLOCK
    qi = lax.broadcasted_iota(jnp.int32, (nb, 2 * nb), 0)
    kj = lax.broadcasted_iota(jnp.int32, (nb, 2 * nb), 1)
    dist = nb + qi - kj
    band = (dist >= 0) & (dist <= nb)
    band_first = band & (kj >= jnp.where(i > 0, 0, nb))
    dist_f = (dist * dil).astype(F32)
    alibi = [-slope_ref[group * H_PER_DIL + h] * dist_f for h in range(H_PER_DIL)]
    bias_first = [jnp.where(band_first, a, -jnp.inf) for a in alibi]
    bias_inner = [jnp.where(band, a, -jnp.inf) for a in alibi] if dil == 1 else bias_first
    lane_head = lax.broadcasted_iota(jnp.int32, (nb, LANES), 1) // ATT_HEAD_DIM
    for u in range(ATT_UNITS):
        if dil == 1:
            rows = pl.ds(u * nb, nb)
            rows_prev = pl.ds((ATT_UNITS - 1) * nb, nb) if u == 0 else pl.ds((u - 1) * nb, nb)
            bias = bias_first if u == 0 else bias_inner
        else:
            rows = rows_prev = pl.ds(pl.program_id(1) * ATT_UNITS + u, nb, stride=dil)
            bias = bias_first
        for half in range(N_HALVES):
            q_ref, kp_ref, kc_ref, vp_ref, vc_ref = in_refs[5 * half:5 * half + 5]
            if dil == 1 and u > 0:
                kp_ref, vp_ref = kc_ref, vc_ref
            q = q_ref[rows, :] * (ATT_HEAD_DIM ** -0.5)
            k = jnp.concatenate([kp_ref[rows_prev, :], kc_ref[rows, :]], axis=0).astype(BF16)
            v = jnp.concatenate([vp_ref[rows_prev, :], vc_ref[rows, :]], axis=0).astype(BF16)
            out = jnp.zeros((nb, LANES), F32)
            lse = jnp.zeros((nb, LANES), F32)
            for hh in range(HEADS_PER_HALF):
                mine = lane_head == hh
                qh = jnp.where(mine, q, 0.0).astype(BF16)
                s = _dot_nt(qh, k) + bias[half * HEADS_PER_HALF + hh]
                m = jnp.max(s, axis=1, keepdims=True)
                p = jnp.exp(s - m)
                l = jnp.sum(p, axis=1, keepdims=True)
                pv = _dot(p.astype(BF16), v)
                out = jnp.where(mine, pv / l, out)
                lse = jnp.where(mine, m + jnp.log(l), lse)
            o_refs[half][rows, :] = out
            l_refs[half][rows, :] = lse


def _attn_prompt(slopes, qkv, *, group, dil, seq):
    assert dil == 1 or dil % ATT_UNITS == 0
    blk_rows = max(dil, ATT_UNITS) * ATT_BLOCK
    n_blk = seq // blk_rows
    n_res_steps = max(dil // ATT_UNITS, 1)
    blk = (blk_rows, LANES)
    cur = lambda c: pl.BlockSpec((None,) + blk, lambda i, r: (c, i, 0))
    prev = lambda c: pl.BlockSpec((None,) + blk, lambda i, r: (c, jnp.maximum(i - 1, 0), 0))
    in_specs = [pl.BlockSpec(memory_space=pltpu.SMEM)]
    for half in range(N_HALVES):
        q_col, k_col, v_col = (N_HALVES * (part * N_DIL + group) + half for part in range(3))
        in_specs += [cur(q_col), prev(k_col), cur(k_col), prev(v_col), cur(v_col)]
    out_blk = pl.BlockSpec(blk, lambda i, r: (i, 0))
    res = pl.pallas_call(
        functools.partial(_attn_prompt_body, dil=dil, group=group),
        grid=(n_blk, n_res_steps),
        in_specs=in_specs,
        out_specs=[out_blk] * (2 * N_HALVES),
        out_shape=[jax.ShapeDtypeStruct((seq, LANES), F32)] * (2 * N_HALVES),
        compiler_params=_params(2),
        name=f"attn_prompt_g{group}",
    )(slopes, *([qkv] * (5 * N_HALVES)))
    return res[:N_HALVES], res[N_HALVES:]


def _window_buffers(caches):
    out = []
    for (win, dil), pair in zip(DILATION_PAIRS, caches):
        assert dil & (dil - 1) == 0, "dilations must be powers of two"
        for cache in pair:
            assert cache.shape[1] == win, "window buffers must hold a full window"
            out.append(jnp.transpose(cache, (0, 2, 3, 1)).reshape(cache.shape[0], D_GROUP, win))
    return out


def _attn_sample_scores(slope_ref, qkv, k_bufs, n_tok):
    n_rows = H_PER_DIL * n_tok
    row2 = lax.broadcasted_iota(jnp.int32, (n_rows, D_GROUP), 0)
    lane2 = lax.broadcasted_iota(jnp.int32, (n_rows, D_GROUP), 1)
    own_head = (lane2 // ATT_HEAD_DIM) == (row2 // n_tok)
    row1 = lax.broadcasted_iota(jnp.int32, (n_rows, 1), 0)
    tok1 = row1 % n_tok
    head1 = row1 // n_tok
    scale = ATT_HEAD_DIM ** -0.5
    state = []
    for g, (win, dil) in enumerate(DILATION_PAIRS):
        q = qkv[:, g * D_GROUP:(g + 1) * D_GROUP]
        k_new = qkv[:, D_ATT + g * D_GROUP:D_ATT + (g + 1) * D_GROUP]
        v_new = qkv[:, 2 * D_ATT + g * D_GROUP:2 * D_ATT + (g + 1) * D_GROUP]
        q_rows = jnp.where(own_head, jnp.concatenate([q] * H_PER_DIL, axis=0), 0.0)
        slope_1 = jnp.zeros((n_rows, 1), F32)
        for h in range(H_PER_DIL):
            slope_1 = jnp.where(head1 == h, slope_ref[g * H_PER_DIL + h], slope_1)

        s_buf = _dot(q_rows.astype(BF16), k_bufs[g].astype(BF16)) * scale
        pos = lax.broadcasted_iota(jnp.int32, (n_rows, win), 1)
        tok_s = lax.broadcasted_iota(jnp.int32, (n_rows, win), 0) % n_tok
        dist = win + tok_s - pos
        on_grid = (pos >= tok_s) & (jnp.bitwise_and(dist, dil - 1) == 0)
        s_buf = jnp.where(on_grid, s_buf - slope_1 * dist.astype(F32), -jnp.inf)
        m = jnp.max(s_buf, axis=1, keepdims=True)
        s_new = []
        for tp in range(n_tok):
            sn = jnp.sum(q_rows * k_new[tp:tp + 1, :], axis=1, keepdims=True) * scale
            ok = (tok1 >= tp) & (jnp.bitwise_and(tok1 - tp, dil - 1) == 0)
            sn = jnp.where(ok, sn - slope_1 * (tok1 - tp).astype(F32), -jnp.inf)
            s_new.append(sn)
            m = jnp.maximum(m, sn)
        state.append((s_buf, s_new, m, v_new))
    return state


def _attn_sample_outputs(state, v_bufs, att_ref, n_tok):
    lane_head8 = lax.broadcasted_iota(jnp.int32, (n_tok, D_GROUP), 1) // ATT_HEAD_DIM
    outs, lses = [], []
    for g, (s_buf, s_new, m, v_new) in enumerate(state):
        p_buf = jnp.exp(s_buf - m)
        l = jnp.sum(p_buf, axis=1, keepdims=True)
        o_rows = _dot_nt(p_buf.astype(BF16), v_bufs[g].astype(BF16))
        for tp in range(n_tok):
            p_new = jnp.exp(s_new[tp] - m)
            l = l + p_new
            o_rows = o_rows + p_new * v_new[tp:tp + 1, :]
        o_rows = o_rows / l
        lse_rows = m + jnp.log(l)
        o = jnp.zeros((n_tok, D_GROUP), F32)
        lse = jnp.zeros((n_tok, D_GROUP), F32)
        for h in range(H_PER_DIL):
            o = jnp.where(lane_head8 == h, o_rows[h * n_tok:(h + 1) * n_tok, :], o)
            lse = jnp.where(lane_head8 == h, lse_rows[h * n_tok:(h + 1) * n_tok, :], lse)
        outs.append(o)
        lses.append(lse)

    m = jnp.maximum(jnp.maximum(lses[0], lses[1]), lses[2])
    e = [jnp.exp(x - m) for x in lses]
    den = e[0] + e[1] + e[2]
    for g in range(N_DIL):
        att_ref[:, g * D_GROUP:(g + 1) * D_GROUP] = outs[g] * (e[g] / den)


def _outproj_body(*refs, combine):
    n_pieces = N_DIL * N_HALVES
    if combine:
        x_ref, ssm_ref = refs[:2]
        o_refs, l_refs = refs[2:2 + n_pieces], refs[2 + n_pieces:2 + 2 * n_pieces]
        w_ref, out_ref = refs[2 + 2 * n_pieces:]
        att = [None] * n_pieces
        for half in range(N_HALVES):
            idx = [g * N_HALVES + half for g in range(N_DIL)]
            lses = [l_refs[p][...] for p in idx]
            m = jnp.maximum(jnp.maximum(lses[0], lses[1]), lses[2])
            e = [jnp.exp(x - m) for x in lses]
            den = e[0] + e[1] + e[2]
            for p, w in zip(idx, e):
                att[p] = o_refs[p][...] * (w / den)
    else:
        x_ref, ssm_ref, att_ref, w_ref, out_ref = refs
        att = [att_ref[:, p * LANES:(p + 1) * LANES] for p in range(n_pieces)]
    mixed = jnp.concatenate([ssm_ref[...]] + att, axis=1).astype(BF16)
    out_ref[...] = x_ref[...] + _dot(mixed, w_ref[...].astype(BF16))


def _outproj(x, ssm, att, w_out, *, tm=512):
    t = x.shape[0]
    row = lambda i: (i, 0)
    combine = isinstance(att, (tuple, list))
    att_args = list(att[0]) + list(att[1]) if combine else [att]
    att_specs = [pl.BlockSpec((tm, a.shape[1]), row) for a in att_args]
    return pl.pallas_call(
        functools.partial(_outproj_body, combine=combine),
        grid=(t // tm,),
        in_specs=[pl.BlockSpec((tm, D_MODEL), row), pl.BlockSpec((tm, D_SSM), row)] + att_specs
                 + [pl.BlockSpec((D_MODEL, D_MODEL), lambda i: (0, 0), pipeline_mode=pl.Buffered(1))],
        out_specs=pl.BlockSpec((tm, D_MODEL), row),
        out_shape=jax.ShapeDtypeStruct((t, D_MODEL), F32),
        compiler_params=_params(1),
        name="outproj_prompt" if combine else "outproj_sample",
    )(x, ssm, *att_args, w_out)


def kernel(x_prompt, x_sample, cache_k0, cache_v0, cache_k1, cache_v1, cache_k2, cache_v2, state_ssm, state_conv, ffn1_norm, ffn1_w_gate, ffn1_w_up, ffn1_w_down, mix_norm, w_in, conv_w, conv_b, dt_bias, a_log, d_skip, ssm_norm, w_out, ffn2_norm, ffn2_w_gate, ffn2_w_up, ffn2_w_down, final_norm):
    assert w_in.shape[0] == 1, "single layer"
    batch, seq, _ = x_prompt.shape
    dec_batch, dec_seq, _ = x_sample.shape
    assert batch == 1 and dec_seq == SUBLANES

    row = lambda v: v.reshape(1, -1).astype(F32)
    pad_heads = lambda v: jnp.pad(v.reshape(1, -1).astype(F32), ((0, 0), (0, DT_PAD - N_SSM_HEADS)))
    w_in_t = jnp.transpose(w_in[0]).astype(BF16)
    ffn1 = (row(ffn1_norm[0]), ffn1_w_gate[0], ffn1_w_up[0], ffn1_w_down[0])
    ffn2 = (row(ffn2_norm[0]), ffn2_w_gate[0], ffn2_w_up[0], ffn2_w_down[0])
    w_o = w_out[0]
    ssd_params = (conv_w[0].astype(F32), row(conv_b[0]), pad_heads(dt_bias[0]), pad_heads(a_log[0]),
                  row(jnp.repeat(d_skip[0], SSM_HEAD_DIM)), row(ssm_norm[0]))
    slopes = jnp.exp2(-ALIBI_MAX_EXP * jnp.arange(1, N_ATT_HEADS + 1, dtype=F32) / N_ATT_HEADS)

    def back(x2):
        return _ffn(x2, *ffn2, row(final_norm))

    n_s = dec_batch * dec_seq
    x1_s = _ffn(x_sample.reshape(n_s, D_MODEL), *ffn1)
    z, xbc_s, qkv_s, dt = _inproj(x1_s, row(mix_norm[0]), w_in_t)
    ssm_s, h_s = _ssd_sample(xbc_s, state_conv[0], z, dt, state_ssm[0].reshape(dec_batch, D_SSM, D_STATE),
                             *ssd_params, n_tok=dec_seq)
    caches = ((cache_k0[0], cache_v0[0]), (cache_k1[0], cache_v1[0]), (cache_k2[0], cache_v2[0]))

    x1, att_s = _ffn(x_prompt.reshape(seq, D_MODEL), *ffn1,
                     host=(slopes, qkv_s, _window_buffers(caches), dec_seq))
    xbc_p, qkv_p, ssm_p, hfin_p = _inproj_ssd(x1, row(mix_norm[0]), w_in_t, *ssd_params)
    outs, lses = [], []
    for g, (win, dil) in enumerate(DILATION_PAIRS):
        o, lse = _attn_prompt(slopes, qkv_p, group=g, dil=dil, seq=seq)
        outs += o
        lses += lse
    y_prompt = back(_outproj(x1, ssm_p, (outs, lses), w_o)).reshape(1, seq, D_MODEL)

    y_sample = back(_outproj(x1_s, ssm_s, att_s, w_o)).reshape(dec_batch, dec_seq, D_MODEL)

    head_shape = (H_PER_DIL, ATT_HEAD_DIM)
    p_kv, s_kv = [], []
    for g, (win, dil) in enumerate(DILATION_PAIRS):
        keep = min(win, seq)
        for base in (D_ATT, 2 * D_ATT):
            c0 = base + g * D_GROUP
            blocks = [qkv_p[c0 // LANES + half, seq - keep:] for half in range(N_HALVES)]
            p_kv.append(jnp.concatenate(blocks, axis=1).reshape((1, 1, keep) + head_shape))
            s_kv.append(qkv_s[:, c0:c0 + D_GROUP].reshape((1, dec_batch, dec_seq) + head_shape))
    p_ssm = hfin_p.reshape(1, 1, N_SSM_HEADS, SSM_HEAD_DIM, D_STATE)
    p_conv = xbc_p[seq - (CONV_W - 1):].reshape(1, 1, CONV_W - 1, CONV_DIM)
    s_ssm = h_s.reshape(1, dec_batch, N_SSM_HEADS, SSM_HEAD_DIM, D_STATE)
    s_conv = xbc_s.reshape(dec_batch, dec_seq, CONV_DIM)[:, dec_seq - (CONV_W - 1):].reshape(
        1, dec_batch, CONV_W - 1, CONV_DIM)
    return (y_prompt, y_sample, *p_kv, p_ssm, p_conv, *s_kv, s_ssm, s_conv)
```

```python
import functools

import numpy as np
import jax
import jax.numpy as jnp
from jax import lax
from jax.experimental import pallas as pl
from jax.experimental.pallas import tpu as pltpu

F32 = jnp.float32
BF16 = jnp.bfloat16

D_MODEL = 2048
D_FF = 5632
D_ATT = 768
ATT_HEAD_DIM = 64
H_PER_DIL = 4
D_GROUP = H_PER_DIL * ATT_HEAD_DIM
DILATION_PAIRS = ((128, 1), (512, 4), (2048, 16))
N_DIL = len(DILATION_PAIRS)
ATT_BLOCK = 128
D_SSM = 1280
SSM_HEAD_DIM = 64
N_SSM_HEADS = 20
N_SSM_GROUPS = 4
HEADS_PER_SSM_GROUP = 5
D_SSM_GROUP = D_SSM // N_SSM_GROUPS
D_STATE = 128
CONV_W = 4
CONV_DIM = 2304
SSD_CHUNK = 128
RMS_EPS = 1e-6
ALIBI_MAX_EXP = 8.0
N_ATT_HEADS = 12

LANES = 128
SUBLANES = 8
DT_PAD = LANES
IN_HEAD = D_SSM + CONV_DIM
IN_TAIL = N_SSM_HEADS + 3 * D_ATT
N_QKV_BLOCKS = 3 * D_ATT // LANES

VMEM_LIMIT = 60 * 1024 * 1024


def _params(n_grid_dims, vmem=VMEM_LIMIT):
    return pltpu.CompilerParams(dimension_semantics=("arbitrary",) * n_grid_dims,
                                vmem_limit_bytes=vmem)


def _silu(x):
    return x * (0.5 + 0.5 * jnp.tanh(0.5 * x))


def _softplus(x):
    return jnp.maximum(x, 0.0) + jnp.log(1.0 + jnp.exp(-jnp.abs(x)))


def _rms_scale(x):
    return x * lax.rsqrt(jnp.mean(x * x, axis=-1, keepdims=True) + RMS_EPS)


def _dot(a, b):
    return jnp.dot(a, b, preferred_element_type=F32)


def _dot_nt(a, b):
    return lax.dot_general(a, b, (((1,), (1,)), ((), ())), preferred_element_type=F32)


def _cumsum_rows(x):
    n = x.shape[0]
    rows = lax.broadcasted_iota(jnp.int32, x.shape, 0)
    shift = 1
    while shift < n:
        x = x + jnp.where(rows >= shift, pltpu.roll(x, shift, 0), 0.0)
        shift *= 2
    return x


FFN_ROW_CHUNKS = 4


def _ffn_body(*refs, n_f, final, n_guest):
    hosted = n_guest > 0
    refs = list(refs)
    x_ref, g_ref, wg_ref, wu_ref, wd_ref = refs[:5]
    del refs[:5]
    fg_ref = refs.pop(0) if final else None
    if hosted:
        slope_ref, qkv_ref, k0_ref, v0_ref, k1_ref, v1_ref, k2_ref, v2_ref = refs[:8]
        del refs[:8]
        o_ref, att_ref, h_ref = refs
    else:
        o_ref, h_ref = refs
    f = pl.program_id(1)
    tm = x_ref.shape[0]
    assert n_f >= 2 and tm % FFN_ROW_CHUNKS == 0

    def guest_scores():
        if hosted:
            k_bufs = [k0_ref[0], k1_ref[0], k2_ref[0]]
            return _attn_sample_scores(slope_ref, qkv_ref[...], k_bufs, qkv_ref.shape[0])

    def guest_outputs(state):
        if hosted:
            v_bufs = [v0_ref[0], v1_ref[0], v2_ref[0]]
            _attn_sample_outputs(state, v_bufs, att_ref, qkv_ref.shape[0])

    def weights():
        return wg_ref[...].astype(BF16), wu_ref[...].astype(BF16), wd_ref[...].astype(BF16)

    def gate_up(h, w, with_guest=False):
        gate = _dot(h, w[0])
        state = guest_scores() if with_guest else None
        up = _dot(h, w[1])
        if with_guest:
            guest_outputs(state)
        return gate, up

    def down(gate_and_up, w):
        gate, up = gate_and_up
        return _dot((_silu(gate) * up).astype(BF16), w[2])

    def partial_out(h, w, with_guest=False):
        return down(gate_up(h, w, with_guest), w)

    chunks = [slice(c * tm // FFN_ROW_CHUNKS, (c + 1) * tm // FFN_ROW_CHUNKS) for c in range(FFN_ROW_CHUNKS)]

    def chunked(h_of, finish):
        w = weights()
        pending = gate_up(h_of(chunks[0]), w)
        for c, rows in enumerate(chunks):
            current = pending
            if c + 1 < len(chunks):
                pending = gate_up(h_of(chunks[c + 1]), w, with_guest=(c == 0))
            finish(rows, down(current, w))

    @pl.when(f == 0)
    def _():
        def h_of(rows):
            h = (_rms_scale(x_ref[rows, :]) * g_ref[...]).astype(BF16)
            h_ref[rows, :] = h
            return h

        def finish(rows, part):
            o_ref[rows, :] = part

        chunked(h_of, finish)

    middle = (f > 0) & (f < n_f - 1)
    guest_left = pl.program_id(0) * n_f + f < n_guest

    if hosted:
        @pl.when(middle & guest_left)
        def _():
            o_ref[...] += partial_out(h_ref[...], weights(), with_guest=True)

    @pl.when(middle & jnp.logical_not(guest_left))
    def _():
        o_ref[...] += partial_out(h_ref[...], weights())

    @pl.when(f == n_f - 1)
    def _():
        def finish(rows, part):
            y = x_ref[rows, :] + 0.5 * (o_ref[rows, :] + part)
            if final:
                y = _rms_scale(y) * fg_ref[...]
            o_ref[rows, :] = y

        chunked(lambda rows: h_ref[rows, :], finish)


def _ffn(x, gain, wg, wu, wd, final_gain=None, *, host=None, tm=1024, tf=256):
    t = x.shape[0]
    n_f = D_FF // tf
    final = final_gain is not None
    hosted = host is not None
    row = lambda i, f: (i, 0)
    const = lambda i, f: (0, 0)
    tile_mode = dict(pipeline_mode=pl.Buffered(1)) if hosted else {}
    n_guest = 0
    in_specs = [pl.BlockSpec((tm, D_MODEL), row), pl.BlockSpec((1, D_MODEL), const),
                pl.BlockSpec((D_MODEL, tf), lambda i, f: (0, f)),
                pl.BlockSpec((D_MODEL, tf), lambda i, f: (0, f)),
                pl.BlockSpec((tf, D_MODEL), lambda i, f: (f, 0))]
    args = [x, gain, wg, wu, wd]
    if final:
        in_specs.append(pl.BlockSpec((1, D_MODEL), const))
        args.append(final_gain)
    out_specs = pl.BlockSpec((tm, D_MODEL), row, **tile_mode)
    out_shape = jax.ShapeDtypeStruct((t, D_MODEL), F32)
    if hosted:
        slopes, qkv, caches, n_tok = host
        n_guest = n_b = qkv.shape[0] // n_tok
        assert n_b <= (t // tm) * n_f, "one sequence per grid step"
        seq = lambda i, f: jnp.minimum(i * n_f + f, n_b - 1)
        in_specs += [pl.BlockSpec(memory_space=pltpu.SMEM),
                     pl.BlockSpec((n_tok, 3 * D_ATT), lambda i, f: (seq(i, f), 0))]
        in_specs += [pl.BlockSpec((1,) + c.shape[1:], lambda i, f: (seq(i, f), 0, 0)) for c in caches]
        args += [slopes, qkv, *caches]
        out_specs = [out_specs, pl.BlockSpec((n_tok, D_ATT), lambda i, f: (seq(i, f), 0))]
        out_shape = [out_shape, jax.ShapeDtypeStruct((qkv.shape[0], D_ATT), F32)]
    return pl.pallas_call(
        functools.partial(_ffn_body, n_f=n_f, final=final, n_guest=n_guest),
        grid=(t // tm, n_f),
        in_specs=in_specs,
        out_specs=out_specs,
        out_shape=out_shape,
        scratch_shapes=[pltpu.VMEM((tm, D_MODEL), BF16)],
        compiler_params=_params(2),
        name=("ffn_final" if final else "ffn") + ("_host" if hosted else ""),
    )(*args)


def _inproj_body(x_ref, g_ref, w_ref, z_ref, xbc_ref, qkv_ref, dt_ref):
    h = (_rms_scale(x_ref[...]) * g_ref[...]).astype(BF16)
    z_ref[...] = _dot_nt(h, w_ref[0:D_SSM, :])
    xbc_ref[...] = _dot_nt(h, w_ref[D_SSM:IN_HEAD, :])
    tail = _dot_nt(h, w_ref[IN_HEAD:IN_HEAD + IN_TAIL, :])
    lane = lax.broadcasted_iota(jnp.int32, (tail.shape[0], DT_PAD), 1)
    dt_ref[...] = jnp.where(lane < N_SSM_HEADS, tail[:, 0:DT_PAD], 0.0)
    qkv_ref[...] = tail[:, N_SSM_HEADS:IN_TAIL]


def _inproj(x, gain, w_t, *, tm=256):
    t = x.shape[0]
    row = lambda i: (i, 0)
    const = lambda i: (0, 0)
    widths = (D_SSM, CONV_DIM, 3 * D_ATT, DT_PAD)
    return pl.pallas_call(
        _inproj_body,
        grid=(t // tm,),
        in_specs=[pl.BlockSpec((tm, D_MODEL), row), pl.BlockSpec((1, D_MODEL), const),
                  pl.BlockSpec((IN_HEAD + IN_TAIL, D_MODEL), const, pipeline_mode=pl.Buffered(1))],
        out_specs=[pl.BlockSpec((tm, w), row) for w in widths],
        out_shape=[jax.ShapeDtypeStruct((t, w), F32) for w in widths],
        compiler_params=_params(1),
        name="inproj",
    )(x, gain, w_t)


def _pair_blockdiag(x, low_half):
    zero = jnp.zeros_like(x)
    return jnp.concatenate([jnp.where(low_half, x, zero), jnp.where(low_half, zero, x)],
                           axis=0).astype(BF16)


def _ssd_chunk_phases(xbc_ref, z_ref, dt_ref, cw_ref, cb_ref, dtb_ref, alog_ref, dsk_ref, nrm_ref,
                      y_ref, xf_scr, ht_scr):
    cl = SSD_CHUNK
    st = {}

    def prologue():
        x = xbc_ref[...]
        xf_scr[SUBLANES:SUBLANES + cl, :] = x
        conv = cb_ref[...]
        for j in range(CONV_W - 1):
            conv = conv + xf_scr[SUBLANES - 3 + j:SUBLANES - 3 + j + cl, :] * cw_ref[j:j + 1, :]
        conv = conv + x * cw_ref[CONV_W - 1:CONV_W, :]
        xf_scr[0:SUBLANES, :] = x[cl - SUBLANES:cl, :]
        xa = _silu(conv)
        st["xs"] = xa[:, :D_SSM]
        bm = xa[:, D_SSM:D_SSM + N_SSM_GROUPS * D_STATE]
        cm = xa[:, D_SSM + N_SSM_GROUPS * D_STATE:]

        dt = _softplus(dt_ref[...] + dtb_ref[...])
        a = -jnp.exp(alog_ref[...])
        acum = _cumsum_rows(dt * a)
        w_end = dt * jnp.exp(acum[cl - 1:cl, :] - acum)
        st["acum_t"] = acum.T
        st["dt_t"] = dt.T
        st["w_end_t"] = w_end.T

        rows = lax.broadcasted_iota(jnp.int32, (cl, cl), 0)
        cols = lax.broadcasted_iota(jnp.int32, (cl, cl), 1)
        st["causal"] = rows >= cols
        st["low_half"] = cols < SSM_HEAD_DIM

        b_g = [bm[:, g * D_STATE:(g + 1) * D_STATE] for g in range(N_SSM_GROUPS)]
        st["c_g"] = [cm[:, g * D_STATE:(g + 1) * D_STATE].astype(BF16) for g in range(N_SSM_GROUPS)]
        st["b_t"] = [b.T for b in b_g]
        st["cb"] = [_dot_nt(st["c_g"][g], b_g[g].astype(BF16)) for g in range(N_SSM_GROUPS)]
        st["sumsq"] = [jnp.zeros((cl, 1), F32) for _ in range(N_SSM_GROUPS)]

    def pair(j):
        lo, hi = j * LANES, (j + 1) * LANES
        heads = (2 * j, 2 * j + 1)
        groups = tuple(h // HEADS_PER_SSM_GROUP for h in heads)
        low_half, c_g = st["low_half"], st["c_g"]
        xs_p = st["xs"][:, lo:hi]
        x_bd = _pair_blockdiag(xs_p, low_half)
        ht_p = ht_scr[:, lo:hi]
        h_bd = _pair_blockdiag(ht_p, low_half)
        g_mats, w_mats, e_cols = [], [], []
        for h, g in zip(heads, groups):
            row_b = jnp.broadcast_to(st["acum_t"][h:h + 1, :], (cl, cl))
            col_b = row_b.T
            decay = jnp.where(st["causal"], jnp.exp(col_b - row_b), 0.0)
            g_mats.append((st["cb"][g] * decay * st["dt_t"][h:h + 1, :]).astype(BF16))
            w_mats.append((st["b_t"][g] * st["w_end_t"][h:h + 1, :]).astype(BF16))
            e_cols.append(jnp.exp(col_b))
        y_diag = _dot(jnp.concatenate(g_mats, axis=1), x_bd)
        y_off = _dot(jnp.concatenate([c_g[groups[0]], c_g[groups[1]]], axis=1), h_bd)
        y_off = y_off * jnp.where(low_half, e_cols[0], e_cols[1])
        state = _dot(jnp.concatenate(w_mats, axis=1), x_bd)
        chunk_decay = jnp.where(low_half[0:1, :], e_cols[0][cl - 1:cl, :], e_cols[1][cl - 1:cl, :])
        ht_scr[:, lo:hi] = ht_p * chunk_decay + state
        y_p = y_diag + y_off + dsk_ref[:, lo:hi] * xs_p
        y_p = y_p * _silu(z_ref[:, lo:hi])
        y_ref[:, lo:hi] = y_p
        sq = y_p * y_p
        sumsq = st["sumsq"]
        sumsq[groups[0]] = sumsq[groups[0]] + jnp.sum(jnp.where(low_half, sq, 0.0), axis=1, keepdims=True)
        sumsq[groups[1]] = sumsq[groups[1]] + jnp.sum(jnp.where(low_half, 0.0, sq), axis=1, keepdims=True)

    def epilogue():
        scale = [lax.rsqrt(s / D_SSM_GROUP + RMS_EPS) for s in st["sumsq"]]
        for j in range(N_SSM_HEADS // 2):
            lo, hi = j * LANES, (j + 1) * LANES
            g0, g1 = (2 * j) // HEADS_PER_SSM_GROUP, (2 * j + 1) // HEADS_PER_SSM_GROUP
            y_ref[:, lo:hi] = (y_ref[:, lo:hi] * jnp.where(st["low_half"], scale[g0], scale[g1])
                               * nrm_ref[:, lo:hi])

    return [prologue] + [functools.partial(pair, j) for j in range(N_SSM_HEADS // 2)] + [epilogue]


PROJ_TILE = 256


def _inproj_ssd_body(x_ref, g_ref, w_ref, cw_ref, cb_ref, dtb_ref, alog_ref, dsk_ref, nrm_ref,
                     xbc_ref, qkv_ref, y_ref, hfin_ref,
                     xbc_a, z_a, dt_a, xbc_b, z_b, dt_b, xf_scr, ht_scr, *, n_blocks):
    s = pl.program_id(0)
    tm = x_ref.shape[0]
    bufs = ((xbc_a, z_a, dt_a), (xbc_b, z_b, dt_b))

    @pl.when(s <= 1)
    def _():
        xf_scr[0:SUBLANES, :] = jnp.zeros((SUBLANES, CONV_DIM), F32)
        ht_scr[...] = jnp.zeros_like(ht_scr)

    @pl.when(s == 0)
    def _():
        for ref in bufs[1]:
            ref[...] = jnp.zeros_like(ref)

    def step(parity):
        xbc_w, z_w, dt_w = bufs[parity]
        xbc_r, z_r, dt_r = bufs[1 - parity]
        h = (_rms_scale(x_ref[...]) * g_ref[...]).astype(BF16)

        def z_tile(c0, c1):
            z_w[:, c0:c1] = _dot_nt(h, w_ref[c0:c1, :])

        def xbc_tile(c0, c1):
            xbc = _dot_nt(h, w_ref[D_SSM + c0:D_SSM + c1, :])
            xbc_w[:, c0:c1] = xbc
            xbc_ref[:, c0:c1] = xbc

        tail_parts = []

        def tail_tile(c0, c1):
            tail_parts.append(_dot_nt(h, w_ref[IN_HEAD + c0:IN_HEAD + c1, :]))
            if c1 == IN_TAIL:
                tail = jnp.concatenate(tail_parts, axis=1)
                lane = lax.broadcasted_iota(jnp.int32, (tm, DT_PAD), 1)
                dt_w[...] = jnp.where(lane < N_SSM_HEADS, tail[:, 0:DT_PAD], 0.0)
                qkv = tail[:, N_SSM_HEADS:IN_TAIL]
                for j in range(N_QKV_BLOCKS):
                    qkv_ref[j] = qkv[:, j * LANES:(j + 1) * LANES]

        tiles = [functools.partial(z_tile, c, c + PROJ_TILE) for c in range(0, D_SSM, PROJ_TILE)]
        tiles += [functools.partial(xbc_tile, c, c + PROJ_TILE) for c in range(0, CONV_DIM, PROJ_TILE)]
        n_tail = IN_TAIL // PROJ_TILE
        tail_cuts = [c * PROJ_TILE for c in range(n_tail)] + [IN_TAIL]
        tiles += [functools.partial(tail_tile, c0, c1) for c0, c1 in zip(tail_cuts[:-1], tail_cuts[1:])]
        phases = []
        for c in range(tm // SSD_CHUNK):
            rows = slice(c * SSD_CHUNK, (c + 1) * SSD_CHUNK)
            phases += _ssd_chunk_phases(xbc_r.at[rows, :], z_r.at[rows, :], dt_r.at[rows, :], cw_ref,
                                        cb_ref, dtb_ref, alog_ref, dsk_ref, nrm_ref, y_ref.at[rows, :],
                                        xf_scr, ht_scr)
        done = 0
        for i, tile in enumerate(tiles):
            tile()
            upto = (i + 1) * len(phases) // len(tiles)
            for phase in phases[done:upto]:
                phase()
            done = upto

    @pl.when(s % 2 == 0)
    def _():
        step(0)

    @pl.when(s % 2 == 1)
    def _():
        step(1)

    @pl.when(s == n_blocks)
    def _():
        for j in range(D_SSM // LANES):
            hfin_ref[j * LANES:(j + 1) * LANES, :] = ht_scr[:, j * LANES:(j + 1) * LANES].T


def _inproj_ssd(x, gain, w_t, cw, cb, dtb, alog, dsk, nrm, *, tm=256):
    t = x.shape[0]
    n_blocks = t // tm
    cur = lambda s: (jnp.minimum(s, n_blocks - 1), 0)
    prev = lambda s: (jnp.maximum(s - 1, 0), 0)
    const = lambda s: (0, 0)
    vec = lambda n: pl.BlockSpec((1, n), const)
    buf = lambda n: pltpu.VMEM((tm, n), F32)
    return pl.pallas_call(
        functools.partial(_inproj_ssd_body, n_blocks=n_blocks),
        grid=(n_blocks + 1,),
        in_specs=[pl.BlockSpec((tm, D_MODEL), cur), vec(D_MODEL),
                  pl.BlockSpec((IN_HEAD + IN_TAIL, D_MODEL), const, pipeline_mode=pl.Buffered(1)),
                  pl.BlockSpec((CONV_W, CONV_DIM), const),
                  vec(CONV_DIM), vec(DT_PAD), vec(DT_PAD), vec(D_SSM), vec(D_SSM)],
        out_specs=[pl.BlockSpec((tm, CONV_DIM), cur),
                   pl.BlockSpec((N_QKV_BLOCKS, tm, LANES), lambda s: (0, jnp.minimum(s, n_blocks - 1), 0)),
                   pl.BlockSpec((tm, D_SSM), prev), pl.BlockSpec((D_SSM, D_STATE), const)],
        out_shape=[jax.ShapeDtypeStruct((t, CONV_DIM), F32),
                   jax.ShapeDtypeStruct((N_QKV_BLOCKS, t, LANES), F32),
                   jax.ShapeDtypeStruct((t, D_SSM), F32), jax.ShapeDtypeStruct((D_SSM, D_STATE), F32)],
        scratch_shapes=[buf(CONV_DIM), buf(D_SSM), buf(DT_PAD), buf(CONV_DIM), buf(D_SSM), buf(DT_PAD),
                        pltpu.VMEM((SUBLANES + SSD_CHUNK, CONV_DIM), F32),
                        pltpu.VMEM((D_STATE, D_SSM), F32)],
        compiler_params=_params(1),
        name="inproj_ssd",
    )(x, gain, w_t, cw, cb, dtb, alog, dsk, nrm)


def _ssd_sample_body(xbc_ref, cs_ref, z_ref, dt_ref, h0_ref, cw_ref, cb_ref, dtb_ref, alog_ref,
                     dsk_ref, nrm_ref, y_ref, hout_ref, hist_scr, xwt_scr, ecd_scr, eexp_scr, yoff_scr,
                     *, n_tok, n_seq):
    n_rows = n_seq * n_tok
    assert n_rows == LANES

    @pl.when(pl.program_id(0) == 0)
    def _():
        hist_scr[...] = jnp.zeros_like(hist_scr)

    x = xbc_ref[...]
    hist_scr[:, 0:CONV_W - 1, :] = cs_ref[...]
    hist = hist_scr[...].reshape(n_rows, CONV_DIM)
    tok = lax.broadcasted_iota(jnp.int32, (n_rows, 1), 0) % n_tok
    conv = cb_ref[...] + x * cw_ref[CONV_W - 1:CONV_W, :]
    for k in range(1, CONV_W):
        back = CONV_W - 1 - k
        from_hist = hist if back == 0 else pltpu.roll(hist, n_rows - back, 0)
        shifted = jnp.where(tok < k, from_hist, pltpu.roll(x, k, 0))
        conv = conv + shifted * cw_ref[back:back + 1, :]
    xa = _silu(conv)
    xs = xa[:, :D_SSM]
    bm = xa[:, D_SSM:D_SSM + N_SSM_GROUPS * D_STATE]
    cm = xa[:, D_SSM + N_SSM_GROUPS * D_STATE:]

    dt = _softplus(dt_ref[...] + dtb_ref[...])
    dta = dt * -jnp.exp(alog_ref[...])
    tok_l = lax.broadcasted_iota(jnp.int32, (n_rows, LANES), 0) % n_tok
    acum = dta
    to_end = jnp.where(tok_l + 1 < n_tok, pltpu.roll(dta, n_rows - 1, 0), 0.0)
    shift = 1
    while shift < n_tok:
        acum = acum + jnp.where(tok_l >= shift, pltpu.roll(acum, shift, 0), 0.0)
        to_end = to_end + jnp.where(tok_l + shift < n_tok, pltpu.roll(to_end, n_rows - shift, 0), 0.0)
        shift *= 2
    w_end = dt * jnp.exp(to_end)
    acum_t = acum.T
    dt_t = dt.T
    w_end_t = w_end.T

    rows = lax.broadcasted_iota(jnp.int32, (n_rows, n_rows), 0)
    cols = lax.broadcasted_iota(jnp.int32, (n_rows, n_rows), 1)
    causal = (rows // n_tok == cols // n_tok) & (rows >= cols)
    low_half = cols < SSM_HEAD_DIM
    b_g = [bm[:, g * D_STATE:(g + 1) * D_STATE].astype(BF16) for g in range(N_SSM_GROUPS)]
    c_g = [cm[:, g * D_STATE:(g + 1) * D_STATE].astype(BF16) for g in range(N_SSM_GROUPS)]
    cb = [_dot_nt(c_g[g], b_g[g]) for g in range(N_SSM_GROUPS)]

    for j in range(N_SSM_HEADS // 2):
        lo, hi = j * LANES, (j + 1) * LANES
        heads = (2 * j, 2 * j + 1)
        xs_p = xs[:, lo:hi]
        g_mats, e_cols, w_cols = [], [], []
        for h in heads:
            g = h // HEADS_PER_SSM_GROUP
            row_b = jnp.broadcast_to(acum_t[h:h + 1, :], (n_rows, n_rows))
            col_b = row_b.T
            decay = jnp.where(causal, jnp.exp(col_b - row_b), 0.0)
            g_mats.append((cb[g] * decay * dt_t[h:h + 1, :]).astype(BF16))
            e_col = jnp.exp(col_b)
            ecd_scr[h] = e_col
            e_cols.append(e_col)
            w_cols.append(jnp.broadcast_to(w_end_t[h:h + 1, :], (n_rows, n_rows)).T)
        y_diag = _dot(jnp.concatenate(g_mats, axis=1), _pair_blockdiag(xs_p, low_half))
        y_ref[:, lo:hi] = y_diag + dsk_ref[:, lo:hi] * xs_p
        eexp_scr[:, lo:hi] = jnp.where(low_half, e_cols[0], e_cols[1])
        xwt_scr[lo:hi, :] = (xs_p * jnp.where(low_half, w_cols[0], w_cols[1])).T

    col_group = lax.broadcasted_iota(jnp.int32, (n_tok, D_SSM), 1) // D_SSM_GROUP
    lane_seq = lax.broadcasted_iota(jnp.int32, (D_SSM_GROUP, n_rows), 1) // n_tok
    for b in range(n_seq):
        r0 = b * n_tok
        h0 = h0_ref[b]
        c_stack = jnp.concatenate([c_g[g][r0:r0 + n_tok, :] for g in range(N_SSM_GROUPS)], axis=0)
        y_all = _dot_nt(c_stack, h0.astype(BF16))
        y_off = jnp.zeros((n_tok, D_SSM), F32)
        for g in range(N_SSM_GROUPS):
            y_off = jnp.where(col_group == g, y_all[g * n_tok:(g + 1) * n_tok, :], y_off)
        yoff_scr[r0:r0 + n_tok, :] = y_off
        for g in range(N_SSM_GROUPS):
            c0 = g * D_SSM_GROUP
            lhs = jnp.where(lane_seq == b, xwt_scr[c0:c0 + D_SSM_GROUP, :], 0.0).astype(BF16)
            upd = _dot(lhs, b_g[g])
            for e in range(HEADS_PER_SSM_GROUP):
                h = g * HEADS_PER_SSM_GROUP + e
                p0 = h * SSM_HEAD_DIM
                decay = ecd_scr[h, r0 + n_tok - 1:r0 + n_tok, :]
                hout_ref[b, p0:p0 + SSM_HEAD_DIM, :] = (
                    h0[p0:p0 + SSM_HEAD_DIM, :] * decay + upd[e * SSM_HEAD_DIM:(e + 1) * SSM_HEAD_DIM, :])

    y = y_ref[...] + yoff_scr[...] * eexp_scr[...]
    y = y * _silu(z_ref[...])
    sq = y * y
    col_group = lax.broadcasted_iota(jnp.int32, (n_rows, D_SSM), 1) // D_SSM_GROUP
    scale = jnp.zeros((n_rows, D_SSM), F32)
    for g in range(N_SSM_GROUPS):
        ms = jnp.sum(jnp.where(col_group == g, sq, 0.0), axis=1, keepdims=True) / D_SSM_GROUP
        scale = jnp.where(col_group == g, lax.rsqrt(ms + RMS_EPS), scale)
    y_ref[...] = y * scale * nrm_ref[...]


def _ssd_sample(xbc, conv_state, z, dt, h0, cw, cb, dtb, alog, dsk, nrm, *, n_tok):
    t = xbc.shape[0]
    n_b = t // n_tok
    n_seq = LANES // n_tok
    rows = n_seq * n_tok
    row = lambda b: (b, 0)
    const = lambda b: (0, 0)
    vec = lambda n: pl.BlockSpec((1, n), const)
    return pl.pallas_call(
        functools.partial(_ssd_sample_body, n_tok=n_tok, n_seq=n_seq),
        grid=(n_b // n_seq,),
        in_specs=[pl.BlockSpec((rows, CONV_DIM), row),
                  pl.BlockSpec((n_seq, CONV_W - 1, CONV_DIM), lambda b: (b, 0, 0)),
                  pl.BlockSpec((rows, D_SSM), row), pl.BlockSpec((rows, DT_PAD), row),
                  pl.BlockSpec((n_seq, D_SSM, D_STATE), lambda b: (b, 0, 0)),
                  pl.BlockSpec((CONV_W, CONV_DIM), const),
                  vec(CONV_DIM), vec(DT_PAD), vec(DT_PAD), vec(D_SSM), vec(D_SSM)],
        out_specs=[pl.BlockSpec((rows, D_SSM), row),
                   pl.BlockSpec((n_seq, D_SSM, D_STATE), lambda b: (b, 0, 0))],
        out_shape=[jax.ShapeDtypeStruct((t, D_SSM), F32),
                   jax.ShapeDtypeStruct((n_b, D_SSM, D_STATE), F32)],
        scratch_shapes=[pltpu.VMEM((n_seq, n_tok, CONV_DIM), F32),
                        pltpu.VMEM((D_SSM, rows), F32),
                        pltpu.VMEM((N_SSM_HEADS, rows, LANES), F32),
                        pltpu.VMEM((rows, D_SSM), F32),
                        pltpu.VMEM((rows, D_SSM), F32)],
        compiler_params=_params(1),
        name="ssd_sample",
    )(xbc, conv_state, z, dt, h0, cw, cb, dtb, alog, dsk, nrm)


HEADS_PER_HALF = LANES // ATT_HEAD_DIM
N_HALVES = D_GROUP // LANES


ATT_UNITS = 4


def _attn_prompt_body(slope_ref, *refs, dil, group):
    n_in = 5 * N_HALVES
    in_refs, o_refs, l_refs = refs[:n_in], refs[n_in:n_in + N_HALVES], refs[n_in + N_HALVES:]
    i = pl.program_id(0)
    nb = ATT_BLOCK
    qi = lax.broadcasted_iota(jnp.int32, (nb, 2 * nb), 0)
    kj = lax.broadcasted_iota(jnp.int32, (nb, 2 * nb), 1)
    dist = nb + qi - kj
    band = (dist >= 0) & (dist <= nb)
    band_first = band & (kj >= jnp.where(i > 0, 0, nb))
    dist_f = (dist * dil).astype(F32)
    alibi = [-slope_ref[group * H_PER_DIL + h] * dist_f for h in range(H_PER_DIL)]
    bias_first = [jnp.where(band_first, a, -jnp.inf) for a in alibi]
    bias_inner = [jnp.where(band, a, -jnp.inf) for a in alibi] if dil == 1 else bias_first
    lane_head = lax.broadcasted_iota(jnp.int32, (nb, LANES), 1) // ATT_HEAD_DIM

    def unit_rows(u):
        if dil == 1:
            rows = pl.ds(u * nb, nb)
            rows_prev = pl.ds((ATT_UNITS - 1) * nb, nb) if u == 0 else pl.ds((u - 1) * nb, nb)
            return rows, rows_prev, (bias_first if u == 0 else bias_inner)
        rows = pl.ds(pl.program_id(1) * ATT_UNITS + u, nb, stride=dil)
        return rows, rows, bias_first

    def unit_scores(u):
        rows, rows_prev, bias = unit_rows(u)
        scores = []
        for half in range(N_HALVES):
            q_ref, kp_ref, kc_ref, vp_ref, vc_ref = in_refs[5 * half:5 * half + 5]
            if dil == 1 and u > 0:
                kp_ref, vp_ref = kc_ref, vc_ref
            q = q_ref[rows, :] * (ATT_HEAD_DIM ** -0.5)
            k = jnp.concatenate([kp_ref[rows_prev, :], kc_ref[rows, :]], axis=0).astype(BF16)
            v = jnp.concatenate([vp_ref[rows_prev, :], vc_ref[rows, :]], axis=0).astype(BF16)
            for hh in range(HEADS_PER_HALF):
                qh = jnp.where(lane_head == hh, q, 0.0).astype(BF16)
                scores.append((_dot_nt(qh, k) + bias[half * HEADS_PER_HALF + hh], v))
        return scores

    def unit_outputs(u, scores):
        rows = unit_rows(u)[0]
        for half in range(N_HALVES):
            out = jnp.zeros((nb, LANES), F32)
            lse = jnp.zeros((nb, LANES), F32)
            for hh in range(HEADS_PER_HALF):
                s, v = scores[half * HEADS_PER_HALF + hh]
                mine = lane_head == hh
                m = jnp.max(s, axis=1, keepdims=True)
                p = jnp.exp(s - m)
                l = jnp.sum(p, axis=1, keepdims=True)
                pv = _dot(p.astype(BF16), v)
                out = jnp.where(mine, pv / l, out)
                lse = jnp.where(mine, m + jnp.log(l), lse)
            o_refs[half][rows, :] = out
            l_refs[half][rows, :] = lse

    pending = unit_scores(0)
    for u in range(ATT_UNITS):
        current, pending = pending, (unit_scores(u + 1) if u + 1 < ATT_UNITS else None)
        unit_outputs(u, current)


def _attn_prompt(slopes, qkv, *, group, dil, seq):
    assert dil == 1 or dil % ATT_UNITS == 0
    blk_rows = max(dil, ATT_UNITS) * ATT_BLOCK
    n_blk = seq // blk_rows
    n_res_steps = max(dil // ATT_UNITS, 1)
    blk = (blk_rows, LANES)
    cur = lambda c: pl.BlockSpec((None,) + blk, lambda i, r: (c, i, 0))
    prev = lambda c: pl.BlockSpec((None,) + blk, lambda i, r: (c, jnp.maximum(i - 1, 0), 0))
    in_specs = [pl.BlockSpec(memory_space=pltpu.SMEM)]
    for half in range(N_HALVES):
        q_col, k_col, v_col = (N_HALVES * (part * N_DIL + group) + half for part in range(3))
        in_specs += [cur(q_col), prev(k_col), cur(k_col), prev(v_col), cur(v_col)]
    out_blk = pl.BlockSpec(blk, lambda i, r: (i, 0))
    res = pl.pallas_call(
        functools.partial(_attn_prompt_body, dil=dil, group=group),
        grid=(n_blk, n_res_steps),
        in_specs=in_specs,
        out_specs=[out_blk] * (2 * N_HALVES),
        out_shape=[jax.ShapeDtypeStruct((seq, LANES), F32)] * (2 * N_HALVES),
        compiler_params=_params(2),
        name=f"attn_prompt_g{group}",
    )(slopes, *([qkv] * (5 * N_HALVES)))
    return res[:N_HALVES], res[N_HALVES:]


def _window_buffers(caches):
    out = []
    for (win, dil), pair in zip(DILATION_PAIRS, caches):
        assert dil & (dil - 1) == 0, "dilations must be powers of two"
        for cache in pair:
            assert cache.shape[1] == win, "window buffers must hold a full window"
            out.append(jnp.transpose(cache, (0, 2, 3, 1)).reshape(cache.shape[0], D_GROUP, win))
    return out


def _attn_sample_scores(slope_ref, qkv, k_bufs, n_tok):
    n_rows = H_PER_DIL * n_tok
    row2 = lax.broadcasted_iota(jnp.int32, (n_rows, D_GROUP), 0)
    lane2 = lax.broadcasted_iota(jnp.int32, (n_rows, D_GROUP), 1)
    own_head = (lane2 // ATT_HEAD_DIM) == (row2 // n_tok)
    row1 = lax.broadcasted_iota(jnp.int32, (n_rows, 1), 0)
    tok1 = row1 % n_tok
    head1 = row1 // n_tok
    scale = ATT_HEAD_DIM ** -0.5
    state = []
    for g, (win, dil) in enumerate(DILATION_PAIRS):
        q = qkv[:, g * D_GROUP:(g + 1) * D_GROUP]
        k_new = qkv[:, D_ATT + g * D_GROUP:D_ATT + (g + 1) * D_GROUP]
        v_new = qkv[:, 2 * D_ATT + g * D_GROUP:2 * D_ATT + (g + 1) * D_GROUP]
        q_rows = jnp.where(own_head, jnp.concatenate([q] * H_PER_DIL, axis=0), 0.0)
        slope_1 = jnp.zeros((n_rows, 1), F32)
        for h in range(H_PER_DIL):
            slope_1 = jnp.where(head1 == h, slope_ref[g * H_PER_DIL + h], slope_1)

        s_buf = _dot(q_rows.astype(BF16), k_bufs[g].astype(BF16)) * scale
        pos = lax.broadcasted_iota(jnp.int32, (n_rows, win), 1)
        tok_s = lax.broadcasted_iota(jnp.int32, (n_rows, win), 0) % n_tok
        dist = win + tok_s - pos
        on_grid = (pos >= tok_s) & (jnp.bitwise_and(dist, dil - 1) == 0)
        s_buf = jnp.where(on_grid, s_buf - slope_1 * dist.astype(F32), -jnp.inf)
        m = jnp.max(s_buf, axis=1, keepdims=True)
        s_new = []
        for tp in range(n_tok):
            sn = jnp.sum(q_rows * k_new[tp:tp + 1, :], axis=1, keepdims=True) * scale
            ok = (tok1 >= tp) & (jnp.bitwise_and(tok1 - tp, dil - 1) == 0)
            sn = jnp.where(ok, sn - slope_1 * (tok1 - tp).astype(F32), -jnp.inf)
            s_new.append(sn)
            m = jnp.maximum(m, sn)
        state.append((s_buf, s_new, m, v_new))
    return state


def _attn_sample_outputs(state, v_bufs, att_ref, n_tok):
    lane_head8 = lax.broadcasted_iota(jnp.int32, (n_tok, D_GROUP), 1) // ATT_HEAD_DIM
    outs, lses = [], []
    for g, (s_buf, s_new, m, v_new) in enumerate(state):
        p_buf = jnp.exp(s_buf - m)
        l = jnp.sum(p_buf, axis=1, keepdims=True)
        o_rows = _dot_nt(p_buf.astype(BF16), v_bufs[g].astype(BF16))
        for tp in range(n_tok):
            p_new = jnp.exp(s_new[tp] - m)
            l = l + p_new
            o_rows = o_rows + p_new * v_new[tp:tp + 1, :]
        o_rows = o_rows / l
        lse_rows = m + jnp.log(l)
        o = jnp.zeros((n_tok, D_GROUP), F32)
        lse = jnp.zeros((n_tok, D_GROUP), F32)
        for h in range(H_PER_DIL):
            o = jnp.where(lane_head8 == h, o_rows[h * n_tok:(h + 1) * n_tok, :], o)
            lse = jnp.where(lane_head8 == h, lse_rows[h * n_tok:(h + 1) * n_tok, :], lse)
        outs.append(o)
        lses.append(lse)

    m = jnp.maximum(jnp.maximum(lses[0], lses[1]), lses[2])
    e = [jnp.exp(x - m) for x in lses]
    den = e[0] + e[1] + e[2]
    for g in range(N_DIL):
        att_ref[:, g * D_GROUP:(g + 1) * D_GROUP] = outs[g] * (e[g] / den)


def _outproj_body(*refs, combine):
    n_pieces = N_DIL * N_HALVES
    if combine:
        x_ref, ssm_ref = refs[:2]
        o_refs, l_refs = refs[2:2 + n_pieces], refs[2 + n_pieces:2 + 2 * n_pieces]
        w_ref, out_ref = refs[2 + 2 * n_pieces:]
        att = [None] * n_pieces
        for half in range(N_HALVES):
            idx = [g * N_HALVES + half for g in range(N_DIL)]
            lses = [l_refs[p][...] for p in idx]
            m = jnp.maximum(jnp.maximum(lses[0], lses[1]), lses[2])
            e = [jnp.exp(x - m) for x in lses]
            den = e[0] + e[1] + e[2]
            for p, w in zip(idx, e):
                att[p] = o_refs[p][...] * (w / den)
    else:
        x_ref, ssm_ref, att_ref, w_ref, out_ref = refs
        att = [att_ref[:, p * LANES:(p + 1) * LANES] for p in range(n_pieces)]
    mixed = jnp.concatenate([ssm_ref[...]] + att, axis=1).astype(BF16)
    out_ref[...] = x_ref[...] + _dot(mixed, w_ref[...].astype(BF16))


def _outproj(x, ssm, att, w_out, *, tm=512):
    t = x.shape[0]
    row = lambda i: (i, 0)
    combine = isinstance(att, (tuple, list))
    att_args = list(att[0]) + list(att[1]) if combine else [att]
    att_specs = [pl.BlockSpec((tm, a.shape[1]), row) for a in att_args]
    return pl.pallas_call(
        functools.partial(_outproj_body, combine=combine),
        grid=(t // tm,),
        in_specs=[pl.BlockSpec((tm, D_MODEL), row), pl.BlockSpec((tm, D_SSM), row)] + att_specs
                 + [pl.BlockSpec((D_MODEL, D_MODEL), lambda i: (0, 0), pipeline_mode=pl.Buffered(1))],
        out_specs=pl.BlockSpec((tm, D_MODEL), row),
        out_shape=jax.ShapeDtypeStruct((t, D_MODEL), F32),
        compiler_params=_params(1),
        name="outproj_prompt" if combine else "outproj_sample",
    )(x, ssm, *att_args, w_out)


def kernel(x_prompt, x_sample, cache_k0, cache_v0, cache_k1, cache_v1, cache_k2, cache_v2, state_ssm, state_conv, ffn1_norm, ffn1_w_gate, ffn1_w_up, ffn1_w_down, mix_norm, w_in, conv_w, conv_b, dt_bias, a_log, d_skip, ssm_norm, w_out, ffn2_norm, ffn2_w_gate, ffn2_w_up, ffn2_w_down, final_norm):
    assert w_in.shape[0] == 1, "single layer"
    batch, seq, _ = x_prompt.shape
    dec_batch, dec_seq, _ = x_sample.shape
    assert batch == 1 and dec_seq == SUBLANES

    row = lambda v: v.reshape(1, -1).astype(F32)
    pad_heads = lambda v: jnp.pad(v.reshape(1, -1).astype(F32), ((0, 0), (0, DT_PAD - N_SSM_HEADS)))
    w_in_t = jnp.transpose(w_in[0]).astype(BF16)
    ffn1 = (row(ffn1_norm[0]), ffn1_w_gate[0], ffn1_w_up[0], ffn1_w_down[0])
    ffn2 = (row(ffn2_norm[0]), ffn2_w_gate[0], ffn2_w_up[0], ffn2_w_down[0])
    w_o = w_out[0]
    ssd_params = (conv_w[0].astype(F32), row(conv_b[0]), pad_heads(dt_bias[0]), pad_heads(a_log[0]),
                  row(jnp.repeat(d_skip[0], SSM_HEAD_DIM)), row(ssm_norm[0]))
    slopes = jnp.exp2(-ALIBI_MAX_EXP * jnp.arange(1, N_ATT_HEADS + 1, dtype=F32) / N_ATT_HEADS)

    def back(x2):
        return _ffn(x2, *ffn2, row(final_norm))

    n_s = dec_batch * dec_seq
    x1_s = _ffn(x_sample.reshape(n_s, D_MODEL), *ffn1)
    z, xbc_s, qkv_s, dt = _inproj(x1_s, row(mix_norm[0]), w_in_t)
    ssm_s, h_s = _ssd_sample(xbc_s, state_conv[0], z, dt, state_ssm[0].reshape(dec_batch, D_SSM, D_STATE),
                             *ssd_params, n_tok=dec_seq)
    caches = ((cache_k0[0], cache_v0[0]), (cache_k1[0], cache_v1[0]), (cache_k2[0], cache_v2[0]))

    x1, att_s = _ffn(x_prompt.reshape(seq, D_MODEL), *ffn1,
                     host=(slopes, qkv_s, _window_buffers(caches), dec_seq))
    xbc_p, qkv_p, ssm_p, hfin_p = _inproj_ssd(x1, row(mix_norm[0]), w_in_t, *ssd_params)
    outs, lses = [], []
    for g, (win, dil) in enumerate(DILATION_PAIRS):
        o, lse = _attn_prompt(slopes, qkv_p, group=g, dil=dil, seq=seq)
        outs += o
        lses += lse
    y_prompt = back(_outproj(x1, ssm_p, (outs, lses), w_o)).reshape(1, seq, D_MODEL)

    y_sample = back(_outproj(x1_s, ssm_s, att_s, w_o)).reshape(dec_batch, dec_seq, D_MODEL)

    head_shape = (H_PER_DIL, ATT_HEAD_DIM)
    p_kv, s_kv = [], []
    for g, (win, dil) in enumerate(DILATION_PAIRS):
        keep = min(win, seq)
        for base in (D_ATT, 2 * D_ATT):
            c0 = base + g * D_GROUP
            blocks = [qkv_p[c0 // LANES + half, seq - keep:] for half in range(N_HALVES)]
            p_kv.append(jnp.concatenate(blocks, axis=1).reshape((1, 1, keep) + head_shape))
            s_kv.append(qkv_s[:, c0:c0 + D_GROUP].reshape((1, dec_batch, dec_seq) + head_shape))
    p_ssm = hfin_p.reshape(1, 1, N_SSM_HEADS, SSM_HEAD_DIM, D_STATE)
    p_conv = xbc_p[seq - (CONV_W - 1):].reshape(1, 1, CONV_W - 1, CONV_DIM)
    s_ssm = h_s.reshape(1, dec_batch, N_SSM_HEADS, SSM_HEAD_DIM, D_STATE)
    s_conv = xbc_s.reshape(dec_batch, dec_seq, CONV_DIM)[:, dec_seq - (CONV_W - 1):].reshape(
        1, dec_batch, CONV_W - 1, CONV_DIM)
    return (y_prompt, y_sample, *p_kv, p_ssm, p_conv, *s_kv, s_ssm, s_conv)
```

```python
import functools

import jax
import jax.numpy as jnp
from jax import lax
from jax.experimental import pallas as pl
from jax.experimental.pallas import tpu as pltpu

F32 = jnp.float32
BF16 = jnp.bfloat16

D_MODEL = 2048
D_FF = 5632
D_ATT = 768
ATT_HEAD_DIM = 64
H_PER_DIL = 4
D_GROUP = H_PER_DIL * ATT_HEAD_DIM
DILATION_PAIRS = ((128, 1), (512, 4), (2048, 16))
N_DIL = len(DILATION_PAIRS)
ATT_BLOCK = 128
D_SSM = 1280
SSM_HEAD_DIM = 64
N_SSM_HEADS = 20
N_SSM_GROUPS = 4
HEADS_PER_SSM_GROUP = 5
D_SSM_GROUP = D_SSM // N_SSM_GROUPS
D_STATE = 128
CONV_W = 4
CONV_DIM = 2304
SSD_CHUNK = 128
RMS_EPS = 1e-6
ALIBI_MAX_EXP = 8.0
N_ATT_HEADS = 12

LANES = 128
SUBLANES = 8
DT_PAD = LANES
IN_HEAD = D_SSM + CONV_DIM
IN_TAIL = N_SSM_HEADS + 3 * D_ATT
N_QKV_BLOCKS = 3 * D_ATT // LANES

VMEM_LIMIT = 60 * 1024 * 1024


def _params(n_grid_dims, vmem=VMEM_LIMIT):
    return pltpu.CompilerParams(dimension_semantics=("arbitrary",) * n_grid_dims,
                                vmem_limit_bytes=vmem)


def _silu(x):
    return x * (0.5 + 0.5 * jnp.tanh(0.5 * x))


def _softplus(x):
    return jnp.maximum(x, 0.0) + jnp.log(1.0 + jnp.exp(-jnp.abs(x)))


def _rms_scale(x):
    return x * lax.rsqrt(jnp.mean(x * x, axis=-1, keepdims=True) + RMS_EPS)


def _dot(a, b):
    return jnp.dot(a, b, preferred_element_type=F32)


def _dot_nt(a, b):
    return lax.dot_general(a, b, (((1,), (1,)), ((), ())), preferred_element_type=F32)


def _cumsum_rows(x):
    n = x.shape[0]
    rows = lax.broadcasted_iota(jnp.int32, x.shape, 0)
    shift = 1
    while shift < n:
        x = x + jnp.where(rows >= shift, pltpu.roll(x, shift, 0), 0.0)
        shift *= 2
    return x


FFN_ROW_CHUNKS = 4


def _ffn_body(*refs, n_f, final, n_guest):
    hosted = n_guest > 0
    refs = list(refs)
    x_ref, g_ref, wg_ref, wu_ref, wd_ref = refs[:5]
    del refs[:5]
    fg_ref = refs.pop(0) if final else None
    if hosted:
        slope_ref, qkv_ref, k0_ref, v0_ref, k1_ref, v1_ref, k2_ref, v2_ref = refs[:8]
        del refs[:8]
        o_ref, att_ref, h_ref = refs
    else:
        o_ref, h_ref = refs
    f = pl.program_id(1)
    tm = x_ref.shape[0]
    assert n_f >= 2 and tm % FFN_ROW_CHUNKS == 0

    def guest_scores():
        if hosted:
            k_bufs = [k0_ref[0], k1_ref[0], k2_ref[0]]
            return _attn_sample_scores(slope_ref, qkv_ref[...], k_bufs, qkv_ref.shape[0])

    def guest_outputs(state):
        if hosted:
            v_bufs = [v0_ref[0], v1_ref[0], v2_ref[0]]
            _attn_sample_outputs(state, v_bufs, att_ref, qkv_ref.shape[0])

    def weights():
        return wg_ref[...].astype(BF16), wu_ref[...].astype(BF16), wd_ref[...].astype(BF16)

    def gate_up(h, w, with_guest=False):
        gate = _dot(h, w[0])
        state = guest_scores() if with_guest else None
        up = _dot(h, w[1])
        if with_guest:
            guest_outputs(state)
        return gate, up

    def down(gate_and_up, w):
        gate, up = gate_and_up
        return _dot((_silu(gate) * up).astype(BF16), w[2])

    def partial_out(h, w, with_guest=False):
        return down(gate_up(h, w, with_guest), w)

    chunks = [slice(c * tm // FFN_ROW_CHUNKS, (c + 1) * tm // FFN_ROW_CHUNKS) for c in range(FFN_ROW_CHUNKS)]

    def chunked(h_of, finish):
        w = weights()
        pending = gate_up(h_of(chunks[0]), w)
        for c, rows in enumerate(chunks):
            current = pending
            if c + 1 < len(chunks):
                pending = gate_up(h_of(chunks[c + 1]), w, with_guest=(c == 0))
            finish(rows, down(current, w))

    @pl.when(f == 0)
    def _():
        def h_of(rows):
            h = (_rms_scale(x_ref[rows, :]) * g_ref[...]).astype(BF16)
            h_ref[rows, :] = h
            return h

        def finish(rows, part):
            o_ref[rows, :] = part

        chunked(h_of, finish)

    middle = (f > 0) & (f < n_f - 1)
    guest_left = pl.program_id(0) * n_f + f < n_guest

    if hosted:
        @pl.when(middle & guest_left)
        def _():
            o_ref[...] += partial_out(h_ref[...], weights(), with_guest=True)

    @pl.when(middle & jnp.logical_not(guest_left))
    def _():
        o_ref[...] += partial_out(h_ref[...], weights())

    @pl.when(f == n_f - 1)
    def _():
        def finish(rows, part):
            y = x_ref[rows, :] + 0.5 * (o_ref[rows, :] + part)
            if final:
                y = _rms_scale(y) * fg_ref[...]
            o_ref[rows, :] = y

        chunked(lambda rows: h_ref[rows, :], finish)


def _ffn(x, gain, wg, wu, wd, final_gain=None, *, host=None, tm=1024, tf=256):
    t = x.shape[0]
    n_f = D_FF // tf
    final = final_gain is not None
    hosted = host is not None
    row = lambda i, f: (i, 0)
    const = lambda i, f: (0, 0)
    tile_mode = dict(pipeline_mode=pl.Buffered(1)) if hosted else {}
    n_guest = 0
    in_specs = [pl.BlockSpec((tm, D_MODEL), row), pl.BlockSpec((1, D_MODEL), const),
                pl.BlockSpec((D_MODEL, tf), lambda i, f: (0, f)),
                pl.BlockSpec((D_MODEL, tf), lambda i, f: (0, f)),
                pl.BlockSpec((tf, D_MODEL), lambda i, f: (f, 0))]
    args = [x, gain, wg, wu, wd]
    if final:
        in_specs.append(pl.BlockSpec((1, D_MODEL), const))
        args.append(final_gain)
    out_specs = pl.BlockSpec((tm, D_MODEL), row, **tile_mode)
    out_shape = jax.ShapeDtypeStruct((t, D_MODEL), F32)
    if hosted:
        slopes, qkv, caches, n_tok = host
        n_guest = n_b = qkv.shape[0] // n_tok
        assert n_b <= (t // tm) * n_f, "one sequence per grid step"
        seq = lambda i, f: jnp.minimum(i * n_f + f, n_b - 1)
        in_specs += [pl.BlockSpec(memory_space=pltpu.SMEM),
                     pl.BlockSpec((n_tok, 3 * D_ATT), lambda i, f: (seq(i, f), 0))]
        in_specs += [pl.BlockSpec((1,) + c.shape[1:], lambda i, f: (seq(i, f), 0, 0)) for c in caches]
        args += [slopes, qkv, *caches]
        out_specs = [out_specs, pl.BlockSpec((n_tok, D_ATT), lambda i, f: (seq(i, f), 0))]
        out_shape = [out_shape, jax.ShapeDtypeStruct((qkv.shape[0], D_ATT), F32)]
    return pl.pallas_call(
        functools.partial(_ffn_body, n_f=n_f, final=final, n_guest=n_guest),
        grid=(t // tm, n_f),
        in_specs=in_specs,
        out_specs=out_specs,
        out_shape=out_shape,
        scratch_shapes=[pltpu.VMEM((tm, D_MODEL), BF16)],
        compiler_params=_params(2),
        name=("ffn_final" if final else "ffn") + ("_host" if hosted else ""),
    )(*args)


def _inproj_body(x_ref, g_ref, w_ref, z_ref, xbc_ref, qkv_ref, dt_ref):
    h = (_rms_scale(x_ref[...]) * g_ref[...]).astype(BF16)
    z_ref[...] = _dot_nt(h, w_ref[0:D_SSM, :])
    xbc_ref[...] = _dot_nt(h, w_ref[D_SSM:IN_HEAD, :])
    tail = _dot_nt(h, w_ref[IN_HEAD:IN_HEAD + IN_TAIL, :])
    lane = lax.broadcasted_iota(jnp.int32, (tail.shape[0], DT_PAD), 1)
    dt_ref[...] = jnp.where(lane < N_SSM_HEADS, tail[:, 0:DT_PAD], 0.0)
    qkv_ref[...] = tail[:, N_SSM_HEADS:IN_TAIL]


def _inproj(x, gain, w_t, *, tm=256):
    t = x.shape[0]
    row = lambda i: (i, 0)
    const = lambda i: (0, 0)
    widths = (D_SSM, CONV_DIM, 3 * D_ATT, DT_PAD)
    return pl.pallas_call(
        _inproj_body,
        grid=(t // tm,),
        in_specs=[pl.BlockSpec((tm, D_MODEL), row), pl.BlockSpec((1, D_MODEL), const),
                  pl.BlockSpec((IN_HEAD + IN_TAIL, D_MODEL), const, pipeline_mode=pl.Buffered(1))],
        out_specs=[pl.BlockSpec((tm, w), row) for w in widths],
        out_shape=[jax.ShapeDtypeStruct((t, w), F32) for w in widths],
        compiler_params=_params(1),
        name="inproj",
    )(x, gain, w_t)


def _pair_blockdiag(x, low_half):
    zero = jnp.zeros_like(x)
    return jnp.concatenate([jnp.where(low_half, x, zero), jnp.where(low_half, zero, x)],
                           axis=0).astype(BF16)


def _ssd_chunk_phases(xbc_ref, z_ref, dt_ref, cw_ref, cb_ref, dtb_ref, alog_ref, dsk_ref, nrm_ref,
                      y_ref, xf_scr, ht_scr):
    cl = SSD_CHUNK
    st = {}

    def prologue():
        x = xbc_ref[...]
        xf_scr[SUBLANES:SUBLANES + cl, :] = x
        conv = cb_ref[...]
        for j in range(CONV_W - 1):
            conv = conv + xf_scr[SUBLANES - 3 + j:SUBLANES - 3 + j + cl, :] * cw_ref[j:j + 1, :]
        conv = conv + x * cw_ref[CONV_W - 1:CONV_W, :]
        xf_scr[0:SUBLANES, :] = x[cl - SUBLANES:cl, :]
        xa = _silu(conv)
        st["xs"] = xa[:, :D_SSM]
        bm = xa[:, D_SSM:D_SSM + N_SSM_GROUPS * D_STATE]
        cm = xa[:, D_SSM + N_SSM_GROUPS * D_STATE:]

        dt = _softplus(dt_ref[...] + dtb_ref[...])
        a = -jnp.exp(alog_ref[...])
        acum = _cumsum_rows(dt * a)
        w_end = dt * jnp.exp(acum[cl - 1:cl, :] - acum)
        st["acum_t"] = acum.T
        st["dt_t"] = dt.T
        st["w_end_t"] = w_end.T

        rows = lax.broadcasted_iota(jnp.int32, (cl, cl), 0)
        cols = lax.broadcasted_iota(jnp.int32, (cl, cl), 1)
        st["causal"] = rows >= cols
        st["low_half"] = cols < SSM_HEAD_DIM

        b_g = [bm[:, g * D_STATE:(g + 1) * D_STATE] for g in range(N_SSM_GROUPS)]
        st["c_g"] = [cm[:, g * D_STATE:(g + 1) * D_STATE].astype(BF16) for g in range(N_SSM_GROUPS)]
        st["b_t"] = [b.T for b in b_g]
        st["cb"] = [_dot_nt(st["c_g"][g], b_g[g].astype(BF16)) for g in range(N_SSM_GROUPS)]
        st["sumsq"] = [jnp.zeros((cl, 1), F32) for _ in range(N_SSM_GROUPS)]

    def pair(j):
        lo, hi = j * LANES, (j + 1) * LANES
        heads = (2 * j, 2 * j + 1)
        groups = tuple(h // HEADS_PER_SSM_GROUP for h in heads)
        low_half, c_g = st["low_half"], st["c_g"]
        xs_p = st["xs"][:, lo:hi]
        x_bd = _pair_blockdiag(xs_p, low_half)
        ht_p = ht_scr[:, lo:hi]
        h_bd = _pair_blockdiag(ht_p, low_half)
        g_mats, w_mats, e_cols = [], [], []
        for h, g in zip(heads, groups):
            row_b = jnp.broadcast_to(st["acum_t"][h:h + 1, :], (cl, cl))
            col_b = row_b.T
            decay = jnp.where(st["causal"], jnp.exp(col_b - row_b), 0.0)
            g_mats.append((st["cb"][g] * decay * st["dt_t"][h:h + 1, :]).astype(BF16))
            w_mats.append((st["b_t"][g] * st["w_end_t"][h:h + 1, :]).astype(BF16))
            e_cols.append(jnp.exp(col_b))
        y_diag = _dot(jnp.concatenate(g_mats, axis=1), x_bd)
        y_off = _dot(jnp.concatenate([c_g[groups[0]], c_g[groups[1]]], axis=1), h_bd)
        y_off = y_off * jnp.where(low_half, e_cols[0], e_cols[1])
        state = _dot(jnp.concatenate(w_mats, axis=1), x_bd)
        chunk_decay = jnp.where(low_half[0:1, :], e_cols[0][cl - 1:cl, :], e_cols[1][cl - 1:cl, :])
        ht_scr[:, lo:hi] = ht_p * chunk_decay + state
        y_p = y_diag + y_off + dsk_ref[:, lo:hi] * xs_p
        y_p = y_p * _silu(z_ref[:, lo:hi])
        y_ref[:, lo:hi] = y_p
        sq = y_p * y_p
        sumsq = st["sumsq"]
        sumsq[groups[0]] = sumsq[groups[0]] + jnp.sum(jnp.where(low_half, sq, 0.0), axis=1, keepdims=True)
        sumsq[groups[1]] = sumsq[groups[1]] + jnp.sum(jnp.where(low_half, 0.0, sq), axis=1, keepdims=True)

    def epilogue():
        scale = [lax.rsqrt(s / D_SSM_GROUP + RMS_EPS) for s in st["sumsq"]]
        for j in range(N_SSM_HEADS // 2):
            lo, hi = j * LANES, (j + 1) * LANES
            g0, g1 = (2 * j) // HEADS_PER_SSM_GROUP, (2 * j + 1) // HEADS_PER_SSM_GROUP
            y_ref[:, lo:hi] = (y_ref[:, lo:hi] * jnp.where(st["low_half"], scale[g0], scale[g1])
                               * nrm_ref[:, lo:hi])

    return [prologue] + [functools.partial(pair, j) for j in range(N_SSM_HEADS // 2)] + [epilogue]


PROJ_TILE = 256


def _inproj_ssd_body(x_ref, g_ref, w_ref, cw_ref, cb_ref, dtb_ref, alog_ref, dsk_ref, nrm_ref,
                     xbc_ref, qkv_ref, y_ref, hfin_ref,
                     xbc_a, z_a, dt_a, xbc_b, z_b, dt_b, xf_scr, ht_scr, *, n_blocks):
    s = pl.program_id(0)
    tm = x_ref.shape[0]
    bufs = ((xbc_a, z_a, dt_a), (xbc_b, z_b, dt_b))

    @pl.when(s <= 1)
    def _():
        xf_scr[0:SUBLANES, :] = jnp.zeros((SUBLANES, CONV_DIM), F32)
        ht_scr[...] = jnp.zeros_like(ht_scr)

    @pl.when(s == 0)
    def _():
        for ref in bufs[1]:
            ref[...] = jnp.zeros_like(ref)

    def step(parity):
        xbc_w, z_w, dt_w = bufs[parity]
        xbc_r, z_r, dt_r = bufs[1 - parity]
        h = (_rms_scale(x_ref[...]) * g_ref[...]).astype(BF16)

        def z_tile(c0, c1):
            z_w[:, c0:c1] = _dot_nt(h, w_ref[c0:c1, :])

        def xbc_tile(c0, c1):
            xbc = _dot_nt(h, w_ref[D_SSM + c0:D_SSM + c1, :])
            xbc_w[:, c0:c1] = xbc
            xbc_ref[:, c0:c1] = xbc

        tail_parts = []

        def tail_tile(c0, c1):
            tail_parts.append(_dot_nt(h, w_ref[IN_HEAD + c0:IN_HEAD + c1, :]))
            if c1 == IN_TAIL:
                tail = jnp.concatenate(tail_parts, axis=1)
                lane = lax.broadcasted_iota(jnp.int32, (tm, DT_PAD), 1)
                dt_w[...] = jnp.where(lane < N_SSM_HEADS, tail[:, 0:DT_PAD], 0.0)
                qkv = tail[:, N_SSM_HEADS:IN_TAIL]
                for j in range(N_QKV_BLOCKS):
                    qkv_ref[j] = qkv[:, j * LANES:(j + 1) * LANES]

        tiles = [functools.partial(z_tile, c, c + PROJ_TILE) for c in range(0, D_SSM, PROJ_TILE)]
        tiles += [functools.partial(xbc_tile, c, c + PROJ_TILE) for c in range(0, CONV_DIM, PROJ_TILE)]
        n_tail = IN_TAIL // PROJ_TILE
        tail_cuts = [c * PROJ_TILE for c in range(n_tail)] + [IN_TAIL]
        tiles += [functools.partial(tail_tile, c0, c1) for c0, c1 in zip(tail_cuts[:-1], tail_cuts[1:])]
        phases = []
        for c in range(tm // SSD_CHUNK):
            rows = slice(c * SSD_CHUNK, (c + 1) * SSD_CHUNK)
            phases += _ssd_chunk_phases(xbc_r.at[rows, :], z_r.at[rows, :], dt_r.at[rows, :], cw_ref,
                                        cb_ref, dtb_ref, alog_ref, dsk_ref, nrm_ref, y_ref.at[rows, :],
                                        xf_scr, ht_scr)
        done = 0
        for i, tile in enumerate(tiles):
            tile()
            upto = (i + 1) * len(phases) // len(tiles)
            for phase in phases[done:upto]:
                phase()
            done = upto

    @pl.when(s % 2 == 0)
    def _():
        step(0)

    @pl.when(s % 2 == 1)
    def _():
        step(1)

    @pl.when(s == n_blocks)
    def _():
        for j in range(D_SSM // LANES):
            hfin_ref[j * LANES:(j + 1) * LANES, :] = ht_scr[:, j * LANES:(j + 1) * LANES].T


def _inproj_ssd(x, gain, w_t, cw, cb, dtb, alog, dsk, nrm, *, tm=256):
    t = x.shape[0]
    n_blocks = t // tm
    cur = lambda s: (jnp.minimum(s, n_blocks - 1), 0)
    prev = lambda s: (jnp.maximum(s - 1, 0), 0)
    const = lambda s: (0, 0)
    vec = lambda n: pl.BlockSpec((1, n), const)
    buf = lambda n: pltpu.VMEM((tm, n), F32)
    return pl.pallas_call(
        functools.partial(_inproj_ssd_body, n_blocks=n_blocks),
        grid=(n_blocks + 1,),
        in_specs=[pl.BlockSpec((tm, D_MODEL), cur), vec(D_MODEL),
                  pl.BlockSpec((IN_HEAD + IN_TAIL, D_MODEL), const, pipeline_mode=pl.Buffered(1)),
                  pl.BlockSpec((CONV_W, CONV_DIM), const),
                  vec(CONV_DIM), vec(DT_PAD), vec(DT_PAD), vec(D_SSM), vec(D_SSM)],
        out_specs=[pl.BlockSpec((tm, CONV_DIM), cur),
                   pl.BlockSpec((N_QKV_BLOCKS, tm, LANES), lambda s: (0, jnp.minimum(s, n_blocks - 1), 0)),
                   pl.BlockSpec((tm, D_SSM), prev), pl.BlockSpec((D_SSM, D_STATE), const)],
        out_shape=[jax.ShapeDtypeStruct((t, CONV_DIM), F32),
                   jax.ShapeDtypeStruct((N_QKV_BLOCKS, t, LANES), F32),
                   jax.ShapeDtypeStruct((t, D_SSM), F32), jax.ShapeDtypeStruct((D_SSM, D_STATE), F32)],
        scratch_shapes=[buf(CONV_DIM), buf(D_SSM), buf(DT_PAD), buf(CONV_DIM), buf(D_SSM), buf(DT_PAD),
                        pltpu.VMEM((SUBLANES + SSD_CHUNK, CONV_DIM), F32),
                        pltpu.VMEM((D_STATE, D_SSM), F32)],
        compiler_params=_params(1),
        name="inproj_ssd",
    )(x, gain, w_t, cw, cb, dtb, alog, dsk, nrm)


def _ssd_sample_body(xbc_ref, cs_ref, z_ref, dt_ref, h0_ref, cw_ref, cb_ref, dtb_ref, alog_ref,
                     dsk_ref, nrm_ref, y_ref, hout_ref, hist_scr, xwt_scr, ecd_scr, eexp_scr, yoff_scr,
                     *, n_tok, n_seq):
    n_rows = n_seq * n_tok
    assert n_rows == LANES

    @pl.when(pl.program_id(0) == 0)
    def _():
        hist_scr[...] = jnp.zeros_like(hist_scr)

    x = xbc_ref[...]
    hist_scr[:, 0:CONV_W - 1, :] = cs_ref[...]
    hist = hist_scr[...].reshape(n_rows, CONV_DIM)
    tok = lax.broadcasted_iota(jnp.int32, (n_rows, 1), 0) % n_tok
    conv = cb_ref[...] + x * cw_ref[CONV_W - 1:CONV_W, :]
    for k in range(1, CONV_W):
        back = CONV_W - 1 - k
        from_hist = hist if back == 0 else pltpu.roll(hist, n_rows - back, 0)
        shifted = jnp.where(tok < k, from_hist, pltpu.roll(x, k, 0))
        conv = conv + shifted * cw_ref[back:back + 1, :]
    xa = _silu(conv)
    xs = xa[:, :D_SSM]
    bm = xa[:, D_SSM:D_SSM + N_SSM_GROUPS * D_STATE]
    cm = xa[:, D_SSM + N_SSM_GROUPS * D_STATE:]

    dt = _softplus(dt_ref[...] + dtb_ref[...])
    dta = dt * -jnp.exp(alog_ref[...])
    tok_l = lax.broadcasted_iota(jnp.int32, (n_rows, LANES), 0) % n_tok
    acum = dta
    to_end = jnp.where(tok_l + 1 < n_tok, pltpu.roll(dta, n_rows - 1, 0), 0.0)
    shift = 1
    while shift < n_tok:
        acum = acum + jnp.where(tok_l >= shift, pltpu.roll(acum, shift, 0), 0.0)
        to_end = to_end + jnp.where(tok_l + shift < n_tok, pltpu.roll(to_end, n_rows - shift, 0), 0.0)
        shift *= 2
    w_end = dt * jnp.exp(to_end)
    acum_t = acum.T
    dt_t = dt.T
    w_end_t = w_end.T

    rows = lax.broadcasted_iota(jnp.int32, (n_rows, n_rows), 0)
    cols = lax.broadcasted_iota(jnp.int32, (n_rows, n_rows), 1)
    causal = (rows // n_tok == cols // n_tok) & (rows >= cols)
    low_half = cols < SSM_HEAD_DIM
    b_g = [bm[:, g * D_STATE:(g + 1) * D_STATE].astype(BF16) for g in range(N_SSM_GROUPS)]
    c_g = [cm[:, g * D_STATE:(g + 1) * D_STATE].astype(BF16) for g in range(N_SSM_GROUPS)]
    cb = [_dot_nt(c_g[g], b_g[g]) for g in range(N_SSM_GROUPS)]

    for j in range(N_SSM_HEADS // 2):
        lo, hi = j * LANES, (j + 1) * LANES
        heads = (2 * j, 2 * j + 1)
        xs_p = xs[:, lo:hi]
        g_mats, e_cols, w_cols = [], [], []
        for h in heads:
            g = h // HEADS_PER_SSM_GROUP
            row_b = jnp.broadcast_to(acum_t[h:h + 1, :], (n_rows, n_rows))
            col_b = row_b.T
            decay = jnp.where(causal, jnp.exp(col_b - row_b), 0.0)
            g_mats.append((cb[g] * decay * dt_t[h:h + 1, :]).astype(BF16))
            e_col = jnp.exp(col_b)
            ecd_scr[h] = e_col
            e_cols.append(e_col)
            w_cols.append(jnp.broadcast_to(w_end_t[h:h + 1, :], (n_rows, n_rows)).T)
        y_diag = _dot(jnp.concatenate(g_mats, axis=1), _pair_blockdiag(xs_p, low_half))
        y_ref[:, lo:hi] = y_diag + dsk_ref[:, lo:hi] * xs_p
        eexp_scr[:, lo:hi] = jnp.where(low_half, e_cols[0], e_cols[1])
        xwt_scr[lo:hi, :] = (xs_p * jnp.where(low_half, w_cols[0], w_cols[1])).T

    col_group = lax.broadcasted_iota(jnp.int32, (n_tok, D_SSM), 1) // D_SSM_GROUP
    lane_seq = lax.broadcasted_iota(jnp.int32, (D_SSM_GROUP, n_rows), 1) // n_tok
    for b in range(n_seq):
        r0 = b * n_tok
        h0 = h0_ref[b]
        c_stack = jnp.concatenate([c_g[g][r0:r0 + n_tok, :] for g in range(N_SSM_GROUPS)], axis=0)
        y_all = _dot_nt(c_stack, h0.astype(BF16))
        y_off = jnp.zeros((n_tok, D_SSM), F32)
        for g in range(N_SSM_GROUPS):
            y_off = jnp.where(col_group == g, y_all[g * n_tok:(g + 1) * n_tok, :], y_off)
        yoff_scr[r0:r0 + n_tok, :] = y_off
        for g in range(N_SSM_GROUPS):
            c0 = g * D_SSM_GROUP
            lhs = jnp.where(lane_seq == b, xwt_scr[c0:c0 + D_SSM_GROUP, :], 0.0).astype(BF16)
            upd = _dot(lhs, b_g[g])
            for e in range(HEADS_PER_SSM_GROUP):
                h = g * HEADS_PER_SSM_GROUP + e
                p0 = h * SSM_HEAD_DIM
                decay = ecd_scr[h, r0 + n_tok - 1:r0 + n_tok, :]
                hout_ref[b, p0:p0 + SSM_HEAD_DIM, :] = (
                    h0[p0:p0 + SSM_HEAD_DIM, :] * decay + upd[e * SSM_HEAD_DIM:(e + 1) * SSM_HEAD_DIM, :])

    y = y_ref[...] + yoff_scr[...] * eexp_scr[...]
    y = y * _silu(z_ref[...])
    sq = y * y
    col_group = lax.broadcasted_iota(jnp.int32, (n_rows, D_SSM), 1) // D_SSM_GROUP
    scale = jnp.zeros((n_rows, D_SSM), F32)
    for g in range(N_SSM_GROUPS):
        ms = jnp.sum(jnp.where(col_group == g, sq, 0.0), axis=1, keepdims=True) / D_SSM_GROUP
        scale = jnp.where(col_group == g, lax.rsqrt(ms + RMS_EPS), scale)
    y_ref[...] = y * scale * nrm_ref[...]


def _ssd_sample(xbc, conv_state, z, dt, h0, cw, cb, dtb, alog, dsk, nrm, *, n_tok):
    t = xbc.shape[0]
    n_b = t // n_tok
    n_seq = LANES // n_tok
    rows = n_seq * n_tok
    row = lambda b: (b, 0)
    const = lambda b: (0, 0)
    vec = lambda n: pl.BlockSpec((1, n), const)
    return pl.pallas_call(
        functools.partial(_ssd_sample_body, n_tok=n_tok, n_seq=n_seq),
        grid=(n_b // n_seq,),
        in_specs=[pl.BlockSpec((rows, CONV_DIM), row),
                  pl.BlockSpec((n_seq, CONV_W - 1, CONV_DIM), lambda b: (b, 0, 0)),
                  pl.BlockSpec((rows, D_SSM), row), pl.BlockSpec((rows, DT_PAD), row),
                  pl.BlockSpec((n_seq, D_SSM, D_STATE), lambda b: (b, 0, 0)),
                  pl.BlockSpec((CONV_W, CONV_DIM), const),
                  vec(CONV_DIM), vec(DT_PAD), vec(DT_PAD), vec(D_SSM), vec(D_SSM)],
        out_specs=[pl.BlockSpec((rows, D_SSM), row),
                   pl.BlockSpec((n_seq, D_SSM, D_STATE), lambda b: (b, 0, 0))],
        out_shape=[jax.ShapeDtypeStruct((t, D_SSM), F32),
                   jax.ShapeDtypeStruct((n_b, D_SSM, D_STATE), F32)],
        scratch_shapes=[pltpu.VMEM((n_seq, n_tok, CONV_DIM), F32),
                        pltpu.VMEM((D_SSM, rows), F32),
                        pltpu.VMEM((N_SSM_HEADS, rows, LANES), F32),
                        pltpu.VMEM((rows, D_SSM), F32),
                        pltpu.VMEM((rows, D_SSM), F32)],
        compiler_params=_params(1),
        name="ssd_sample",
    )(xbc, conv_state, z, dt, h0, cw, cb, dtb, alog, dsk, nrm)


HEADS_PER_HALF = LANES // ATT_HEAD_DIM
N_HALVES = D_GROUP // LANES


ATT_UNITS = (8, 4)


def _attn_prompt_body(slope_ref, *refs, dil, group, n_units):
    n_in = 5 * N_HALVES
    in_refs, o_refs, l_refs = refs[:n_in], refs[n_in:n_in + N_HALVES], refs[n_in + N_HALVES:]
    i = pl.program_id(0)
    nb = ATT_BLOCK
    qi = lax.broadcasted_iota(jnp.int32, (nb, 2 * nb), 0)
    kj = lax.broadcasted_iota(jnp.int32, (nb, 2 * nb), 1)
    dist = nb + qi - kj
    band = (dist >= 0) & (dist <= nb)
    band_first = band & (kj >= jnp.where(i > 0, 0, nb))
    dist_f = (dist * dil).astype(F32)
    alibi = [-slope_ref[group * H_PER_DIL + h] * dist_f for h in range(H_PER_DIL)]
    bias_first = [jnp.where(band_first, a, -jnp.inf) for a in alibi]
    bias_inner = [jnp.where(band, a, -jnp.inf) for a in alibi] if dil == 1 else bias_first
    lane_head = lax.broadcasted_iota(jnp.int32, (nb, LANES), 1) // ATT_HEAD_DIM

    def unit_rows(u):
        if dil == 1:
            rows = pl.ds(u * nb, nb)
            rows_prev = pl.ds((n_units - 1) * nb, nb) if u == 0 else pl.ds((u - 1) * nb, nb)
            return rows, rows_prev, (bias_first if u == 0 else bias_inner)
        rows = pl.ds(pl.program_id(1) * n_units + u, nb, stride=dil)
        return rows, rows, bias_first

    def unit_scores(u):
        rows, rows_prev, bias = unit_rows(u)
        scores = []
        for half in range(N_HALVES):
            q_ref, kp_ref, kc_ref, vp_ref, vc_ref = in_refs[5 * half:5 * half + 5]
            if dil == 1 and u > 0:
                kp_ref, vp_ref = kc_ref, vc_ref
            q = q_ref[rows, :] * (ATT_HEAD_DIM ** -0.5)
            k = jnp.concatenate([kp_ref[rows_prev, :], kc_ref[rows, :]], axis=0).astype(BF16)
            v = jnp.concatenate([vp_ref[rows_prev, :], vc_ref[rows, :]], axis=0).astype(BF16)
            for hh in range(HEADS_PER_HALF):
                qh = jnp.where(lane_head == hh, q, 0.0).astype(BF16)
                scores.append((_dot_nt(qh, k) + bias[half * HEADS_PER_HALF + hh], v))
        return scores

    def unit_outputs(u, scores):
        rows = unit_rows(u)[0]
        for half in range(N_HALVES):
            out = jnp.zeros((nb, LANES), F32)
            lse = jnp.zeros((nb, LANES), F32)
            for hh in range(HEADS_PER_HALF):
                s, v = scores[half * HEADS_PER_HALF + hh]
                mine = lane_head == hh
                m = jnp.max(s, axis=1, keepdims=True)
                p = jnp.exp(s - m)
                l = jnp.sum(p, axis=1, keepdims=True)
                pv = _dot(p.astype(BF16), v)
                out = jnp.where(mine, pv / l, out)
                lse = jnp.where(mine, m + jnp.log(l), lse)
            o_refs[half][rows, :] = out
            l_refs[half][rows, :] = lse

    pending = unit_scores(0)
    for u in range(n_units):
        current, pending = pending, (unit_scores(u + 1) if u + 1 < n_units else None)
        unit_outputs(u, current)


def _attn_prompt(slopes, qkv, *, group, dil, seq):
    n_units = next(n for n in ATT_UNITS if dil == 1 or dil % n == 0)
    blk_rows = max(dil, n_units) * ATT_BLOCK
    n_blk = seq // blk_rows
    n_res_steps = max(dil // n_units, 1)
    blk = (blk_rows, LANES)
    cur = lambda c: pl.BlockSpec((None,) + blk, lambda i, r: (c, i, 0))
    prev = lambda c: pl.BlockSpec((None,) + blk, lambda i, r: (c, jnp.maximum(i - 1, 0), 0))
    in_specs = [pl.BlockSpec(memory_space=pltpu.SMEM)]
    for half in range(N_HALVES):
        q_col, k_col, v_col = (N_HALVES * (part * N_DIL + group) + half for part in range(3))
        in_specs += [cur(q_col), prev(k_col), cur(k_col), prev(v_col), cur(v_col)]
    out_blk = pl.BlockSpec(blk, lambda i, r: (i, 0))
    res = pl.pallas_call(
        functools.partial(_attn_prompt_body, dil=dil, group=group, n_units=n_units),
        grid=(n_blk, n_res_steps),
        in_specs=in_specs,
        out_specs=[out_blk] * (2 * N_HALVES),
        out_shape=[jax.ShapeDtypeStruct((seq, LANES), F32)] * (2 * N_HALVES),
        compiler_params=_params(2),
        name=f"attn_prompt_g{group}",
    )(slopes, *([qkv] * (5 * N_HALVES)))
    return res[:N_HALVES], res[N_HALVES:]


def _window_buffers(caches):
    out = []
    for (win, dil), pair in zip(DILATION_PAIRS, caches):
        assert dil & (dil - 1) == 0, "dilations must be powers of two"
        for cache in pair:
            assert cache.shape[1] == win, "window buffers must hold a full window"
            out.append(jnp.transpose(cache, (0, 2, 3, 1)).reshape(cache.shape[0], D_GROUP, win))
    return out


def _attn_sample_scores(slope_ref, qkv, k_bufs, n_tok):
    n_rows = H_PER_DIL * n_tok
    row2 = lax.broadcasted_iota(jnp.int32, (n_rows, D_GROUP), 0)
    lane2 = lax.broadcasted_iota(jnp.int32, (n_rows, D_GROUP), 1)
    own_head = (lane2 // ATT_HEAD_DIM) == (row2 // n_tok)
    row1 = lax.broadcasted_iota(jnp.int32, (n_rows, 1), 0)
    tok1 = row1 % n_tok
    head1 = row1 // n_tok
    scale = ATT_HEAD_DIM ** -0.5
    state = []
    for g, (win, dil) in enumerate(DILATION_PAIRS):
        q = qkv[:, g * D_GROUP:(g + 1) * D_GROUP]
        k_new = qkv[:, D_ATT + g * D_GROUP:D_ATT + (g + 1) * D_GROUP]
        v_new = qkv[:, 2 * D_ATT + g * D_GROUP:2 * D_ATT + (g + 1) * D_GROUP]
        q_rows = jnp.where(own_head, jnp.concatenate([q] * H_PER_DIL, axis=0), 0.0)
        slope_1 = jnp.zeros((n_rows, 1), F32)
        for h in range(H_PER_DIL):
            slope_1 = jnp.where(head1 == h, slope_ref[g * H_PER_DIL + h], slope_1)

        s_buf = _dot(q_rows.astype(BF16), k_bufs[g].astype(BF16)) * scale
        pos = lax.broadcasted_iota(jnp.int32, (n_rows, win), 1)
        tok_s = lax.broadcasted_iota(jnp.int32, (n_rows, win), 0) % n_tok
        dist = win + tok_s - pos
        on_grid = (pos >= tok_s) & (jnp.bitwise_and(dist, dil - 1) == 0)
        s_buf = jnp.where(on_grid, s_buf - slope_1 * dist.astype(F32), -jnp.inf)
        m = jnp.max(s_buf, axis=1, keepdims=True)
        s_new = []
        for tp in range(n_tok):
            sn = jnp.sum(q_rows * k_new[tp:tp + 1, :], axis=1, keepdims=True) * scale
            ok = (tok1 >= tp) & (jnp.bitwise_and(tok1 - tp, dil - 1) == 0)
            sn = jnp.where(ok, sn - slope_1 * (tok1 - tp).astype(F32), -jnp.inf)
            s_new.append(sn)
            m = jnp.maximum(m, sn)
        state.append((s_buf, s_new, m, v_new))
    return state


def _attn_sample_outputs(state, v_bufs, att_ref, n_tok):
    lane_head8 = lax.broadcasted_iota(jnp.int32, (n_tok, D_GROUP), 1) // ATT_HEAD_DIM
    outs, lses = [], []
    for g, (s_buf, s_new, m, v_new) in enumerate(state):
        p_buf = jnp.exp(s_buf - m)
        l = jnp.sum(p_buf, axis=1, keepdims=True)
        o_rows = _dot_nt(p_buf.astype(BF16), v_bufs[g].astype(BF16))
        for tp in range(n_tok):
            p_new = jnp.exp(s_new[tp] - m)
            l = l + p_new
            o_rows = o_rows + p_new * v_new[tp:tp + 1, :]
        o_rows = o_rows / l
        lse_rows = m + jnp.log(l)
        o = jnp.zeros((n_tok, D_GROUP), F32)
        lse = jnp.zeros((n_tok, D_GROUP), F32)
        for h in range(H_PER_DIL):
            o = jnp.where(lane_head8 == h, o_rows[h * n_tok:(h + 1) * n_tok, :], o)
            lse = jnp.where(lane_head8 == h, lse_rows[h * n_tok:(h + 1) * n_tok, :], lse)
        outs.append(o)
        lses.append(lse)

    m = jnp.maximum(jnp.maximum(lses[0], lses[1]), lses[2])
    e = [jnp.exp(x - m) for x in lses]
    den = e[0] + e[1] + e[2]
    for g in range(N_DIL):
        att_ref[:, g * D_GROUP:(g + 1) * D_GROUP] = outs[g] * (e[g] / den)


def _outproj_body(*refs, combine):
    n_pieces = N_DIL * N_HALVES
    if combine:
        x_ref, ssm_ref = refs[:2]
        o_refs, l_refs = refs[2:2 + n_pieces], refs[2 + n_pieces:2 + 2 * n_pieces]
        w_ref, out_ref = refs[2 + 2 * n_pieces:]
        att = [None] * n_pieces
        for half in range(N_HALVES):
            idx = [g * N_HALVES + half for g in range(N_DIL)]
            lses = [l_refs[p][...] for p in idx]
            m = jnp.maximum(jnp.maximum(lses[0], lses[1]), lses[2])
            e = [jnp.exp(x - m) for x in lses]
            den = e[0] + e[1] + e[2]
            for p, w in zip(idx, e):
                att[p] = o_refs[p][...] * (w / den)
    else:
        x_ref, ssm_ref, att_ref, w_ref, out_ref = refs
        att = [att_ref[:, p * LANES:(p + 1) * LANES] for p in range(n_pieces)]
    mixed = jnp.concatenate([ssm_ref[...]] + att, axis=1).astype(BF16)
    out_ref[...] = x_ref[...] + _dot(mixed, w_ref[...].astype(BF16))


def _outproj(x, ssm, att, w_out, *, tm=512):
    t = x.shape[0]
    row = lambda i: (i, 0)
    combine = isinstance(att, (tuple, list))
    att_args = list(att[0]) + list(att[1]) if combine else [att]
    att_specs = [pl.BlockSpec((tm, a.shape[1]), row) for a in att_args]
    return pl.pallas_call(
        functools.partial(_outproj_body, combine=combine),
        grid=(t // tm,),
        in_specs=[pl.BlockSpec((tm, D_MODEL), row), pl.BlockSpec((tm, D_SSM), row)] + att_specs
                 + [pl.BlockSpec((D_MODEL, D_MODEL), lambda i: (0, 0), pipeline_mode=pl.Buffered(1))],
        out_specs=pl.BlockSpec((tm, D_MODEL), row),
        out_shape=jax.ShapeDtypeStruct((t, D_MODEL), F32),
        compiler_params=_params(1),
        name="outproj_prompt" if combine else "outproj_sample",
    )(x, ssm, *att_args, w_out)


def kernel(x_prompt, x_sample, cache_k0, cache_v0, cache_k1, cache_v1, cache_k2, cache_v2, state_ssm, state_conv, ffn1_norm, ffn1_w_gate, ffn1_w_up, ffn1_w_down, mix_norm, w_in, conv_w, conv_b, dt_bias, a_log, d_skip, ssm_norm, w_out, ffn2_norm, ffn2_w_gate, ffn2_w_up, ffn2_w_down, final_norm):
    assert w_in.shape[0] == 1, "single layer"
    batch, seq, _ = x_prompt.shape
    dec_batch, dec_seq, _ = x_sample.shape
    assert batch == 1 and dec_seq == SUBLANES

    row = lambda v: v.reshape(1, -1).astype(F32)
    pad_heads = lambda v: jnp.pad(v.reshape(1, -1).astype(F32), ((0, 0), (0, DT_PAD - N_SSM_HEADS)))
    w_in_t = jnp.transpose(w_in[0]).astype(BF16)
    ffn1 = (row(ffn1_norm[0]), ffn1_w_gate[0], ffn1_w_up[0], ffn1_w_down[0])
    ffn2 = (row(ffn2_norm[0]), ffn2_w_gate[0], ffn2_w_up[0], ffn2_w_down[0])
    w_o = w_out[0]
    ssd_params = (conv_w[0].astype(F32), row(conv_b[0]), pad_heads(dt_bias[0]), pad_heads(a_log[0]),
                  row(jnp.repeat(d_skip[0], SSM_HEAD_DIM)), row(ssm_norm[0]))
    slopes = jnp.exp2(-ALIBI_MAX_EXP * jnp.arange(1, N_ATT_HEADS + 1, dtype=F32) / N_ATT_HEADS)

    def back(x2):
        return _ffn(x2, *ffn2, row(final_norm))

    n_s = dec_batch * dec_seq
    x1_s = _ffn(x_sample.reshape(n_s, D_MODEL), *ffn1)
    z, xbc_s, qkv_s, dt = _inproj(x1_s, row(mix_norm[0]), w_in_t)
    ssm_s, h_s = _ssd_sample(xbc_s, state_conv[0], z, dt, state_ssm[0].reshape(dec_batch, D_SSM, D_STATE),
                             *ssd_params, n_tok=dec_seq)
    caches = ((cache_k0[0], cache_v0[0]), (cache_k1[0], cache_v1[0]), (cache_k2[0], cache_v2[0]))

    x1, att_s = _ffn(x_prompt.reshape(seq, D_MODEL), *ffn1,
                     host=(slopes, qkv_s, _window_buffers(caches), dec_seq))
    xbc_p, qkv_p, ssm_p, hfin_p = _inproj_ssd(x1, row(mix_norm[0]), w_in_t, *ssd_params)
    outs, lses = [], []
    for g, (win, dil) in enumerate(DILATION_PAIRS):
        o, lse = _attn_prompt(slopes, qkv_p, group=g, dil=dil, seq=seq)
        outs += o
        lses += lse
    y_prompt = back(_outproj(x1, ssm_p, (outs, lses), w_o)).reshape(1, seq, D_MODEL)

    y_sample = back(_outproj(x1_s, ssm_s, att_s, w_o)).reshape(dec_batch, dec_seq, D_MODEL)

    head_shape = (H_PER_DIL, ATT_HEAD_DIM)
    p_kv, s_kv = [], []
    for g, (win, dil) in enumerate(DILATION_PAIRS):
        keep = min(win, seq)
        for base in (D_ATT, 2 * D_ATT):
            c0 = base + g * D_GROUP
            blocks = [qkv_p[c0 // LANES + half, seq - keep:] for half in range(N_HALVES)]
            p_kv.append(jnp.concatenate(blocks, axis=1).reshape((1, 1, keep) + head_shape))
            s_kv.append(qkv_s[:, c0:c0 + D_GROUP].reshape((1, dec_batch, dec_seq) + head_shape))
    p_ssm = hfin_p.reshape(1, 1, N_SSM_HEADS, SSM_HEAD_DIM, D_STATE)
    p_conv = xbc_p[seq - (CONV_W - 1):].reshape(1, 1, CONV_W - 1, CONV_DIM)
    s_ssm = h_s.reshape(1, dec_batch, N_SSM_HEADS, SSM_HEAD_DIM, D_STATE)
    s_conv = xbc_s.reshape(dec_batch, dec_seq, CONV_DIM)[:, dec_seq - (CONV_W - 1):].reshape(
        1, dec_batch, CONV_W - 1, CONV_DIM)
    return (y_prompt, y_sample, *p_kv, p_ssm, p_conv, *s_kv, s_ssm, s_conv)
```

```python
import functools

import jax
import jax.numpy as jnp
from jax import lax
from jax.experimental import pallas as pl
from jax.experimental.pallas import tpu as pltpu

F32 = jnp.float32
BF16 = jnp.bfloat16

D_MODEL = 2048
D_FF = 5632
D_ATT = 768
ATT_HEAD_DIM = 64
H_PER_DIL = 4
D_GROUP = H_PER_DIL * ATT_HEAD_DIM
DILATION_PAIRS = ((128, 1), (512, 4), (2048, 16))
N_DIL = len(DILATION_PAIRS)
ATT_BLOCK = 128
D_SSM = 1280
SSM_HEAD_DIM = 64
N_SSM_HEADS = 20
N_SSM_GROUPS = 4
HEADS_PER_SSM_GROUP = 5
D_SSM_GROUP = D_SSM // N_SSM_GROUPS
D_STATE = 128
CONV_W = 4
CONV_DIM = 2304
SSD_CHUNK = 128
RMS_EPS = 1e-6
ALIBI_MAX_EXP = 8.0
N_ATT_HEADS = 12

LANES = 128
SUBLANES = 8
DT_PAD = LANES
IN_HEAD = D_SSM + CONV_DIM
IN_TAIL = N_SSM_HEADS + 3 * D_ATT
N_QKV_BLOCKS = 3 * D_ATT // LANES

VMEM_LIMIT = 60 * 1024 * 1024


def _params(n_grid_dims, vmem=VMEM_LIMIT):
    return pltpu.CompilerParams(dimension_semantics=("arbitrary",) * n_grid_dims,
                                vmem_limit_bytes=vmem)


def _silu(x):
    return x * (0.5 + 0.5 * jnp.tanh(0.5 * x))


def _softplus(x):
    return jnp.maximum(x, 0.0) + jnp.log(1.0 + jnp.exp(-jnp.abs(x)))


def _rms_scale(x):
    return x * lax.rsqrt(jnp.mean(x * x, axis=-1, keepdims=True) + RMS_EPS)


def _dot(a, b):
    return jnp.dot(a, b, preferred_element_type=F32)


def _dot_nt(a, b):
    return lax.dot_general(a, b, (((1,), (1,)), ((), ())), preferred_element_type=F32)


def _cumsum_rows(x):
    n = x.shape[0]
    rows = lax.broadcasted_iota(jnp.int32, x.shape, 0)
    shift = 1
    while shift < n:
        x = x + jnp.where(rows >= shift, pltpu.roll(x, shift, 0), 0.0)
        shift *= 2
    return x


FFN_ROW_CHUNKS = 4


def _ffn_body(*refs, n_f, final, n_guest):
    hosted = n_guest > 0
    refs = list(refs)
    x_ref, g_ref, wg_ref, wu_ref, wd_ref = refs[:5]
    del refs[:5]
    fg_ref = refs.pop(0) if final else None
    if hosted:
        slope_ref, qkv_ref, k0_ref, v0_ref, k1_ref, v1_ref, k2_ref, v2_ref = refs[:8]
        del refs[:8]
        o_ref, att_ref, h_ref = refs
    else:
        o_ref, h_ref = refs
    f = pl.program_id(1)
    tm = x_ref.shape[0]
    assert n_f >= 2 and tm % FFN_ROW_CHUNKS == 0

    def guest_scores():
        if hosted:
            k_bufs = [k0_ref[0], k1_ref[0], k2_ref[0]]
            return _attn_sample_scores(slope_ref, qkv_ref[...], k_bufs, qkv_ref.shape[0])

    def guest_outputs(state):
        if hosted:
            v_bufs = [v0_ref[0], v1_ref[0], v2_ref[0]]
            _attn_sample_outputs(state, v_bufs, att_ref, qkv_ref.shape[0])

    def weights():
        return wg_ref[...].astype(BF16), wu_ref[...].astype(BF16), wd_ref[...].astype(BF16)

    def gate_up(h, w, with_guest=False):
        gate = _dot(h, w[0])
        state = guest_scores() if with_guest else None
        up = _dot(h, w[1])
        if with_guest:
            guest_outputs(state)
        return gate, up

    def down(gate_and_up, w):
        gate, up = gate_and_up
        return _dot((_silu(gate) * up).astype(BF16), w[2])

    def partial_out(h, w, with_guest=False):
        return down(gate_up(h, w, with_guest), w)

    chunks = [slice(c * tm // FFN_ROW_CHUNKS, (c + 1) * tm // FFN_ROW_CHUNKS) for c in range(FFN_ROW_CHUNKS)]

    def chunked(h_of, finish):
        w = weights()
        pending = gate_up(h_of(chunks[0]), w)
        for c, rows in enumerate(chunks):
            current = pending
            if c + 1 < len(chunks):
                pending = gate_up(h_of(chunks[c + 1]), w, with_guest=(c == 0))
            finish(rows, down(current, w))

    @pl.when(f == 0)
    def _():
        def h_of(rows):
            h = (_rms_scale(x_ref[rows, :]) * g_ref[...]).astype(BF16)
            h_ref[rows, :] = h
            return h

        def finish(rows, part):
            o_ref[rows, :] = part

        chunked(h_of, finish)

    middle = (f > 0) & (f < n_f - 1)
    guest_left = pl.program_id(0) * n_f + f < n_guest

    if hosted:
        @pl.when(middle & guest_left)
        def _():
            o_ref[...] += partial_out(h_ref[...], weights(), with_guest=True)

    @pl.when(middle & jnp.logical_not(guest_left))
    def _():
        o_ref[...] += partial_out(h_ref[...], weights())

    @pl.when(f == n_f - 1)
    def _():
        def finish(rows, part):
            y = x_ref[rows, :] + 0.5 * (o_ref[rows, :] + part)
            if final:
                y = _rms_scale(y) * fg_ref[...]
            o_ref[rows, :] = y

        chunked(lambda rows: h_ref[rows, :], finish)


def _ffn(x, gain, wg, wu, wd, final_gain=None, *, host=None, tm=1024, tf=256):
    t = x.shape[0]
    n_f = D_FF // tf
    final = final_gain is not None
    hosted = host is not None
    row = lambda i, f: (i, 0)
    const = lambda i, f: (0, 0)
    tile_mode = dict(pipeline_mode=pl.Buffered(1)) if hosted else {}
    n_guest = 0
    in_specs = [pl.BlockSpec((tm, D_MODEL), row), pl.BlockSpec((1, D_MODEL), const),
                pl.BlockSpec((D_MODEL, tf), lambda i, f: (0, f)),
                pl.BlockSpec((D_MODEL, tf), lambda i, f: (0, f)),
                pl.BlockSpec((tf, D_MODEL), lambda i, f: (f, 0))]
    args = [x, gain, wg, wu, wd]
    if final:
        in_specs.append(pl.BlockSpec((1, D_MODEL), const))
        args.append(final_gain)
    out_specs = pl.BlockSpec((tm, D_MODEL), row, **tile_mode)
    out_shape = jax.ShapeDtypeStruct((t, D_MODEL), F32)
    if hosted:
        slopes, qkv, caches, n_tok = host
        n_guest = n_b = qkv.shape[0] // n_tok
        assert n_b <= (t // tm) * n_f, "one sequence per grid step"
        seq = lambda i, f: jnp.minimum(i * n_f + f, n_b - 1)
        in_specs += [pl.BlockSpec(memory_space=pltpu.SMEM),
                     pl.BlockSpec((n_tok, 3 * D_ATT), lambda i, f: (seq(i, f), 0))]
        in_specs += [pl.BlockSpec((1,) + c.shape[1:], lambda i, f: (seq(i, f), 0, 0)) for c in caches]
        args += [slopes, qkv, *caches]
        out_specs = [out_specs, pl.BlockSpec((n_tok, D_ATT), lambda i, f: (seq(i, f), 0))]
        out_shape = [out_shape, jax.ShapeDtypeStruct((qkv.shape[0], D_ATT), F32)]
    return pl.pallas_call(
        functools.partial(_ffn_body, n_f=n_f, final=final, n_guest=n_guest),
        grid=(t // tm, n_f),
        in_specs=in_specs,
        out_specs=out_specs,
        out_shape=out_shape,
        scratch_shapes=[pltpu.VMEM((tm, D_MODEL), BF16)],
        compiler_params=_params(2),
        name=("ffn_final" if final else "ffn") + ("_host" if hosted else ""),
    )(*args)


def _inproj_body(x_ref, g_ref, w_ref, z_ref, xbc_ref, qkv_ref, dt_ref):
    h = (_rms_scale(x_ref[...]) * g_ref[...]).astype(BF16)
    z_ref[...] = _dot_nt(h, w_ref[0:D_SSM, :])
    xbc_ref[...] = _dot_nt(h, w_ref[D_SSM:IN_HEAD, :])
    tail = _dot_nt(h, w_ref[IN_HEAD:IN_HEAD + IN_TAIL, :])
    lane = lax.broadcasted_iota(jnp.int32, (tail.shape[0], DT_PAD), 1)
    dt_ref[...] = jnp.where(lane < N_SSM_HEADS, tail[:, 0:DT_PAD], 0.0)
    qkv_ref[...] = tail[:, N_SSM_HEADS:IN_TAIL]


def _inproj(x, gain, w_t, *, tm=512):
    t = x.shape[0]
    row = lambda i: (i, 0)
    const = lambda i: (0, 0)
    widths = (D_SSM, CONV_DIM, 3 * D_ATT, DT_PAD)
    return pl.pallas_call(
        _inproj_body,
        grid=(t // tm,),
        in_specs=[pl.BlockSpec((tm, D_MODEL), row), pl.BlockSpec((1, D_MODEL), const),
                  pl.BlockSpec((IN_HEAD + IN_TAIL, D_MODEL), const, pipeline_mode=pl.Buffered(1))],
        out_specs=[pl.BlockSpec((tm, w), row) for w in widths],
        out_shape=[jax.ShapeDtypeStruct((t, w), F32) for w in widths],
        compiler_params=_params(1),
        name="inproj",
    )(x, gain, w_t)


def _pair_blockdiag(x, low_half):
    zero = jnp.zeros_like(x)
    return jnp.concatenate([jnp.where(low_half, x, zero), jnp.where(low_half, zero, x)],
                           axis=0).astype(BF16)


def _ssd_chunk_phases(xbc_ref, z_ref, dt_ref, cw_ref, cb_ref, dtb_ref, alog_ref, dsk_ref, nrm_ref,
                      y_ref, xf_scr, ht_scr):
    cl = SSD_CHUNK
    st = {}

    def prologue():
        x = xbc_ref[...]
        xf_scr[SUBLANES:SUBLANES + cl, :] = x
        conv = cb_ref[...]
        for j in range(CONV_W - 1):
            conv = conv + xf_scr[SUBLANES - 3 + j:SUBLANES - 3 + j + cl, :] * cw_ref[j:j + 1, :]
        conv = conv + x * cw_ref[CONV_W - 1:CONV_W, :]
        xf_scr[0:SUBLANES, :] = x[cl - SUBLANES:cl, :]
        xa = _silu(conv)
        st["xs"] = xa[:, :D_SSM]
        bm = xa[:, D_SSM:D_SSM + N_SSM_GROUPS * D_STATE]
        cm = xa[:, D_SSM + N_SSM_GROUPS * D_STATE:]

        dt = _softplus(dt_ref[...] + dtb_ref[...])
        a = -jnp.exp(alog_ref[...])
        acum = _cumsum_rows(dt * a)
        w_end = dt * jnp.exp(acum[cl - 1:cl, :] - acum)
        st["acum_t"] = acum.T
        st["dt_t"] = dt.T
        st["w_end_t"] = w_end.T

        rows = lax.broadcasted_iota(jnp.int32, (cl, cl), 0)
        cols = lax.broadcasted_iota(jnp.int32, (cl, cl), 1)
        st["causal"] = rows >= cols
        st["low_half"] = cols < SSM_HEAD_DIM

        b_g = [bm[:, g * D_STATE:(g + 1) * D_STATE] for g in range(N_SSM_GROUPS)]
        st["c_g"] = [cm[:, g * D_STATE:(g + 1) * D_STATE].astype(BF16) for g in range(N_SSM_GROUPS)]
        st["b_t"] = [b.T for b in b_g]
        st["cb"] = [_dot_nt(st["c_g"][g], b_g[g].astype(BF16)) for g in range(N_SSM_GROUPS)]
        st["sumsq"] = [jnp.zeros((cl, 1), F32) for _ in range(N_SSM_GROUPS)]

    def pair(j):
        lo, hi = j * LANES, (j + 1) * LANES
        heads = (2 * j, 2 * j + 1)
        groups = tuple(h // HEADS_PER_SSM_GROUP for h in heads)
        low_half, c_g = st["low_half"], st["c_g"]
        xs_p = st["xs"][:, lo:hi]
        x_bd = _pair_blockdiag(xs_p, low_half)
        ht_p = ht_scr[:, lo:hi]
        h_bd = _pair_blockdiag(ht_p, low_half)
        g_mats, w_mats, e_cols = [], [], []
        for h, g in zip(heads, groups):
            row_b = jnp.broadcast_to(st["acum_t"][h:h + 1, :], (cl, cl))
            col_b = row_b.T
            decay = jnp.where(st["causal"], jnp.exp(col_b - row_b), 0.0)
            g_mats.append((st["cb"][g] * decay * st["dt_t"][h:h + 1, :]).astype(BF16))
            w_mats.append((st["b_t"][g] * st["w_end_t"][h:h + 1, :]).astype(BF16))
            e_cols.append(jnp.exp(col_b))
        y_diag = _dot(jnp.concatenate(g_mats, axis=1), x_bd)
        y_off = _dot(jnp.concatenate([c_g[groups[0]], c_g[groups[1]]], axis=1), h_bd)
        y_off = y_off * jnp.where(low_half, e_cols[0], e_cols[1])
        state = _dot(jnp.concatenate(w_mats, axis=1), x_bd)
        chunk_decay = jnp.where(low_half[0:1, :], e_cols[0][cl - 1:cl, :], e_cols[1][cl - 1:cl, :])
        ht_scr[:, lo:hi] = ht_p * chunk_decay + state
        y_p = y_diag + y_off + dsk_ref[:, lo:hi] * xs_p
        y_p = y_p * _silu(z_ref[:, lo:hi])
        y_ref[:, lo:hi] = y_p
        sq = y_p * y_p
        sumsq = st["sumsq"]
        sumsq[groups[0]] = sumsq[groups[0]] + jnp.sum(jnp.where(low_half, sq, 0.0), axis=1, keepdims=True)
        sumsq[groups[1]] = sumsq[groups[1]] + jnp.sum(jnp.where(low_half, 0.0, sq), axis=1, keepdims=True)

    def epilogue():
        scale = [lax.rsqrt(s / D_SSM_GROUP + RMS_EPS) for s in st["sumsq"]]
        for j in range(N_SSM_HEADS // 2):
            lo, hi = j * LANES, (j + 1) * LANES
            g0, g1 = (2 * j) // HEADS_PER_SSM_GROUP, (2 * j + 1) // HEADS_PER_SSM_GROUP
            y_ref[:, lo:hi] = (y_ref[:, lo:hi] * jnp.where(st["low_half"], scale[g0], scale[g1])
                               * nrm_ref[:, lo:hi])

    return [prologue] + [functools.partial(pair, j) for j in range(N_SSM_HEADS // 2)] + [epilogue]


PROJ_TILE = 256


def _inproj_ssd_body(x_ref, g_ref, w_ref, cw_ref, cb_ref, dtb_ref, alog_ref, dsk_ref, nrm_ref,
                     xbc_ref, qkv_ref, y_ref, hfin_ref,
                     xbc_a, z_a, dt_a, xbc_b, z_b, dt_b, xf_scr, ht_scr, *, n_blocks):
    s = pl.program_id(0)
    tm = x_ref.shape[0]
    bufs = ((xbc_a, z_a, dt_a), (xbc_b, z_b, dt_b))

    @pl.when(s <= 1)
    def _():
        xf_scr[0:SUBLANES, :] = jnp.zeros((SUBLANES, CONV_DIM), F32)
        ht_scr[...] = jnp.zeros_like(ht_scr)

    @pl.when(s == 0)
    def _():
        for ref in bufs[1]:
            ref[...] = jnp.zeros_like(ref)

    def step(parity):
        xbc_w, z_w, dt_w = bufs[parity]
        xbc_r, z_r, dt_r = bufs[1 - parity]
        h = (_rms_scale(x_ref[...]) * g_ref[...]).astype(BF16)

        def z_tile(c0, c1):
            z_w[:, c0:c1] = _dot_nt(h, w_ref[c0:c1, :])

        def xbc_tile(c0, c1):
            xbc = _dot_nt(h, w_ref[D_SSM + c0:D_SSM + c1, :])
            xbc_w[:, c0:c1] = xbc
            xbc_ref[:, c0:c1] = xbc

        tail_parts = []

        def tail_tile(c0, c1):
            tail_parts.append(_dot_nt(h, w_ref[IN_HEAD + c0:IN_HEAD + c1, :]))
            if c1 == IN_TAIL:
                tail = jnp.concatenate(tail_parts, axis=1)
                lane = lax.broadcasted_iota(jnp.int32, (tm, DT_PAD), 1)
                dt_w[...] = jnp.where(lane < N_SSM_HEADS, tail[:, 0:DT_PAD], 0.0)
                qkv = tail[:, N_SSM_HEADS:IN_TAIL]
                for j in range(N_QKV_BLOCKS):
                    qkv_ref[j] = qkv[:, j * LANES:(j + 1) * LANES]

        tiles = [functools.partial(z_tile, c, c + PROJ_TILE) for c in range(0, D_SSM, PROJ_TILE)]
        tiles += [functools.partial(xbc_tile, c, c + PROJ_TILE) for c in range(0, CONV_DIM, PROJ_TILE)]
        n_tail = IN_TAIL // PROJ_TILE
        tail_cuts = [c * PROJ_TILE for c in range(n_tail)] + [IN_TAIL]
        tiles += [functools.partial(tail_tile, c0, c1) for c0, c1 in zip(tail_cuts[:-1], tail_cuts[1:])]
        phases = []
        for c in range(tm // SSD_CHUNK):
            rows = slice(c * SSD_CHUNK, (c + 1) * SSD_CHUNK)
            phases += _ssd_chunk_phases(xbc_r.at[rows, :], z_r.at[rows, :], dt_r.at[rows, :], cw_ref,
                                        cb_ref, dtb_ref, alog_ref, dsk_ref, nrm_ref, y_ref.at[rows, :],
                                        xf_scr, ht_scr)
        done = 0
        for i, tile in enumerate(tiles):
            tile()
            upto = (i + 1) * len(phases) // len(tiles)
            for phase in phases[done:upto]:
                phase()
            done = upto

    @pl.when(s % 2 == 0)
    def _():
        step(0)

    @pl.when(s % 2 == 1)
    def _():
        step(1)

    @pl.when(s == n_blocks)
    def _():
        for j in range(D_SSM // LANES):
            hfin_ref[j * LANES:(j + 1) * LANES, :] = ht_scr[:, j * LANES:(j + 1) * LANES].T


def _inproj_ssd(x, gain, w_t, cw, cb, dtb, alog, dsk, nrm, *, tm=256):
    t = x.shape[0]
    n_blocks = t // tm
    cur = lambda s: (jnp.minimum(s, n_blocks - 1), 0)
    prev = lambda s: (jnp.maximum(s - 1, 0), 0)
    const = lambda s: (0, 0)
    vec = lambda n: pl.BlockSpec((1, n), const)
    buf = lambda n: pltpu.VMEM((tm, n), F32)
    return pl.pallas_call(
        functools.partial(_inproj_ssd_body, n_blocks=n_blocks),
        grid=(n_blocks + 1,),
        in_specs=[pl.BlockSpec((tm, D_MODEL), cur), vec(D_MODEL),
                  pl.BlockSpec((IN_HEAD + IN_TAIL, D_MODEL), const, pipeline_mode=pl.Buffered(1)),
                  pl.BlockSpec((CONV_W, CONV_DIM), const),
                  vec(CONV_DIM), vec(DT_PAD), vec(DT_PAD), vec(D_SSM), vec(D_SSM)],
        out_specs=[pl.BlockSpec((tm, CONV_DIM), cur),
                   pl.BlockSpec((N_QKV_BLOCKS, tm, LANES), lambda s: (0, jnp.minimum(s, n_blocks - 1), 0)),
                   pl.BlockSpec((tm, D_SSM), prev), pl.BlockSpec((D_SSM, D_STATE), const)],
        out_shape=[jax.ShapeDtypeStruct((t, CONV_DIM), F32),
                   jax.ShapeDtypeStruct((N_QKV_BLOCKS, t, LANES), F32),
                   jax.ShapeDtypeStruct((t, D_SSM), F32), jax.ShapeDtypeStruct((D_SSM, D_STATE), F32)],
        scratch_shapes=[buf(CONV_DIM), buf(D_SSM), buf(DT_PAD), buf(CONV_DIM), buf(D_SSM), buf(DT_PAD),
                        pltpu.VMEM((SUBLANES + SSD_CHUNK, CONV_DIM), F32),
                        pltpu.VMEM((D_STATE, D_SSM), F32)],
        compiler_params=_params(1),
        name="inproj_ssd",
    )(x, gain, w_t, cw, cb, dtb, alog, dsk, nrm)


def _ssd_sample_body(xbc_ref, cs_ref, z_ref, dt_ref, h0_ref, cw_ref, cb_ref, dtb_ref, alog_ref,
                     dsk_ref, nrm_ref, y_ref, hout_ref, hist_scr, xwt_scr, ecd_scr, eexp_scr, yoff_scr,
                     *, n_tok, n_seq):
    n_rows = n_seq * n_tok
    assert n_rows == LANES

    @pl.when(pl.program_id(0) == 0)
    def _():
        hist_scr[...] = jnp.zeros_like(hist_scr)

    x = xbc_ref[...]
    hist_scr[:, 0:CONV_W - 1, :] = cs_ref[...]
    hist = hist_scr[...].reshape(n_rows, CONV_DIM)
    tok = lax.broadcasted_iota(jnp.int32, (n_rows, 1), 0) % n_tok
    conv = cb_ref[...] + x * cw_ref[CONV_W - 1:CONV_W, :]
    for k in range(1, CONV_W):
        back = CONV_W - 1 - k
        from_hist = hist if back == 0 else pltpu.roll(hist, n_rows - back, 0)
        shifted = jnp.where(tok < k, from_hist, pltpu.roll(x, k, 0))
        conv = conv + shifted * cw_ref[back:back + 1, :]
    xa = _silu(conv)
    xs = xa[:, :D_SSM]
    bm = xa[:, D_SSM:D_SSM + N_SSM_GROUPS * D_STATE]
    cm = xa[:, D_SSM + N_SSM_GROUPS * D_STATE:]

    dt = _softplus(dt_ref[...] + dtb_ref[...])
    dta = dt * -jnp.exp(alog_ref[...])
    tok_l = lax.broadcasted_iota(jnp.int32, (n_rows, LANES), 0) % n_tok
    acum = dta
    to_end = jnp.where(tok_l + 1 < n_tok, pltpu.roll(dta, n_rows - 1, 0), 0.0)
    shift = 1
    while shift < n_tok:
        acum = acum + jnp.where(tok_l >= shift, pltpu.roll(acum, shift, 0), 0.0)
        to_end = to_end + jnp.where(tok_l + shift < n_tok, pltpu.roll(to_end, n_rows - shift, 0), 0.0)
        shift *= 2
    w_end = dt * jnp.exp(to_end)
    acum_t = acum.T
    dt_t = dt.T
    w_end_t = w_end.T

    rows = lax.broadcasted_iota(jnp.int32, (n_rows, n_rows), 0)
    cols = lax.broadcasted_iota(jnp.int32, (n_rows, n_rows), 1)
    causal = (rows // n_tok == cols // n_tok) & (rows >= cols)
    low_half = cols < SSM_HEAD_DIM
    b_g = [bm[:, g * D_STATE:(g + 1) * D_STATE].astype(BF16) for g in range(N_SSM_GROUPS)]
    c_g = [cm[:, g * D_STATE:(g + 1) * D_STATE].astype(BF16) for g in range(N_SSM_GROUPS)]
    cb = [_dot_nt(c_g[g], b_g[g]) for g in range(N_SSM_GROUPS)]

    for j in range(N_SSM_HEADS // 2):
        lo, hi = j * LANES, (j + 1) * LANES
        heads = (2 * j, 2 * j + 1)
        xs_p = xs[:, lo:hi]
        g_mats, e_cols, w_cols = [], [], []
        for h in heads:
            g = h // HEADS_PER_SSM_GROUP
            row_b = jnp.broadcast_to(acum_t[h:h + 1, :], (n_rows, n_rows))
            col_b = row_b.T
            decay = jnp.where(causal, jnp.exp(col_b - row_b), 0.0)
            g_mats.append((cb[g] * decay * dt_t[h:h + 1, :]).astype(BF16))
            e_col = jnp.exp(col_b)
            ecd_scr[h] = e_col
            e_cols.append(e_col)
            w_cols.append(jnp.broadcast_to(w_end_t[h:h + 1, :], (n_rows, n_rows)).T)
        y_diag = _dot(jnp.concatenate(g_mats, axis=1), _pair_blockdiag(xs_p, low_half))
        y_ref[:, lo:hi] = y_diag + dsk_ref[:, lo:hi] * xs_p
        eexp_scr[:, lo:hi] = jnp.where(low_half, e_cols[0], e_cols[1])
        xwt_scr[lo:hi, :] = (xs_p * jnp.where(low_half, w_cols[0], w_cols[1])).T

    col_group = lax.broadcasted_iota(jnp.int32, (n_tok, D_SSM), 1) // D_SSM_GROUP
    lane_seq = lax.broadcasted_iota(jnp.int32, (D_SSM_GROUP, n_rows), 1) // n_tok
    for b in range(n_seq):
        r0 = b * n_tok
        h0 = h0_ref[b]
        c_stack = jnp.concatenate([c_g[g][r0:r0 + n_tok, :] for g in range(N_SSM_GROUPS)], axis=0)
        y_all = _dot_nt(c_stack, h0.astype(BF16))
        y_off = jnp.zeros((n_tok, D_SSM), F32)
        for g in range(N_SSM_GROUPS):
            y_off = jnp.where(col_group == g, y_all[g * n_tok:(g + 1) * n_tok, :], y_off)
        yoff_scr[r0:r0 + n_tok, :] = y_off
        for g in range(N_SSM_GROUPS):
            c0 = g * D_SSM_GROUP
            lhs = jnp.where(lane_seq == b, xwt_scr[c0:c0 + D_SSM_GROUP, :], 0.0).astype(BF16)
            upd = _dot(lhs, b_g[g])
            for e in range(HEADS_PER_SSM_GROUP):
                h = g * HEADS_PER_SSM_GROUP + e
                p0 = h * SSM_HEAD_DIM
                decay = ecd_scr[h, r0 + n_tok - 1:r0 + n_tok, :]
                hout_ref[b, p0:p0 + SSM_HEAD_DIM, :] = (
                    h0[p0:p0 + SSM_HEAD_DIM, :] * decay + upd[e * SSM_HEAD_DIM:(e + 1) * SSM_HEAD_DIM, :])

    y = y_ref[...] + yoff_scr[...] * eexp_scr[...]
    y = y * _silu(z_ref[...])
    sq = y * y
    col_group = lax.broadcasted_iota(jnp.int32, (n_rows, D_SSM), 1) // D_SSM_GROUP
    scale = jnp.zeros((n_rows, D_SSM), F32)
    for g in range(N_SSM_GROUPS):
        ms = jnp.sum(jnp.where(col_group == g, sq, 0.0), axis=1, keepdims=True) / D_SSM_GROUP
        scale = jnp.where(col_group == g, lax.rsqrt(ms + RMS_EPS), scale)
    y_ref[...] = y * scale * nrm_ref[...]


def _ssd_sample(xbc, conv_state, z, dt, h0, cw, cb, dtb, alog, dsk, nrm, *, n_tok):
    t = xbc.shape[0]
    n_b = t // n_tok
    n_seq = LANES // n_tok
    rows = n_seq * n_tok
    row = lambda b: (b, 0)
    const = lambda b: (0, 0)
    vec = lambda n: pl.BlockSpec((1, n), const)
    return pl.pallas_call(
        functools.partial(_ssd_sample_body, n_tok=n_tok, n_seq=n_seq),
        grid=(n_b // n_seq,),
        in_specs=[pl.BlockSpec((rows, CONV_DIM), row),
                  pl.BlockSpec((n_seq, CONV_W - 1, CONV_DIM), lambda b: (b, 0, 0)),
                  pl.BlockSpec((rows, D_SSM), row), pl.BlockSpec((rows, DT_PAD), row),
                  pl.BlockSpec((n_seq, D_SSM, D_STATE), lambda b: (b, 0, 0)),
                  pl.BlockSpec((CONV_W, CONV_DIM), const),
                  vec(CONV_DIM), vec(DT_PAD), vec(DT_PAD), vec(D_SSM), vec(D_SSM)],
        out_specs=[pl.BlockSpec((rows, D_SSM), row),
                   pl.BlockSpec((n_seq, D_SSM, D_STATE), lambda b: (b, 0, 0))],
        out_shape=[jax.ShapeDtypeStruct((t, D_SSM), F32),
                   jax.ShapeDtypeStruct((n_b, D_SSM, D_STATE), F32)],
        scratch_shapes=[pltpu.VMEM((n_seq, n_tok, CONV_DIM), F32),
                        pltpu.VMEM((D_SSM, rows), F32),
                        pltpu.VMEM((N_SSM_HEADS, rows, LANES), F32),
                        pltpu.VMEM((rows, D_SSM), F32),
                        pltpu.VMEM((rows, D_SSM), F32)],
        compiler_params=_params(1),
        name="ssd_sample",
    )(xbc, conv_state, z, dt, h0, cw, cb, dtb, alog, dsk, nrm)


HEADS_PER_HALF = LANES // ATT_HEAD_DIM
N_HALVES = D_GROUP // LANES


ATT_UNITS = (8, 4)


def _attn_prompt_body(slope_ref, *refs, dil, group, n_units):
    n_in = 5 * N_HALVES
    in_refs, o_refs, l_refs = refs[:n_in], refs[n_in:n_in + N_HALVES], refs[n_in + N_HALVES:]
    i = pl.program_id(0)
    nb = ATT_BLOCK
    qi = lax.broadcasted_iota(jnp.int32, (nb, 2 * nb), 0)
    kj = lax.broadcasted_iota(jnp.int32, (nb, 2 * nb), 1)
    dist = nb + qi - kj
    band = (dist >= 0) & (dist <= nb)
    band_first = band & (kj >= jnp.where(i > 0, 0, nb))
    dist_f = (dist * dil).astype(F32)
    alibi = [-slope_ref[group * H_PER_DIL + h] * dist_f for h in range(H_PER_DIL)]
    bias_first = [jnp.where(band_first, a, -jnp.inf) for a in alibi]
    bias_inner = [jnp.where(band, a, -jnp.inf) for a in alibi] if dil == 1 else bias_first
    lane_head = lax.broadcasted_iota(jnp.int32, (nb, LANES), 1) // ATT_HEAD_DIM

    def unit_rows(u):
        if dil == 1:
            rows = pl.ds(u * nb, nb)
            rows_prev = pl.ds((n_units - 1) * nb, nb) if u == 0 else pl.ds((u - 1) * nb, nb)
            return rows, rows_prev, (bias_first if u == 0 else bias_inner)
        rows = pl.ds(pl.program_id(1) * n_units + u, nb, stride=dil)
        return rows, rows, bias_first

    def unit_scores(u):
        rows, rows_prev, bias = unit_rows(u)
        scores = []
        for half in range(N_HALVES):
            q_ref, kp_ref, kc_ref, vp_ref, vc_ref = in_refs[5 * half:5 * half + 5]
            if dil == 1 and u > 0:
                kp_ref, vp_ref = kc_ref, vc_ref
            q = q_ref[rows, :] * (ATT_HEAD_DIM ** -0.5)
            k = jnp.concatenate([kp_ref[rows_prev, :], kc_ref[rows, :]], axis=0).astype(BF16)
            v = jnp.concatenate([vp_ref[rows_prev, :], vc_ref[rows, :]], axis=0).astype(BF16)
            for hh in range(HEADS_PER_HALF):
                qh = jnp.where(lane_head == hh, q, 0.0).astype(BF16)
                scores.append((_dot_nt(qh, k) + bias[half * HEADS_PER_HALF + hh], v))
        return scores

    def unit_outputs(u, scores):
        rows = unit_rows(u)[0]
        for half in range(N_HALVES):
            out = jnp.zeros((nb, LANES), F32)
            lse = jnp.zeros((nb, LANES), F32)
            for hh in range(HEADS_PER_HALF):
                s, v = scores[half * HEADS_PER_HALF + hh]
                mine = lane_head == hh
                m = jnp.max(s, axis=1, keepdims=True)
                p = jnp.exp(s - m)
                l = jnp.sum(p, axis=1, keepdims=True)
                pv = _dot(p.astype(BF16), v)
                out = jnp.where(mine, pv / l, out)
                lse = jnp.where(mine, m + jnp.log(l), lse)
            o_refs[half][rows, :] = out
            l_refs[half][rows, :] = lse

    pending = unit_scores(0)
    for u in range(n_units):
        current, pending = pending, (unit_scores(u + 1) if u + 1 < n_units else None)
        unit_outputs(u, current)


def _attn_prompt(slopes, qkv, *, group, dil, seq):
    n_units = next(n for n in ATT_UNITS if dil == 1 or dil % n == 0)
    blk_rows = max(dil, n_units) * ATT_BLOCK
    n_blk = seq // blk_rows
    n_res_steps = max(dil // n_units, 1)
    blk = (blk_rows, LANES)
    cur = lambda c: pl.BlockSpec((None,) + blk, lambda i, r: (c, i, 0))
    prev = lambda c: pl.BlockSpec((None,) + blk, lambda i, r: (c, jnp.maximum(i - 1, 0), 0))
    in_specs = [pl.BlockSpec(memory_space=pltpu.SMEM)]
    for half in range(N_HALVES):
        q_col, k_col, v_col = (N_HALVES * (part * N_DIL + group) + half for part in range(3))
        in_specs += [cur(q_col), prev(k_col), cur(k_col), prev(v_col), cur(v_col)]
    out_blk = pl.BlockSpec(blk, lambda i, r: (i, 0))
    res = pl.pallas_call(
        functools.partial(_attn_prompt_body, dil=dil, group=group, n_units=n_units),
        grid=(n_blk, n_res_steps),
        in_specs=in_specs,
        out_specs=[out_blk] * (2 * N_HALVES),
        out_shape=[jax.ShapeDtypeStruct((seq, LANES), F32)] * (2 * N_HALVES),
        compiler_params=_params(2),
        name=f"attn_prompt_g{group}",
    )(slopes, *([qkv] * (5 * N_HALVES)))
    return res[:N_HALVES], res[N_HALVES:]


def _window_buffers(caches):
    out = []
    for (win, dil), pair in zip(DILATION_PAIRS, caches):
        assert dil & (dil - 1) == 0, "dilations must be powers of two"
        for cache in pair:
            assert cache.shape[1] == win, "window buffers must hold a full window"
            out.append(jnp.transpose(cache, (0, 2, 3, 1)).reshape(cache.shape[0], D_GROUP, win))
    return out


def _attn_sample_scores(slope_ref, qkv, k_bufs, n_tok):
    n_rows = H_PER_DIL * n_tok
    row2 = lax.broadcasted_iota(jnp.int32, (n_rows, D_GROUP), 0)
    lane2 = lax.broadcasted_iota(jnp.int32, (n_rows, D_GROUP), 1)
    own_head = (lane2 // ATT_HEAD_DIM) == (row2 // n_tok)
    row1 = lax.broadcasted_iota(jnp.int32, (n_rows, 1), 0)
    tok1 = row1 % n_tok
    head1 = row1 // n_tok
    scale = ATT_HEAD_DIM ** -0.5
    state = []
    for g, (win, dil) in enumerate(DILATION_PAIRS):
        q = qkv[:, g * D_GROUP:(g + 1) * D_GROUP]
        k_new = qkv[:, D_ATT + g * D_GROUP:D_ATT + (g + 1) * D_GROUP]
        v_new = qkv[:, 2 * D_ATT + g * D_GROUP:2 * D_ATT + (g + 1) * D_GROUP]
        q_rows = jnp.where(own_head, jnp.concatenate([q] * H_PER_DIL, axis=0), 0.0)
        slope_1 = jnp.zeros((n_rows, 1), F32)
        for h in range(H_PER_DIL):
            slope_1 = jnp.where(head1 == h, slope_ref[g * H_PER_DIL + h], slope_1)

        s_buf = _dot(q_rows.astype(BF16), k_bufs[g].astype(BF16)) * scale
        pos = lax.broadcasted_iota(jnp.int32, (n_rows, win), 1)
        tok_s = lax.broadcasted_iota(jnp.int32, (n_rows, win), 0) % n_tok
        dist = win + tok_s - pos
        on_grid = (pos >= tok_s) & (jnp.bitwise_and(dist, dil - 1) == 0)
        s_buf = jnp.where(on_grid, s_buf - slope_1 * dist.astype(F32), -jnp.inf)
        m = jnp.max(s_buf, axis=1, keepdims=True)
        s_new = []
        for tp in range(n_tok):
            sn = jnp.sum(q_rows * k_new[tp:tp + 1, :], axis=1, keepdims=True) * scale
            ok = (tok1 >= tp) & (jnp.bitwise_and(tok1 - tp, dil - 1) == 0)
            sn = jnp.where(ok, sn - slope_1 * (tok1 - tp).astype(F32), -jnp.inf)
            s_new.append(sn)
            m = jnp.maximum(m, sn)
        state.append((s_buf, s_new, m, v_new))
    return state


def _attn_sample_outputs(state, v_bufs, att_ref, n_tok):
    lane_head8 = lax.broadcasted_iota(jnp.int32, (n_tok, D_GROUP), 1) // ATT_HEAD_DIM
    outs, lses = [], []
    for g, (s_buf, s_new, m, v_new) in enumerate(state):
        p_buf = jnp.exp(s_buf - m)
        l = jnp.sum(p_buf, axis=1, keepdims=True)
        o_rows = _dot_nt(p_buf.astype(BF16), v_bufs[g].astype(BF16))
        for tp in range(n_tok):
            p_new = jnp.exp(s_new[tp] - m)
            l = l + p_new
            o_rows = o_rows + p_new * v_new[tp:tp + 1, :]
        o_rows = o_rows / l
        lse_rows = m + jnp.log(l)
        o = jnp.zeros((n_tok, D_GROUP), F32)
        lse = jnp.zeros((n_tok, D_GROUP), F32)
        for h in range(H_PER_DIL):
            o = jnp.where(lane_head8 == h, o_rows[h * n_tok:(h + 1) * n_tok, :], o)
            lse = jnp.where(lane_head8 == h, lse_rows[h * n_tok:(h + 1) * n_tok, :], lse)
        outs.append(o)
        lses.append(lse)

    m = jnp.maximum(jnp.maximum(lses[0], lses[1]), lses[2])
    e = [jnp.exp(x - m) for x in lses]
    den = e[0] + e[1] + e[2]
    for g in range(N_DIL):
        att_ref[:, g * D_GROUP:(g + 1) * D_GROUP] = outs[g] * (e[g] / den)


def _outproj_body(*refs, combine):
    n_pieces = N_DIL * N_HALVES
    if combine:
        x_ref, ssm_ref = refs[:2]
        o_refs, l_refs = refs[2:2 + n_pieces], refs[2 + n_pieces:2 + 2 * n_pieces]
        w_ref, out_ref = refs[2 + 2 * n_pieces:]
        att = [None] * n_pieces
        for half in range(N_HALVES):
            idx = [g * N_HALVES + half for g in range(N_DIL)]
            lses = [l_refs[p][...] for p in idx]
            m = jnp.maximum(jnp.maximum(lses[0], lses[1]), lses[2])
            e = [jnp.exp(x - m) for x in lses]
            den = e[0] + e[1] + e[2]
            for p, w in zip(idx, e):
                att[p] = o_refs[p][...] * (w / den)
    else:
        x_ref, ssm_ref, att_ref, w_ref, out_ref = refs
        att = [att_ref[:, p * LANES:(p + 1) * LANES] for p in range(n_pieces)]
    mixed = jnp.concatenate([ssm_ref[...]] + att, axis=1).astype(BF16)
    out_ref[...] = x_ref[...] + _dot(mixed, w_ref[...].astype(BF16))


def _outproj(x, ssm, att, w_out, *, tm=512):
    t = x.shape[0]
    row = lambda i: (i, 0)
    combine = isinstance(att, (tuple, list))
    att_args = list(att[0]) + list(att[1]) if combine else [att]
    att_specs = [pl.BlockSpec((tm, a.shape[1]), row) for a in att_args]
    return pl.pallas_call(
        functools.partial(_outproj_body, combine=combine),
        grid=(t // tm,),
        in_specs=[pl.BlockSpec((tm, D_MODEL), row), pl.BlockSpec((tm, D_SSM), row)] + att_specs
                 + [pl.BlockSpec((D_MODEL, D_MODEL), lambda i: (0, 0), pipeline_mode=pl.Buffered(1))],
        out_specs=pl.BlockSpec((tm, D_MODEL), row),
        out_shape=jax.ShapeDtypeStruct((t, D_MODEL), F32),
        compiler_params=_params(1),
        name="outproj_prompt" if combine else "outproj_sample",
    )(x, ssm, *att_args, w_out)


def kernel(x_prompt, x_sample, cache_k0, cache_v0, cache_k1, cache_v1, cache_k2, cache_v2, state_ssm, state_conv, ffn1_norm, ffn1_w_gate, ffn1_w_up, ffn1_w_down, mix_norm, w_in, conv_w, conv_b, dt_bias, a_log, d_skip, ssm_norm, w_out, ffn2_norm, ffn2_w_gate, ffn2_w_up, ffn2_w_down, final_norm):
    assert w_in.shape[0] == 1, "single layer"
    batch, seq, _ = x_prompt.shape
    dec_batch, dec_seq, _ = x_sample.shape
    assert batch == 1 and dec_seq == SUBLANES

    row = lambda v: v.reshape(1, -1).astype(F32)
    pad_heads = lambda v: jnp.pad(v.reshape(1, -1).astype(F32), ((0, 0), (0, DT_PAD - N_SSM_HEADS)))
    w_in_t = jnp.transpose(w_in[0]).astype(BF16)
    ffn1 = (row(ffn1_norm[0]), ffn1_w_gate[0], ffn1_w_up[0], ffn1_w_down[0])
    ffn2 = (row(ffn2_norm[0]), ffn2_w_gate[0], ffn2_w_up[0], ffn2_w_down[0])
    w_o = w_out[0]
    ssd_params = (conv_w[0].astype(F32), row(conv_b[0]), pad_heads(dt_bias[0]), pad_heads(a_log[0]),
                  row(jnp.repeat(d_skip[0], SSM_HEAD_DIM)), row(ssm_norm[0]))
    slopes = jnp.exp2(-ALIBI_MAX_EXP * jnp.arange(1, N_ATT_HEADS + 1, dtype=F32) / N_ATT_HEADS)

    def back(x2):
        return _ffn(x2, *ffn2, row(final_norm))

    n_s = dec_batch * dec_seq
    x1_s = _ffn(x_sample.reshape(n_s, D_MODEL), *ffn1)
    z, xbc_s, qkv_s, dt = _inproj(x1_s, row(mix_norm[0]), w_in_t)
    ssm_s, h_s = _ssd_sample(xbc_s, state_conv[0], z, dt, state_ssm[0].reshape(dec_batch, D_SSM, D_STATE),
                             *ssd_params, n_tok=dec_seq)
    caches = ((cache_k0[0], cache_v0[0]), (cache_k1[0], cache_v1[0]), (cache_k2[0], cache_v2[0]))

    x1, att_s = _ffn(x_prompt.reshape(seq, D_MODEL), *ffn1,
                     host=(slopes, qkv_s, _window_buffers(caches), dec_seq))
    xbc_p, qkv_p, ssm_p, hfin_p = _inproj_ssd(x1, row(mix_norm[0]), w_in_t, *ssd_params)
    outs, lses = [], []
    for g, (win, dil) in enumerate(DILATION_PAIRS):
        o, lse = _attn_prompt(slopes, qkv_p, group=g, dil=dil, seq=seq)
        outs += o
        lses += lse
    y_prompt = back(_outproj(x1, ssm_p, (outs, lses), w_o)).reshape(1, seq, D_MODEL)

    y_sample = back(_outproj(x1_s, ssm_s, att_s, w_o)).reshape(dec_batch, dec_seq, D_MODEL)

    head_shape = (H_PER_DIL, ATT_HEAD_DIM)
    p_kv, s_kv = [], []
    for g, (win, dil) in enumerate(DILATION_PAIRS):
        keep = min(win, seq)
        for base in (D_ATT, 2 * D_ATT):
            c0 = base + g * D_GROUP
            blocks = [qkv_p[c0 // LANES + half, seq - keep:] for half in range(N_HALVES)]
            p_kv.append(jnp.concatenate(blocks, axis=1).reshape((1, 1, keep) + head_shape))
            s_kv.append(qkv_s[:, c0:c0 + D_GROUP].reshape((1, dec_batch, dec_seq) + head_shape))
    p_ssm = hfin_p.reshape(1, 1, N_SSM_HEADS, SSM_HEAD_DIM, D_STATE)
    p_conv = xbc_p[seq - (CONV_W - 1):].reshape(1, 1, CONV_W - 1, CONV_DIM)
    s_ssm = h_s.reshape(1, dec_batch, N_SSM_HEADS, SSM_HEAD_DIM, D_STATE)
    s_conv = xbc_s.reshape(dec_batch, dec_seq, CONV_DIM)[:, dec_seq - (CONV_W - 1):].reshape(
        1, dec_batch, CONV_W - 1, CONV_DIM)
    return (y_prompt, y_sample, *p_kv, p_ssm, p_conv, *s_kv, s_ssm, s_conv)
```

```python
import functools

import jax
import jax.numpy as jnp
from jax import lax
from jax.experimental import pallas as pl
from jax.experimental.pallas import tpu as pltpu

F32 = jnp.float32
BF16 = jnp.bfloat16

D_MODEL = 2048
D_FF = 5632
D_ATT = 768
ATT_HEAD_DIM = 64
H_PER_DIL = 4
D_GROUP = H_PER_DIL * ATT_HEAD_DIM
DILATION_PAIRS = ((128, 1), (512, 4), (2048, 16))
N_DIL = len(DILATION_PAIRS)
ATT_BLOCK = 128
D_SSM = 1280
SSM_HEAD_DIM = 64
N_SSM_HEADS = 20
N_SSM_GROUPS = 4
HEADS_PER_SSM_GROUP = 5
D_SSM_GROUP = D_SSM // N_SSM_GROUPS
D_STATE = 128
CONV_W = 4
CONV_DIM = 2304
SSD_CHUNK = 128
RMS_EPS = 1e-6
ALIBI_MAX_EXP = 8.0
N_ATT_HEADS = 12

LANES = 128
SUBLANES = 8
DT_PAD = LANES
IN_HEAD = D_SSM + CONV_DIM
IN_TAIL = N_SSM_HEADS + 3 * D_ATT
N_QKV_BLOCKS = 3 * D_ATT // LANES

VMEM_LIMIT = 60 * 1024 * 1024


def _params(n_grid_dims, vmem=VMEM_LIMIT):
    return pltpu.CompilerParams(dimension_semantics=("arbitrary",) * n_grid_dims,
                                vmem_limit_bytes=vmem)


def _silu(x):
    return x * (0.5 + 0.5 * jnp.tanh(0.5 * x))


def _softplus(x):
    return jnp.maximum(x, 0.0) + jnp.log(1.0 + jnp.exp(-jnp.abs(x)))


def _rms_scale(x):
    return x * lax.rsqrt(jnp.mean(x * x, axis=-1, keepdims=True) + RMS_EPS)


def _dot(a, b):
    return jnp.dot(a, b, preferred_element_type=F32)


def _dot_nt(a, b):
    return lax.dot_general(a, b, (((1,), (1,)), ((), ())), preferred_element_type=F32)


def _cumsum_rows(x):
    n = x.shape[0]
    rows = lax.broadcasted_iota(jnp.int32, x.shape, 0)
    shift = 1
    while shift < n:
        x = x + jnp.where(rows >= shift, pltpu.roll(x, shift, 0), 0.0)
        shift *= 2
    return x


FFN_ROW_CHUNKS = 4


def _ffn_body(*refs, n_f, final, n_guest):
    hosted = n_guest > 0
    refs = list(refs)
    x_ref, g_ref, wg_ref, wu_ref, wd_ref = refs[:5]
    del refs[:5]
    fg_ref = refs.pop(0) if final else None
    if hosted:
        slope_ref, qkv_ref, k0_ref, v0_ref, k1_ref, v1_ref, k2_ref, v2_ref = refs[:8]
        del refs[:8]
        o_ref, att_ref, h_ref = refs
    else:
        o_ref, h_ref = refs
    f = pl.program_id(1)
    tm = x_ref.shape[0]
    assert n_f >= 2 and tm % FFN_ROW_CHUNKS == 0

    def guest_scores():
        if hosted:
            k_bufs = [k0_ref[0], k1_ref[0], k2_ref[0]]
            return _attn_sample_scores(slope_ref, qkv_ref[...], k_bufs, qkv_ref.shape[0])

    def guest_outputs(state):
        if hosted:
            v_bufs = [v0_ref[0], v1_ref[0], v2_ref[0]]
            _attn_sample_outputs(state, v_bufs, att_ref, qkv_ref.shape[0])

    def weights():
        return wg_ref[...].astype(BF16), wu_ref[...].astype(BF16), wd_ref[...].astype(BF16)

    def gate_up(h, w, with_guest=False):
        gate = _dot(h, w[0])
        state = guest_scores() if with_guest else None
        up = _dot(h, w[1])
        if with_guest:
            guest_outputs(state)
        return gate, up

    def down(gate_and_up, w):
        gate, up = gate_and_up
        return _dot((_silu(gate) * up).astype(BF16), w[2])

    def partial_out(h, w, with_guest=False):
        return down(gate_up(h, w, with_guest), w)

    chunks = [slice(c * tm // FFN_ROW_CHUNKS, (c + 1) * tm // FFN_ROW_CHUNKS) for c in range(FFN_ROW_CHUNKS)]

    def chunked(h_of, finish):
        w = weights()
        pending = gate_up(h_of(chunks[0]), w)
        for c, rows in enumerate(chunks):
            current = pending
            if c + 1 < len(chunks):
                pending = gate_up(h_of(chunks[c + 1]), w, with_guest=(c == 0))
            finish(rows, down(current, w))

    @pl.when(f == 0)
    def _():
        def h_of(rows):
            h = (_rms_scale(x_ref[rows, :]) * g_ref[...]).astype(BF16)
            h_ref[rows, :] = h
            return h

        def finish(rows, part):
            o_ref[rows, :] = part

        chunked(h_of, finish)

    middle = (f > 0) & (f < n_f - 1)
    guest_left = pl.program_id(0) * n_f + f < n_guest

    if hosted:
        @pl.when(middle & guest_left)
        def _():
            o_ref[...] += partial_out(h_ref[...], weights(), with_guest=True)

    @pl.when(middle & jnp.logical_not(guest_left))
    def _():
        o_ref[...] += partial_out(h_ref[...], weights())

    @pl.when(f == n_f - 1)
    def _():
        def finish(rows, part):
            y = x_ref[rows, :] + 0.5 * (o_ref[rows, :] + part)
            if final:
                y = _rms_scale(y) * fg_ref[...]
            o_ref[rows, :] = y

        chunked(lambda rows: h_ref[rows, :], finish)


def _ffn(x, gain, wg, wu, wd, final_gain=None, *, host=None, tm=1024, tf=256):
    t = x.shape[0]
    n_f = D_FF // tf
    final = final_gain is not None
    hosted = host is not None
    row = lambda i, f: (i, 0)
    const = lambda i, f: (0, 0)
    tile_mode = dict(pipeline_mode=pl.Buffered(1)) if hosted else {}
    n_guest = 0
    in_specs = [pl.BlockSpec((tm, D_MODEL), row), pl.BlockSpec((1, D_MODEL), const),
                pl.BlockSpec((D_MODEL, tf), lambda i, f: (0, f)),
                pl.BlockSpec((D_MODEL, tf), lambda i, f: (0, f)),
                pl.BlockSpec((tf, D_MODEL), lambda i, f: (f, 0))]
    args = [x, gain, wg, wu, wd]
    if final:
        in_specs.append(pl.BlockSpec((1, D_MODEL), const))
        args.append(final_gain)
    out_specs = pl.BlockSpec((tm, D_MODEL), row, **tile_mode)
    out_shape = jax.ShapeDtypeStruct((t, D_MODEL), F32)
    if hosted:
        slopes, qkv, caches, n_tok = host
        n_guest = n_b = qkv.shape[0] // n_tok
        assert n_b <= (t // tm) * n_f, "one sequence per grid step"
        seq = lambda i, f: jnp.minimum(i * n_f + f, n_b - 1)
        in_specs += [pl.BlockSpec(memory_space=pltpu.SMEM),
                     pl.BlockSpec((n_tok, 3 * D_ATT), lambda i, f: (seq(i, f), 0))]
        in_specs += [pl.BlockSpec((1,) + c.shape[1:], lambda i, f: (seq(i, f), 0, 0)) for c in caches]
        args += [slopes, qkv, *caches]
        out_specs = [out_specs, pl.BlockSpec((n_tok, D_ATT), lambda i, f: (seq(i, f), 0))]
        out_shape = [out_shape, jax.ShapeDtypeStruct((qkv.shape[0], D_ATT), F32)]
    return pl.pallas_call(
        functools.partial(_ffn_body, n_f=n_f, final=final, n_guest=n_guest),
        grid=(t // tm, n_f),
        in_specs=in_specs,
        out_specs=out_specs,
        out_shape=out_shape,
        scratch_shapes=[pltpu.VMEM((tm, D_MODEL), BF16)],
        compiler_params=_params(2),
        name=("ffn_final" if final else "ffn") + ("_host" if hosted else ""),
    )(*args)


def _inproj_body(x_ref, g_ref, w_ref, z_ref, xbc_ref, qkv_ref, dt_ref):
    h = (_rms_scale(x_ref[...]) * g_ref[...]).astype(BF16)
    z_ref[...] = _dot_nt(h, w_ref[0:D_SSM, :])
    xbc_ref[...] = _dot_nt(h, w_ref[D_SSM:IN_HEAD, :])
    tail = _dot_nt(h, w_ref[IN_HEAD:IN_HEAD + IN_TAIL, :])
    lane = lax.broadcasted_iota(jnp.int32, (tail.shape[0], DT_PAD), 1)
    dt_ref[...] = jnp.where(lane < N_SSM_HEADS, tail[:, 0:DT_PAD], 0.0)
    qkv_ref[...] = tail[:, N_SSM_HEADS:IN_TAIL]


def _inproj(x, gain, w_t, *, tm=512):
    t = x.shape[0]
    row = lambda i: (i, 0)
    const = lambda i: (0, 0)
    widths = (D_SSM, CONV_DIM, 3 * D_ATT, DT_PAD)
    return pl.pallas_call(
        _inproj_body,
        grid=(t // tm,),
        in_specs=[pl.BlockSpec((tm, D_MODEL), row), pl.BlockSpec((1, D_MODEL), const),
                  pl.BlockSpec((IN_HEAD + IN_TAIL, D_MODEL), const, pipeline_mode=pl.Buffered(1))],
        out_specs=[pl.BlockSpec((tm, w), row) for w in widths],
        out_shape=[jax.ShapeDtypeStruct((t, w), F32) for w in widths],
        compiler_params=_params(1),
        name="inproj",
    )(x, gain, w_t)


def _pair_blockdiag(x, low_half):
    zero = jnp.zeros_like(x)
    return jnp.concatenate([jnp.where(low_half, x, zero), jnp.where(low_half, zero, x)],
                           axis=0).astype(BF16)


def _ssd_chunk_phases(xbc_ref, z_ref, dt_ref, cw_ref, cb_ref, dtb_ref, alog_ref, dsk_ref, nrm_ref,
                      y_ref, xf_scr, ht_scr):
    cl = SSD_CHUNK
    st = {}

    def prologue():
        x = xbc_ref[...]
        xf_scr[SUBLANES:SUBLANES + cl, :] = x
        conv = cb_ref[...]
        for j in range(CONV_W - 1):
            conv = conv + xf_scr[SUBLANES - 3 + j:SUBLANES - 3 + j + cl, :] * cw_ref[j:j + 1, :]
        conv = conv + x * cw_ref[CONV_W - 1:CONV_W, :]
        xf_scr[0:SUBLANES, :] = x[cl - SUBLANES:cl, :]
        xa = _silu(conv)
        st["xs"] = xa[:, :D_SSM]
        bm = xa[:, D_SSM:D_SSM + N_SSM_GROUPS * D_STATE]
        cm = xa[:, D_SSM + N_SSM_GROUPS * D_STATE:]

        dt = _softplus(dt_ref[...] + dtb_ref[...])
        a = -jnp.exp(alog_ref[...])
        acum = _cumsum_rows(dt * a)
        w_end = dt * jnp.exp(acum[cl - 1:cl, :] - acum)
        st["acum_t"] = acum.T
        st["dt_t"] = dt.T
        st["w_end_t"] = w_end.T

        rows = lax.broadcasted_iota(jnp.int32, (cl, cl), 0)
        cols = lax.broadcasted_iota(jnp.int32, (cl, cl), 1)
        st["causal"] = rows >= cols
        st["low_half"] = cols < SSM_HEAD_DIM

        b_g = [bm[:, g * D_STATE:(g + 1) * D_STATE] for g in range(N_SSM_GROUPS)]
        st["c_g"] = [cm[:, g * D_STATE:(g + 1) * D_STATE].astype(BF16) for g in range(N_SSM_GROUPS)]
        st["b_t"] = [b.T for b in b_g]
        st["cb"] = [_dot_nt(st["c_g"][g], b_g[g].astype(BF16)) for g in range(N_SSM_GROUPS)]
        st["sumsq"] = [jnp.zeros((cl, 1), F32) for _ in range(N_SSM_GROUPS)]

    def pair(j):
        lo, hi = j * LANES, (j + 1) * LANES
        heads = (2 * j, 2 * j + 1)
        groups = tuple(h // HEADS_PER_SSM_GROUP for h in heads)
        low_half, c_g = st["low_half"], st["c_g"]
        xs_p = st["xs"][:, lo:hi]
        x_bd = _pair_blockdiag(xs_p, low_half)
        ht_p = ht_scr[:, lo:hi]
        h_bd = _pair_blockdiag(ht_p, low_half)
        g_mats, w_mats, e_cols = [], [], []
        for h, g in zip(heads, groups):
            row_b = jnp.broadcast_to(st["acum_t"][h:h + 1, :], (cl, cl))
            col_b = row_b.T
            decay = jnp.where(st["causal"], jnp.exp(col_b - row_b), 0.0)
            g_mats.append((st["cb"][g] * decay * st["dt_t"][h:h + 1, :]).astype(BF16))
            w_mats.append((st["b_t"][g] * st["w_end_t"][h:h + 1, :]).astype(BF16))
            e_cols.append(jnp.exp(col_b))
        y_diag = _dot(jnp.concatenate(g_mats, axis=1), x_bd)
        y_off = _dot(jnp.concatenate([c_g[groups[0]], c_g[groups[1]]], axis=1), h_bd)
        y_off = y_off * jnp.where(low_half, e_cols[0], e_cols[1])
        state = _dot(jnp.concatenate(w_mats, axis=1), x_bd)
        chunk_decay = jnp.where(low_half[0:1, :], e_cols[0][cl - 1:cl, :], e_cols[1][cl - 1:cl, :])
        ht_scr[:, lo:hi] = ht_p * chunk_decay + state
        y_p = y_diag + y_off + dsk_ref[:, lo:hi] * xs_p
        y_p = y_p * _silu(z_ref[:, lo:hi])
        y_ref[:, lo:hi] = y_p
        sq = y_p * y_p
        sumsq = st["sumsq"]
        sumsq[groups[0]] = sumsq[groups[0]] + jnp.sum(jnp.where(low_half, sq, 0.0), axis=1, keepdims=True)
        sumsq[groups[1]] = sumsq[groups[1]] + jnp.sum(jnp.where(low_half, 0.0, sq), axis=1, keepdims=True)

    def epilogue():
        scale = [lax.rsqrt(s / D_SSM_GROUP + RMS_EPS) for s in st["sumsq"]]
        for j in range(N_SSM_HEADS // 2):
            lo, hi = j * LANES, (j + 1) * LANES
            g0, g1 = (2 * j) // HEADS_PER_SSM_GROUP, (2 * j + 1) // HEADS_PER_SSM_GROUP
            y_ref[:, lo:hi] = (y_ref[:, lo:hi] * jnp.where(st["low_half"], scale[g0], scale[g1])
                               * nrm_ref[:, lo:hi])

    return [prologue] + [functools.partial(pair, j) for j in range(N_SSM_HEADS // 2)] + [epilogue]


PROJ_TILE = 256


def _inproj_ssd_body(x_ref, g_ref, w_ref, cw_ref, cb_ref, dtb_ref, alog_ref, dsk_ref, nrm_ref,
                     xbc_ref, qkv_ref, y_ref, hfin_ref,
                     xbc_a, z_a, dt_a, xbc_b, z_b, dt_b, xf_scr, ht_scr, *, n_blocks):
    s = pl.program_id(0)
    tm = x_ref.shape[0]
    bufs = ((xbc_a, z_a, dt_a), (xbc_b, z_b, dt_b))

    @pl.when(s <= 1)
    def _():
        xf_scr[0:SUBLANES, :] = jnp.zeros((SUBLANES, CONV_DIM), F32)
        ht_scr[...] = jnp.zeros_like(ht_scr)

    @pl.when(s == 0)
    def _():
        for ref in bufs[1]:
            ref[...] = jnp.zeros_like(ref)

    def step(parity):
        xbc_w, z_w, dt_w = bufs[parity]
        xbc_r, z_r, dt_r = bufs[1 - parity]
        h = (_rms_scale(x_ref[...]) * g_ref[...]).astype(BF16)

        def z_tile(c0, c1):
            z_w[:, c0:c1] = _dot_nt(h, w_ref[c0:c1, :])

        def xbc_tile(c0, c1):
            xbc = _dot_nt(h, w_ref[D_SSM + c0:D_SSM + c1, :])
            xbc_w[:, c0:c1] = xbc
            xbc_ref[:, c0:c1] = xbc

        tail_parts = []

        def tail_tile(c0, c1):
            tail_parts.append(_dot_nt(h, w_ref[IN_HEAD + c0:IN_HEAD + c1, :]))
            if c1 == IN_TAIL:
                tail = jnp.concatenate(tail_parts, axis=1)
                lane = lax.broadcasted_iota(jnp.int32, (tm, DT_PAD), 1)
                dt_w[...] = jnp.where(lane < N_SSM_HEADS, tail[:, 0:DT_PAD], 0.0)
                qkv = tail[:, N_SSM_HEADS:IN_TAIL]
                for j in range(N_QKV_BLOCKS):
                    qkv_ref[j] = qkv[:, j * LANES:(j + 1) * LANES]

        tiles = [functools.partial(z_tile, c, c + PROJ_TILE) for c in range(0, D_SSM, PROJ_TILE)]
        tiles += [functools.partial(xbc_tile, c, c + PROJ_TILE) for c in range(0, CONV_DIM, PROJ_TILE)]
        n_tail = IN_TAIL // PROJ_TILE
        tail_cuts = [c * PROJ_TILE for c in range(n_tail)] + [IN_TAIL]
        tiles += [functools.partial(tail_tile, c0, c1) for c0, c1 in zip(tail_cuts[:-1], tail_cuts[1:])]
        phases = []
        for c in range(tm // SSD_CHUNK):
            rows = slice(c * SSD_CHUNK, (c + 1) * SSD_CHUNK)
            phases += _ssd_chunk_phases(xbc_r.at[rows, :], z_r.at[rows, :], dt_r.at[rows, :], cw_ref,
                                        cb_ref, dtb_ref, alog_ref, dsk_ref, nrm_ref, y_ref.at[rows, :],
                                        xf_scr, ht_scr)
        done = 0
        for i, tile in enumerate(tiles):
            tile()
            upto = (i + 1) * len(phases) // len(tiles)
            for phase in phases[done:upto]:
                phase()
            done = upto

    @pl.when(s % 2 == 0)
    def _():
        step(0)

    @pl.when(s % 2 == 1)
    def _():
        step(1)

    @pl.when(s == n_blocks)
    def _():
        for j in range(D_SSM // LANES):
            hfin_ref[j * LANES:(j + 1) * LANES, :] = ht_scr[:, j * LANES:(j + 1) * LANES].T


def _inproj_ssd(x, gain, w_t, cw, cb, dtb, alog, dsk, nrm, *, tm=256):
    t = x.shape[0]
    n_blocks = t // tm
    cur = lambda s: (jnp.minimum(s, n_blocks - 1), 0)
    prev = lambda s: (jnp.maximum(s - 1, 0), 0)
    const = lambda s: (0, 0)
    vec = lambda n: pl.BlockSpec((1, n), const)
    buf = lambda n: pltpu.VMEM((tm, n), F32)
    return pl.pallas_call(
        functools.partial(_inproj_ssd_body, n_blocks=n_blocks),
        grid=(n_blocks + 1,),
        in_specs=[pl.BlockSpec((tm, D_MODEL), cur), vec(D_MODEL),
                  pl.BlockSpec((IN_HEAD + IN_TAIL, D_MODEL), const, pipeline_mode=pl.Buffered(1)),
                  pl.BlockSpec((CONV_W, CONV_DIM), const),
                  vec(CONV_DIM), vec(DT_PAD), vec(DT_PAD), vec(D_SSM), vec(D_SSM)],
        out_specs=[pl.BlockSpec((tm, CONV_DIM), cur),
                   pl.BlockSpec((N_QKV_BLOCKS, tm, LANES), lambda s: (0, jnp.minimum(s, n_blocks - 1), 0)),
                   pl.BlockSpec((tm, D_SSM), prev), pl.BlockSpec((D_SSM, D_STATE), const)],
        out_shape=[jax.ShapeDtypeStruct((t, CONV_DIM), F32),
                   jax.ShapeDtypeStruct((N_QKV_BLOCKS, t, LANES), F32),
                   jax.ShapeDtypeStruct((t, D_SSM), F32), jax.ShapeDtypeStruct((D_SSM, D_STATE), F32)],
        scratch_shapes=[buf(CONV_DIM), buf(D_SSM), buf(DT_PAD), buf(CONV_DIM), buf(D_SSM), buf(DT_PAD),
                        pltpu.VMEM((SUBLANES + SSD_CHUNK, CONV_DIM), F32),
                        pltpu.VMEM((D_STATE, D_SSM), F32)],
        compiler_params=_params(1),
        name="inproj_ssd",
    )(x, gain, w_t, cw, cb, dtb, alog, dsk, nrm)


def _ssd_sample_body(xbc_ref, cs_ref, z_ref, dt_ref, h0_ref, cw_ref, cb_ref, dtb_ref, alog_ref,
                     dsk_ref, nrm_ref, y_ref, hout_ref, hist_scr, xwt_scr, ecd_scr, eexp_scr, yoff_scr,
                     *, n_tok, n_seq):
    n_rows = n_seq * n_tok
    assert n_rows == LANES

    @pl.when(pl.program_id(0) == 0)
    def _():
        hist_scr[...] = jnp.zeros_like(hist_scr)

    x = xbc_ref[...]
    hist_scr[:, 0:CONV_W - 1, :] = cs_ref[...]
    hist = hist_scr[...].reshape(n_rows, CONV_DIM)
    tok = lax.broadcasted_iota(jnp.int32, (n_rows, 1), 0) % n_tok
    conv = cb_ref[...] + x * cw_ref[CONV_W - 1:CONV_W, :]
    for k in range(1, CONV_W):
        back = CONV_W - 1 - k
        from_hist = hist if back == 0 else pltpu.roll(hist, n_rows - back, 0)
        shifted = jnp.where(tok < k, from_hist, pltpu.roll(x, k, 0))
        conv = conv + shifted * cw_ref[back:back + 1, :]
    xa = _silu(conv)
    xs = xa[:, :D_SSM]
    bm = xa[:, D_SSM:D_SSM + N_SSM_GROUPS * D_STATE]
    cm = xa[:, D_SSM + N_SSM_GROUPS * D_STATE:]

    dt = _softplus(dt_ref[...] + dtb_ref[...])
    dta = dt * -jnp.exp(alog_ref[...])
    tok_l = lax.broadcasted_iota(jnp.int32, (n_rows, LANES), 0) % n_tok
    acum = dta
    to_end = jnp.where(tok_l + 1 < n_tok, pltpu.roll(dta, n_rows - 1, 0), 0.0)
    shift = 1
    while shift < n_tok:
        acum = acum + jnp.where(tok_l >= shift, pltpu.roll(acum, shift, 0), 0.0)
        to_end = to_end + jnp.where(tok_l + shift < n_tok, pltpu.roll(to_end, n_rows - shift, 0), 0.0)
        shift *= 2
    w_end = dt * jnp.exp(to_end)
    acum_t = acum.T
    dt_t = dt.T
    w_end_t = w_end.T

    rows = lax.broadcasted_iota(jnp.int32, (n_rows, n_rows), 0)
    cols = lax.broadcasted_iota(jnp.int32, (n_rows, n_rows), 1)
    causal = (rows // n_tok == cols // n_tok) & (rows >= cols)
    low_half = cols < SSM_HEAD_DIM
    b_g = [bm[:, g * D_STATE:(g + 1) * D_STATE].astype(BF16) for g in range(N_SSM_GROUPS)]
    c_g = [cm[:, g * D_STATE:(g + 1) * D_STATE].astype(BF16) for g in range(N_SSM_GROUPS)]
    cb = [_dot_nt(c_g[g], b_g[g]) for g in range(N_SSM_GROUPS)]

    for j in range(N_SSM_HEADS // 2):
        lo, hi = j * LANES, (j + 1) * LANES
        heads = (2 * j, 2 * j + 1)
        xs_p = xs[:, lo:hi]
        g_mats, e_cols, w_cols = [], [], []
        for h in heads:
            g = h // HEADS_PER_SSM_GROUP
            row_b = jnp.broadcast_to(acum_t[h:h + 1, :], (n_rows, n_rows))
            col_b = row_b.T
            decay = jnp.where(causal, jnp.exp(col_b - row_b), 0.0)
            g_mats.append((cb[g] * decay * dt_t[h:h + 1, :]).astype(BF16))
            e_col = jnp.exp(col_b)
            ecd_scr[h] = e_col
            e_cols.append(e_col)
            w_cols.append(jnp.broadcast_to(w_end_t[h:h + 1, :], (n_rows, n_rows)).T)
        y_diag = _dot(jnp.concatenate(g_mats, axis=1), _pair_blockdiag(xs_p, low_half))
        y_ref[:, lo:hi] = y_diag + dsk_ref[:, lo:hi] * xs_p
        eexp_scr[:, lo:hi] = jnp.where(low_half, e_cols[0], e_cols[1])
        xwt_scr[lo:hi, :] = (xs_p * jnp.where(low_half, w_cols[0], w_cols[1])).T

    col_group = lax.broadcasted_iota(jnp.int32, (n_tok, D_SSM), 1) // D_SSM_GROUP
    lane_seq = lax.broadcasted_iota(jnp.int32, (D_SSM_GROUP, n_rows), 1) // n_tok
    for b in range(n_seq):
        r0 = b * n_tok
        h0 = h0_ref[b]
        c_stack = jnp.concatenate([c_g[g][r0:r0 + n_tok, :] for g in range(N_SSM_GROUPS)], axis=0)
        y_all = _dot_nt(c_stack, h0.astype(BF16))
        y_off = jnp.zeros((n_tok, D_SSM), F32)
        for g in range(N_SSM_GROUPS):
            y_off = jnp.where(col_group == g, y_all[g * n_tok:(g + 1) * n_tok, :], y_off)
        yoff_scr[r0:r0 + n_tok, :] = y_off
        for g in range(N_SSM_GROUPS):
            c0 = g * D_SSM_GROUP
            lhs = jnp.where(lane_seq == b, xwt_scr[c0:c0 + D_SSM_GROUP, :], 0.0).astype(BF16)
            upd = _dot(lhs, b_g[g])
            for e in range(HEADS_PER_SSM_GROUP):
                h = g * HEADS_PER_SSM_GROUP + e
                p0 = h * SSM_HEAD_DIM
                decay = ecd_scr[h, r0 + n_tok - 1:r0 + n_tok, :]
                hout_ref[b, p0:p0 + SSM_HEAD_DIM, :] = (
                    h0[p0:p0 + SSM_HEAD_DIM, :] * decay + upd[e * SSM_HEAD_DIM:(e + 1) * SSM_HEAD_DIM, :])

    y = y_ref[...] + yoff_scr[...] * eexp_scr[...]
    y = y * _silu(z_ref[...])
    sq = y * y
    col_group = lax.broadcasted_iota(jnp.int32, (n_rows, D_SSM), 1) // D_SSM_GROUP
    scale = jnp.zeros((n_rows, D_SSM), F32)
    for g in range(N_SSM_GROUPS):
        ms = jnp.sum(jnp.where(col_group == g, sq, 0.0), axis=1, keepdims=True) / D_SSM_GROUP
        scale = jnp.where(col_group == g, lax.rsqrt(ms + RMS_EPS), scale)
    y_ref[...] = y * scale * nrm_ref[...]


def _ssd_sample(xbc, conv_state, z, dt, h0, cw, cb, dtb, alog, dsk, nrm, *, n_tok):
    t = xbc.shape[0]
    n_b = t // n_tok
    n_seq = LANES // n_tok
    rows = n_seq * n_tok
    row = lambda b: (b, 0)
    const = lambda b: (0, 0)
    vec = lambda n: pl.BlockSpec((1, n), const)
    return pl.pallas_call(
        functools.partial(_ssd_sample_body, n_tok=n_tok, n_seq=n_seq),
        grid=(n_b // n_seq,),
        in_specs=[pl.BlockSpec((rows, CONV_DIM), row),
                  pl.BlockSpec((n_seq, CONV_W - 1, CONV_DIM), lambda b: (b, 0, 0)),
                  pl.BlockSpec((rows, D_SSM), row), pl.BlockSpec((rows, DT_PAD), row),
                  pl.BlockSpec((n_seq, D_SSM, D_STATE), lambda b: (b, 0, 0)),
                  pl.BlockSpec((CONV_W, CONV_DIM), const),
                  vec(CONV_DIM), vec(DT_PAD), vec(DT_PAD), vec(D_SSM), vec(D_SSM)],
        out_specs=[pl.BlockSpec((rows, D_SSM), row),
                   pl.BlockSpec((n_seq, D_SSM, D_STATE), lambda b: (b, 0, 0))],
        out_shape=[jax.ShapeDtypeStruct((t, D_SSM), F32),
                   jax.ShapeDtypeStruct((n_b, D_SSM, D_STATE), F32)],
        scratch_shapes=[pltpu.VMEM((n_seq, n_tok, CONV_DIM), F32),
                        pltpu.VMEM((D_SSM, rows), F32),
                        pltpu.VMEM((N_SSM_HEADS, rows, LANES), F32),
                        pltpu.VMEM((rows, D_SSM), F32),
                        pltpu.VMEM((rows, D_SSM), F32)],
        compiler_params=_params(1),
        name="ssd_sample",
    )(xbc, conv_state, z, dt, h0, cw, cb, dtb, alog, dsk, nrm)


HEADS_PER_HALF = LANES // ATT_HEAD_DIM
N_HALVES = D_GROUP // LANES


ATT_UNITS = (16, 8, 4)


def _attn_prompt_body(slope_ref, *refs, dil, group, n_units):
    n_in = 5 * N_HALVES
    in_refs, o_refs, l_refs = refs[:n_in], refs[n_in:n_in + N_HALVES], refs[n_in + N_HALVES:]
    i = pl.program_id(0)
    nb = ATT_BLOCK
    qi = lax.broadcasted_iota(jnp.int32, (nb, 2 * nb), 0)
    kj = lax.broadcasted_iota(jnp.int32, (nb, 2 * nb), 1)
    dist = nb + qi - kj
    band = (dist >= 0) & (dist <= nb)
    band_first = band & (kj >= jnp.where(i > 0, 0, nb))
    dist_f = (dist * dil).astype(F32)
    alibi = [-slope_ref[group * H_PER_DIL + h] * dist_f for h in range(H_PER_DIL)]
    bias_first = [jnp.where(band_first, a, -jnp.inf) for a in alibi]
    bias_inner = [jnp.where(band, a, -jnp.inf) for a in alibi] if dil == 1 else bias_first
    lane_head = lax.broadcasted_iota(jnp.int32, (nb, LANES), 1) // ATT_HEAD_DIM

    def unit_rows(u):
        if dil == 1:
            rows = pl.ds(u * nb, nb)
            rows_prev = pl.ds((n_units - 1) * nb, nb) if u == 0 else pl.ds((u - 1) * nb, nb)
            return rows, rows_prev, (bias_first if u == 0 else bias_inner)
        rows = pl.ds(pl.program_id(1) * n_units + u, nb, stride=dil)
        return rows, rows, bias_first

    def unit_scores(u):
        rows, rows_prev, bias = unit_rows(u)
        scores = []
        for half in range(N_HALVES):
            q_ref, kp_ref, kc_ref, vp_ref, vc_ref = in_refs[5 * half:5 * half + 5]
            if dil == 1 and u > 0:
                kp_ref, vp_ref = kc_ref, vc_ref
            q = q_ref[rows, :] * (ATT_HEAD_DIM ** -0.5)
            k = jnp.concatenate([kp_ref[rows_prev, :], kc_ref[rows, :]], axis=0).astype(BF16)
            v = jnp.concatenate([vp_ref[rows_prev, :], vc_ref[rows, :]], axis=0).astype(BF16)
            for hh in range(HEADS_PER_HALF):
                qh = jnp.where(lane_head == hh, q, 0.0).astype(BF16)
                scores.append((_dot_nt(qh, k) + bias[half * HEADS_PER_HALF + hh], v))
        return scores

    def unit_outputs(u, scores):
        rows = unit_rows(u)[0]
        for half in range(N_HALVES):
            out = jnp.zeros((nb, LANES), F32)
            lse = jnp.zeros((nb, LANES), F32)
            for hh in range(HEADS_PER_HALF):
                s, v = scores[half * HEADS_PER_HALF + hh]
                mine = lane_head == hh
                m = jnp.max(s, axis=1, keepdims=True)
                p = jnp.exp(s - m)
                l = jnp.sum(p, axis=1, keepdims=True)
                pv = _dot(p.astype(BF16), v)
                out = jnp.where(mine, pv / l, out)
                lse = jnp.where(mine, m + jnp.log(l), lse)
            o_refs[half][rows, :] = out
            l_refs[half][rows, :] = lse

    pending = unit_scores(0)
    for u in range(n_units):
        current, pending = pending, (unit_scores(u + 1) if u + 1 < n_units else None)
        unit_outputs(u, current)


def _attn_prompt(slopes, qkv, *, group, dil, seq):
    n_units = next(n for n in ATT_UNITS if dil == 1 or dil % n == 0)
    blk_rows = max(dil, n_units) * ATT_BLOCK
    n_blk = seq // blk_rows
    n_res_steps = max(dil // n_units, 1)
    blk = (blk_rows, LANES)
    cur = lambda c: pl.BlockSpec((None,) + blk, lambda i, r: (c, i, 0))
    prev = lambda c: pl.BlockSpec((None,) + blk, lambda i, r: (c, jnp.maximum(i - 1, 0), 0))
    in_specs = [pl.BlockSpec(memory_space=pltpu.SMEM)]
    for half in range(N_HALVES):
        q_col, k_col, v_col = (N_HALVES * (part * N_DIL + group) + half for part in range(3))
        in_specs += [cur(q_col), prev(k_col), cur(k_col), prev(v_col), cur(v_col)]
    out_blk = pl.BlockSpec(blk, lambda i, r: (i, 0))
    res = pl.pallas_call(
        functools.partial(_attn_prompt_body, dil=dil, group=group, n_units=n_units),
        grid=(n_blk, n_res_steps),
        in_specs=in_specs,
        out_specs=[out_blk] * (2 * N_HALVES),
        out_shape=[jax.ShapeDtypeStruct((seq, LANES), F32)] * (2 * N_HALVES),
        compiler_params=_params(2),
        name=f"attn_prompt_g{group}",
    )(slopes, *([qkv] * (5 * N_HALVES)))
    return res[:N_HALVES], res[N_HALVES:]


def _window_buffers(caches):
    out = []
    for (win, dil), pair in zip(DILATION_PAIRS, caches):
        assert dil & (dil - 1) == 0, "dilations must be powers of two"
        for cache in pair:
            assert cache.shape[1] == win, "window buffers must hold a full window"
            out.append(jnp.transpose(cache, (0, 2, 3, 1)).reshape(cache.shape[0], D_GROUP, win))
    return out


def _attn_sample_scores(slope_ref, qkv, k_bufs, n_tok):
    n_rows = H_PER_DIL * n_tok
    row2 = lax.broadcasted_iota(jnp.int32, (n_rows, D_GROUP), 0)
    lane2 = lax.broadcasted_iota(jnp.int32, (n_rows, D_GROUP), 1)
    own_head = (lane2 // ATT_HEAD_DIM) == (row2 // n_tok)
    row1 = lax.broadcasted_iota(jnp.int32, (n_rows, 1), 0)
    tok1 = row1 % n_tok
    head1 = row1 // n_tok
    scale = ATT_HEAD_DIM ** -0.5
    state = []
    for g, (win, dil) in enumerate(DILATION_PAIRS):
        q = qkv[:, g * D_GROUP:(g + 1) * D_GROUP]
        k_new = qkv[:, D_ATT + g * D_GROUP:D_ATT + (g + 1) * D_GROUP]
        v_new = qkv[:, 2 * D_ATT + g * D_GROUP:2 * D_ATT + (g + 1) * D_GROUP]
        q_rows = jnp.where(own_head, jnp.concatenate([q] * H_PER_DIL, axis=0), 0.0)
        slope_1 = jnp.zeros((n_rows, 1), F32)
        for h in range(H_PER_DIL):
            slope_1 = jnp.where(head1 == h, slope_ref[g * H_PER_DIL + h], slope_1)

        s_buf = _dot(q_rows.astype(BF16), k_bufs[g].astype(BF16)) * scale
        pos = lax.broadcasted_iota(jnp.int32, (n_rows, win), 1)
        tok_s = lax.broadcasted_iota(jnp.int32, (n_rows, win), 0) % n_tok
        dist = win + tok_s - pos
        on_grid = (pos >= tok_s) & (jnp.bitwise_and(dist, dil - 1) == 0)
        s_buf = jnp.where(on_grid, s_buf - slope_1 * dist.astype(F32), -jnp.inf)
        m = jnp.max(s_buf, axis=1, keepdims=True)
        s_new = []
        for tp in range(n_tok):
            sn = jnp.sum(q_rows * k_new[tp:tp + 1, :], axis=1, keepdims=True) * scale
            ok = (tok1 >= tp) & (jnp.bitwise_and(tok1 - tp, dil - 1) == 0)
            sn = jnp.where(ok, sn - slope_1 * (tok1 - tp).astype(F32), -jnp.inf)
            s_new.append(sn)
            m = jnp.maximum(m, sn)
        state.append((s_buf, s_new, m, v_new))
    return state


def _attn_sample_outputs(state, v_bufs, att_ref, n_tok):
    lane_head8 = lax.broadcasted_iota(jnp.int32, (n_tok, D_GROUP), 1) // ATT_HEAD_DIM
    outs, lses = [], []
    for g, (s_buf, s_new, m, v_new) in enumerate(state):
        p_buf = jnp.exp(s_buf - m)
        l = jnp.sum(p_buf, axis=1, keepdims=True)
        o_rows = _dot_nt(p_buf.astype(BF16), v_bufs[g].astype(BF16))
        for tp in range(n_tok):
            p_new = jnp.exp(s_new[tp] - m)
            l = l + p_new
            o_rows = o_rows + p_new * v_new[tp:tp + 1, :]
        o_rows = o_rows / l
        lse_rows = m + jnp.log(l)
        o = jnp.zeros((n_tok, D_GROUP), F32)
        lse = jnp.zeros((n_tok, D_GROUP), F32)
        for h in range(H_PER_DIL):
            o = jnp.where(lane_head8 == h, o_rows[h * n_tok:(h + 1) * n_tok, :], o)
            lse = jnp.where(lane_head8 == h, lse_rows[h * n_tok:(h + 1) * n_tok, :], lse)
        outs.append(o)
        lses.append(lse)

    m = jnp.maximum(jnp.maximum(lses[0], lses[1]), lses[2])
    e = [jnp.exp(x - m) for x in lses]
    den = e[0] + e[1] + e[2]
    for g in range(N_DIL):
        att_ref[:, g * D_GROUP:(g + 1) * D_GROUP] = outs[g] * (e[g] / den)


def _outproj_body(*refs, combine):
    n_pieces = N_DIL * N_HALVES
    if combine:
        x_ref, ssm_ref = refs[:2]
        o_refs, l_refs = refs[2:2 + n_pieces], refs[2 + n_pieces:2 + 2 * n_pieces]
        w_ref, out_ref = refs[2 + 2 * n_pieces:]
        att = [None] * n_pieces
        for half in range(N_HALVES):
            idx = [g * N_HALVES + half for g in range(N_DIL)]
            lses = [l_refs[p][...] for p in idx]
            m = jnp.maximum(jnp.maximum(lses[0], lses[1]), lses[2])
            e = [jnp.exp(x - m) for x in lses]
            den = e[0] + e[1] + e[2]
            for p, w in zip(idx, e):
                att[p] = o_refs[p][...] * (w / den)
    else:
        x_ref, ssm_ref, att_ref, w_ref, out_ref = refs
        att = [att_ref[:, p * LANES:(p + 1) * LANES] for p in range(n_pieces)]
    mixed = jnp.concatenate([ssm_ref[...]] + att, axis=1).astype(BF16)
    out_ref[...] = x_ref[...] + _dot(mixed, w_ref[...].astype(BF16))


def _outproj(x, ssm, att, w_out, *, tm=512):
    t = x.shape[0]
    row = lambda i: (i, 0)
    combine = isinstance(att, (tuple, list))
    att_args = list(att[0]) + list(att[1]) if combine else [att]
    att_specs = [pl.BlockSpec((tm, a.shape[1]), row) for a in att_args]
    return pl.pallas_call(
        functools.partial(_outproj_body, combine=combine),
        grid=(t // tm,),
        in_specs=[pl.BlockSpec((tm, D_MODEL), row), pl.BlockSpec((tm, D_SSM), row)] + att_specs
                 + [pl.BlockSpec((D_MODEL, D_MODEL), lambda i: (0, 0), pipeline_mode=pl.Buffered(1))],
        out_specs=pl.BlockSpec((tm, D_MODEL), row),
        out_shape=jax.ShapeDtypeStruct((t, D_MODEL), F32),
        compiler_params=_params(1),
        name="outproj_prompt" if combine else "outproj_sample",
    )(x, ssm, *att_args, w_out)


def kernel(x_prompt, x_sample, cache_k0, cache_v0, cache_k1, cache_v1, cache_k2, cache_v2, state_ssm, state_conv, ffn1_norm, ffn1_w_gate, ffn1_w_up, ffn1_w_down, mix_norm, w_in, conv_w, conv_b, dt_bias, a_log, d_skip, ssm_norm, w_out, ffn2_norm, ffn2_w_gate, ffn2_w_up, ffn2_w_down, final_norm):
    assert w_in.shape[0] == 1, "single layer"
    batch, seq, _ = x_prompt.shape
    dec_batch, dec_seq, _ = x_sample.shape
    assert batch == 1 and dec_seq == SUBLANES

    row = lambda v: v.reshape(1, -1).astype(F32)
    pad_heads = lambda v: jnp.pad(v.reshape(1, -1).astype(F32), ((0, 0), (0, DT_PAD - N_SSM_HEADS)))
    w_in_t = jnp.transpose(w_in[0]).astype(BF16)
    ffn1 = (row(ffn1_norm[0]), ffn1_w_gate[0], ffn1_w_up[0], ffn1_w_down[0])
    ffn2 = (row(ffn2_norm[0]), ffn2_w_gate[0], ffn2_w_up[0], ffn2_w_down[0])
    w_o = w_out[0]
    ssd_params = (conv_w[0].astype(F32), row(conv_b[0]), pad_heads(dt_bias[0]), pad_heads(a_log[0]),
                  row(jnp.repeat(d_skip[0], SSM_HEAD_DIM)), row(ssm_norm[0]))
    slopes = jnp.exp2(-ALIBI_MAX_EXP * jnp.arange(1, N_ATT_HEADS + 1, dtype=F32) / N_ATT_HEADS)

    def back(x2):
        return _ffn(x2, *ffn2, row(final_norm))

    n_s = dec_batch * dec_seq
    x1_s = _ffn(x_sample.reshape(n_s, D_MODEL), *ffn1)
    z, xbc_s, qkv_s, dt = _inproj(x1_s, row(mix_norm[0]), w_in_t)
    ssm_s, h_s = _ssd_sample(xbc_s, state_conv[0], z, dt, state_ssm[0].reshape(dec_batch, D_SSM, D_STATE),
                             *ssd_params, n_tok=dec_seq)
    caches = ((cache_k0[0], cache_v0[0]), (cache_k1[0], cache_v1[0]), (cache_k2[0], cache_v2[0]))

    x1, att_s = _ffn(x_prompt.reshape(seq, D_MODEL), *ffn1,
                     host=(slopes, qkv_s, _window_buffers(caches), dec_seq))
    xbc_p, qkv_p, ssm_p, hfin_p = _inproj_ssd(x1, row(mix_norm[0]), w_in_t, *ssd_params)
    outs, lses = [], []
    for g, (win, dil) in enumerate(DILATION_PAIRS):
        o, lse = _attn_prompt(slopes, qkv_p, group=g, dil=dil, seq=seq)
        outs += o
        lses += lse
    y_prompt = back(_outproj(x1, ssm_p, (outs, lses), w_o)).reshape(1, seq, D_MODEL)

    y_sample = back(_outproj(x1_s, ssm_s, att_s, w_o)).reshape(dec_batch, dec_seq, D_MODEL)

    head_shape = (H_PER_DIL, ATT_HEAD_DIM)
    p_kv, s_kv = [], []
    for g, (win, dil) in enumerate(DILATION_PAIRS):
        keep = min(win, seq)
        for base in (D_ATT, 2 * D_ATT):
            c0 = base + g * D_GROUP
            blocks = [qkv_p[c0 // LANES + half, seq - keep:] for half in range(N_HALVES)]
            p_kv.append(jnp.concatenate(blocks, axis=1).reshape((1, 1, keep) + head_shape))
            s_kv.append(qkv_s[:, c0:c0 + D_GROUP].reshape((1, dec_batch, dec_seq) + head_shape))
    p_ssm = hfin_p.reshape(1, 1, N_SSM_HEADS, SSM_HEAD_DIM, D_STATE)
    p_conv = xbc_p[seq - (CONV_W - 1):].reshape(1, 1, CONV_W - 1, CONV_DIM)
    s_ssm = h_s.reshape(1, dec_batch, N_SSM_HEADS, SSM_HEAD_DIM, D_STATE)
    s_conv = xbc_s.reshape(dec_batch, dec_seq, CONV_DIM)[:, dec_seq - (CONV_W - 1):].reshape(
        1, dec_batch, CONV_W - 1, CONV_DIM)
    return (y_prompt, y_sample, *p_kv, p_ssm, p_conv, *s_kv, s_ssm, s_conv)
```
